```python
import math
import jax
import jax.numpy as jnp
from jax import lax

D_MODEL = 2048
BATCH = 2
SEQ = 4096
DEPTH = 1

CHUNK = 64
Q_BLOCK = 128

ATT_HEADS = 8
ATT_HALF_DIM = 64
ATT_V_DIM = 2 * ATT_HALF_DIM
ATT_QK_WIDTH = ATT_HEADS * 2 * ATT_HALF_DIM
ATT_WIDTH = ATT_HEADS * ATT_V_DIM

REC_HEADS = 8
REC_KEY_DIM = 128
REC_VAL_DIM = 128
REC_WIDTH = REC_HEADS * REC_KEY_DIM

FFN_HIDDEN = ((8 * D_MODEL + 3 * 256 - 1) // (3 * 256)) * 256

IN_SIZES = (ATT_QK_WIDTH, ATT_QK_WIDTH, ATT_WIDTH,
            REC_WIDTH, REC_WIDTH, REC_WIDTH, REC_WIDTH,
            D_MODEL, D_MODEL)
IN_WIDTH = sum(IN_SIZES)

NORM_EPS = 1e-6

kernel_name = "hybrid_diffattn_hgrn2_gated_block"


def rms_norm(x, g):
    xf = x.astype(jnp.float32)
    y = xf * lax.rsqrt(jnp.mean(xf * xf, axis=-1, keepdims=True) + NORM_EPS)
    return (y * g.astype(jnp.float32)).astype(x.dtype)


def modulate(h, shift, scale):
    return h * (1.0 + scale[:, None, :]) + shift[:, None, :]


def lambda_init_fn(layer):
    return 0.8 - 0.6 * math.exp(-0.3 * layer)


def split_columns(p):
    outs = []
    start = 0
    for n in IN_SIZES:
        outs.append(p[..., start:start + n])
        start += n
    return outs


def diff_attention(q, k, v, lam, subln_g, lambda_init):
    B, S, _ = q.shape
    nqb = S // Q_BLOCK
    qh = q.reshape(B, nqb, Q_BLOCK, ATT_HEADS, 2, ATT_HALF_DIM).transpose(1, 0, 2, 3, 4, 5)
    kh = k.reshape(B, S, ATT_HEADS, 2, ATT_HALF_DIM)
    vh = v.reshape(B, S, ATT_HEADS, ATT_V_DIM)
    k_chunk = jnp.arange(S) // CHUNK
    scale = ATT_HALF_DIM ** -0.5

    def block(args):
        qb, bi = args
        q_chunk = (bi * Q_BLOCK + jnp.arange(Q_BLOCK)) // CHUNK
        mask = k_chunk[None, :] <= q_chunk[:, None]
        s = jnp.einsum('bqhmd,bkhmd->bhmqk', qb, kh).astype(jnp.float32) * scale
        s = jnp.where(mask, s, -jnp.inf)
        p = jax.nn.softmax(s, axis=-1)
        w = p[:, :, 0] - lam * p[:, :, 1]
        return jnp.einsum('bhqk,bkhd->bqhd', w.astype(vh.dtype), vh)

    o = lax.map(block, (qh, jnp.arange(nqb)))
    o = o.transpose(1, 0, 2, 3, 4).reshape(B, S, ATT_HEADS, ATT_V_DIM)
    o = rms_norm(o, subln_g) * (1.0 - lambda_init)
    return o.reshape(B, S, ATT_WIDTH)


def hgrn2(q, f_pre, i, g_pre, lb, gnorm_g):
    B, S, _ = q.shape
    nc = S // CHUNK
    f = lb + (1.0 - lb) * jax.nn.sigmoid(f_pre.astype(jnp.float32))
    log_f = jnp.log(f)
    key = 1.0 - f

    def to_chunks(t, d):
        return t.reshape(B, nc, CHUNK, REC_HEADS, d).transpose(1, 0, 3, 2, 4)

    qc = to_chunks(q.astype(jnp.float32), REC_KEY_DIM)
    kc = to_chunks(key, REC_KEY_DIM)
    vc = to_chunks(i.astype(jnp.float32), REC_VAL_DIM)
    gc = to_chunks(log_f, REC_KEY_DIM)
    causal = jnp.tril(jnp.ones((CHUNK, CHUNK), dtype=bool))

    def step(state, inp):
        qt, kt, vt, gt = inp
        b = jnp.cumsum(gt, axis=2)
        o_inter = jnp.einsum('bhtd,bhde->bhte', qt * jnp.exp(b), state)
        diff = b[:, :, :, None, :] - b[:, :, None, :, :]
        decay = jnp.exp(jnp.where(causal[:, :, None], diff, -jnp.inf))
        att = jnp.einsum('bhtd,bhsd,bhtsd->bhts', qt, kt, decay)
        o_intra = jnp.einsum('bhts,bhse->bhte', att, vt)
        b_last = b[:, :, -1:, :]
        new_state = (jnp.exp(b_last[:, :, 0, :])[..., None] * state
                     + jnp.einsum('bhsd,bhse->bhde', kt * jnp.exp(b_last - b), vt))
        return new_state, o_inter + o_intra

    s0 = jnp.zeros((B, REC_HEADS, REC_KEY_DIM, REC_VAL_DIM), jnp.float32)
    _, o = lax.scan(step, s0, (qc, kc, vc, gc))
    o = o.transpose(1, 0, 3, 2, 4).reshape(B, S, REC_HEADS, REC_VAL_DIM)
    o = rms_norm(o, gnorm_g).reshape(B, S, REC_WIDTH)
    o = o * jax.nn.sigmoid(g_pre.astype(jnp.float32))
    return o.astype(q.dtype)


def setup_inputs(seed: int = 0) -> dict:
    key = jax.random.key(seed)
    ks = jax.random.split(key, 24)
    D = D_MODEL
    nrm = jax.random.normal
    f32 = jnp.float32
    return {
        "x": nrm(ks[0], (BATCH, SEQ, D), f32),
        "c": nrm(ks[1], (BATCH, D), f32),
        "w_mod": nrm(ks[2], (DEPTH, D, 6 * D), f32) * D ** -0.5,
        "b_mod": nrm(ks[3], (DEPTH, 6 * D), f32) * 0.01,
        "norm1_g": 1.0 + 0.02 * nrm(ks[4], (DEPTH, D), f32),
        "w_in": nrm(ks[5], (DEPTH, D, IN_WIDTH), f32) * D ** -0.5,
        "lambda_q1": nrm(ks[6], (DEPTH, ATT_HALF_DIM), f32) * 0.1,
        "lambda_k1": nrm(ks[7], (DEPTH, ATT_HALF_DIM), f32) * 0.1,
        "lambda_q2": nrm(ks[8], (DEPTH, ATT_HALF_DIM), f32) * 0.1,
        "lambda_k2": nrm(ks[9], (DEPTH, ATT_HALF_DIM), f32) * 0.1,
        "subln_g": 1.0 + 0.02 * nrm(ks[10], (DEPTH, ATT_V_DIM), f32),
        "lb_logits": nrm(ks[11], (DEPTH + 1, REC_WIDTH), f32) * 0.5,
        "gnorm_g": 1.0 + 0.02 * nrm(ks[12], (DEPTH, REC_HEADS, REC_VAL_DIM), f32),
        "w_att_out": nrm(ks[13], (DEPTH, ATT_WIDTH, D), f32) * ATT_WIDTH ** -0.5,
        "w_rec_out": nrm(ks[14], (DEPTH, REC_WIDTH, D), f32) * REC_WIDTH ** -0.5,
        "w_o": nrm(ks[15], (DEPTH, D, D), f32) * D ** -0.5,
        "norm2_g": 1.0 + 0.02 * nrm(ks[16], (DEPTH, D), f32),
        "w_ffn_gate": nrm(ks[17], (DEPTH, D, FFN_HIDDEN), f32) * D ** -0.5,
        "w_ffn_up": nrm(ks[18], (DEPTH, D, FFN_HIDDEN), f32) * D ** -0.5,
        "w_ffn_down": nrm(ks[19], (DEPTH, FFN_HIDDEN, D), f32) * FFN_HIDDEN ** -0.5,
        "final_g": 1.0 + 0.02 * nrm(ks[20], (D,), f32),
    }


def reference(x, c, w_mod, b_mod, norm1_g, w_in, lambda_q1, lambda_k1, lambda_q2, lambda_k2,
              subln_g, lb_logits, gnorm_g, w_att_out, w_rec_out, w_o, norm2_g,
              w_ffn_gate, w_ffn_up, w_ffn_down, final_g):
    lower_bounds = jnp.cumsum(jax.nn.softmax(lb_logits.astype(jnp.float32), axis=0), axis=0)
    cond = jax.nn.silu(c)
    h = x
    for l in range(DEPTH):
        mod = cond @ w_mod[l] + b_mod[l]
        sh1, sc1, gt1, sh2, sc2, gt2 = jnp.split(mod, 6, axis=-1)

        u = modulate(rms_norm(h, norm1_g[l]), sh1, sc1)
        aq, ak, av, rq, rf, ri, rg, ga, gr = split_columns(u @ w_in[l])
        lam_init = lambda_init_fn(l)
        lam = (jnp.exp(jnp.sum(lambda_q1[l] * lambda_k1[l]).astype(jnp.float32))
               - jnp.exp(jnp.sum(lambda_q2[l] * lambda_k2[l]).astype(jnp.float32)) + lam_init)
        ya = diff_attention(aq, ak, av, lam, subln_g[l], lam_init)
        yr = hgrn2(rq, rf, ri, rg, lower_bounds[l], gnorm_g[l])
        merged = (jax.nn.sigmoid(ga) * (ya @ w_att_out[l])
                  + jax.nn.sigmoid(gr) * (yr @ w_rec_out[l]))
        h = h + gt1[:, None, :] * (merged @ w_o[l])

        u = modulate(rms_norm(h, norm2_g[l]), sh2, sc2)
        ffn = (jax.nn.silu(u @ w_ffn_gate[l]) * (u @ w_ffn_up[l])) @ w_ffn_down[l]
        h = h + gt2[:, None, :] * ffn
    return rms_norm(h, final_g)
```

```python
import functools
import math

import jax
import jax.numpy as jnp
from jax import lax
from jax.experimental import pallas as pl
from jax.experimental.pallas import tpu as pltpu

F32 = jnp.float32
BF16 = jnp.bfloat16

NORM_EPS = 1e-6
ATT_HEADS = 8
ATT_HALF_DIM = 64
HEAD_W = 128
REC_HEADS = 8
MASK_CHUNK = 64
NEG_BIG = -1e30

VMEM_LIMIT = 60 * 1024 * 1024


def _params(sem, vmem=VMEM_LIMIT):
    return pltpu.CompilerParams(dimension_semantics=sem, vmem_limit_bytes=vmem)


def _rms(x):
    return x * lax.rsqrt(jnp.mean(x * x, axis=-1, keepdims=True) + NORM_EPS)


def _mod_kernel(ct_ref, w_ref, b_ref, o_ref):
    ct = ct_ref[...]
    cond = ct * jax.nn.sigmoid(ct)
    w = w_ref[...]
    for b in range(ct.shape[1]):
        o_ref[b:b + 1, :] = jnp.sum(w * cond[:, b:b + 1], axis=0, keepdims=True) + b_ref[...]


def _mod(c, w_mod, b_mod, tn=1024):
    nb, d = c.shape
    n = w_mod.shape[1]
    return pl.pallas_call(
        _mod_kernel,
        grid=(n // tn,),
        in_specs=[pl.BlockSpec((d, nb), lambda j: (0, 0)),
                  pl.BlockSpec((d, tn), lambda j: (0, j)),
                  pl.BlockSpec((1, tn), lambda j: (0, j))],
        out_specs=pl.BlockSpec((nb, tn), lambda j: (0, j)),
        out_shape=jax.ShapeDtypeStruct((nb, n), F32),
        compiler_params=_params(("arbitrary",)),
        name="mod",
    )(c.T, w_mod, b_mod.reshape(1, n))


def _inproj_kernel(x_ref, g_ref, sh_ref, sc_ref, w_ref, o_ref, u_ref):
    @pl.when(pl.program_id(1) == 0)
    def _():
        y = _rms(x_ref[...]) * g_ref[...]
        u_ref[...] = (y * (1.0 + sc_ref[...]) + sh_ref[...]).astype(BF16)

    o_ref[...] = jnp.dot(u_ref[...], w_ref[...].astype(BF16),
                         preferred_element_type=F32).astype(o_ref.dtype)


def _in_proj(x2, g, mod4, w, rows_per_batch, tm=1024, tn=1024):
    t, d = x2.shape
    n = w.shape[1]
    tpb = rows_per_batch // tm
    return pl.pallas_call(
        _inproj_kernel,
        grid=(t // tm, n // tn),
        in_specs=[pl.BlockSpec((tm, d), lambda i, j: (i, 0)),
                  pl.BlockSpec((1, d), lambda i, j: (0, 0)),
                  pl.BlockSpec((None, None, 1, d), lambda i, j: (i // tpb, 0, 0, 0)),
                  pl.BlockSpec((None, None, 1, d), lambda i, j: (i // tpb, 1, 0, 0)),
                  pl.BlockSpec((d, tn), lambda i, j: (0, j))],
        out_specs=pl.BlockSpec((tm, tn), lambda i, j: (i, j)),
        out_shape=jax.ShapeDtypeStruct((t, n), BF16),
        scratch_shapes=[pltpu.VMEM((tm, d), BF16)],
        compiler_params=_params(("parallel", "arbitrary")),
        name="in_proj",
    )(x2, g, mod4, mod4, w)


def _attn_kernel(lam_ref, sg_ref, q_ref, k_ref, v_ref, o_ref, vt_ref, acc_ref, m_ref, l_ref,
                 *, tq, tk, lam_init):
    qi = pl.program_id(2)
    nkb = vt_ref.shape[0]
    r = tq // tk

    @pl.when(qi == 0)
    def _():
        for j in range(nkb):
            vt_ref[j] = v_ref[j * tk:(j + 1) * tk, :].astype(F32).T.astype(BF16)

    q = q_ref[...] * (ATT_HALF_DIM ** -0.5)
    lane = lax.broadcasted_iota(jnp.int32, q.shape, 1)
    zero = jnp.zeros_like(q)
    qm = (jnp.where(lane < ATT_HALF_DIM, q, zero), jnp.where(lane >= ATT_HALF_DIM, q, zero))

    m_ref[...] = jnp.full(m_ref.shape, NEG_BIG, F32)
    l_ref[...] = jnp.zeros(l_ref.shape, F32)
    acc_ref[...] = jnp.zeros(acc_ref.shape, F32)

    def step(j, mask):
        k = k_ref[pl.ds(pl.multiple_of(j * tk, tk), tk), :]
        vt = vt_ref[j]
        for mi in range(2):
            s = lax.dot_general(k, qm[mi], (((1,), (1,)), ((), ())),
                                preferred_element_type=F32)
            if mask is not None:
                s = jnp.where(mask, s, NEG_BIG)
            m_old = m_ref[mi]
            m_new = jnp.maximum(m_old, jnp.max(s, axis=0, keepdims=True))
            p = jnp.exp(s - m_new)
            alpha = jnp.exp(m_old - m_new)
            l_ref[mi] = alpha * l_ref[mi] + jnp.sum(p, axis=0, keepdims=True)
            acc_ref[mi] = alpha * acc_ref[mi] + jnp.dot(vt, p.astype(BF16),
                                                        preferred_element_type=F32)
            m_ref[mi] = m_new

    def full_step(j, carry):
        step(j, None)
        return carry

    lax.fori_loop(0, qi * r, full_step, 0)

    krow = lax.broadcasted_iota(jnp.int32, (tk, tq), 0)
    qcol = lax.broadcasted_iota(jnp.int32, (tk, tq), 1)
    for d in range(r):
        mask = (krow + d * tk) // MASK_CHUNK <= qcol // MASK_CHUNK
        step(qi * r + d, mask)

    lp = lam_ref[...]
    lam = (jnp.exp(jnp.sum(lp[0:1] * lp[1:2], axis=-1, keepdims=True))
           - jnp.exp(jnp.sum(lp[2:3] * lp[3:4], axis=-1, keepdims=True)) + lam_init)
    o = acc_ref[0] / l_ref[0] - lam * (acc_ref[1] / l_ref[1])
    ms = jnp.mean(o * o, axis=0, keepdims=True)
    y = o * lax.rsqrt(ms + NORM_EPS) * sg_ref[...] * (1.0 - lam_init)
    o_ref[...] = y.T.astype(o_ref.dtype)


def _attention(p, lam4, sg_col, nb, s, lam_init, tq=512, tk=512):
    h = ATT_HEADS
    nq = s // tq
    kern = functools.partial(_attn_kernel, tq=tq, tk=tk, lam_init=lam_init)
    return pl.pallas_call(
        kern,
        grid=(nb, h, nq),
        in_specs=[pl.BlockSpec((4, ATT_HALF_DIM), lambda b, hh, i: (0, 0)),
                  pl.BlockSpec((HEAD_W, 1), lambda b, hh, i: (0, 0)),
                  pl.BlockSpec((tq, HEAD_W), lambda b, hh, i: (b * nq + i, hh)),
                  pl.BlockSpec((s, HEAD_W), lambda b, hh, i: (b, h + hh)),
                  pl.BlockSpec((s, HEAD_W), lambda b, hh, i: (b, 2 * h + hh))],
        out_specs=pl.BlockSpec((tq, HEAD_W), lambda b, hh, i: (b * nq + i, hh)),
        out_shape=jax.ShapeDtypeStruct((nb * s, h * HEAD_W), BF16),
        scratch_shapes=[pltpu.VMEM((s // tk, HEAD_W, tk), BF16),
                        pltpu.VMEM((2, HEAD_W, tq), F32),
                        pltpu.VMEM((2, 1, tq), F32),
                        pltpu.VMEM((2, 1, tq), F32)],
        compiler_params=_params(("parallel", "parallel", "arbitrary")),
        name="diff_attn",
    )(lam4, sg_col, p, p, p)


def _cumsum_rows(g):
    n = g.shape[0]
    row = lax.broadcasted_iota(jnp.int32, g.shape, 0)
    sh = 1
    while sh < n:
        g = g + jnp.where(row >= sh, pltpu.roll(g, sh, axis=0), 0.0)
        sh *= 2
    return g


def _group_row(a, s, grp=8):
    n, w = a.shape
    a3 = a.reshape(n // grp, grp, w)
    return jnp.broadcast_to(a3[:, s:s + 1, :], (n // grp, grp, w)).reshape(n, w)


def _hgrn2_kernel(lbl_ref, gn_ref, q_ref, f_ref, i_ref, g_ref, o_ref, st_ref, *, ch, layer):
    @pl.when(pl.program_id(1) == 0)
    def _():
        st_ref[...] = jnp.zeros(st_ref.shape, F32)

    w = HEAD_W
    lbl = lbl_ref[...]
    mx = jnp.max(lbl, axis=0, keepdims=True)
    e = jnp.exp(lbl - mx)
    lb = (jnp.sum(e[0:layer + 1], axis=0, keepdims=True)
          / jnp.sum(e, axis=0, keepdims=True))

    row = lax.broadcasted_iota(jnp.int32, (ch, w), 0)
    arow = lax.broadcasted_iota(jnp.int32, (ch, ch), 0)
    acol = lax.broadcasted_iota(jnp.int32, (ch, ch), 1)
    levels = []
    hs = ch // 2
    while hs >= 8:
        levels.append(hs)
        hs //= 2

    for c in range(q_ref.shape[0] // ch):
        rs = slice(c * ch, (c + 1) * ch)
        f = lb + (1.0 - lb) * jax.nn.sigmoid(f_ref[rs, :].astype(F32))
        b_all = _cumsum_rows(jnp.log(f))
        k_all = 1.0 - f
        for h in range(REC_HEADS):
            cs = slice(h * w, (h + 1) * w)
            b = b_all[:, cs]
            kk = k_all[:, cs]
            q = q_ref[rs, cs].astype(F32)
            v = i_ref[rs, cs]
            st = st_ref[h]
            blast = b[ch - 1:ch, :]

            o = lax.dot_general((q * jnp.exp(b)).astype(BF16), st.astype(BF16),
                                (((1,), (1,)), ((), ())), preferred_element_type=F32)

            att = jnp.zeros((ch, ch), F32)
            for hs in levels:
                ref = jnp.concatenate(
                    [jnp.broadcast_to(b[blk * 2 * hs + hs - 1:blk * 2 * hs + hs, :], (2 * hs, w))
                     for blk in range(ch // (2 * hs))], axis=0)
                upper = (row % (2 * hs)) >= hs
                dec = jnp.exp(jnp.where(upper, b - ref, ref - b))
                qt = jnp.where(upper, q * dec, 0.0).astype(BF16)
                kt = jnp.where(upper, 0.0, kk * dec).astype(BF16)
                a = lax.dot_general(qt, kt, (((1,), (1,)), ((), ())), preferred_element_type=F32)
                if 2 * hs < ch:
                    a = jnp.where(arow // (2 * hs) == acol // (2 * hs), a, 0.0)
                att = att + a
            for s in range(8):
                dec = jnp.exp(jnp.minimum(b - _group_row(b, s), 0.0))
                col = jnp.sum(q * _group_row(kk, s) * dec, axis=-1, keepdims=True)
                sel = (acol == (arow // 8) * 8 + s) & ((arow % 8) >= s)
                att = att + jnp.where(sel, col, 0.0)
            o = o + jnp.dot(att.astype(BF16), v, preferred_element_type=F32)

            kt = (kk * jnp.exp(blast - b)).astype(BF16)
            st_ref[h] = jnp.exp(blast) * st + lax.dot_general(
                v, kt, (((0,), (0,)), ((), ())), preferred_element_type=F32)

            y = _rms(o) * gn_ref[:, cs] * jax.nn.sigmoid(g_ref[rs, cs].astype(F32))
            o_ref[rs, cs] = y.astype(o_ref.dtype)


def _hgrn2(p, lb_logits, gn, nb, s, layer, rb=256, ch=128):
    hw = REC_HEADS * HEAD_W
    nr = s // rb
    c0 = 3 * ATT_HEADS * HEAD_W // hw
    kern = functools.partial(_hgrn2_kernel, ch=ch, layer=layer)
    spec = lambda off: pl.BlockSpec((rb, hw), lambda b, i: (b * nr + i, c0 + off))
    return pl.pallas_call(
        kern,
        grid=(nb, nr),
        in_specs=[pl.BlockSpec(lb_logits.shape, lambda b, i: (0, 0)),
                  pl.BlockSpec((1, hw), lambda b, i: (0, 0)),
                  spec(0), spec(1), spec(2), spec(3)],
        out_specs=pl.BlockSpec((rb, hw), lambda b, i: (b * nr + i, 0)),
        out_shape=jax.ShapeDtypeStruct((nb * s, hw), BF16),
        scratch_shapes=[pltpu.VMEM((REC_HEADS, HEAD_W, HEAD_W), F32)],
        compiler_params=_params(("parallel", "arbitrary")),
        name="hgrn2",
    )(lb_logits, gn, p, p, p, p)


def _merge_kernel(ya_ref, yr_ref, ga_ref, gr_ref, wa_ref, wr_ref, o_ref):
    ta = jnp.dot(ya_ref[...], wa_ref[...].astype(BF16), preferred_element_type=F32)
    tr = jnp.dot(yr_ref[...], wr_ref[...].astype(BF16), preferred_element_type=F32)
    m = (jax.nn.sigmoid(ga_ref[...].astype(F32)) * ta
         + jax.nn.sigmoid(gr_ref[...].astype(F32)) * tr)
    o_ref[...] = m.astype(o_ref.dtype)


def _merge(ya, yr, p, wa, wr, ga_col0, tm=1024, tn=1024):
    t, ka = ya.shape
    d = wa.shape[1]
    ga0 = ga_col0 // tn
    gr0 = (ga_col0 + d) // tn
    return pl.pallas_call(
        _merge_kernel,
        grid=(t // tm, d // tn),
        in_specs=[pl.BlockSpec((tm, ka), lambda i, j: (i, 0)),
                  pl.BlockSpec((tm, ka), lambda i, j: (i, 0)),
                  pl.BlockSpec((tm, tn), lambda i, j: (i, ga0 + j)),
                  pl.BlockSpec((tm, tn), lambda i, j: (i, gr0 + j)),
                  pl.BlockSpec((ka, tn), lambda i, j: (0, j)),
                  pl.BlockSpec((ka, tn), lambda i, j: (0, j))],
        out_specs=pl.BlockSpec((tm, tn), lambda i, j: (i, j)),
        out_shape=jax.ShapeDtypeStruct((t, d), BF16),
        compiler_params=_params(("parallel", "arbitrary")),
        name="merge",
    )(ya, yr, p, p, wa, wr)


def _oproj_kernel(m_ref, x_ref, gt_ref, w_ref, o_ref):
    y = jnp.dot(m_ref[...], w_ref[...].astype(BF16), preferred_element_type=F32)
    o_ref[...] = x_ref[...] + gt_ref[...] * y


def _o_proj(m, x2, mod4, w, rows_per_batch, tm=1024, tn=1024):
    t, d = x2.shape
    tpb = rows_per_batch // tm
    return pl.pallas_call(
        _oproj_kernel,
        grid=(t // tm, d // tn),
        in_specs=[pl.BlockSpec((tm, d), lambda i, j: (i, 0)),
                  pl.BlockSpec((tm, tn), lambda i, j: (i, j)),
                  pl.BlockSpec((None, None, 1, tn), lambda i, j: (i // tpb, 2, 0, j)),
                  pl.BlockSpec((d, tn), lambda i, j: (0, j))],
        out_specs=pl.BlockSpec((tm, tn), lambda i, j: (i, j)),
        out_shape=jax.ShapeDtypeStruct((t, d), F32),
        compiler_params=_params(("parallel", "arbitrary")),
        name="o_proj",
    )(m, x2, mod4, w)


def _ffn_up_kernel(h_ref, g_ref, sh_ref, sc_ref, wg_ref, wu_ref, o_ref, u_ref):
    @pl.when(pl.program_id(1) == 0)
    def _():
        y = _rms(h_ref[...]) * g_ref[...]
        u_ref[...] = (y * (1.0 + sc_ref[...]) + sh_ref[...]).astype(BF16)

    u = u_ref[...]
    g = jnp.dot(u, wg_ref[...].astype(BF16), preferred_element_type=F32)
    up = jnp.dot(u, wu_ref[...].astype(BF16), preferred_element_type=F32)
    o_ref[...] = (g * jax.nn.sigmoid(g) * up).astype(o_ref.dtype)


def _ffn_up(h1, g, mod4, wg, wu, rows_per_batch, tm=1024, tf=512):
    t, d = h1.shape
    f = wg.shape[1]
    tpb = rows_per_batch // tm
    return pl.pallas_call(
        _ffn_up_kernel,
        grid=(t // tm, f // tf),
        in_specs=[pl.BlockSpec((tm, d), lambda i, j: (i, 0)),
                  pl.BlockSpec((1, d), lambda i, j: (0, 0)),
                  pl.BlockSpec((None, None, 1, d), lambda i, j: (i // tpb, 3, 0, 0)),
                  pl.BlockSpec((None, None, 1, d), lambda i, j: (i // tpb, 4, 0, 0)),
                  pl.BlockSpec((d, tf), lambda i, j: (0, j)),
                  pl.BlockSpec((d, tf), lambda i, j: (0, j))],
        out_specs=pl.BlockSpec((tm, tf), lambda i, j: (i, j)),
        out_shape=jax.ShapeDtypeStruct((t, f), BF16),
        scratch_shapes=[pltpu.VMEM((tm, d), BF16)],
        compiler_params=_params(("parallel", "arbitrary")),
        name="ffn_up",
    )(h1, g, mod4, mod4, wg, wu)


def _ffn_down_kernel(a_ref, h_ref, gt_ref, fg_ref, w_ref, o_ref, acc_ref, *, final):
    kk = pl.program_id(1)

    @pl.when(kk == 0)
    def _():
        acc_ref[...] = jnp.zeros(acc_ref.shape, F32)

    acc_ref[...] += jnp.dot(a_ref[...], w_ref[...].astype(BF16), preferred_element_type=F32)

    @pl.when(kk == pl.num_programs(1) - 1)
    def _():
        h2 = h_ref[...] + gt_ref[...] * acc_ref[...]
        o_ref[...] = _rms(h2) * fg_ref[...] if final else h2


def _ffn_down(a, h1, mod4, fg, w, rows_per_batch, final, tm=1024, tk=512):
    t, d = h1.shape
    f = a.shape[1]
    tpb = rows_per_batch // tm
    return pl.pallas_call(
        functools.partial(_ffn_down_kernel, final=final),
        grid=(t // tm, f // tk),
        in_specs=[pl.BlockSpec((tm, tk), lambda i, k: (i, k)),
                  pl.BlockSpec((tm, d), lambda i, k: (i, 0)),
                  pl.BlockSpec((None, None, 1, d), lambda i, k: (i // tpb, 5, 0, 0)),
                  pl.BlockSpec((1, d), lambda i, k: (0, 0)),
                  pl.BlockSpec((tk, d), lambda i, k: (k, 0))],
        out_specs=pl.BlockSpec((tm, d), lambda i, k: (i, 0)),
        out_shape=jax.ShapeDtypeStruct((t, d), F32),
        scratch_shapes=[pltpu.VMEM((tm, d), F32)],
        compiler_params=_params(("parallel", "arbitrary")),
        name="ffn_down",
    )(a, h1, mod4, fg, w)


def kernel(x, c, w_mod, b_mod, norm1_g, w_in, lambda_q1, lambda_k1, lambda_q2, lambda_k2,
           subln_g, lb_logits, gnorm_g, w_att_out, w_rec_out, w_o, norm2_g,
           w_ffn_gate, w_ffn_up, w_ffn_down, final_g):
    nb, s, d = x.shape
    depth = w_mod.shape[0]
    h = x.reshape(nb * s, d)
    for l in range(depth):
        lam_init = 0.8 - 0.6 * math.exp(-0.3 * l)
        mod4 = _mod(c, w_mod[l], b_mod[l]).reshape(nb, 6, 1, d)
        p = _in_proj(h, norm1_g[l].reshape(1, d), mod4, w_in[l], s)
        lam4 = jnp.stack([lambda_q1[l], lambda_k1[l], lambda_q2[l], lambda_k2[l]])
        ya = _attention(p, lam4, subln_g[l].reshape(HEAD_W, 1), nb, s, lam_init)
        yr = _hgrn2(p, lb_logits, gnorm_g[l].reshape(1, REC_HEADS * HEAD_W), nb, s, l)
        ga_col0 = 3 * ATT_HEADS * HEAD_W + 4 * REC_HEADS * HEAD_W
        m = _merge(ya, yr, p, w_att_out[l], w_rec_out[l], ga_col0)
        h1 = _o_proj(m, h, mod4, w_o[l], s)
        a = _ffn_up(h1, norm2_g[l].reshape(1, d), mod4, w_ffn_gate[l], w_ffn_up[l], s)
        h = _ffn_down(a, h1, mod4, final_g.reshape(1, d), w_ffn_down[l], s, l == depth - 1)
    return h.reshape(nb, s, d)
```

```python
import functools
import math

import jax
import jax.numpy as jnp
from jax import lax
from jax.experimental import pallas as pl
from jax.experimental.pallas import tpu as pltpu

F32 = jnp.float32
BF16 = jnp.bfloat16

NORM_EPS = 1e-6
ATT_HEADS = 8
ATT_HALF_DIM = 64
HEAD_W = 128
REC_HEADS = 8
MASK_CHUNK = 64
NEG_BIG = -1e30

VMEM_LIMIT = 60 * 1024 * 1024


def _params(sem, vmem=VMEM_LIMIT):
    return pltpu.CompilerParams(dimension_semantics=sem, vmem_limit_bytes=vmem)


def _rms(x):
    return x * lax.rsqrt(jnp.mean(x * x, axis=-1, keepdims=True) + NORM_EPS)


def _mod_kernel(ct_ref, w_ref, b_ref, o_ref):
    ct = ct_ref[...]
    cond = ct * jax.nn.sigmoid(ct)
    w = w_ref[...]
    for b in range(ct.shape[1]):
        o_ref[b:b + 1, :] = jnp.sum(w * cond[:, b:b + 1], axis=0, keepdims=True) + b_ref[...]


def _mod(c, w_mod, b_mod, tn=1024):
    nb, d = c.shape
    n = w_mod.shape[1]
    return pl.pallas_call(
        _mod_kernel,
        grid=(n // tn,),
        in_specs=[pl.BlockSpec((d, nb), lambda j: (0, 0)),
                  pl.BlockSpec((d, tn), lambda j: (0, j)),
                  pl.BlockSpec((1, tn), lambda j: (0, j))],
        out_specs=pl.BlockSpec((nb, tn), lambda j: (0, j)),
        out_shape=jax.ShapeDtypeStruct((nb, n), F32),
        compiler_params=_params(("arbitrary",)),
        name="mod",
    )(c.T, w_mod, b_mod.reshape(1, n))


def _inproj_kernel(x_ref, g_ref, sh_ref, sc_ref, w_ref, o_ref, u_ref):
    @pl.when(pl.program_id(1) == 0)
    def _():
        y = _rms(x_ref[...]) * g_ref[...]
        u_ref[...] = (y * (1.0 + sc_ref[...]) + sh_ref[...]).astype(BF16)

    o_ref[...] = jnp.dot(u_ref[...], w_ref[...].astype(BF16),
                         preferred_element_type=F32).astype(o_ref.dtype)


def _in_proj(x2, g, mod4, w, rows_per_batch, tm=1024, tn=1024):
    t, d = x2.shape
    n = w.shape[1]
    tpb = rows_per_batch // tm
    return pl.pallas_call(
        _inproj_kernel,
        grid=(t // tm, n // tn),
        in_specs=[pl.BlockSpec((tm, d), lambda i, j: (i, 0)),
                  pl.BlockSpec((1, d), lambda i, j: (0, 0)),
                  pl.BlockSpec((None, None, 1, d), lambda i, j: (i // tpb, 0, 0, 0)),
                  pl.BlockSpec((None, None, 1, d), lambda i, j: (i // tpb, 1, 0, 0)),
                  pl.BlockSpec((d, tn), lambda i, j: (0, j))],
        out_specs=pl.BlockSpec((tm, tn), lambda i, j: (i, j)),
        out_shape=jax.ShapeDtypeStruct((t, n), BF16),
        scratch_shapes=[pltpu.VMEM((tm, d), BF16)],
        compiler_params=_params(("parallel", "arbitrary")),
        name="in_proj",
    )(x2, g, mod4, mod4, w)


def _col_reduce(x, op, groups=8):
    k, n = x.shape
    if k % (groups * 8) == 0:
        x = op(x.reshape(groups, k // groups, n), axis=0)
    return op(x, axis=0, keepdims=True)


def _attn_kernel(lam_ref, sg_ref, q_ref, k_ref, v_ref, o_ref, vt_ref, acc_ref, m_ref, l_ref,
                 *, tq, sq, tk, ahead, lam_init):
    qi = pl.program_id(2)
    nkb = vt_ref.shape[0]
    nc = tq // sq

    @pl.when(qi == 0)
    def _():
        for j in range(nkb):
            vt_ref[j] = v_ref[j * tk:(j + 1) * tk, :].astype(F32).T.astype(BF16)

    q = (q_ref[...].astype(F32) * (ATT_HALF_DIM ** -0.5 * math.log2(math.e))).astype(BF16)
    lane = lax.broadcasted_iota(jnp.int32, q.shape, 1)
    zero = jnp.zeros_like(q)
    qm = (jnp.where(lane < ATT_HALF_DIM, q, zero), jnp.where(lane >= ATT_HALF_DIM, q, zero))

    m_ref[...] = jnp.full(m_ref.shape, NEG_BIG, F32)
    l_ref[...] = jnp.zeros(l_ref.shape, F32)
    acc_ref[...] = jnp.zeros(acc_ref.shape, F32)

    def scores(c, mi, k, vt, mask):
        s = lax.dot_general(k, qm[mi][c * sq:(c + 1) * sq, :], (((1,), (1,)), ((), ())),
                            preferred_element_type=F32)
        return s if mask is None else jnp.where(mask, s, NEG_BIG)

    def update(c, mi, k, vt, mask, s):
        cs = slice(c * sq, (c + 1) * sq)
        m_old = m_ref[mi, :, cs]
        m_new = jnp.maximum(m_old, _col_reduce(s, jnp.max))
        p = jnp.exp2(s - m_new)
        alpha = jnp.exp2(m_old - m_new)
        l_ref[mi, :, cs] = alpha * l_ref[mi, :, cs] + _col_reduce(p, jnp.sum)
        acc_ref[mi, :, cs] = alpha * acc_ref[mi, :, cs] + jnp.dot(
            vt, p.astype(BF16), preferred_element_type=F32)
        m_ref[mi, :, cs] = m_new

    def run_chains(items):
        pend = []
        for it in items:
            pend.append(it + (scores(*it),))
            if len(pend) > ahead:
                update(*pend.pop(0))
        for it in pend:
            update(*it)

    def full_step(j, carry):
        k = k_ref[pl.ds(pl.multiple_of(j * tk, tk), tk), :]
        vt = vt_ref[j]
        run_chains([(c, mi, k, vt, None) for c in range(nc) for mi in range(2)])
        return carry

    lax.fori_loop(0, qi * (tq // tk), full_step, 0)

    krow = lax.broadcasted_iota(jnp.int32, (sq, sq), 0)
    qcol = lax.broadcasted_iota(jnp.int32, (sq, sq), 1)
    dmask = krow // MASK_CHUNK <= qcol // MASK_CHUNK
    per = tk // sq
    items = []
    for i in range(nc):
        k = k_ref[pl.ds(pl.multiple_of(qi * tq + i * sq, sq), sq), :]
        vt = vt_ref[qi * (tq // tk) + i // per][:, (i % per) * sq:(i % per + 1) * sq]
        items += [(c, mi, k, vt, dmask if c == i else None)
                  for c in range(i, nc) for mi in range(2)]
    run_chains(items)

    lp = lam_ref[...]
    lam = (jnp.exp(jnp.sum(lp[0:1] * lp[1:2], axis=-1, keepdims=True))
           - jnp.exp(jnp.sum(lp[2:3] * lp[3:4], axis=-1, keepdims=True)) + lam_init)
    o = acc_ref[0] / l_ref[0] - lam * (acc_ref[1] / l_ref[1])
    ms = jnp.mean(o * o, axis=0, keepdims=True)
    y = o * lax.rsqrt(ms + NORM_EPS) * sg_ref[...] * (1.0 - lam_init)
    o_ref[...] = y.T.astype(o_ref.dtype)


def _attention(p, lam4, sg_col, nb, s, lam_init, tq=1024, sq=256, tk=512, ahead=7):
    h = ATT_HEADS
    nq = s // tq
    kern = functools.partial(_attn_kernel, tq=tq, sq=sq, tk=tk, ahead=ahead, lam_init=lam_init)
    return pl.pallas_call(
        kern,
        grid=(nb, h, nq),
        in_specs=[pl.BlockSpec((4, ATT_HALF_DIM), lambda b, hh, i: (0, 0)),
                  pl.BlockSpec((HEAD_W, 1), lambda b, hh, i: (0, 0)),
                  pl.BlockSpec((tq, HEAD_W), lambda b, hh, i: (b * nq + i, hh)),
                  pl.BlockSpec((s, HEAD_W), lambda b, hh, i: (b, h + hh)),
                  pl.BlockSpec((s, HEAD_W), lambda b, hh, i: (b, 2 * h + hh))],
        out_specs=pl.BlockSpec((tq, HEAD_W), lambda b, hh, i: (b * nq + i, hh)),
        out_shape=jax.ShapeDtypeStruct((nb * s, h * HEAD_W), BF16),
        scratch_shapes=[pltpu.VMEM((s // tk, HEAD_W, tk), BF16),
                        pltpu.VMEM((2, HEAD_W, tq), F32),
                        pltpu.VMEM((2, 1, tq), F32),
                        pltpu.VMEM((2, 1, tq), F32)],
        compiler_params=_params(("parallel", "parallel", "arbitrary")),
        name="diff_attn",
    )(lam4, sg_col, p, p, p)


def _cumsum_rows(g):
    n = g.shape[0]
    row = lax.broadcasted_iota(jnp.int32, g.shape, 0)
    sh = 1
    while sh < n:
        g = g + jnp.where(row >= sh, pltpu.roll(g, sh, axis=0), 0.0)
        sh *= 2
    return g


def _group_row(a, s, grp=8):
    n, w = a.shape
    a3 = a.reshape(n // grp, grp, w)
    return jnp.broadcast_to(a3[:, s:s + 1, :], (n // grp, grp, w)).reshape(n, w)


def _hgrn2_kernel(lbl_ref, gn_ref, q_ref, f_ref, i_ref, g_ref, o_ref, st_ref, *, ch, layer):
    @pl.when(pl.program_id(1) == 0)
    def _():
        st_ref[...] = jnp.zeros(st_ref.shape, F32)

    w = HEAD_W
    lbl = lbl_ref[...]
    mx = jnp.max(lbl, axis=0, keepdims=True)
    e = jnp.exp(lbl - mx)
    lb = (jnp.sum(e[0:layer + 1], axis=0, keepdims=True)
          / jnp.sum(e, axis=0, keepdims=True))

    row = lax.broadcasted_iota(jnp.int32, (ch, w), 0)
    arow = lax.broadcasted_iota(jnp.int32, (ch, ch), 0)
    acol = lax.broadcasted_iota(jnp.int32, (ch, ch), 1)
    levels = []
    hs = ch // 2
    while hs >= 8:
        levels.append(hs)
        hs //= 2

    for c in range(q_ref.shape[0] // ch):
        rs = slice(c * ch, (c + 1) * ch)
        f = lb + (1.0 - lb) * jax.nn.sigmoid(f_ref[rs, :].astype(F32))
        b_all = _cumsum_rows(jnp.log(f))
        k_all = 1.0 - f
        for h in range(REC_HEADS):
            cs = slice(h * w, (h + 1) * w)
            b = b_all[:, cs]
            kk = k_all[:, cs]
            q = q_ref[rs, cs].astype(F32)
            v = i_ref[rs, cs]
            st = st_ref[h]
            blast = b[ch - 1:ch, :]

            o = lax.dot_general((q * jnp.exp(b)).astype(BF16), st.astype(BF16),
                                (((1,), (1,)), ((), ())), preferred_element_type=F32)

            att = jnp.zeros((ch, ch), F32)
            for hs in levels:
                ref = jnp.concatenate(
                    [jnp.broadcast_to(b[blk * 2 * hs + hs - 1:blk * 2 * hs + hs, :], (2 * hs, w))
                     for blk in range(ch // (2 * hs))], axis=0)
                upper = (row % (2 * hs)) >= hs
                dec = jnp.exp(jnp.where(upper, b - ref, ref - b))
                qt = jnp.where(upper, q * dec, 0.0).astype(BF16)
                kt = jnp.where(upper, 0.0, kk * dec).astype(BF16)
                a = lax.dot_general(qt, kt, (((1,), (1,)), ((), ())), preferred_element_type=F32)
                if 2 * hs < ch:
                    a = jnp.where(arow // (2 * hs) == acol // (2 * hs), a, 0.0)
                att = att + a
            for s in range(8):
                dec = jnp.exp(jnp.minimum(b - _group_row(b, s), 0.0))
                col = jnp.sum(q * _group_row(kk, s) * dec, axis=-1, keepdims=True)
                sel = (acol == (arow // 8) * 8 + s) & ((arow % 8) >= s)
                att = att + jnp.where(sel, col, 0.0)
            o = o + jnp.dot(att.astype(BF16), v, preferred_element_type=F32)

            kt = (kk * jnp.exp(blast - b)).astype(BF16)
            st_ref[h] = jnp.exp(blast) * st + lax.dot_general(
                v, kt, (((0,), (0,)), ((), ())), preferred_element_type=F32)

            y = _rms(o) * gn_ref[:, cs] * jax.nn.sigmoid(g_ref[rs, cs].astype(F32))
            o_ref[rs, cs] = y.astype(o_ref.dtype)


def _hgrn2(p, lb_logits, gn, nb, s, layer, rb=256, ch=128):
    hw = REC_HEADS * HEAD_W
    nr = s // rb
    c0 = 3 * ATT_HEADS * HEAD_W // hw
    kern = functools.partial(_hgrn2_kernel, ch=ch, layer=layer)
    spec = lambda off: pl.BlockSpec((rb, hw), lambda b, i: (b * nr + i, c0 + off))
    return pl.pallas_call(
        kern,
        grid=(nb, nr),
        in_specs=[pl.BlockSpec(lb_logits.shape, lambda b, i: (0, 0)),
                  pl.BlockSpec((1, hw), lambda b, i: (0, 0)),
                  spec(0), spec(1), spec(2), spec(3)],
        out_specs=pl.BlockSpec((rb, hw), lambda b, i: (b * nr + i, 0)),
        out_shape=jax.ShapeDtypeStruct((nb * s, hw), BF16),
        scratch_shapes=[pltpu.VMEM((REC_HEADS, HEAD_W, HEAD_W), F32)],
        compiler_params=_params(("parallel", "arbitrary")),
        name="hgrn2",
    )(lb_logits, gn, p, p, p, p)


def _merge_kernel(ya_ref, yr_ref, ga_ref, gr_ref, wa_ref, wr_ref, o_ref):
    ta = jnp.dot(ya_ref[...], wa_ref[...].astype(BF16), preferred_element_type=F32)
    tr = jnp.dot(yr_ref[...], wr_ref[...].astype(BF16), preferred_element_type=F32)
    m = (jax.nn.sigmoid(ga_ref[...].astype(F32)) * ta
         + jax.nn.sigmoid(gr_ref[...].astype(F32)) * tr)
    o_ref[...] = m.astype(o_ref.dtype)


def _merge(ya, yr, p, wa, wr, ga_col0, tm=1024, tn=1024):
    t, ka = ya.shape
    d = wa.shape[1]
    ga0 = ga_col0 // tn
    gr0 = (ga_col0 + d) // tn
    return pl.pallas_call(
        _merge_kernel,
        grid=(t // tm, d // tn),
        in_specs=[pl.BlockSpec((tm, ka), lambda i, j: (i, 0)),
                  pl.BlockSpec((tm, ka), lambda i, j: (i, 0)),
                  pl.BlockSpec((tm, tn), lambda i, j: (i, ga0 + j)),
                  pl.BlockSpec((tm, tn), lambda i, j: (i, gr0 + j)),
                  pl.BlockSpec((ka, tn), lambda i, j: (0, j)),
                  pl.BlockSpec((ka, tn), lambda i, j: (0, j))],
        out_specs=pl.BlockSpec((tm, tn), lambda i, j: (i, j)),
        out_shape=jax.ShapeDtypeStruct((t, d), BF16),
        compiler_params=_params(("parallel", "arbitrary")),
        name="merge",
    )(ya, yr, p, p, wa, wr)


def _oproj_kernel(m_ref, x_ref, gt_ref, w_ref, o_ref):
    y = jnp.dot(m_ref[...], w_ref[...].astype(BF16), preferred_element_type=F32)
    o_ref[...] = x_ref[...] + gt_ref[...] * y


def _o_proj(m, x2, mod4, w, rows_per_batch, tm=1024, tn=1024):
    t, d = x2.shape
    tpb = rows_per_batch // tm
    return pl.pallas_call(
        _oproj_kernel,
        grid=(t // tm, d // tn),
        in_specs=[pl.BlockSpec((tm, d), lambda i, j: (i, 0)),
                  pl.BlockSpec((tm, tn), lambda i, j: (i, j)),
                  pl.BlockSpec((None, None, 1, tn), lambda i, j: (i // tpb, 2, 0, j)),
                  pl.BlockSpec((d, tn), lambda i, j: (0, j))],
        out_specs=pl.BlockSpec((tm, tn), lambda i, j: (i, j)),
        out_shape=jax.ShapeDtypeStruct((t, d), F32),
        compiler_params=_params(("parallel", "arbitrary")),
        name="o_proj",
    )(m, x2, mod4, w)


def _ffn_up_kernel(h_ref, g_ref, sh_ref, sc_ref, wg_ref, wu_ref, o_ref, u_ref):
    @pl.when(pl.program_id(1) == 0)
    def _():
        y = _rms(h_ref[...]) * g_ref[...]
        u_ref[...] = (y * (1.0 + sc_ref[...]) + sh_ref[...]).astype(BF16)

    u = u_ref[...]
    g = jnp.dot(u, wg_ref[...].astype(BF16), preferred_element_type=F32)
    up = jnp.dot(u, wu_ref[...].astype(BF16), preferred_element_type=F32)
    o_ref[...] = (g * jax.nn.sigmoid(g) * up).astype(o_ref.dtype)


def _ffn_up(h1, g, mod4, wg, wu, rows_per_batch, tm=1024, tf=512):
    t, d = h1.shape
    f = wg.shape[1]
    tpb = rows_per_batch // tm
    return pl.pallas_call(
        _ffn_up_kernel,
        grid=(t // tm, f // tf),
        in_specs=[pl.BlockSpec((tm, d), lambda i, j: (i, 0)),
                  pl.BlockSpec((1, d), lambda i, j: (0, 0)),
                  pl.BlockSpec((None, None, 1, d), lambda i, j: (i // tpb, 3, 0, 0)),
                  pl.BlockSpec((None, None, 1, d), lambda i, j: (i // tpb, 4, 0, 0)),
                  pl.BlockSpec((d, tf), lambda i, j: (0, j)),
                  pl.BlockSpec((d, tf), lambda i, j: (0, j))],
        out_specs=pl.BlockSpec((tm, tf), lambda i, j: (i, j)),
        out_shape=jax.ShapeDtypeStruct((t, f), BF16),
        scratch_shapes=[pltpu.VMEM((tm, d), BF16)],
        compiler_params=_params(("parallel", "arbitrary")),
        name="ffn_up",
    )(h1, g, mod4, mod4, wg, wu)


def _ffn_down_kernel(a_ref, h_ref, gt_ref, fg_ref, w_ref, o_ref, acc_ref, *, final):
    kk = pl.program_id(1)

    @pl.when(kk == 0)
    def _():
        acc_ref[...] = jnp.zeros(acc_ref.shape, F32)

    acc_ref[...] += jnp.dot(a_ref[...], w_ref[...].astype(BF16), preferred_element_type=F32)

    @pl.when(kk == pl.num_programs(1) - 1)
    def _():
        h2 = h_ref[...] + gt_ref[...] * acc_ref[...]
        o_ref[...] = _rms(h2) * fg_ref[...] if final else h2


def _ffn_down(a, h1, mod4, fg, w, rows_per_batch, final, tm=1024, tk=512):
    t, d = h1.shape
    f = a.shape[1]
    tpb = rows_per_batch // tm
    return pl.pallas_call(
        functools.partial(_ffn_down_kernel, final=final),
        grid=(t // tm, f // tk),
        in_specs=[pl.BlockSpec((tm, tk), lambda i, k: (i, k)),
                  pl.BlockSpec((tm, d), lambda i, k: (i, 0)),
                  pl.BlockSpec((None, None, 1, d), lambda i, k: (i // tpb, 5, 0, 0)),
                  pl.BlockSpec((1, d), lambda i, k: (0, 0)),
                  pl.BlockSpec((tk, d), lambda i, k: (k, 0))],
        out_specs=pl.BlockSpec((tm, d), lambda i, k: (i, 0)),
        out_shape=jax.ShapeDtypeStruct((t, d), F32),
        scratch_shapes=[pltpu.VMEM((tm, d), F32)],
        compiler_params=_params(("parallel", "arbitrary")),
        name="ffn_down",
    )(a, h1, mod4, fg, w)


def kernel(x, c, w_mod, b_mod, norm1_g, w_in, lambda_q1, lambda_k1, lambda_q2, lambda_k2,
           subln_g, lb_logits, gnorm_g, w_att_out, w_rec_out, w_o, norm2_g,
           w_ffn_gate, w_ffn_up, w_ffn_down, final_g):
    nb, s, d = x.shape
    depth = w_mod.shape[0]
    h = x.reshape(nb * s, d)
    for l in range(depth):
        lam_init = 0.8 - 0.6 * math.exp(-0.3 * l)
        mod4 = _mod(c, w_mod[l], b_mod[l]).reshape(nb, 6, 1, d)
        p = _in_proj(h, norm1_g[l].reshape(1, d), mod4, w_in[l], s)
        lam4 = jnp.stack([lambda_q1[l], lambda_k1[l], lambda_q2[l], lambda_k2[l]])
        ya = _attention(p, lam4, subln_g[l].reshape(HEAD_W, 1), nb, s, lam_init)
        yr = _hgrn2(p, lb_logits, gnorm_g[l].reshape(1, REC_HEADS * HEAD_W), nb, s, l)
        ga_col0 = 3 * ATT_HEADS * HEAD_W + 4 * REC_HEADS * HEAD_W
        m = _merge(ya, yr, p, w_att_out[l], w_rec_out[l], ga_col0)
        h1 = _o_proj(m, h, mod4, w_o[l], s)
        a = _ffn_up(h1, norm2_g[l].reshape(1, d), mod4, w_ffn_gate[l], w_ffn_up[l], s)
        h = _ffn_down(a, h1, mod4, final_g.reshape(1, d), w_ffn_down[l], s, l == depth - 1)
    return h.reshape(nb, s, d)
```

```python
import functools
import math

import jax
import jax.numpy as jnp
from jax import lax
from jax.experimental import pallas as pl
from jax.experimental.pallas import tpu as pltpu

F32 = jnp.float32
BF16 = jnp.bfloat16

NORM_EPS = 1e-6
ATT_HEADS = 8
ATT_HALF_DIM = 64
HEAD_W = 128
REC_HEADS = 8
MASK_CHUNK = 64
NEG_BIG = -1e30

VMEM_LIMIT = 60 * 1024 * 1024


def _params(sem, vmem=VMEM_LIMIT):
    return pltpu.CompilerParams(dimension_semantics=sem, vmem_limit_bytes=vmem)


def _rms(x):
    return x * lax.rsqrt(jnp.mean(x * x, axis=-1, keepdims=True) + NORM_EPS)


def _mod_kernel(ct_ref, w_ref, b_ref, o_ref):
    ct = ct_ref[...]
    cond = ct * jax.nn.sigmoid(ct)
    w = w_ref[...]
    for b in range(ct.shape[1]):
        o_ref[b:b + 1, :] = jnp.sum(w * cond[:, b:b + 1], axis=0, keepdims=True) + b_ref[...]


def _mod(c, w_mod, b_mod, tn=1024):
    nb, d = c.shape
    n = w_mod.shape[1]
    return pl.pallas_call(
        _mod_kernel,
        grid=(n // tn,),
        in_specs=[pl.BlockSpec((d, nb), lambda j: (0, 0)),
                  pl.BlockSpec((d, tn), lambda j: (0, j)),
                  pl.BlockSpec((1, tn), lambda j: (0, j))],
        out_specs=pl.BlockSpec((nb, tn), lambda j: (0, j)),
        out_shape=jax.ShapeDtypeStruct((nb, n), F32),
        compiler_params=_params(("arbitrary",)),
        name="mod",
    )(c.T, w_mod, b_mod.reshape(1, n))


def _inproj_kernel(x_ref, g_ref, sh_ref, sc_ref, w_ref, o_ref, u_ref):
    @pl.when(pl.program_id(1) == 0)
    def _():
        y = _rms(x_ref[...]) * g_ref[...]
        u_ref[...] = (y * (1.0 + sc_ref[...]) + sh_ref[...]).astype(BF16)

    o_ref[...] = jnp.dot(u_ref[...], w_ref[...].astype(BF16),
                         preferred_element_type=F32).astype(o_ref.dtype)


def _in_proj(x2, g, mod4, w, rows_per_batch, tm=1024, tn=1024):
    t, d = x2.shape
    n = w.shape[1]
    tpb = rows_per_batch // tm
    return pl.pallas_call(
        _inproj_kernel,
        grid=(t // tm, n // tn),
        in_specs=[pl.BlockSpec((tm, d), lambda i, j: (i, 0)),
                  pl.BlockSpec((1, d), lambda i, j: (0, 0)),
                  pl.BlockSpec((None, None, 1, d), lambda i, j: (i // tpb, 0, 0, 0)),
                  pl.BlockSpec((None, None, 1, d), lambda i, j: (i // tpb, 1, 0, 0)),
                  pl.BlockSpec((d, tn), lambda i, j: (0, j))],
        out_specs=pl.BlockSpec((tm, tn), lambda i, j: (i, j)),
        out_shape=jax.ShapeDtypeStruct((t, n), BF16),
        scratch_shapes=[pltpu.VMEM((tm, d), BF16)],
        compiler_params=_params(("parallel", "arbitrary")),
        name="in_proj",
    )(x2, g, mod4, mod4, w)


def _col_reduce(x, op, groups=8):
    k, n = x.shape
    if k % (groups * 8) == 0:
        x = op(x.reshape(groups, k // groups, n), axis=0)
    return op(x, axis=0, keepdims=True)


def _attn_kernel(lam_ref, sg_ref, q_ref, k_ref, v_ref, o_ref, vt_ref, acc_ref, m_ref, l_ref,
                 *, tq, sq, tk, ahead, lam_init):
    qi = pl.program_id(2)
    nkb = vt_ref.shape[0]
    nc = tq // sq

    @pl.when(qi == 0)
    def _():
        for j in range(nkb):
            vt_ref[j] = v_ref[j * tk:(j + 1) * tk, :].astype(F32).T.astype(BF16)

    q = (q_ref[...].astype(F32) * (ATT_HALF_DIM ** -0.5 * math.log2(math.e))).astype(BF16)
    lane = lax.broadcasted_iota(jnp.int32, q.shape, 1)
    zero = jnp.zeros_like(q)
    qm = (jnp.where(lane < ATT_HALF_DIM, q, zero), jnp.where(lane >= ATT_HALF_DIM, q, zero))

    m_ref[...] = jnp.full(m_ref.shape, NEG_BIG, F32)
    l_ref[...] = jnp.zeros(l_ref.shape, F32)
    acc_ref[...] = jnp.zeros(acc_ref.shape, F32)

    def scores(c, mi, k, vt, mask):
        s = lax.dot_general(k, qm[mi][c * sq:(c + 1) * sq, :], (((1,), (1,)), ((), ())),
                            preferred_element_type=F32)
        return s if mask is None else jnp.where(mask, s, NEG_BIG)

    def update(c, mi, k, vt, mask, s):
        cs = slice(c * sq, (c + 1) * sq)
        m_old = m_ref[mi, :, cs]
        m_new = jnp.maximum(m_old, _col_reduce(s, jnp.max))
        p = jnp.exp2(s - m_new)
        alpha = jnp.exp2(m_old - m_new)
        l_ref[mi, :, cs] = alpha * l_ref[mi, :, cs] + _col_reduce(p, jnp.sum)
        acc_ref[mi, :, cs] = alpha * acc_ref[mi, :, cs] + jnp.dot(
            vt, p.astype(BF16), preferred_element_type=F32)
        m_ref[mi, :, cs] = m_new

    def run_chains(items):
        pend = []
        for it in items:
            pend.append(it + (scores(*it),))
            if len(pend) > ahead:
                update(*pend.pop(0))
        for it in pend:
            update(*it)

    def full_step(j, carry):
        k = k_ref[pl.ds(pl.multiple_of(j * tk, tk), tk), :]
        vt = vt_ref[j]
        run_chains([(c, mi, k, vt, None) for c in range(nc) for mi in range(2)])
        return carry

    lax.fori_loop(0, qi * (tq // tk), full_step, 0)

    krow = lax.broadcasted_iota(jnp.int32, (sq, sq), 0)
    qcol = lax.broadcasted_iota(jnp.int32, (sq, sq), 1)
    dmask = krow // MASK_CHUNK <= qcol // MASK_CHUNK
    per = tk // sq
    items = []
    for i in range(nc):
        k = k_ref[pl.ds(pl.multiple_of(qi * tq + i * sq, sq), sq), :]
        vt = vt_ref[qi * (tq // tk) + i // per][:, (i % per) * sq:(i % per + 1) * sq]
        items += [(c, mi, k, vt, dmask if c == i else None)
                  for c in range(i, nc) for mi in range(2)]
    run_chains(items)

    lp = lam_ref[...]
    lam = (jnp.exp(jnp.sum(lp[0:1] * lp[1:2], axis=-1, keepdims=True))
           - jnp.exp(jnp.sum(lp[2:3] * lp[3:4], axis=-1, keepdims=True)) + lam_init)
    o = acc_ref[0] / l_ref[0] - lam * (acc_ref[1] / l_ref[1])
    ms = jnp.mean(o * o, axis=0, keepdims=True)
    y = o * lax.rsqrt(ms + NORM_EPS) * sg_ref[...] * (1.0 - lam_init)
    o_ref[...] = y.T.astype(o_ref.dtype)


def _attention(p, lam4, sg_col, nb, s, lam_init, tq=1024, sq=256, tk=512, ahead=7):
    h = ATT_HEADS
    nq = s // tq
    kern = functools.partial(_attn_kernel, tq=tq, sq=sq, tk=tk, ahead=ahead, lam_init=lam_init)
    return pl.pallas_call(
        kern,
        grid=(nb, h, nq),
        in_specs=[pl.BlockSpec((4, ATT_HALF_DIM), lambda b, hh, i: (0, 0)),
                  pl.BlockSpec((HEAD_W, 1), lambda b, hh, i: (0, 0)),
                  pl.BlockSpec((tq, HEAD_W), lambda b, hh, i: (b * nq + i, hh)),
                  pl.BlockSpec((s, HEAD_W), lambda b, hh, i: (b, h + hh)),
                  pl.BlockSpec((s, HEAD_W), lambda b, hh, i: (b, 2 * h + hh))],
        out_specs=pl.BlockSpec((tq, HEAD_W), lambda b, hh, i: (b * nq + i, hh)),
        out_shape=jax.ShapeDtypeStruct((nb * s, h * HEAD_W), BF16),
        scratch_shapes=[pltpu.VMEM((s // tk, HEAD_W, tk), BF16),
                        pltpu.VMEM((2, HEAD_W, tq), F32),
                        pltpu.VMEM((2, 1, tq), F32),
                        pltpu.VMEM((2, 1, tq), F32)],
        compiler_params=_params(("parallel", "parallel", "arbitrary")),
        name="diff_attn",
    )(lam4, sg_col, p, p, p)


def _neg_abs(x):
    bits = lax.bitcast_convert_type(x, jnp.uint32) | jnp.uint32(0x80000000)
    return lax.bitcast_convert_type(bits, F32)


def _group_row(a, s, grp=8):
    n, w = a.shape
    a3 = a.reshape(n // grp, grp, w)
    return jnp.broadcast_to(a3[:, s:s + 1, :], (n // grp, grp, w)).reshape(n, w)


def _hgrn2_kernel(lbl_ref, gn_ref, q_ref, f_ref, i_ref, g_ref, o_ref, st_ref, *, ch, layer):
    @pl.when(pl.program_id(1) == 0)
    def _():
        st_ref[...] = jnp.zeros(st_ref.shape, F32)

    w = HEAD_W
    lbl = lbl_ref[...]
    mx = jnp.max(lbl, axis=0, keepdims=True)
    e = jnp.exp(lbl - mx)
    lb = (jnp.sum(e[0:layer + 1], axis=0, keepdims=True)
          / jnp.sum(e, axis=0, keepdims=True))

    nt = (((1,), (1,)), ((), ()))
    row = lax.broadcasted_iota(jnp.int32, (ch, w), 0)
    arow = lax.broadcasted_iota(jnp.int32, (ch, ch), 0)
    acol = lax.broadcasted_iota(jnp.int32, (ch, ch), 1)
    tril = (arow >= acol).astype(BF16)
    on_diag = arow == acol
    levels = []
    hs = ch // 2
    while hs >= 1:
        levels.append((hs, (row % (2 * hs)) >= hs,
                       (arow // (2 * hs) == acol // (2 * hs))
                       & ((arow % (2 * hs)) >= hs) & ((acol % (2 * hs)) < hs)))
        hs //= 2

    def midpoint_row(b, hs):
        if hs >= 8:
            return jnp.concatenate(
                [jnp.broadcast_to(b[blk * 2 * hs + hs - 1:blk * 2 * hs + hs, :], (2 * hs, w))
                 for blk in range(ch // (2 * hs))], axis=0)
        if hs == 4:
            return _group_row(b, 3)
        return jnp.where((row % 8) < 4, _group_row(b, 1), _group_row(b, 5))

    chunk_vals = {}

    def chunk_prep(c):
        if c not in chunk_vals:
            rs = slice(c * ch, (c + 1) * ch)
            f_all = lb + (1.0 - lb) * jax.nn.sigmoid(f_ref[rs, :].astype(F32))
            g2 = jnp.log2(f_all)
            g2_hi = g2.astype(BF16)
            g2_lo = (g2 - g2_hi.astype(F32)).astype(BF16)
            b_all = (jnp.dot(tril, g2_hi, preferred_element_type=F32)
                     + jnp.dot(tril, g2_lo, preferred_element_type=F32))
            chunk_vals[c] = (f_all, b_all)
        return chunk_vals[c]

    def operands(c, h):
        rs, cs = slice(c * ch, (c + 1) * ch), slice(h * w, (h + 1) * w)
        f_all, b_all = chunk_prep(c)
        b, f = b_all[:, cs], f_all[:, cs]
        kk = 1.0 - f
        q = q_ref[rs, cs].astype(F32)
        blast = b[ch - 1:ch, :]
        ys = []
        for hs, upper, _ in levels:
            if hs == 1:
                y = jnp.where(upper, q * f, kk)
            else:
                y = jnp.where(upper, q, kk) * jnp.exp2(_neg_abs(b - midpoint_row(b, hs)))
            ys.append(y.astype(BF16))
        return dict(rs=rs, cs=cs, h=h, ys=ys,
                    qin=(q * jnp.exp2(b)).astype(BF16),
                    kout=(kk * jnp.exp2(blast - b)).astype(BF16),
                    keep=jnp.exp2(blast),
                    dsum=jnp.sum(q * kk, axis=-1, keepdims=True))

    def level_matmuls(x):
        st = st_ref[x["h"]]
        x["st"] = st
        x["o"] = lax.dot_general(x["qin"], st.astype(BF16), nt, preferred_element_type=F32)
        x["aa"] = [lax.dot_general(y, y, nt, preferred_element_type=F32) for y in x["ys"]]

    def finish(x):
        rs, cs = x["rs"], x["cs"]
        v = i_ref[rs, cs]
        att = jnp.where(on_diag, x["dsum"], 0.0)
        for (_, _, amask), a in zip(levels, x["aa"]):
            att = jnp.where(amask, a, att)
        o = x["o"] + jnp.dot(att.astype(BF16), v, preferred_element_type=F32)
        st_ref[x["h"]] = x["keep"] * x["st"] + lax.dot_general(
            v, x["kout"], (((0,), (0,)), ((), ())), preferred_element_type=F32)
        y = _rms(o) * gn_ref[:, cs] * jax.nn.sigmoid(g_ref[rs, cs].astype(F32))
        o_ref[rs, cs] = y.astype(o_ref.dtype)

    items = [(c, h) for c in range(q_ref.shape[0] // ch) for h in range(REC_HEADS)]
    staged = []
    d1, d2 = 1, 3
    for i in range(len(items) + d2):
        if i < len(items):
            staged.append(operands(*items[i]))
        if d1 <= i < len(items) + d1:
            level_matmuls(staged[i - d1])
        if i >= d2:
            finish(staged[i - d2])


def _hgrn2(p, lb_logits, gn, nb, s, layer, rb=256, ch=128):
    hw = REC_HEADS * HEAD_W
    nr = s // rb
    c0 = 3 * ATT_HEADS * HEAD_W // hw
    kern = functools.partial(_hgrn2_kernel, ch=ch, layer=layer)
    spec = lambda off: pl.BlockSpec((rb, hw), lambda b, i: (b * nr + i, c0 + off))
    return pl.pallas_call(
        kern,
        grid=(nb, nr),
        in_specs=[pl.BlockSpec(lb_logits.shape, lambda b, i: (0, 0)),
                  pl.BlockSpec((1, hw), lambda b, i: (0, 0)),
                  spec(0), spec(1), spec(2), spec(3)],
        out_specs=pl.BlockSpec((rb, hw), lambda b, i: (b * nr + i, 0)),
        out_shape=jax.ShapeDtypeStruct((nb * s, hw), BF16),
        scratch_shapes=[pltpu.VMEM((REC_HEADS, HEAD_W, HEAD_W), F32)],
        compiler_params=_params(("parallel", "arbitrary")),
        name="hgrn2",
    )(lb_logits, gn, p, p, p, p)


def _merge_kernel(ya_ref, yr_ref, ga_ref, gr_ref, wa_ref, wr_ref, o_ref):
    ta = jnp.dot(ya_ref[...], wa_ref[...].astype(BF16), preferred_element_type=F32)
    tr = jnp.dot(yr_ref[...], wr_ref[...].astype(BF16), preferred_element_type=F32)
    m = (jax.nn.sigmoid(ga_ref[...].astype(F32)) * ta
         + jax.nn.sigmoid(gr_ref[...].astype(F32)) * tr)
    o_ref[...] = m.astype(o_ref.dtype)


def _merge(ya, yr, p, wa, wr, ga_col0, tm=1024, tn=1024):
    t, ka = ya.shape
    d = wa.shape[1]
    ga0 = ga_col0 // tn
    gr0 = (ga_col0 + d) // tn
    return pl.pallas_call(
        _merge_kernel,
        grid=(t // tm, d // tn),
        in_specs=[pl.BlockSpec((tm, ka), lambda i, j: (i, 0)),
                  pl.BlockSpec((tm, ka), lambda i, j: (i, 0)),
                  pl.BlockSpec((tm, tn), lambda i, j: (i, ga0 + j)),
                  pl.BlockSpec((tm, tn), lambda i, j: (i, gr0 + j)),
                  pl.BlockSpec((ka, tn), lambda i, j: (0, j)),
                  pl.BlockSpec((ka, tn), lambda i, j: (0, j))],
        out_specs=pl.BlockSpec((tm, tn), lambda i, j: (i, j)),
        out_shape=jax.ShapeDtypeStruct((t, d), BF16),
        compiler_params=_params(("parallel", "arbitrary")),
        name="merge",
    )(ya, yr, p, p, wa, wr)


def _oproj_kernel(m_ref, x_ref, gt_ref, w_ref, o_ref):
    y = jnp.dot(m_ref[...], w_ref[...].astype(BF16), preferred_element_type=F32)
    o_ref[...] = x_ref[...] + gt_ref[...] * y


def _o_proj(m, x2, mod4, w, rows_per_batch, tm=1024, tn=1024):
    t, d = x2.shape
    tpb = rows_per_batch // tm
    return pl.pallas_call(
        _oproj_kernel,
        grid=(t // tm, d // tn),
        in_specs=[pl.BlockSpec((tm, d), lambda i, j: (i, 0)),
                  pl.BlockSpec((tm, tn), lambda i, j: (i, j)),
                  pl.BlockSpec((None, None, 1, tn), lambda i, j: (i // tpb, 2, 0, j)),
                  pl.BlockSpec((d, tn), lambda i, j: (0, j))],
        out_specs=pl.BlockSpec((tm, tn), lambda i, j: (i, j)),
        out_shape=jax.ShapeDtypeStruct((t, d), F32),
        compiler_params=_params(("parallel", "arbitrary")),
        name="o_proj",
    )(m, x2, mod4, w)


def _ffn_up_kernel(h_ref, g_ref, sh_ref, sc_ref, wg_ref, wu_ref, o_ref, u_ref):
    @pl.when(pl.program_id(1) == 0)
    def _():
        y = _rms(h_ref[...]) * g_ref[...]
        u_ref[...] = (y * (1.0 + sc_ref[...]) + sh_ref[...]).astype(BF16)

    u = u_ref[...]
    g = jnp.dot(u, wg_ref[...].astype(BF16), preferred_element_type=F32)
    up = jnp.dot(u, wu_ref[...].astype(BF16), preferred_element_type=F32)
    o_ref[...] = (g * jax.nn.sigmoid(g) * up).astype(o_ref.dtype)


def _ffn_up(h1, g, mod4, wg, wu, rows_per_batch, tm=1024, tf=512):
    t, d = h1.shape
    f = wg.shape[1]
    tpb = rows_per_batch // tm
    return pl.pallas_call(
        _ffn_up_kernel,
        grid=(t // tm, f // tf),
        in_specs=[pl.BlockSpec((tm, d), lambda i, j: (i, 0)),
                  pl.BlockSpec((1, d), lambda i, j: (0, 0)),
                  pl.BlockSpec((None, None, 1, d), lambda i, j: (i // tpb, 3, 0, 0)),
                  pl.BlockSpec((None, None, 1, d), lambda i, j: (i // tpb, 4, 0, 0)),
                  pl.BlockSpec((d, tf), lambda i, j: (0, j)),
                  pl.BlockSpec((d, tf), lambda i, j: (0, j))],
        out_specs=pl.BlockSpec((tm, tf), lambda i, j: (i, j)),
        out_shape=jax.ShapeDtypeStruct((t, f), BF16),
        scratch_shapes=[pltpu.VMEM((tm, d), BF16)],
        compiler_params=_params(("parallel", "arbitrary")),
        name="ffn_up",
    )(h1, g, mod4, mod4, wg, wu)


def _ffn_down_kernel(a_ref, h_ref, gt_ref, fg_ref, w_ref, o_ref, acc_ref, *, final):
    kk = pl.program_id(1)

    @pl.when(kk == 0)
    def _():
        acc_ref[...] = jnp.zeros(acc_ref.shape, F32)

    acc_ref[...] += jnp.dot(a_ref[...], w_ref[...].astype(BF16), preferred_element_type=F32)

    @pl.when(kk == pl.num_programs(1) - 1)
    def _():
        h2 = h_ref[...] + gt_ref[...] * acc_ref[...]
        o_ref[...] = _rms(h2) * fg_ref[...] if final else h2


def _ffn_down(a, h1, mod4, fg, w, rows_per_batch, final, tm=1024, tk=512):
    t, d = h1.shape
    f = a.shape[1]
    tpb = rows_per_batch // tm
    return pl.pallas_call(
        functools.partial(_ffn_down_kernel, final=final),
        grid=(t // tm, f // tk),
        in_specs=[pl.BlockSpec((tm, tk), lambda i, k: (i, k)),
                  pl.BlockSpec((tm, d), lambda i, k: (i, 0)),
                  pl.BlockSpec((None, None, 1, d), lambda i, k: (i // tpb, 5, 0, 0)),
                  pl.BlockSpec((1, d), lambda i, k: (0, 0)),
                  pl.BlockSpec((tk, d), lambda i, k: (k, 0))],
        out_specs=pl.BlockSpec((tm, d), lambda i, k: (i, 0)),
        out_shape=jax.ShapeDtypeStruct((t, d), F32),
        scratch_shapes=[pltpu.VMEM((tm, d), F32)],
        compiler_params=_params(("parallel", "arbitrary")),
        name="ffn_down",
    )(a, h1, mod4, fg, w)


def kernel(x, c, w_mod, b_mod, norm1_g, w_in, lambda_q1, lambda_k1, lambda_q2, lambda_k2,
           subln_g, lb_logits, gnorm_g, w_att_out, w_rec_out, w_o, norm2_g,
           w_ffn_gate, w_ffn_up, w_ffn_down, final_g):
    nb, s, d = x.shape
    depth = w_mod.shape[0]
    h = x.reshape(nb * s, d)
    for l in range(depth):
        lam_init = 0.8 - 0.6 * math.exp(-0.3 * l)
        mod4 = _mod(c, w_mod[l], b_mod[l]).reshape(nb, 6, 1, d)
        p = _in_proj(h, norm1_g[l].reshape(1, d), mod4, w_in[l], s)
        lam4 = jnp.stack([lambda_q1[l], lambda_k1[l], lambda_q2[l], lambda_k2[l]])
        ya = _attention(p, lam4, subln_g[l].reshape(HEAD_W, 1), nb, s, lam_init)
        yr = _hgrn2(p, lb_logits, gnorm_g[l].reshape(1, REC_HEADS * HEAD_W), nb, s, l)
        ga_col0 = 3 * ATT_HEADS * HEAD_W + 4 * REC_HEADS * HEAD_W
        m = _merge(ya, yr, p, w_att_out[l], w_rec_out[l], ga_col0)
        h1 = _o_proj(m, h, mod4, w_o[l], s)
        a = _ffn_up(h1, norm2_g[l].reshape(1, d), mod4, w_ffn_gate[l], w_ffn_up[l], s)
        h = _ffn_down(a, h1, mod4, final_g.reshape(1, d), w_ffn_down[l], s, l == depth - 1)
    return h.reshape(nb, s, d)
```

```python
import functools
import math

import jax
import jax.numpy as jnp
from jax import lax
from jax.experimental import pallas as pl
from jax.experimental.pallas import tpu as pltpu

F32 = jnp.float32
BF16 = jnp.bfloat16

NORM_EPS = 1e-6
ATT_HEADS = 8
ATT_HALF_DIM = 64
HEAD_W = 128
REC_HEADS = 8
MASK_CHUNK = 64
NEG_BIG = -1e30
ONES_ROWS = 16

VMEM_LIMIT = 60 * 1024 * 1024


def _params(sem, vmem=VMEM_LIMIT):
    return pltpu.CompilerParams(dimension_semantics=sem, vmem_limit_bytes=vmem)


def _rms(x):
    return x * lax.rsqrt(jnp.mean(x * x, axis=-1, keepdims=True) + NORM_EPS)


def _mod_kernel(ct_ref, w_ref, b_ref, o_ref):
    ct = ct_ref[...]
    cond = ct * jax.nn.sigmoid(ct)
    w = w_ref[...]
    for b in range(ct.shape[1]):
        o_ref[b:b + 1, :] = jnp.sum(w * cond[:, b:b + 1], axis=0, keepdims=True) + b_ref[...]


def _mod(c, w_mod, b_mod, tn=1024):
    nb, d = c.shape
    n = w_mod.shape[1]
    return pl.pallas_call(
        _mod_kernel,
        grid=(n // tn,),
        in_specs=[pl.BlockSpec((d, nb), lambda j: (0, 0)),
                  pl.BlockSpec((d, tn), lambda j: (0, j)),
                  pl.BlockSpec((1, tn), lambda j: (0, j))],
        out_specs=pl.BlockSpec((nb, tn), lambda j: (0, j)),
        out_shape=jax.ShapeDtypeStruct((nb, n), F32),
        compiler_params=_params(("arbitrary",)),
        name="mod",
    )(c.T, w_mod, b_mod.reshape(1, n))


def _inproj_kernel(x_ref, g_ref, sh_ref, sc_ref, w_ref, o_ref, u_ref):
    @pl.when(pl.program_id(1) == 0)
    def _():
        y = _rms(x_ref[...]) * g_ref[...]
        u_ref[...] = (y * (1.0 + sc_ref[...]) + sh_ref[...]).astype(BF16)

    o_ref[...] = jnp.dot(u_ref[...], w_ref[...].astype(BF16),
                         preferred_element_type=F32).astype(o_ref.dtype)


def _in_proj(x2, g, mod4, w, rows_per_batch, tm=1024, tn=1024):
    t, d = x2.shape
    n = w.shape[1]
    tpb = rows_per_batch // tm
    return pl.pallas_call(
        _inproj_kernel,
        grid=(t // tm, n // tn),
        in_specs=[pl.BlockSpec((tm, d), lambda i, j: (i, 0)),
                  pl.BlockSpec((1, d), lambda i, j: (0, 0)),
                  pl.BlockSpec((None, None, 1, d), lambda i, j: (i // tpb, 0, 0, 0)),
                  pl.BlockSpec((None, None, 1, d), lambda i, j: (i // tpb, 1, 0, 0)),
                  pl.BlockSpec((d, tn), lambda i, j: (0, j))],
        out_specs=pl.BlockSpec((tm, tn), lambda i, j: (i, j)),
        out_shape=jax.ShapeDtypeStruct((t, n), BF16),
        scratch_shapes=[pltpu.VMEM((tm, d), BF16)],
        compiler_params=_params(("parallel", "arbitrary")),
        name="in_proj",
    )(x2, g, mod4, mod4, w)


def _col_reduce(x, op, groups=8):
    k, n = x.shape
    if k % (groups * 8) == 0:
        x = op(x.reshape(groups, k // groups, n), axis=0)
    return op(x, axis=0, keepdims=True)


def _attn_kernel(lam_ref, sg_ref, q_ref, k_ref, v_ref, o_ref, vt_ref, acc_ref, m_ref,
                 sa_ref, sb_ref, *, tq, sq, tk, lead, lam_init):
    qi = pl.program_id(2)
    nkb = vt_ref.shape[0]
    nc = tq // sq

    @pl.when(qi == 0)
    def _():
        for j in range(nkb):
            vt_ref[j, 0:HEAD_W, :] = v_ref[j * tk:(j + 1) * tk, :].astype(F32).T.astype(BF16)
            vt_ref[j, HEAD_W:, :] = jnp.ones((vt_ref.shape[1] - HEAD_W, tk), BF16)

    q = (q_ref[...].astype(F32) * (ATT_HALF_DIM ** -0.5 * math.log2(math.e))).astype(BF16)
    lane = lax.broadcasted_iota(jnp.int32, q.shape, 1)
    zero = jnp.zeros_like(q)
    qm = (jnp.where(lane < ATT_HALF_DIM, q, zero), jnp.where(lane >= ATT_HALF_DIM, q, zero))

    m_ref[...] = jnp.full(m_ref.shape, NEG_BIG, F32)
    acc_ref[...] = jnp.zeros(acc_ref.shape, F32)

    def scores(c, mi, k, vt, mask):
        s = lax.dot_general(k, qm[mi][c * sq:(c + 1) * sq, :], (((1,), (1,)), ((), ())),
                            preferred_element_type=F32)
        return s if mask is None else jnp.where(mask, s, NEG_BIG)

    def update(c, mi, k, vt, mask, s):
        cs = slice(c * sq, (c + 1) * sq)
        m_old = m_ref[mi, :, cs]
        m_new = jnp.maximum(m_old, _col_reduce(s, jnp.max))
        p = jnp.exp2(s - m_new)
        alpha = jnp.exp2(m_old - m_new)
        acc_ref[mi, :, cs] = alpha * acc_ref[mi, :, cs] + jnp.dot(
            vt, p.astype(BF16), preferred_element_type=F32)
        m_ref[mi, :, cs] = m_new

    assert tq == 2 * tk
    chains = [(c, mi) for c in range(nc) for mi in range(2)]

    def stage(j, src, dst):
        vt = vt_ref[j]
        if dst is not None:
            kn = k_ref[pl.ds(pl.multiple_of((j + 1) * tk, tk), tk), :]
        for idx in range(len(chains) + lead):
            if dst is not None and idx < len(chains):
                dst[idx] = scores(*chains[idx], kn, None, None)
            if idx >= lead:
                c, mi = chains[idx - lead]
                update(c, mi, None, vt, None, src[idx - lead])

    @pl.when(qi > 0)
    def _():
        k0 = k_ref[0:tk, :]
        for idx, (c, mi) in enumerate(chains):
            sa_ref[idx] = scores(c, mi, k0, None, None)

        def pair(pp, carry):
            stage(2 * pp, sa_ref, sb_ref)
            stage(2 * pp + 1, sb_ref, sa_ref)
            return carry

        lax.fori_loop(0, qi - 1, pair, 0)
        stage(2 * qi - 2, sa_ref, sb_ref)
        stage(2 * qi - 1, sb_ref, None)

    krow = lax.broadcasted_iota(jnp.int32, (sq, sq), 0)
    qcol = lax.broadcasted_iota(jnp.int32, (sq, sq), 1)
    dmask = krow // MASK_CHUNK <= qcol // MASK_CHUNK
    vt_tile = jnp.concatenate([vt_ref[qi * 2], vt_ref[qi * 2 + 1]], axis=1)
    pend = []
    for c in reversed(range(nc)):
        nk = (c + 1) * sq
        k = k_ref[pl.ds(pl.multiple_of(qi * tq, tq), nk), :]
        for mi in range(2):
            s = scores(c, mi, k, None, None)
            tail = jnp.where(dmask, s[c * sq:, :], NEG_BIG)
            s = tail if c == 0 else jnp.concatenate([s[:c * sq, :], tail], axis=0)
            pend.append((c, mi, None, vt_tile[:, :nk], None, s))
    for it in pend:
        update(*it)

    lp = lam_ref[...]
    lam = (jnp.exp(jnp.sum(lp[0:1] * lp[1:2], axis=-1, keepdims=True))
           - jnp.exp(jnp.sum(lp[2:3] * lp[3:4], axis=-1, keepdims=True)) + lam_init)
    o = (acc_ref[0, 0:HEAD_W, :] / acc_ref[0, HEAD_W:HEAD_W + 1, :]
         - lam * (acc_ref[1, 0:HEAD_W, :] / acc_ref[1, HEAD_W:HEAD_W + 1, :]))
    ms = jnp.mean(o * o, axis=0, keepdims=True)
    y = o * lax.rsqrt(ms + NORM_EPS) * sg_ref[...] * (1.0 - lam_init)
    o_ref[...] = y.T.astype(o_ref.dtype)


def _attention(p, lam4, sg_col, nb, s, lam_init, tq=1024, sq=256, tk=512, lead=0):
    h = ATT_HEADS
    nq = s // tq
    kern = functools.partial(_attn_kernel, tq=tq, sq=sq, tk=tk, lead=lead, lam_init=lam_init)
    nchains = 2 * tq // sq
    return pl.pallas_call(
        kern,
        grid=(nb, h, nq),
        in_specs=[pl.BlockSpec((4, ATT_HALF_DIM), lambda b, hh, i: (0, 0)),
                  pl.BlockSpec((HEAD_W, 1), lambda b, hh, i: (0, 0)),
                  pl.BlockSpec((tq, HEAD_W), lambda b, hh, i: (b * nq + i, hh)),
                  pl.BlockSpec((s, HEAD_W), lambda b, hh, i: (b, h + hh)),
                  pl.BlockSpec((s, HEAD_W), lambda b, hh, i: (b, 2 * h + hh))],
        out_specs=pl.BlockSpec((tq, HEAD_W), lambda b, hh, i: (b * nq + i, hh)),
        out_shape=jax.ShapeDtypeStruct((nb * s, h * HEAD_W), BF16),
        scratch_shapes=[pltpu.VMEM((s // tk, HEAD_W + ONES_ROWS, tk), BF16),
                        pltpu.VMEM((2, HEAD_W + ONES_ROWS, tq), F32),
                        pltpu.VMEM((2, 1, tq), F32),
                        pltpu.VMEM((nchains, tk, sq), F32),
                        pltpu.VMEM((nchains, tk, sq), F32)],
        compiler_params=_params(("parallel", "parallel", "arbitrary")),
        name="diff_attn",
    )(lam4, sg_col, p, p, p)


def _group_row(a, s, grp=8):
    n, w = a.shape
    a3 = a.reshape(n // grp, grp, w)
    return jnp.broadcast_to(a3[:, s:s + 1, :], (n // grp, grp, w)).reshape(n, w)


def _hgrn2_kernel(lbl_ref, gn_ref, q_ref, f_ref, i_ref, g_ref, o_ref, st_ref, *, ch, layer):
    @pl.when(pl.program_id(1) == 0)
    def _():
        st_ref[...] = jnp.zeros(st_ref.shape, F32)

    w = HEAD_W
    lbl = lbl_ref[...]
    mx = jnp.max(lbl, axis=0, keepdims=True)
    e = jnp.exp(lbl - mx)
    lb = (jnp.sum(e[0:layer + 1], axis=0, keepdims=True)
          / jnp.sum(e, axis=0, keepdims=True))

    nt = (((1,), (1,)), ((), ()))
    row = lax.broadcasted_iota(jnp.int32, (ch, w), 0)
    arow = lax.broadcasted_iota(jnp.int32, (ch, ch), 0)
    acol = lax.broadcasted_iota(jnp.int32, (ch, ch), 1)
    tril = (arow >= acol).astype(BF16)
    on_diag = arow == acol
    levels = []
    hs = ch // 2
    while hs >= 1:
        upper = (row % (2 * hs)) >= hs
        levels.append((hs, upper, jnp.where(upper, 1.0, -1.0),
                       (arow // (2 * hs) == acol // (2 * hs))
                       & ((arow % (2 * hs)) >= hs) & ((acol % (2 * hs)) < hs)))
        hs //= 2

    def midpoint_row(b, hs):
        if hs >= 8:
            return jnp.concatenate(
                [jnp.broadcast_to(b[blk * 2 * hs + hs - 1:blk * 2 * hs + hs, :], (2 * hs, w))
                 for blk in range(ch // (2 * hs))], axis=0)
        if hs == 4:
            return _group_row(b, 3)
        return jnp.where((row % 8) < 4, _group_row(b, 1), _group_row(b, 5))

    chunk_vals = {}

    def chunk_prep(c):
        if c not in chunk_vals:
            rs = slice(c * ch, (c + 1) * ch)
            f_all = lb + (1.0 - lb) * jax.nn.sigmoid(f_ref[rs, :].astype(F32))
            g2 = jnp.log2(f_all)
            g2_hi = g2.astype(BF16)
            g2_lo = (g2 - g2_hi.astype(F32)).astype(BF16)
            b_all = (jnp.dot(tril, g2_hi, preferred_element_type=F32)
                     + jnp.dot(tril, g2_lo, preferred_element_type=F32))
            chunk_vals[c] = (f_all, b_all)
        return chunk_vals[c]

    def operands(c, h):
        rs, cs = slice(c * ch, (c + 1) * ch), slice(h * w, (h + 1) * w)
        f_all, b_all = chunk_prep(c)
        b, f = b_all[:, cs], f_all[:, cs]
        kk = 1.0 - f
        q = q_ref[rs, cs].astype(F32)
        blast = b[ch - 1:ch, :]
        ys = []
        for hs, upper, sign, _ in levels:
            if hs == 1:
                y = jnp.where(upper, q * f, kk)
            else:
                y = jnp.where(upper, q, kk) * jnp.exp2((b - midpoint_row(b, hs)) * sign)
            ys.append(y.astype(BF16))
        return dict(rs=rs, cs=cs, h=h, ys=ys,
                    qin=(q * jnp.exp2(b)).astype(BF16),
                    kout=(kk * jnp.exp2(blast - b)).astype(BF16),
                    keep=jnp.exp2(blast),
                    dsum=jnp.sum(q * kk, axis=-1, keepdims=True))

    def level_matmuls(x):
        st = st_ref[x["h"]]
        x["st"] = st
        x["o"] = lax.dot_general(x["qin"], st.astype(BF16), nt, preferred_element_type=F32)
        x["aa"] = [lax.dot_general(y, y, nt, preferred_element_type=F32) for y in x["ys"]]

    def finish(x):
        rs, cs = x["rs"], x["cs"]
        v = i_ref[rs, cs]
        att = jnp.where(on_diag, x["dsum"], 0.0)
        for (_, _, _, amask), a in zip(levels, x["aa"]):
            att = jnp.where(amask, a, att)
        o = x["o"] + jnp.dot(att.astype(BF16), v, preferred_element_type=F32)
        st_ref[x["h"]] = x["keep"] * x["st"] + lax.dot_general(
            v, x["kout"], (((0,), (0,)), ((), ())), preferred_element_type=F32)
        y = _rms(o) * gn_ref[:, cs] * jax.nn.sigmoid(g_ref[rs, cs].astype(F32))
        o_ref[rs, cs] = y.astype(o_ref.dtype)

    items = [(c, h) for c in range(q_ref.shape[0] // ch) for h in range(REC_HEADS)]
    staged = []
    d1, d2 = 1, 3
    for i in range(len(items) + d2):
        if i < len(items):
            staged.append(operands(*items[i]))
        if d1 <= i < len(items) + d1:
            level_matmuls(staged[i - d1])
        if i >= d2:
            finish(staged[i - d2])


def _hgrn2(p, lb_logits, gn, nb, s, layer, rb=256, ch=128):
    hw = REC_HEADS * HEAD_W
    nr = s // rb
    c0 = 3 * ATT_HEADS * HEAD_W // hw
    kern = functools.partial(_hgrn2_kernel, ch=ch, layer=layer)
    spec = lambda off: pl.BlockSpec((rb, hw), lambda b, i: (b * nr + i, c0 + off))
    return pl.pallas_call(
        kern,
        grid=(nb, nr),
        in_specs=[pl.BlockSpec(lb_logits.shape, lambda b, i: (0, 0)),
                  pl.BlockSpec((1, hw), lambda b, i: (0, 0)),
                  spec(0), spec(1), spec(2), spec(3)],
        out_specs=pl.BlockSpec((rb, hw), lambda b, i: (b * nr + i, 0)),
        out_shape=jax.ShapeDtypeStruct((nb * s, hw), BF16),
        scratch_shapes=[pltpu.VMEM((REC_HEADS, HEAD_W, HEAD_W), F32)],
        compiler_params=_params(("parallel", "arbitrary")),
        name="hgrn2",
    )(lb_logits, gn, p, p, p, p)


def _merge_kernel(ya_ref, yr_ref, ga_ref, gr_ref, wa_ref, wr_ref, o_ref):
    ta = jnp.dot(ya_ref[...], wa_ref[...].astype(BF16), preferred_element_type=F32)
    tr = jnp.dot(yr_ref[...], wr_ref[...].astype(BF16), preferred_element_type=F32)
    m = (jax.nn.sigmoid(ga_ref[...].astype(F32)) * ta
         + jax.nn.sigmoid(gr_ref[...].astype(F32)) * tr)
    o_ref[...] = m.astype(o_ref.dtype)


def _merge(ya, yr, p, wa, wr, ga_col0, tm=1024, tn=1024):
    t, ka = ya.shape
    d = wa.shape[1]
    ga0 = ga_col0 // tn
    gr0 = (ga_col0 + d) // tn
    return pl.pallas_call(
        _merge_kernel,
        grid=(t // tm, d // tn),
        in_specs=[pl.BlockSpec((tm, ka), lambda i, j: (i, 0)),
                  pl.BlockSpec((tm, ka), lambda i, j: (i, 0)),
                  pl.BlockSpec((tm, tn), lambda i, j: (i, ga0 + j)),
                  pl.BlockSpec((tm, tn), lambda i, j: (i, gr0 + j)),
                  pl.BlockSpec((ka, tn), lambda i, j: (0, j)),
                  pl.BlockSpec((ka, tn), lambda i, j: (0, j))],
        out_specs=pl.BlockSpec((tm, tn), lambda i, j: (i, j)),
        out_shape=jax.ShapeDtypeStruct((t, d), BF16),
        compiler_params=_params(("parallel", "arbitrary")),
        name="merge",
    )(ya, yr, p, p, wa, wr)


def _oproj_kernel(m_ref, x_ref, gt_ref, w_ref, o_ref):
    y = jnp.dot(m_ref[...], w_ref[...].astype(BF16), preferred_element_type=F32)
    o_ref[...] = x_ref[...] + gt_ref[...] * y


def _o_proj(m, x2, mod4, w, rows_per_batch, tm=1024, tn=1024):
    t, d = x2.shape
    tpb = rows_per_batch // tm
    return pl.pallas_call(
        _oproj_kernel,
        grid=(t // tm, d // tn),
        in_specs=[pl.BlockSpec((tm, d), lambda i, j: (i, 0)),
                  pl.BlockSpec((tm, tn), lambda i, j: (i, j)),
                  pl.BlockSpec((None, None, 1, tn), lambda i, j: (i // tpb, 2, 0, j)),
                  pl.BlockSpec((d, tn), lambda i, j: (0, j))],
        out_specs=pl.BlockSpec((tm, tn), lambda i, j: (i, j)),
        out_shape=jax.ShapeDtypeStruct((t, d), F32),
        compiler_params=_params(("parallel", "arbitrary")),
        name="o_proj",
    )(m, x2, mod4, w)


def _ffn_up_kernel(h_ref, g_ref, sh_ref, sc_ref, wg_ref, wu_ref, o_ref, u_ref):
    @pl.when(pl.program_id(1) == 0)
    def _():
        y = _rms(h_ref[...]) * g_ref[...]
        u_ref[...] = (y * (1.0 + sc_ref[...]) + sh_ref[...]).astype(BF16)

    u = u_ref[...]
    g = jnp.dot(u, wg_ref[...].astype(BF16), preferred_element_type=F32)
    up = jnp.dot(u, wu_ref[...].astype(BF16), preferred_element_type=F32)
    o_ref[...] = (g * jax.nn.sigmoid(g) * up).astype(o_ref.dtype)


def _ffn_up(h1, g, mod4, wg, wu, rows_per_batch, tm=1024, tf=512):
    t, d = h1.shape
    f = wg.shape[1]
    tpb = rows_per_batch // tm
    return pl.pallas_call(
        _ffn_up_kernel,
        grid=(t // tm, f // tf),
        in_specs=[pl.BlockSpec((tm, d), lambda i, j: (i, 0)),
                  pl.BlockSpec((1, d), lambda i, j: (0, 0)),
                  pl.BlockSpec((None, None, 1, d), lambda i, j: (i // tpb, 3, 0, 0)),
                  pl.BlockSpec((None, None, 1, d), lambda i, j: (i // tpb, 4, 0, 0)),
                  pl.BlockSpec((d, tf), lambda i, j: (0, j)),
                  pl.BlockSpec((d, tf), lambda i, j: (0, j))],
        out_specs=pl.BlockSpec((tm, tf), lambda i, j: (i, j)),
        out_shape=jax.ShapeDtypeStruct((t, f), BF16),
        scratch_shapes=[pltpu.VMEM((tm, d), BF16)],
        compiler_params=_params(("parallel", "arbitrary")),
        name="ffn_up",
    )(h1, g, mod4, mod4, wg, wu)


def _ffn_down_kernel(a_ref, h_ref, gt_ref, fg_ref, w_ref, o_ref, acc_ref, *, final):
    kk = pl.program_id(1)

    @pl.when(kk == 0)
    def _():
        acc_ref[...] = jnp.zeros(acc_ref.shape, F32)

    acc_ref[...] += jnp.dot(a_ref[...], w_ref[...].astype(BF16), preferred_element_type=F32)

    @pl.when(kk == pl.num_programs(1) - 1)
    def _():
        h2 = h_ref[...] + gt_ref[...] * acc_ref[...]
        o_ref[...] = _rms(h2) * fg_ref[...] if final else h2


def _ffn_down(a, h1, mod4, fg, w, rows_per_batch, final, tm=1024, tk=512):
    t, d = h1.shape
    f = a.shape[1]
    tpb = rows_per_batch // tm
    return pl.pallas_call(
        functools.partial(_ffn_down_kernel, final=final),
        grid=(t // tm, f // tk),
        in_specs=[pl.BlockSpec((tm, tk), lambda i, k: (i, k)),
                  pl.BlockSpec((tm, d), lambda i, k: (i, 0)),
                  pl.BlockSpec((None, None, 1, d), lambda i, k: (i // tpb, 5, 0, 0)),
                  pl.BlockSpec((1, d), lambda i, k: (0, 0)),
                  pl.BlockSpec((tk, d), lambda i, k: (k, 0))],
        out_specs=pl.BlockSpec((tm, d), lambda i, k: (i, 0)),
        out_shape=jax.ShapeDtypeStruct((t, d), F32),
        scratch_shapes=[pltpu.VMEM((tm, d), F32)],
        compiler_params=_params(("parallel", "arbitrary")),
        name="ffn_down",
    )(a, h1, mod4, fg, w)


def kernel(x, c, w_mod, b_mod, norm1_g, w_in, lambda_q1, lambda_k1, lambda_q2, lambda_k2,
           subln_g, lb_logits, gnorm_g, w_att_out, w_rec_out, w_o, norm2_g,
           w_ffn_gate, w_ffn_up, w_ffn_down, final_g):
    nb, s, d = x.shape
    depth = w_mod.shape[0]
    h = x.reshape(nb * s, d)
    for l in range(depth):
        lam_init = 0.8 - 0.6 * math.exp(-0.3 * l)
        mod4 = _mod(c, w_mod[l], b_mod[l]).reshape(nb, 6, 1, d)
        p = _in_proj(h, norm1_g[l].reshape(1, d), mod4, w_in[l], s)
        lam4 = jnp.stack([lambda_q1[l], lambda_k1[l], lambda_q2[l], lambda_k2[l]])
        ya = _attention(p, lam4, subln_g[l].reshape(HEAD_W, 1), nb, s, lam_init)
        yr = _hgrn2(p, lb_logits, gnorm_g[l].reshape(1, REC_HEADS * HEAD_W), nb, s, l)
        ga_col0 = 3 * ATT_HEADS * HEAD_W + 4 * REC_HEADS * HEAD_W
        m = _merge(ya, yr, p, w_att_out[l], w_rec_out[l], ga_col0)
        h1 = _o_proj(m, h, mod4, w_o[l], s)
        a = _ffn_up(h1, norm2_g[l].reshape(1, d), mod4, w_ffn_gate[l], w_ffn_up[l], s)
        h = _ffn_down(a, h1, mod4, final_g.reshape(1, d), w_ffn_down[l], s, l == depth - 1)
    return h.reshape(nb, s, d)
```

```python
import functools
import math

import jax
import jax.numpy as jnp
from jax import lax
from jax.experimental import pallas as pl
from jax.experimental.pallas import tpu as pltpu

F32 = jnp.float32
BF16 = jnp.bfloat16

NORM_EPS = 1e-6
ATT_HEADS = 8
ATT_HALF_DIM = 64
HEAD_W = 128
REC_HEADS = 8
MASK_CHUNK = 64
NEG_BIG = -1e30
ONES_ROWS = 16

VMEM_LIMIT = 60 * 1024 * 1024


def _params(sem, vmem=VMEM_LIMIT):
    return pltpu.CompilerParams(dimension_semantics=sem, vmem_limit_bytes=vmem)


def _rms(x):
    return x * lax.rsqrt(jnp.mean(x * x, axis=-1, keepdims=True) + NORM_EPS)


def _adaln(x, g, sc, sh):
    return _rms(x) * (g * (1.0 + sc)) + sh


def _mod_kernel(ct_ref, w_ref, b_ref, o_ref):
    ct = ct_ref[...]
    cond = ct * jax.nn.sigmoid(ct)
    w = w_ref[...]
    for b in range(ct.shape[1]):
        o_ref[b:b + 1, :] = jnp.sum(w * cond[:, b:b + 1], axis=0, keepdims=True) + b_ref[...]


def _mod(c, w_mod, b_mod, tn=1024):
    nb, d = c.shape
    n = w_mod.shape[1]
    return pl.pallas_call(
        _mod_kernel,
        grid=(n // tn,),
        in_specs=[pl.BlockSpec((d, nb), lambda j: (0, 0)),
                  pl.BlockSpec((d, tn), lambda j: (0, j)),
                  pl.BlockSpec((1, tn), lambda j: (0, j))],
        out_specs=pl.BlockSpec((nb, tn), lambda j: (0, j)),
        out_shape=jax.ShapeDtypeStruct((nb, n), F32),
        compiler_params=_params(("arbitrary",)),
        name="mod",
    )(c.T, w_mod, b_mod.reshape(1, n))


def _inproj_kernel(x_ref, g_ref, sh_ref, sc_ref, w_ref, o_ref, u_ref):
    @pl.when(pl.program_id(1) == 0)
    def _():
        u_ref[...] = _adaln(x_ref[...], g_ref[...], sc_ref[...], sh_ref[...]).astype(BF16)

    o_ref[...] = jnp.dot(u_ref[...], w_ref[...].astype(BF16),
                         preferred_element_type=F32).astype(o_ref.dtype)


def _in_proj(x2, g, mod4, w, rows_per_batch, tm=1024, tn=1024):
    t, d = x2.shape
    n = w.shape[1]
    tpb = rows_per_batch // tm
    return pl.pallas_call(
        _inproj_kernel,
        grid=(t // tm, n // tn),
        in_specs=[pl.BlockSpec((tm, d), lambda i, j: (i, 0)),
                  pl.BlockSpec((1, d), lambda i, j: (0, 0)),
                  pl.BlockSpec((None, None, 1, d), lambda i, j: (i // tpb, 0, 0, 0)),
                  pl.BlockSpec((None, None, 1, d), lambda i, j: (i // tpb, 1, 0, 0)),
                  pl.BlockSpec((d, tn), lambda i, j: (0, j))],
        out_specs=pl.BlockSpec((tm, tn), lambda i, j: (i, j)),
        out_shape=jax.ShapeDtypeStruct((t, n), BF16),
        scratch_shapes=[pltpu.VMEM((tm, d), BF16)],
        compiler_params=_params(("parallel", "arbitrary")),
        name="in_proj",
    )(x2, g, mod4, mod4, w)


def _col_reduce(x, op, groups=8):
    k, n = x.shape
    if k % (groups * 8) == 0:
        x = op(x.reshape(groups, k // groups, n), axis=0)
    return op(x, axis=0, keepdims=True)


def _attn_kernel(lam_ref, sg_ref, q_ref, k_ref, v_ref, o_ref, vt_ref, acc_ref, m_ref,
                 sa_ref, sb_ref, *, tq, sq, tk, lead, lam_init):
    qi = pl.program_id(2)
    nkb = vt_ref.shape[0]
    nc = tq // sq

    @pl.when(qi == 0)
    def _():
        for j in range(nkb):
            vt_ref[j, 0:HEAD_W, :] = v_ref[j * tk:(j + 1) * tk, :].astype(F32).T.astype(BF16)
            vt_ref[j, HEAD_W:, :] = jnp.ones((vt_ref.shape[1] - HEAD_W, tk), BF16)

    q = (q_ref[...].astype(F32) * (ATT_HALF_DIM ** -0.5 * math.log2(math.e))).astype(BF16)
    lane = lax.broadcasted_iota(jnp.int32, q.shape, 1)
    zero = jnp.zeros_like(q)
    qm = (jnp.where(lane < ATT_HALF_DIM, q, zero), jnp.where(lane >= ATT_HALF_DIM, q, zero))

    m_ref[...] = jnp.full(m_ref.shape, NEG_BIG, F32)
    acc_ref[...] = jnp.zeros(acc_ref.shape, F32)

    def scores(c, mi, k, vt, mask):
        s = lax.dot_general(k, qm[mi][c * sq:(c + 1) * sq, :], (((1,), (1,)), ((), ())),
                            preferred_element_type=F32)
        return s if mask is None else jnp.where(mask, s, NEG_BIG)

    def update(c, mi, k, vt, mask, s):
        cs = slice(c * sq, (c + 1) * sq)
        m_old = m_ref[mi, :, cs]
        m_new = jnp.maximum(m_old, _col_reduce(s, jnp.max))
        p = jnp.exp2(s - m_new)
        alpha = jnp.exp2(m_old - m_new)
        acc_ref[mi, :, cs] = alpha * acc_ref[mi, :, cs] + jnp.dot(
            vt, p.astype(BF16), preferred_element_type=F32)
        m_ref[mi, :, cs] = m_new

    assert tq == 2 * tk
    chains = [(c, mi) for c in range(nc) for mi in range(2)]

    def stage(j, src, dst):
        vt = vt_ref[j]
        if dst is not None:
            kn = k_ref[pl.ds(pl.multiple_of((j + 1) * tk, tk), tk), :]
        for idx in range(len(chains) + lead):
            if dst is not None and idx < len(chains):
                dst[idx] = scores(*chains[idx], kn, None, None)
            if idx >= lead:
                c, mi = chains[idx - lead]
                update(c, mi, None, vt, None, src[idx - lead])

    @pl.when(qi > 0)
    def _():
        k0 = k_ref[0:tk, :]
        for idx, (c, mi) in enumerate(chains):
            sa_ref[idx] = scores(c, mi, k0, None, None)

        def pair(pp, carry):
            stage(2 * pp, sa_ref, sb_ref)
            stage(2 * pp + 1, sb_ref, sa_ref)
            return carry

        lax.fori_loop(0, qi - 1, pair, 0)
        stage(2 * qi - 2, sa_ref, sb_ref)
        stage(2 * qi - 1, sb_ref, None)

    krow = lax.broadcasted_iota(jnp.int32, (sq, sq), 0)
    qcol = lax.broadcasted_iota(jnp.int32, (sq, sq), 1)
    dmask = krow // MASK_CHUNK <= qcol // MASK_CHUNK
    vt_tile = jnp.concatenate([vt_ref[qi * 2], vt_ref[qi * 2 + 1]], axis=1)
    pend = []
    for c in reversed(range(nc)):
        nk = (c + 1) * sq
        k = k_ref[pl.ds(pl.multiple_of(qi * tq, tq), nk), :]
        for mi in range(2):
            s = scores(c, mi, k, None, None)
            tail = jnp.where(dmask, s[c * sq:, :], NEG_BIG)
            s = tail if c == 0 else jnp.concatenate([s[:c * sq, :], tail], axis=0)
            pend.append((c, mi, None, vt_tile[:, :nk], None, s))
    for it in pend:
        update(*it)

    lp = lam_ref[...]
    lam = (jnp.exp(jnp.sum(lp[0:1] * lp[1:2], axis=-1, keepdims=True))
           - jnp.exp(jnp.sum(lp[2:3] * lp[3:4], axis=-1, keepdims=True)) + lam_init)
    o = (acc_ref[0, 0:HEAD_W, :] / acc_ref[0, HEAD_W:HEAD_W + 1, :]
         - lam * (acc_ref[1, 0:HEAD_W, :] / acc_ref[1, HEAD_W:HEAD_W + 1, :]))
    ms = jnp.mean(o * o, axis=0, keepdims=True)
    y = o * lax.rsqrt(ms + NORM_EPS) * sg_ref[...] * (1.0 - lam_init)
    o_ref[...] = y.T.astype(o_ref.dtype)


def _attention(p, lam4, sg_col, nb, s, lam_init, tq=1024, sq=256, tk=512, lead=0):
    h = ATT_HEADS
    nq = s // tq
    kern = functools.partial(_attn_kernel, tq=tq, sq=sq, tk=tk, lead=lead, lam_init=lam_init)
    nchains = 2 * tq // sq
    return pl.pallas_call(
        kern,
        grid=(nb, h, nq),
        in_specs=[pl.BlockSpec((4, ATT_HALF_DIM), lambda b, hh, i: (0, 0)),
                  pl.BlockSpec((HEAD_W, 1), lambda b, hh, i: (0, 0)),
                  pl.BlockSpec((tq, HEAD_W), lambda b, hh, i: (b * nq + i, hh)),
                  pl.BlockSpec((s, HEAD_W), lambda b, hh, i: (b, h + hh)),
                  pl.BlockSpec((s, HEAD_W), lambda b, hh, i: (b, 2 * h + hh))],
        out_specs=pl.BlockSpec((tq, HEAD_W), lambda b, hh, i: (b * nq + i, hh)),
        out_shape=jax.ShapeDtypeStruct((nb * s, h * HEAD_W), BF16),
        scratch_shapes=[pltpu.VMEM((s // tk, HEAD_W + ONES_ROWS, tk), BF16),
                        pltpu.VMEM((2, HEAD_W + ONES_ROWS, tq), F32),
                        pltpu.VMEM((2, 1, tq), F32),
                        pltpu.VMEM((nchains, tk, sq), F32),
                        pltpu.VMEM((nchains, tk, sq), F32)],
        compiler_params=_params(("parallel", "parallel", "arbitrary")),
        name="diff_attn",
    )(lam4, sg_col, p, p, p)


def _group_row(a, s, grp=8):
    n, w = a.shape
    a3 = a.reshape(n // grp, grp, w)
    return jnp.broadcast_to(a3[:, s:s + 1, :], (n // grp, grp, w)).reshape(n, w)


def _hgrn2_kernel(lbl_ref, gn_ref, q_ref, f_ref, i_ref, g_ref, o_ref, st_ref, *, ch, layer):
    @pl.when(pl.program_id(1) == 0)
    def _():
        st_ref[...] = jnp.zeros(st_ref.shape, F32)

    w = HEAD_W
    lbl = lbl_ref[...]
    mx = jnp.max(lbl, axis=0, keepdims=True)
    e = jnp.exp(lbl - mx)
    lb = (jnp.sum(e[0:layer + 1], axis=0, keepdims=True)
          / jnp.sum(e, axis=0, keepdims=True))

    nt = (((1,), (1,)), ((), ()))
    row = lax.broadcasted_iota(jnp.int32, (ch, w), 0)
    arow = lax.broadcasted_iota(jnp.int32, (ch, ch), 0)
    acol = lax.broadcasted_iota(jnp.int32, (ch, ch), 1)
    tril = (arow >= acol).astype(BF16)
    on_diag = arow == acol
    levels = []
    hs = ch // 2
    while hs >= 1:
        upper = (row % (2 * hs)) >= hs
        levels.append((hs, upper, jnp.where(upper, 1.0, -1.0),
                       (arow // (2 * hs) == acol // (2 * hs))
                       & ((arow % (2 * hs)) >= hs) & ((acol % (2 * hs)) < hs)))
        hs //= 2

    def midpoint_row(b, hs):
        if hs >= 8:
            return jnp.concatenate(
                [jnp.broadcast_to(b[blk * 2 * hs + hs - 1:blk * 2 * hs + hs, :], (2 * hs, w))
                 for blk in range(ch // (2 * hs))], axis=0)
        if hs == 4:
            return _group_row(b, 3)
        return jnp.where((row % 8) < 4, _group_row(b, 1), _group_row(b, 5))

    chunk_vals = {}

    def chunk_prep(c):
        if c not in chunk_vals:
            rs = slice(c * ch, (c + 1) * ch)
            f_all = lb + (1.0 - lb) * jax.nn.sigmoid(f_ref[rs, :].astype(F32))
            g2 = jnp.log2(f_all)
            g2_hi = g2.astype(BF16)
            g2_lo = (g2 - g2_hi.astype(F32)).astype(BF16)
            b_all = (jnp.dot(tril, g2_hi, preferred_element_type=F32)
                     + jnp.dot(tril, g2_lo, preferred_element_type=F32))
            chunk_vals[c] = (f_all, b_all)
        return chunk_vals[c]

    def operands(c, h):
        rs, cs = slice(c * ch, (c + 1) * ch), slice(h * w, (h + 1) * w)
        f_all, b_all = chunk_prep(c)
        b, f = b_all[:, cs], f_all[:, cs]
        kk = 1.0 - f
        q = q_ref[rs, cs].astype(F32)
        blast = b[ch - 1:ch, :]
        ys = []
        for hs, upper, sign, _ in levels:
            if hs == 1:
                y = jnp.where(upper, q * f, kk)
            else:
                y = jnp.where(upper, q, kk) * jnp.exp2((b - midpoint_row(b, hs)) * sign)
            ys.append(y.astype(BF16))
        return dict(rs=rs, cs=cs, h=h, ys=ys,
                    qin=(q * jnp.exp2(b)).astype(BF16),
                    kout=(kk * jnp.exp2(blast - b)).astype(BF16),
                    keep=jnp.exp2(blast),
                    dsum=jnp.sum(q * kk, axis=-1, keepdims=True))

    def level_matmuls(x):
        st = st_ref[x["h"]]
        x["st"] = st
        x["o"] = lax.dot_general(x["qin"], st.astype(BF16), nt, preferred_element_type=F32)
        x["aa"] = [lax.dot_general(y, y, nt, preferred_element_type=F32) for y in x["ys"]]

    def finish(x):
        rs, cs = x["rs"], x["cs"]
        v = i_ref[rs, cs]
        att = jnp.where(on_diag, x["dsum"], 0.0)
        for (_, _, _, amask), a in zip(levels, x["aa"]):
            att = jnp.where(amask, a, att)
        o = x["o"] + jnp.dot(att.astype(BF16), v, preferred_element_type=F32)
        st_ref[x["h"]] = x["keep"] * x["st"] + lax.dot_general(
            v, x["kout"], (((0,), (0,)), ((), ())), preferred_element_type=F32)
        y = _rms(o) * gn_ref[:, cs] * jax.nn.sigmoid(g_ref[rs, cs].astype(F32))
        o_ref[rs, cs] = y.astype(o_ref.dtype)

    items = [(c, h) for c in range(q_ref.shape[0] // ch) for h in range(REC_HEADS)]
    staged = []
    d1, d2 = 1, 3
    for i in range(len(items) + d2):
        if i < len(items):
            staged.append(operands(*items[i]))
        if d1 <= i < len(items) + d1:
            level_matmuls(staged[i - d1])
        if i >= d2:
            finish(staged[i - d2])


def _hgrn2(p, lb_logits, gn, nb, s, layer, rb=256, ch=128):
    hw = REC_HEADS * HEAD_W
    nr = s // rb
    c0 = 3 * ATT_HEADS * HEAD_W // hw
    kern = functools.partial(_hgrn2_kernel, ch=ch, layer=layer)
    spec = lambda off: pl.BlockSpec((rb, hw), lambda b, i: (b * nr + i, c0 + off))
    return pl.pallas_call(
        kern,
        grid=(nb, nr),
        in_specs=[pl.BlockSpec(lb_logits.shape, lambda b, i: (0, 0)),
                  pl.BlockSpec((1, hw), lambda b, i: (0, 0)),
                  spec(0), spec(1), spec(2), spec(3)],
        out_specs=pl.BlockSpec((rb, hw), lambda b, i: (b * nr + i, 0)),
        out_shape=jax.ShapeDtypeStruct((nb * s, hw), BF16),
        scratch_shapes=[pltpu.VMEM((REC_HEADS, HEAD_W, HEAD_W), F32)],
        compiler_params=_params(("parallel", "arbitrary")),
        name="hgrn2",
    )(lb_logits, gn, p, p, p, p)


def _merge_kernel(ya_ref, yr_ref, ga_ref, gr_ref, wa_ref, wr_ref, o_ref, wab_ref, wrb_ref):
    @pl.when(pl.program_id(1) == 0)
    def _():
        wab_ref[...] = wa_ref[...].astype(BF16)
        wrb_ref[...] = wr_ref[...].astype(BF16)

    ta = jnp.dot(ya_ref[...], wab_ref[...], preferred_element_type=F32)
    tr = jnp.dot(yr_ref[...], wrb_ref[...], preferred_element_type=F32)
    m = (jax.nn.sigmoid(ga_ref[...].astype(F32)) * ta
         + jax.nn.sigmoid(gr_ref[...].astype(F32)) * tr)
    o_ref[...] = m.astype(o_ref.dtype)


def _merge(ya, yr, p, wa, wr, ga_col0, tm=1024, tn=1024):
    t, ka = ya.shape
    d = wa.shape[1]
    ga0 = ga_col0 // tn
    gr0 = (ga_col0 + d) // tn
    return pl.pallas_call(
        _merge_kernel,
        grid=(d // tn, t // tm),
        in_specs=[pl.BlockSpec((tm, ka), lambda j, i: (i, 0)),
                  pl.BlockSpec((tm, ka), lambda j, i: (i, 0)),
                  pl.BlockSpec((tm, tn), lambda j, i: (i, ga0 + j)),
                  pl.BlockSpec((tm, tn), lambda j, i: (i, gr0 + j)),
                  pl.BlockSpec((ka, tn), lambda j, i: (0, j)),
                  pl.BlockSpec((ka, tn), lambda j, i: (0, j))],
        out_specs=pl.BlockSpec((tm, tn), lambda j, i: (i, j)),
        out_shape=jax.ShapeDtypeStruct((t, d), BF16),
        scratch_shapes=[pltpu.VMEM((ka, tn), BF16), pltpu.VMEM((ka, tn), BF16)],
        compiler_params=_params(("parallel", "arbitrary")),
        name="merge",
    )(ya, yr, p, p, wa, wr)


def _oproj_kernel(m_ref, x_ref, gt_ref, w_ref, o_ref, wb_ref):
    @pl.when(pl.program_id(1) == 0)
    def _():
        wb_ref[...] = w_ref[...].astype(BF16)

    y = jnp.dot(m_ref[...], wb_ref[...], preferred_element_type=F32)
    o_ref[...] = x_ref[...] + gt_ref[...] * y


def _o_proj(m, x2, mod4, w, rows_per_batch, tm=1024, tn=1024):
    t, d = x2.shape
    tpb = rows_per_batch // tm
    return pl.pallas_call(
        _oproj_kernel,
        grid=(d // tn, t // tm),
        in_specs=[pl.BlockSpec((tm, d), lambda j, i: (i, 0)),
                  pl.BlockSpec((tm, tn), lambda j, i: (i, j)),
                  pl.BlockSpec((None, None, 1, tn), lambda j, i: (i // tpb, 2, 0, j)),
                  pl.BlockSpec((d, tn), lambda j, i: (0, j))],
        out_specs=pl.BlockSpec((tm, tn), lambda j, i: (i, j)),
        out_shape=jax.ShapeDtypeStruct((t, d), F32),
        scratch_shapes=[pltpu.VMEM((d, tn), BF16)],
        compiler_params=_params(("parallel", "arbitrary")),
        name="o_proj",
    )(m, x2, mod4, w)


def _ffn_up_kernel(h_ref, g_ref, sh_ref, sc_ref, wg_ref, wu_ref, o_ref, u_ref):
    @pl.when(pl.program_id(1) == 0)
    def _():
        u_ref[...] = _adaln(h_ref[...], g_ref[...], sc_ref[...], sh_ref[...]).astype(BF16)

    u = u_ref[...]
    g = jnp.dot(u, wg_ref[...].astype(BF16), preferred_element_type=F32)
    up = jnp.dot(u, wu_ref[...].astype(BF16), preferred_element_type=F32)
    o_ref[...] = (g * jax.nn.sigmoid(g) * up).astype(o_ref.dtype)


def _ffn_up(h1, g, mod4, wg, wu, rows_per_batch, tm=1024, tf=512):
    t, d = h1.shape
    f = wg.shape[1]
    tpb = rows_per_batch // tm
    return pl.pallas_call(
        _ffn_up_kernel,
        grid=(t // tm, f // tf),
        in_specs=[pl.BlockSpec((tm, d), lambda i, j: (i, 0)),
                  pl.BlockSpec((1, d), lambda i, j: (0, 0)),
                  pl.BlockSpec((None, None, 1, d), lambda i, j: (i // tpb, 3, 0, 0)),
                  pl.BlockSpec((None, None, 1, d), lambda i, j: (i // tpb, 4, 0, 0)),
                  pl.BlockSpec((d, tf), lambda i, j: (0, j)),
                  pl.BlockSpec((d, tf), lambda i, j: (0, j))],
        out_specs=pl.BlockSpec((tm, tf), lambda i, j: (i, j)),
        out_shape=jax.ShapeDtypeStruct((t, f), BF16),
        scratch_shapes=[pltpu.VMEM((tm, d), BF16)],
        compiler_params=_params(("parallel", "arbitrary")),
        name="ffn_up",
    )(h1, g, mod4, mod4, wg, wu)


def _ffn_down_kernel(a_ref, h_ref, gt_ref, fg_ref, w_ref, o_ref, acc_ref, *, final):
    kk = pl.program_id(1)

    @pl.when(kk == 0)
    def _():
        acc_ref[...] = jnp.zeros(acc_ref.shape, F32)

    acc_ref[...] += jnp.dot(a_ref[...], w_ref[...].astype(BF16), preferred_element_type=F32)

    @pl.when(kk == pl.num_programs(1) - 1)
    def _():
        h2 = h_ref[...] + gt_ref[...] * acc_ref[...]
        o_ref[...] = _rms(h2) * fg_ref[...] if final else h2


def _ffn_down(a, h1, mod4, fg, w, rows_per_batch, final, tm=1024, tk=512):
    t, d = h1.shape
    f = a.shape[1]
    tpb = rows_per_batch // tm
    return pl.pallas_call(
        functools.partial(_ffn_down_kernel, final=final),
        grid=(t // tm, f // tk),
        in_specs=[pl.BlockSpec((tm, tk), lambda i, k: (i, k)),
                  pl.BlockSpec((tm, d), lambda i, k: (i, 0)),
                  pl.BlockSpec((None, None, 1, d), lambda i, k: (i // tpb, 5, 0, 0)),
                  pl.BlockSpec((1, d), lambda i, k: (0, 0)),
                  pl.BlockSpec((tk, d), lambda i, k: (k, 0))],
        out_specs=pl.BlockSpec((tm, d), lambda i, k: (i, 0)),
        out_shape=jax.ShapeDtypeStruct((t, d), F32),
        scratch_shapes=[pltpu.VMEM((tm, d), F32)],
        compiler_params=_params(("parallel", "arbitrary")),
        name="ffn_down",
    )(a, h1, mod4, fg, w)


def kernel(x, c, w_mod, b_mod, norm1_g, w_in, lambda_q1, lambda_k1, lambda_q2, lambda_k2,
           subln_g, lb_logits, gnorm_g, w_att_out, w_rec_out, w_o, norm2_g,
           w_ffn_gate, w_ffn_up, w_ffn_down, final_g):
    nb, s, d = x.shape
    depth = w_mod.shape[0]
    h = x.reshape(nb * s, d)
    for l in range(depth):
        lam_init = 0.8 - 0.6 * math.exp(-0.3 * l)
        mod4 = _mod(c, w_mod[l], b_mod[l]).reshape(nb, 6, 1, d)
        p = _in_proj(h, norm1_g[l].reshape(1, d), mod4, w_in[l], s)
        lam4 = jnp.stack([lambda_q1[l], lambda_k1[l], lambda_q2[l], lambda_k2[l]])
        ya = _attention(p, lam4, subln_g[l].reshape(HEAD_W, 1), nb, s, lam_init)
        yr = _hgrn2(p, lb_logits, gnorm_g[l].reshape(1, REC_HEADS * HEAD_W), nb, s, l)
        ga_col0 = 3 * ATT_HEADS * HEAD_W + 4 * REC_HEADS * HEAD_W
        m = _merge(ya, yr, p, w_att_out[l], w_rec_out[l], ga_col0)
        h1 = _o_proj(m, h, mod4, w_o[l], s)
        a = _ffn_up(h1, norm2_g[l].reshape(1, d), mod4, w_ffn_gate[l], w_ffn_up[l], s)
        h = _ffn_down(a, h1, mod4, final_g.reshape(1, d), w_ffn_down[l], s, l == depth - 1)
    return h.reshape(nb, s, d)
```

```python
import functools
import math

import jax
import jax.numpy as jnp
from jax import lax
from jax.experimental import pallas as pl
from jax.experimental.pallas import tpu as pltpu

F32 = jnp.float32
BF16 = jnp.bfloat16

NORM_EPS = 1e-6
ATT_HEADS = 8
ATT_HALF_DIM = 64
HEAD_W = 128
REC_HEADS = 8
MASK_CHUNK = 64
NEG_BIG = -1e30
ONES_ROWS = 16

VMEM_LIMIT = 60 * 1024 * 1024


def _params(sem, vmem=VMEM_LIMIT):
    return pltpu.CompilerParams(dimension_semantics=sem, vmem_limit_bytes=vmem)


def _rms(x):
    return x * lax.rsqrt(jnp.mean(x * x, axis=-1, keepdims=True) + NORM_EPS)


def _adaln(x, g, sc, sh):
    return _rms(x) * (g * (1.0 + sc)) + sh


def _mod_kernel(ct_ref, w_ref, b_ref, o_ref):
    ct = ct_ref[...]
    cond = ct * jax.nn.sigmoid(ct)
    w = w_ref[...]
    for b in range(ct.shape[1]):
        o_ref[b:b + 1, :] = jnp.sum(w * cond[:, b:b + 1], axis=0, keepdims=True) + b_ref[...]


def _mod(c, w_mod, b_mod, tn=1024):
    nb, d = c.shape
    n = w_mod.shape[1]
    return pl.pallas_call(
        _mod_kernel,
        grid=(n // tn,),
        in_specs=[pl.BlockSpec((d, nb), lambda j: (0, 0)),
                  pl.BlockSpec((d, tn), lambda j: (0, j)),
                  pl.BlockSpec((1, tn), lambda j: (0, j))],
        out_specs=pl.BlockSpec((nb, tn), lambda j: (0, j)),
        out_shape=jax.ShapeDtypeStruct((nb, n), F32),
        compiler_params=_params(("arbitrary",)),
        name="mod",
    )(c.T, w_mod, b_mod.reshape(1, n))


def _inproj_kernel(x_ref, g_ref, sh_ref, sc_ref, w_ref, o_ref, u_ref):
    @pl.when(pl.program_id(1) == 0)
    def _():
        u_ref[...] = _adaln(x_ref[...], g_ref[...], sc_ref[...], sh_ref[...]).astype(BF16)

    o_ref[...] = jnp.dot(u_ref[...], w_ref[...].astype(BF16),
                         preferred_element_type=F32).astype(o_ref.dtype)


def _in_proj(x2, g, mod4, w, rows_per_batch, tm=2048, tn=1024):
    t, d = x2.shape
    n = w.shape[1]
    tpb = rows_per_batch // tm
    return pl.pallas_call(
        _inproj_kernel,
        grid=(t // tm, n // tn),
        in_specs=[pl.BlockSpec((tm, d), lambda i, j: (i, 0), pipeline_mode=pl.Buffered(1)),
                  pl.BlockSpec((1, d), lambda i, j: (0, 0)),
                  pl.BlockSpec((None, None, 1, d), lambda i, j: (i // tpb, 0, 0, 0)),
                  pl.BlockSpec((None, None, 1, d), lambda i, j: (i // tpb, 1, 0, 0)),
                  pl.BlockSpec((d, tn), lambda i, j: (0, j))],
        out_specs=pl.BlockSpec((tm, tn), lambda i, j: (i, j)),
        out_shape=jax.ShapeDtypeStruct((t, n), BF16),
        scratch_shapes=[pltpu.VMEM((tm, d), BF16)],
        compiler_params=_params(("parallel", "arbitrary")),
        name="in_proj",
    )(x2, g, mod4, mod4, w)


def _col_reduce(x, op, groups=8):
    k, n = x.shape
    if k % (groups * 8) == 0:
        x = op(x.reshape(groups, k // groups, n), axis=0)
    return op(x, axis=0, keepdims=True)


def _attn_kernel(lam_ref, sg_ref, q_ref, k_ref, v_ref, o_ref, vt_ref, acc_ref, m_ref,
                 sa_ref, sb_ref, *, tq, sq, tk, lead, lam_init):
    qi = pl.program_id(2)
    nkb = vt_ref.shape[0]
    nc = tq // sq

    @pl.when(qi == 0)
    def _():
        for j in range(nkb):
            vt_ref[j, 0:HEAD_W, :] = v_ref[j * tk:(j + 1) * tk, :].astype(F32).T.astype(BF16)
            vt_ref[j, HEAD_W:, :] = jnp.ones((vt_ref.shape[1] - HEAD_W, tk), BF16)

    q = (q_ref[...].astype(F32) * (ATT_HALF_DIM ** -0.5 * math.log2(math.e))).astype(BF16)
    lane = lax.broadcasted_iota(jnp.int32, q.shape, 1)
    zero = jnp.zeros_like(q)
    qm = (jnp.where(lane < ATT_HALF_DIM, q, zero), jnp.where(lane >= ATT_HALF_DIM, q, zero))

    m_ref[...] = jnp.full(m_ref.shape, NEG_BIG, F32)
    acc_ref[...] = jnp.zeros(acc_ref.shape, F32)

    def scores(c, mi, k, vt, mask):
        s = lax.dot_general(k, qm[mi][c * sq:(c + 1) * sq, :], (((1,), (1,)), ((), ())),
                            preferred_element_type=F32)
        return s if mask is None else jnp.where(mask, s, NEG_BIG)

    def update(c, mi, k, vt, mask, s):
        cs = slice(c * sq, (c + 1) * sq)
        m_old = m_ref[mi, :, cs]
        m_new = jnp.maximum(m_old, _col_reduce(s, jnp.max))
        p = jnp.exp2(s - m_new)
        alpha = jnp.exp2(m_old - m_new)
        acc_ref[mi, :, cs] = alpha * acc_ref[mi, :, cs] + jnp.dot(
            vt, p.astype(BF16), preferred_element_type=F32)
        m_ref[mi, :, cs] = m_new

    assert tq == 2 * tk
    chains = [(c, mi) for c in range(nc) for mi in range(2)]

    def stage(j, src, dst):
        vt = vt_ref[j]
        if dst is not None:
            kn = k_ref[pl.ds(pl.multiple_of((j + 1) * tk, tk), tk), :]
        for idx in range(len(chains) + lead):
            if dst is not None and idx < len(chains):
                dst[idx] = scores(*chains[idx], kn, None, None)
            if idx >= lead:
                c, mi = chains[idx - lead]
                update(c, mi, None, vt, None, src[idx - lead])

    @pl.when(qi > 0)
    def _():
        k0 = k_ref[0:tk, :]
        for idx, (c, mi) in enumerate(chains):
            sa_ref[idx] = scores(c, mi, k0, None, None)

        def pair(pp, carry):
            stage(2 * pp, sa_ref, sb_ref)
            stage(2 * pp + 1, sb_ref, sa_ref)
            return carry

        lax.fori_loop(0, qi - 1, pair, 0)
        stage(2 * qi - 2, sa_ref, sb_ref)
        stage(2 * qi - 1, sb_ref, None)

    krow = lax.broadcasted_iota(jnp.int32, (sq, sq), 0)
    qcol = lax.broadcasted_iota(jnp.int32, (sq, sq), 1)
    dmask = krow // MASK_CHUNK <= qcol // MASK_CHUNK
    vt_tile = jnp.concatenate([vt_ref[qi * 2], vt_ref[qi * 2 + 1]], axis=1)
    pend = []
    for c in reversed(range(nc)):
        nk = (c + 1) * sq
        k = k_ref[pl.ds(pl.multiple_of(qi * tq, tq), nk), :]
        for mi in range(2):
            s = scores(c, mi, k, None, None)
            tail = jnp.where(dmask, s[c * sq:, :], NEG_BIG)
            s = tail if c == 0 else jnp.concatenate([s[:c * sq, :], tail], axis=0)
            pend.append((c, mi, None, vt_tile[:, :nk], None, s))
    for it in pend:
        update(*it)

    lp = lam_ref[...]
    lam = (jnp.exp(jnp.sum(lp[0:1] * lp[1:2], axis=-1, keepdims=True))
           - jnp.exp(jnp.sum(lp[2:3] * lp[3:4], axis=-1, keepdims=True)) + lam_init)
    o = (acc_ref[0, 0:HEAD_W, :] / acc_ref[0, HEAD_W:HEAD_W + 1, :]
         - lam * (acc_ref[1, 0:HEAD_W, :] / acc_ref[1, HEAD_W:HEAD_W + 1, :]))
    ms = jnp.mean(o * o, axis=0, keepdims=True)
    y = o * lax.rsqrt(ms + NORM_EPS) * sg_ref[...] * (1.0 - lam_init)
    o_ref[...] = y.T.astype(o_ref.dtype)


def _attention(p, lam4, sg_col, nb, s, lam_init, tq=1024, sq=256, tk=512, lead=0):
    h = ATT_HEADS
    nq = s // tq
    kern = functools.partial(_attn_kernel, tq=tq, sq=sq, tk=tk, lead=lead, lam_init=lam_init)
    nchains = 2 * tq // sq
    return pl.pallas_call(
        kern,
        grid=(nb, h, nq),
        in_specs=[pl.BlockSpec((4, ATT_HALF_DIM), lambda b, hh, i: (0, 0)),
                  pl.BlockSpec((HEAD_W, 1), lambda b, hh, i: (0, 0)),
                  pl.BlockSpec((tq, HEAD_W), lambda b, hh, i: (b * nq + i, hh)),
                  pl.BlockSpec((s, HEAD_W), lambda b, hh, i: (b, h + hh)),
                  pl.BlockSpec((s, HEAD_W), lambda b, hh, i: (b, 2 * h + hh))],
        out_specs=pl.BlockSpec((tq, HEAD_W), lambda b, hh, i: (b * nq + i, hh)),
        out_shape=jax.ShapeDtypeStruct((nb * s, h * HEAD_W), BF16),
        scratch_shapes=[pltpu.VMEM((s // tk, HEAD_W + ONES_ROWS, tk), BF16),
                        pltpu.VMEM((2, HEAD_W + ONES_ROWS, tq), F32),
                        pltpu.VMEM((2, 1, tq), F32),
                        pltpu.VMEM((nchains, tk, sq), F32),
                        pltpu.VMEM((nchains, tk, sq), F32)],
        compiler_params=_params(("parallel", "parallel", "arbitrary")),
        name="diff_attn",
    )(lam4, sg_col, p, p, p)


def _group_row(a, s, grp=8):
    n, w = a.shape
    a3 = a.reshape(n // grp, grp, w)
    return jnp.broadcast_to(a3[:, s:s + 1, :], (n // grp, grp, w)).reshape(n, w)


def _hgrn2_kernel(lbl_ref, gn_ref, q_ref, f_ref, i_ref, g_ref, o_ref, st_ref, *, ch, layer):
    @pl.when(pl.program_id(1) == 0)
    def _():
        st_ref[...] = jnp.zeros(st_ref.shape, F32)

    w = HEAD_W
    lbl = lbl_ref[...]
    mx = jnp.max(lbl, axis=0, keepdims=True)
    e = jnp.exp(lbl - mx)
    lb = (jnp.sum(e[0:layer + 1], axis=0, keepdims=True)
          / jnp.sum(e, axis=0, keepdims=True))

    nt = (((1,), (1,)), ((), ()))
    row = lax.broadcasted_iota(jnp.int32, (ch, w), 0)
    arow = lax.broadcasted_iota(jnp.int32, (ch, ch), 0)
    acol = lax.broadcasted_iota(jnp.int32, (ch, ch), 1)
    tril = (arow >= acol).astype(BF16)
    on_diag = arow == acol
    levels = []
    hs = ch // 2
    while hs >= 1:
        upper = (row % (2 * hs)) >= hs
        levels.append((hs, upper, jnp.where(upper, 1.0, -1.0),
                       (arow // (2 * hs) == acol // (2 * hs))
                       & ((arow % (2 * hs)) >= hs) & ((acol % (2 * hs)) < hs)))
        hs //= 2

    def midpoint_row(b, hs):
        if hs >= 8:
            return jnp.concatenate(
                [jnp.broadcast_to(b[blk * 2 * hs + hs - 1:blk * 2 * hs + hs, :], (2 * hs, w))
                 for blk in range(ch // (2 * hs))], axis=0)
        if hs == 4:
            return _group_row(b, 3)
        return jnp.where((row % 8) < 4, _group_row(b, 1), _group_row(b, 5))

    chunk_vals = {}

    def chunk_prep(c):
        if c not in chunk_vals:
            rs = slice(c * ch, (c + 1) * ch)
            f_all = lb + (1.0 - lb) * jax.nn.sigmoid(f_ref[rs, :].astype(F32))
            g2 = jnp.log2(f_all)
            g2_hi = g2.astype(BF16)
            g2_lo = (g2 - g2_hi.astype(F32)).astype(BF16)
            b_all = (jnp.dot(tril, g2_hi, preferred_element_type=F32)
                     + jnp.dot(tril, g2_lo, preferred_element_type=F32))
            chunk_vals[c] = (f_all, b_all)
        return chunk_vals[c]

    def operands(c, h):
        rs, cs = slice(c * ch, (c + 1) * ch), slice(h * w, (h + 1) * w)
        f_all, b_all = chunk_prep(c)
        b, f = b_all[:, cs], f_all[:, cs]
        kk = 1.0 - f
        q = q_ref[rs, cs].astype(F32)
        blast = b[ch - 1:ch, :]
        ys = []
        for hs, upper, sign, _ in levels:
            if hs == 1:
                y = jnp.where(upper, q * f, kk)
            else:
                y = jnp.where(upper, q, kk) * jnp.exp2((b - midpoint_row(b, hs)) * sign)
            ys.append(y.astype(BF16))
        return dict(rs=rs, cs=cs, h=h, ys=ys,
                    qin=(q * jnp.exp2(b)).astype(BF16),
                    kout=(kk * jnp.exp2(blast - b)).astype(BF16),
                    keep=jnp.exp2(blast),
                    dsum=jnp.sum(q * kk, axis=-1, keepdims=True))

    def level_matmuls(x):
        st = st_ref[x["h"]]
        x["st"] = st
        x["o"] = lax.dot_general(x["qin"], st.astype(BF16), nt, preferred_element_type=F32)
        x["aa"] = [lax.dot_general(y, y, nt, preferred_element_type=F32) for y in x["ys"]]

    def finish(x):
        rs, cs = x["rs"], x["cs"]
        v = i_ref[rs, cs]
        att = jnp.where(on_diag, x["dsum"], 0.0)
        for (_, _, _, amask), a in zip(levels, x["aa"]):
            att = jnp.where(amask, a, att)
        o = x["o"] + jnp.dot(att.astype(BF16), v, preferred_element_type=F32)
        st_ref[x["h"]] = x["keep"] * x["st"] + lax.dot_general(
            v, x["kout"], (((0,), (0,)), ((), ())), preferred_element_type=F32)
        y = _rms(o) * gn_ref[:, cs] * jax.nn.sigmoid(g_ref[rs, cs].astype(F32))
        o_ref[rs, cs] = y.astype(o_ref.dtype)

    items = [(c, h) for c in range(q_ref.shape[0] // ch) for h in range(REC_HEADS)]
    staged = []
    d1, d2 = 1, 3
    for i in range(len(items) + d2):
        if i < len(items):
            staged.append(operands(*items[i]))
        if d1 <= i < len(items) + d1:
            level_matmuls(staged[i - d1])
        if i >= d2:
            finish(staged[i - d2])


def _hgrn2(p, lb_logits, gn, nb, s, layer, rb=256, ch=128):
    hw = REC_HEADS * HEAD_W
    nr = s // rb
    c0 = 3 * ATT_HEADS * HEAD_W // hw
    kern = functools.partial(_hgrn2_kernel, ch=ch, layer=layer)
    spec = lambda off: pl.BlockSpec((rb, hw), lambda b, i: (b * nr + i, c0 + off))
    return pl.pallas_call(
        kern,
        grid=(nb, nr),
        in_specs=[pl.BlockSpec(lb_logits.shape, lambda b, i: (0, 0)),
                  pl.BlockSpec((1, hw), lambda b, i: (0, 0)),
                  spec(0), spec(1), spec(2), spec(3)],
        out_specs=pl.BlockSpec((rb, hw), lambda b, i: (b * nr + i, 0)),
        out_shape=jax.ShapeDtypeStruct((nb * s, hw), BF16),
        scratch_shapes=[pltpu.VMEM((REC_HEADS, HEAD_W, HEAD_W), F32)],
        compiler_params=_params(("parallel", "arbitrary")),
        name="hgrn2",
    )(lb_logits, gn, p, p, p, p)


def _merge_kernel(ya_ref, yr_ref, ga_ref, gr_ref, wa_ref, wr_ref, o_ref, wab_ref, wrb_ref):
    @pl.when(pl.program_id(1) == 0)
    def _():
        wab_ref[...] = wa_ref[...].astype(BF16)
        wrb_ref[...] = wr_ref[...].astype(BF16)

    ta = jnp.dot(ya_ref[...], wab_ref[...], preferred_element_type=F32)
    tr = jnp.dot(yr_ref[...], wrb_ref[...], preferred_element_type=F32)
    m = (jax.nn.sigmoid(ga_ref[...].astype(F32)) * ta
         + jax.nn.sigmoid(gr_ref[...].astype(F32)) * tr)
    o_ref[...] = m.astype(o_ref.dtype)


def _merge(ya, yr, p, wa, wr, ga_col0, tm=1024, tn=1024):
    t, ka = ya.shape
    d = wa.shape[1]
    ga0 = ga_col0 // tn
    gr0 = (ga_col0 + d) // tn
    return pl.pallas_call(
        _merge_kernel,
        grid=(d // tn, t // tm),
        in_specs=[pl.BlockSpec((tm, ka), lambda j, i: (i, 0)),
                  pl.BlockSpec((tm, ka), lambda j, i: (i, 0)),
                  pl.BlockSpec((tm, tn), lambda j, i: (i, ga0 + j)),
                  pl.BlockSpec((tm, tn), lambda j, i: (i, gr0 + j)),
                  pl.BlockSpec((ka, tn), lambda j, i: (0, j)),
                  pl.BlockSpec((ka, tn), lambda j, i: (0, j))],
        out_specs=pl.BlockSpec((tm, tn), lambda j, i: (i, j)),
        out_shape=jax.ShapeDtypeStruct((t, d), BF16),
        scratch_shapes=[pltpu.VMEM((ka, tn), BF16), pltpu.VMEM((ka, tn), BF16)],
        compiler_params=_params(("parallel", "arbitrary")),
        name="merge",
    )(ya, yr, p, p, wa, wr)


def _oproj_kernel(m_ref, x_ref, gt_ref, w_ref, o_ref, wb_ref):
    @pl.when(pl.program_id(1) == 0)
    def _():
        wb_ref[...] = w_ref[...].astype(BF16)

    y = jnp.dot(m_ref[...], wb_ref[...], preferred_element_type=F32)
    o_ref[...] = x_ref[...] + gt_ref[...] * y


def _o_proj(m, x2, mod4, w, rows_per_batch, tm=1024, tn=1024):
    t, d = x2.shape
    tpb = rows_per_batch // tm
    return pl.pallas_call(
        _oproj_kernel,
        grid=(d // tn, t // tm),
        in_specs=[pl.BlockSpec((tm, d), lambda j, i: (i, 0)),
                  pl.BlockSpec((tm, tn), lambda j, i: (i, j)),
                  pl.BlockSpec((None, None, 1, tn), lambda j, i: (i // tpb, 2, 0, j)),
                  pl.BlockSpec((d, tn), lambda j, i: (0, j))],
        out_specs=pl.BlockSpec((tm, tn), lambda j, i: (i, j)),
        out_shape=jax.ShapeDtypeStruct((t, d), F32),
        scratch_shapes=[pltpu.VMEM((d, tn), BF16)],
        compiler_params=_params(("parallel", "arbitrary")),
        name="o_proj",
    )(m, x2, mod4, w)


def _ffn_up_kernel(h_ref, g_ref, sh_ref, sc_ref, wg_ref, wu_ref, o_ref, u_ref):
    @pl.when(pl.program_id(1) == 0)
    def _():
        u_ref[...] = _adaln(h_ref[...], g_ref[...], sc_ref[...], sh_ref[...]).astype(BF16)

    u = u_ref[...]
    g = jnp.dot(u, wg_ref[...].astype(BF16), preferred_element_type=F32)
    up = jnp.dot(u, wu_ref[...].astype(BF16), preferred_element_type=F32)
    o_ref[...] = (g * jax.nn.sigmoid(g) * up).astype(o_ref.dtype)


def _ffn_up(h1, g, mod4, wg, wu, rows_per_batch, tm=2048, tf=512):
    t, d = h1.shape
    f = wg.shape[1]
    tpb = rows_per_batch // tm
    return pl.pallas_call(
        _ffn_up_kernel,
        grid=(t // tm, f // tf),
        in_specs=[pl.BlockSpec((tm, d), lambda i, j: (i, 0), pipeline_mode=pl.Buffered(1)),
                  pl.BlockSpec((1, d), lambda i, j: (0, 0)),
                  pl.BlockSpec((None, None, 1, d), lambda i, j: (i // tpb, 3, 0, 0)),
                  pl.BlockSpec((None, None, 1, d), lambda i, j: (i // tpb, 4, 0, 0)),
                  pl.BlockSpec((d, tf), lambda i, j: (0, j)),
                  pl.BlockSpec((d, tf), lambda i, j: (0, j))],
        out_specs=pl.BlockSpec((tm, tf), lambda i, j: (i, j)),
        out_shape=jax.ShapeDtypeStruct((t, f), BF16),
        scratch_shapes=[pltpu.VMEM((tm, d), BF16)],
        compiler_params=_params(("parallel", "arbitrary")),
        name="ffn_up",
    )(h1, g, mod4, mod4, wg, wu)


def _ffn_down_kernel(a_ref, h_ref, gt_ref, fg_ref, w_ref, o_ref, acc_ref, *, final):
    kk = pl.program_id(1)

    @pl.when(kk == 0)
    def _():
        acc_ref[...] = jnp.zeros(acc_ref.shape, F32)

    acc_ref[...] += jnp.dot(a_ref[...], w_ref[...].astype(BF16), preferred_element_type=F32)

    @pl.when(kk == pl.num_programs(1) - 1)
    def _():
        h2 = h_ref[...] + gt_ref[...] * acc_ref[...]
        o_ref[...] = _rms(h2) * fg_ref[...] if final else h2


def _ffn_down(a, h1, mod4, fg, w, rows_per_batch, final, tm=1024, tk=512):
    t, d = h1.shape
    f = a.shape[1]
    tpb = rows_per_batch // tm
    return pl.pallas_call(
        functools.partial(_ffn_down_kernel, final=final),
        grid=(t // tm, f // tk),
        in_specs=[pl.BlockSpec((tm, tk), lambda i, k: (i, k)),
                  pl.BlockSpec((tm, d), lambda i, k: (i, 0)),
                  pl.BlockSpec((None, None, 1, d), lambda i, k: (i // tpb, 5, 0, 0)),
                  pl.BlockSpec((1, d), lambda i, k: (0, 0)),
                  pl.BlockSpec((tk, d), lambda i, k: (k, 0))],
        out_specs=pl.BlockSpec((tm, d), lambda i, k: (i, 0)),
        out_shape=jax.ShapeDtypeStruct((t, d), F32),
        scratch_shapes=[pltpu.VMEM((tm, d), F32)],
        compiler_params=_params(("parallel", "arbitrary")),
        name="ffn_down",
    )(a, h1, mod4, fg, w)


def kernel(x, c, w_mod, b_mod, norm1_g, w_in, lambda_q1, lambda_k1, lambda_q2, lambda_k2,
           subln_g, lb_logits, gnorm_g, w_att_out, w_rec_out, w_o, norm2_g,
           w_ffn_gate, w_ffn_up, w_ffn_down, final_g):
    nb, s, d = x.shape
    depth = w_mod.shape[0]
    h = x.reshape(nb * s, d)
    for l in range(depth):
        lam_init = 0.8 - 0.6 * math.exp(-0.3 * l)
        mod4 = _mod(c, w_mod[l], b_mod[l]).reshape(nb, 6, 1, d)
        p = _in_proj(h, norm1_g[l].reshape(1, d), mod4, w_in[l], s)
        lam4 = jnp.stack([lambda_q1[l], lambda_k1[l], lambda_q2[l], lambda_k2[l]])
        ya = _attention(p, lam4, subln_g[l].reshape(HEAD_W, 1), nb, s, lam_init)
        yr = _hgrn2(p, lb_logits, gnorm_g[l].reshape(1, REC_HEADS * HEAD_W), nb, s, l)
        ga_col0 = 3 * ATT_HEADS * HEAD_W + 4 * REC_HEADS * HEAD_W
        m = _merge(ya, yr, p, w_att_out[l], w_rec_out[l], ga_col0)
        h1 = _o_proj(m, h, mod4, w_o[l], s)
        a = _ffn_up(h1, norm2_g[l].reshape(1, d), mod4, w_ffn_gate[l], w_ffn_up[l], s)
        h = _ffn_down(a, h1, mod4, final_g.reshape(1, d), w_ffn_down[l], s, l == depth - 1)
    return h.reshape(nb, s, d)
```

```python
import functools
import math

import jax
import jax.numpy as jnp
from jax import lax
from jax.experimental import pallas as pl
from jax.experimental.pallas import tpu as pltpu

F32 = jnp.float32
BF16 = jnp.bfloat16

NORM_EPS = 1e-6
ATT_HEADS = 8
ATT_HALF_DIM = 64
HEAD_W = 128
REC_HEADS = 8
MASK_CHUNK = 64
NEG_BIG = -1e30
ONES_ROWS = 16

VMEM_LIMIT = 60 * 1024 * 1024


def _params(sem, vmem=VMEM_LIMIT):
    return pltpu.CompilerParams(dimension_semantics=sem, vmem_limit_bytes=vmem)


def _rms(x):
    return x * lax.rsqrt(jnp.mean(x * x, axis=-1, keepdims=True) + NORM_EPS)


def _adaln(x, g, sc, sh):
    return _rms(x) * (g * (1.0 + sc)) + sh


def _mod_kernel(ct_ref, w_ref, b_ref, o_ref):
    ct = ct_ref[...]
    cond = ct * jax.nn.sigmoid(ct)
    w = w_ref[...]
    for b in range(ct.shape[1]):
        o_ref[b:b + 1, :] = jnp.sum(w * cond[:, b:b + 1], axis=0, keepdims=True) + b_ref[...]


def _mod(c, w_mod, b_mod, tn=1024):
    nb, d = c.shape
    n = w_mod.shape[1]
    return pl.pallas_call(
        _mod_kernel,
        grid=(n // tn,),
        in_specs=[pl.BlockSpec((d, nb), lambda j: (0, 0)),
                  pl.BlockSpec((d, tn), lambda j: (0, j)),
                  pl.BlockSpec((1, tn), lambda j: (0, j))],
        out_specs=pl.BlockSpec((nb, tn), lambda j: (0, j)),
        out_shape=jax.ShapeDtypeStruct((nb, n), F32),
        compiler_params=_params(("arbitrary",)),
        name="mod",
    )(c.T, w_mod, b_mod.reshape(1, n))


def _inproj_kernel(x_ref, g_ref, sh_ref, sc_ref, w_ref, o_ref, u_ref):
    @pl.when(pl.program_id(1) == 0)
    def _():
        u_ref[...] = _adaln(x_ref[...], g_ref[...], sc_ref[...], sh_ref[...]).astype(BF16)

    o_ref[...] = jnp.dot(u_ref[...], w_ref[...].astype(BF16),
                         preferred_element_type=F32).astype(o_ref.dtype)


def _in_proj(x2, g, mod4, w, rows_per_batch, tm=2048, tn=1024):
    t, d = x2.shape
    n = w.shape[1]
    tpb = rows_per_batch // tm
    return pl.pallas_call(
        _inproj_kernel,
        grid=(t // tm, n // tn),
        in_specs=[pl.BlockSpec((tm, d), lambda i, j: (i, 0), pipeline_mode=pl.Buffered(1)),
                  pl.BlockSpec((1, d), lambda i, j: (0, 0)),
                  pl.BlockSpec((None, None, 1, d), lambda i, j: (i // tpb, 0, 0, 0)),
                  pl.BlockSpec((None, None, 1, d), lambda i, j: (i // tpb, 1, 0, 0)),
                  pl.BlockSpec((d, tn), lambda i, j: (0, j))],
        out_specs=pl.BlockSpec((tm, tn), lambda i, j: (i, j)),
        out_shape=jax.ShapeDtypeStruct((t, n), BF16),
        scratch_shapes=[pltpu.VMEM((tm, d), BF16)],
        compiler_params=_params(("parallel", "arbitrary")),
        name="in_proj",
    )(x2, g, mod4, mod4, w)


def _col_reduce(x, op, groups=8):
    k, n = x.shape
    if k % (groups * 8) == 0:
        x = op(x.reshape(groups, k // groups, n), axis=0)
    return op(x, axis=0, keepdims=True)


def _attn_kernel(lam_ref, sg_ref, q_ref, k_ref, v_ref, o_ref, vt_ref, acc_ref, m_ref,
                 sa_ref, sb_ref, *, tq, sq, tk, lam_init):
    qi = pl.program_id(2)
    nkb = vt_ref.shape[0]
    nc = tq // sq

    @pl.when(qi == 0)
    def _():
        for j in range(nkb):
            vt_ref[j, 0:HEAD_W, :] = v_ref[j * tk:(j + 1) * tk, :].astype(F32).T.astype(BF16)
            vt_ref[j, HEAD_W:, :] = jnp.ones((vt_ref.shape[1] - HEAD_W, tk), BF16)

    q = (q_ref[...].astype(F32) * (ATT_HALF_DIM ** -0.5 * math.log2(math.e))).astype(BF16)
    lane = lax.broadcasted_iota(jnp.int32, q.shape, 1)
    zero = jnp.zeros_like(q)
    qm = (jnp.where(lane < ATT_HALF_DIM, q, zero), jnp.where(lane >= ATT_HALF_DIM, q, zero))

    m_ref[...] = jnp.full(m_ref.shape, NEG_BIG, F32)
    acc_ref[...] = jnp.zeros(acc_ref.shape, F32)

    assert tq == 2 * tk and tk == 2 * sq
    chains = [(c, mi) for c in range(nc) for mi in range(2)]
    kinds = {"F": [(tk, None)] * nc,
             "D0": [(sq, 0), (tk, sq), (tk, None), (tk, None)],
             "D1": [(0, None), (0, None), (sq, 0), (tk, sq)]}
    krow = lax.broadcasted_iota(jnp.int32, (sq, sq), 0)
    qcol = lax.broadcasted_iota(jnp.int32, (sq, sq), 1)
    dmask = krow // MASK_CHUNK <= qcol // MASK_CHUNK

    def produce(idx, kind, j, dst):
        c, mi = chains[idx]
        nk, m0 = kinds[kind][c]
        if nk == 0:
            return
        k = k_ref[pl.ds(pl.multiple_of(j * tk, tk), nk), :]
        s = lax.dot_general(k, qm[mi][c * sq:(c + 1) * sq, :], (((1,), (1,)), ((), ())),
                            preferred_element_type=F32)
        if m0 is not None:
            corner = jnp.where(dmask, s[m0:m0 + sq, :], NEG_BIG)
            s = corner if m0 == 0 else jnp.concatenate([s[:m0, :], corner], axis=0)
        dst[idx, 0:nk, :] = s

    def consume(idx, kind, j, src):
        c, mi = chains[idx]
        nk = kinds[kind][c][0]
        if nk == 0:
            return
        cs = slice(c * sq, (c + 1) * sq)
        s = src[idx, 0:nk, :]
        m_old = m_ref[mi, :, cs]
        m_new = jnp.maximum(m_old, _col_reduce(s, jnp.max))
        p = jnp.exp2(s - m_new)
        alpha = jnp.exp2(m_old - m_new)
        acc_ref[mi, :, cs] = alpha * acc_ref[mi, :, cs] + jnp.dot(
            vt_ref[j][:, 0:nk], p.astype(BF16), preferred_element_type=F32)
        m_ref[mi, :, cs] = m_new

    def stage(kind, j, src, nxt_kind, dst):
        for idx in range(len(chains)):
            if nxt_kind is not None:
                produce(idx, nxt_kind, j + 1, dst)
            consume(idx, kind, j, src)

    n_full = 2 * qi

    @pl.when(qi > 0)
    def _():
        for idx in range(len(chains)):
            produce(idx, "F", 0, sa_ref)

        def pair(pp, carry):
            stage("F", 2 * pp, sa_ref, "F", sb_ref)
            stage("F", 2 * pp + 1, sb_ref, "F", sa_ref)
            return carry

        lax.fori_loop(0, qi - 1, pair, 0)
        stage("F", n_full - 2, sa_ref, "F", sb_ref)
        stage("F", n_full - 1, sb_ref, "D0", sa_ref)

    @pl.when(qi == 0)
    def _():
        for idx in range(len(chains)):
            produce(idx, "D0", 0, sa_ref)

    stage("D0", n_full, sa_ref, "D1", sb_ref)
    stage("D1", n_full + 1, sb_ref, None, None)

    lp = lam_ref[...]
    lam = (jnp.exp(jnp.sum(lp[0:1] * lp[1:2], axis=-1, keepdims=True))
           - jnp.exp(jnp.sum(lp[2:3] * lp[3:4], axis=-1, keepdims=True)) + lam_init)
    o = (acc_ref[0, 0:HEAD_W, :] / acc_ref[0, HEAD_W:HEAD_W + 1, :]
         - lam * (acc_ref[1, 0:HEAD_W, :] / acc_ref[1, HEAD_W:HEAD_W + 1, :]))
    ms = jnp.mean(o * o, axis=0, keepdims=True)
    y = o * lax.rsqrt(ms + NORM_EPS) * sg_ref[...] * (1.0 - lam_init)
    o_ref[...] = y.T.astype(o_ref.dtype)


def _attention(p, lam4, sg_col, nb, s, lam_init, tq=1024, sq=256, tk=512):
    h = ATT_HEADS
    nq = s // tq
    kern = functools.partial(_attn_kernel, tq=tq, sq=sq, tk=tk, lam_init=lam_init)
    nchains = 2 * tq // sq
    return pl.pallas_call(
        kern,
        grid=(nb, h, nq),
        in_specs=[pl.BlockSpec((4, ATT_HALF_DIM), lambda b, hh, i: (0, 0)),
                  pl.BlockSpec((HEAD_W, 1), lambda b, hh, i: (0, 0)),
                  pl.BlockSpec((tq, HEAD_W), lambda b, hh, i: (b * nq + i, hh)),
                  pl.BlockSpec((s, HEAD_W), lambda b, hh, i: (b, h + hh)),
                  pl.BlockSpec((s, HEAD_W), lambda b, hh, i: (b, 2 * h + hh))],
        out_specs=pl.BlockSpec((tq, HEAD_W), lambda b, hh, i: (b * nq + i, hh)),
        out_shape=jax.ShapeDtypeStruct((nb * s, h * HEAD_W), BF16),
        scratch_shapes=[pltpu.VMEM((s // tk, HEAD_W + ONES_ROWS, tk), BF16),
                        pltpu.VMEM((2, HEAD_W + ONES_ROWS, tq), F32),
                        pltpu.VMEM((2, 1, tq), F32),
                        pltpu.VMEM((nchains, tk, sq), F32),
                        pltpu.VMEM((nchains, tk, sq), F32)],
        compiler_params=_params(("parallel", "parallel", "arbitrary")),
        name="diff_attn",
    )(lam4, sg_col, p, p, p)


def _group_row(a, s, grp=8):
    n, w = a.shape
    a3 = a.reshape(n // grp, grp, w)
    return jnp.broadcast_to(a3[:, s:s + 1, :], (n // grp, grp, w)).reshape(n, w)


def _hgrn2_kernel(lbl_ref, gn_ref, q_ref, f_ref, i_ref, g_ref, o_ref, st_ref, *, ch, layer):
    @pl.when(pl.program_id(1) == 0)
    def _():
        st_ref[...] = jnp.zeros(st_ref.shape, F32)

    w = HEAD_W
    lbl = lbl_ref[...]
    mx = jnp.max(lbl, axis=0, keepdims=True)
    e = jnp.exp(lbl - mx)
    lb = (jnp.sum(e[0:layer + 1], axis=0, keepdims=True)
          / jnp.sum(e, axis=0, keepdims=True))

    nt = (((1,), (1,)), ((), ()))
    row = lax.broadcasted_iota(jnp.int32, (ch, w), 0)
    arow = lax.broadcasted_iota(jnp.int32, (ch, ch), 0)
    acol = lax.broadcasted_iota(jnp.int32, (ch, ch), 1)
    tril = (arow >= acol).astype(BF16)
    on_diag = arow == acol
    levels = []
    hs = ch // 2
    while hs >= 1:
        upper = (row % (2 * hs)) >= hs
        levels.append((hs, upper, jnp.where(upper, 1.0, -1.0),
                       (arow // (2 * hs) == acol // (2 * hs))
                       & ((arow % (2 * hs)) >= hs) & ((acol % (2 * hs)) < hs)))
        hs //= 2

    def midpoint_row(b, hs):
        if hs >= 8:
            return jnp.concatenate(
                [jnp.broadcast_to(b[blk * 2 * hs + hs - 1:blk * 2 * hs + hs, :], (2 * hs, w))
                 for blk in range(ch // (2 * hs))], axis=0)
        if hs == 4:
            return _group_row(b, 3)
        return jnp.where((row % 8) < 4, _group_row(b, 1), _group_row(b, 5))

    chunk_vals = {}

    def chunk_prep(c):
        if c not in chunk_vals:
            rs = slice(c * ch, (c + 1) * ch)
            f_all = lb + (1.0 - lb) * jax.nn.sigmoid(f_ref[rs, :].astype(F32))
            g2 = jnp.log2(f_all)
            g2_hi = g2.astype(BF16)
            g2_lo = (g2 - g2_hi.astype(F32)).astype(BF16)
            b_all = (jnp.dot(tril, g2_hi, preferred_element_type=F32)
                     + jnp.dot(tril, g2_lo, preferred_element_type=F32))
            chunk_vals[c] = (f_all, b_all)
        return chunk_vals[c]

    def operands(c, h):
        rs, cs = slice(c * ch, (c + 1) * ch), slice(h * w, (h + 1) * w)
        f_all, b_all = chunk_prep(c)
        b, f = b_all[:, cs], f_all[:, cs]
        kk = 1.0 - f
        q = q_ref[rs, cs].astype(F32)
        blast = b[ch - 1:ch, :]
        ys = []
        for hs, upper, sign, _ in levels:
            if hs == 1:
                y = jnp.where(upper, q * f, kk)
            else:
                y = jnp.where(upper, q, kk) * jnp.exp2((b - midpoint_row(b, hs)) * sign)
            ys.append(y.astype(BF16))
        return dict(rs=rs, cs=cs, h=h, ys=ys,
                    qin=(q * jnp.exp2(b)).astype(BF16),
                    kout=(kk * jnp.exp2(blast - b)).astype(BF16),
                    keep=jnp.exp2(blast),
                    dsum=jnp.sum(q * kk, axis=-1, keepdims=True))

    def level_matmuls(x):
        st = st_ref[x["h"]]
        x["st"] = st
        x["o"] = lax.dot_general(x["qin"], st.astype(BF16), nt, preferred_element_type=F32)
        x["aa"] = [lax.dot_general(y, y, nt, preferred_element_type=F32) for y in x["ys"]]

    def finish(x):
        rs, cs = x["rs"], x["cs"]
        v = i_ref[rs, cs]
        att = jnp.where(on_diag, x["dsum"], 0.0)
        for (_, _, _, amask), a in zip(levels, x["aa"]):
            att = jnp.where(amask, a, att)
        o = x["o"] + jnp.dot(att.astype(BF16), v, preferred_element_type=F32)
        st_ref[x["h"]] = x["keep"] * x["st"] + lax.dot_general(
            v, x["kout"], (((0,), (0,)), ((), ())), preferred_element_type=F32)
        y = _rms(o) * gn_ref[:, cs] * jax.nn.sigmoid(g_ref[rs, cs].astype(F32))
        o_ref[rs, cs] = y.astype(o_ref.dtype)

    items = [(c, h) for c in range(q_ref.shape[0] // ch) for h in range(REC_HEADS)]
    staged = []
    d1, d2 = 1, 3
    for i in range(len(items) + d2):
        if i < len(items):
            staged.append(operands(*items[i]))
        if d1 <= i < len(items) + d1:
            level_matmuls(staged[i - d1])
        if i >= d2:
            finish(staged[i - d2])


def _hgrn2(p, lb_logits, gn, nb, s, layer, rb=256, ch=128):
    hw = REC_HEADS * HEAD_W
    nr = s // rb
    c0 = 3 * ATT_HEADS * HEAD_W // hw
    kern = functools.partial(_hgrn2_kernel, ch=ch, layer=layer)
    spec = lambda off: pl.BlockSpec((rb, hw), lambda b, i: (b * nr + i, c0 + off))
    return pl.pallas_call(
        kern,
        grid=(nb, nr),
        in_specs=[pl.BlockSpec(lb_logits.shape, lambda b, i: (0, 0)),
                  pl.BlockSpec((1, hw), lambda b, i: (0, 0)),
                  spec(0), spec(1), spec(2), spec(3)],
        out_specs=pl.BlockSpec((rb, hw), lambda b, i: (b * nr + i, 0)),
        out_shape=jax.ShapeDtypeStruct((nb * s, hw), BF16),
        scratch_shapes=[pltpu.VMEM((REC_HEADS, HEAD_W, HEAD_W), F32)],
        compiler_params=_params(("parallel", "arbitrary")),
        name="hgrn2",
    )(lb_logits, gn, p, p, p, p)


def _merge_kernel(ya_ref, yr_ref, ga_ref, gr_ref, wa_ref, wr_ref, o_ref, wab_ref, wrb_ref):
    @pl.when(pl.program_id(1) == 0)
    def _():
        wab_ref[...] = wa_ref[...].astype(BF16)
        wrb_ref[...] = wr_ref[...].astype(BF16)

    ta = jnp.dot(ya_ref[...], wab_ref[...], preferred_element_type=F32)
    tr = jnp.dot(yr_ref[...], wrb_ref[...], preferred_element_type=F32)
    m = (jax.nn.sigmoid(ga_ref[...].astype(F32)) * ta
         + jax.nn.sigmoid(gr_ref[...].astype(F32)) * tr)
    o_ref[...] = m.astype(o_ref.dtype)


def _merge(ya, yr, p, wa, wr, ga_col0, tm=1024, tn=1024):
    t, ka = ya.shape
    d = wa.shape[1]
    ga0 = ga_col0 // tn
    gr0 = (ga_col0 + d) // tn
    return pl.pallas_call(
        _merge_kernel,
        grid=(d // tn, t // tm),
        in_specs=[pl.BlockSpec((tm, ka), lambda j, i: (i, 0)),
                  pl.BlockSpec((tm, ka), lambda j, i: (i, 0)),
                  pl.BlockSpec((tm, tn), lambda j, i: (i, ga0 + j)),
                  pl.BlockSpec((tm, tn), lambda j, i: (i, gr0 + j)),
                  pl.BlockSpec((ka, tn), lambda j, i: (0, j)),
                  pl.BlockSpec((ka, tn), lambda j, i: (0, j))],
        out_specs=pl.BlockSpec((tm, tn), lambda j, i: (i, j)),
        out_shape=jax.ShapeDtypeStruct((t, d), BF16),
        scratch_shapes=[pltpu.VMEM((ka, tn), BF16), pltpu.VMEM((ka, tn), BF16)],
        compiler_params=_params(("parallel", "arbitrary")),
        name="merge",
    )(ya, yr, p, p, wa, wr)


def _oproj_kernel(m_ref, x_ref, gt_ref, w_ref, o_ref, wb_ref):
    @pl.when(pl.program_id(1) == 0)
    def _():
        wb_ref[...] = w_ref[...].astype(BF16)

    y = jnp.dot(m_ref[...], wb_ref[...], preferred_element_type=F32)
    o_ref[...] = x_ref[...] + gt_ref[...] * y


def _o_proj(m, x2, mod4, w, rows_per_batch, tm=1024, tn=1024):
    t, d = x2.shape
    tpb = rows_per_batch // tm
    return pl.pallas_call(
        _oproj_kernel,
        grid=(d // tn, t // tm),
        in_specs=[pl.BlockSpec((tm, d), lambda j, i: (i, 0)),
                  pl.BlockSpec((tm, tn), lambda j, i: (i, j)),
                  pl.BlockSpec((None, None, 1, tn), lambda j, i: (i // tpb, 2, 0, j)),
                  pl.BlockSpec((d, tn), lambda j, i: (0, j))],
        out_specs=pl.BlockSpec((tm, tn), lambda j, i: (i, j)),
        out_shape=jax.ShapeDtypeStruct((t, d), F32),
        scratch_shapes=[pltpu.VMEM((d, tn), BF16)],
        compiler_params=_params(("parallel", "arbitrary")),
        name="o_proj",
    )(m, x2, mod4, w)


def _ffn_up_kernel(h_ref, g_ref, sh_ref, sc_ref, wg_ref, wu_ref, o_ref, u_ref):
    @pl.when(pl.program_id(1) == 0)
    def _():
        u_ref[...] = _adaln(h_ref[...], g_ref[...], sc_ref[...], sh_ref[...]).astype(BF16)

    u = u_ref[...]
    g = jnp.dot(u, wg_ref[...].astype(BF16), preferred_element_type=F32)
    up = jnp.dot(u, wu_ref[...].astype(BF16), preferred_element_type=F32)
    o_ref[...] = (g * jax.nn.sigmoid(g) * up).astype(o_ref.dtype)


def _ffn_up(h1, g, mod4, wg, wu, rows_per_batch, tm=1024, tf=512):
    t, d = h1.shape
    f = wg.shape[1]
    tpb = rows_per_batch // tm
    return pl.pallas_call(
        _ffn_up_kernel,
        grid=(t // tm, f // tf),
        in_specs=[pl.BlockSpec((tm, d), lambda i, j: (i, 0)),
                  pl.BlockSpec((1, d), lambda i, j: (0, 0)),
                  pl.BlockSpec((None, None, 1, d), lambda i, j: (i // tpb, 3, 0, 0)),
                  pl.BlockSpec((None, None, 1, d), lambda i, j: (i // tpb, 4, 0, 0)),
                  pl.BlockSpec((d, tf), lambda i, j: (0, j)),
                  pl.BlockSpec((d, tf), lambda i, j: (0, j))],
        out_specs=pl.BlockSpec((tm, tf), lambda i, j: (i, j)),
        out_shape=jax.ShapeDtypeStruct((t, f), BF16),
        scratch_shapes=[pltpu.VMEM((tm, d), BF16)],
        compiler_params=_params(("parallel", "arbitrary")),
        name="ffn_up",
    )(h1, g, mod4, mod4, wg, wu)


def _ffn_down_kernel(a_ref, h_ref, gt_ref, fg_ref, w_ref, o_ref, *, tn, final):
    j = pl.program_id(1)
    y = jnp.dot(a_ref[...], w_ref[...].astype(BF16), preferred_element_type=F32)
    o_ref[:, pl.ds(pl.multiple_of(j * tn, tn), tn)] = h_ref[...] + gt_ref[...] * y

    if final:
        @pl.when(j == pl.num_programs(1) - 1)
        def _():
            o_ref[...] = _rms(o_ref[...]) * fg_ref[...]


def _ffn_down(a, h1, mod4, fg, w, rows_per_batch, final, tm=1024, tn=256):
    t, d = h1.shape
    f = a.shape[1]
    tpb = rows_per_batch // tm
    return pl.pallas_call(
        functools.partial(_ffn_down_kernel, tn=tn, final=final),
        grid=(t // tm, d // tn),
        in_specs=[pl.BlockSpec((tm, f), lambda i, j: (i, 0)),
                  pl.BlockSpec((tm, tn), lambda i, j: (i, j)),
                  pl.BlockSpec((None, None, 1, tn), lambda i, j: (i // tpb, 5, 0, j)),
                  pl.BlockSpec((1, d), lambda i, j: (0, 0)),
                  pl.BlockSpec((f, tn), lambda i, j: (0, j))],
        out_specs=pl.BlockSpec((tm, d), lambda i, j: (i, 0)),
        out_shape=jax.ShapeDtypeStruct((t, d), F32),
        compiler_params=_params(("parallel", "arbitrary")),
        name="ffn_down",
    )(a, h1, mod4, fg, w)


def kernel(x, c, w_mod, b_mod, norm1_g, w_in, lambda_q1, lambda_k1, lambda_q2, lambda_k2,
           subln_g, lb_logits, gnorm_g, w_att_out, w_rec_out, w_o, norm2_g,
           w_ffn_gate, w_ffn_up, w_ffn_down, final_g):
    nb, s, d = x.shape
    depth = w_mod.shape[0]
    h = x.reshape(nb * s, d)
    for l in range(depth):
        lam_init = 0.8 - 0.6 * math.exp(-0.3 * l)
        mod4 = _mod(c, w_mod[l], b_mod[l]).reshape(nb, 6, 1, d)
        p = _in_proj(h, norm1_g[l].reshape(1, d), mod4, w_in[l], s)
        lam4 = jnp.stack([lambda_q1[l], lambda_k1[l], lambda_q2[l], lambda_k2[l]])
        ya = _attention(p, lam4, subln_g[l].reshape(HEAD_W, 1), nb, s, lam_init)
        yr = _hgrn2(p, lb_logits, gnorm_g[l].reshape(1, REC_HEADS * HEAD_W), nb, s, l)
        ga_col0 = 3 * ATT_HEADS * HEAD_W + 4 * REC_HEADS * HEAD_W
        m = _merge(ya, yr, p, w_att_out[l], w_rec_out[l], ga_col0)
        h1 = _o_proj(m, h, mod4, w_o[l], s)
        a = _ffn_up(h1, norm2_g[l].reshape(1, d), mod4, w_ffn_gate[l], w_ffn_up[l], s)
        h = _ffn_down(a, h1, mod4, final_g.reshape(1, d), w_ffn_down[l], s, l == depth - 1)
    return h.reshape(nb, s, d)
```

```python
import functools
import math

import jax
import jax.numpy as jnp
from jax import lax
from jax.experimental import pallas as pl
from jax.experimental.pallas import tpu as pltpu

F32 = jnp.float32
BF16 = jnp.bfloat16

NORM_EPS = 1e-6
ATT_HEADS = 8
ATT_HALF_DIM = 64
HEAD_W = 128
REC_HEADS = 8
MASK_CHUNK = 64
NEG_BIG = -1e30
ONES_ROWS = 16

VMEM_LIMIT = 60 * 1024 * 1024


def _params(sem, vmem=VMEM_LIMIT):
    return pltpu.CompilerParams(dimension_semantics=sem, vmem_limit_bytes=vmem)


def _rms(x):
    return x * lax.rsqrt(jnp.mean(x * x, axis=-1, keepdims=True) + NORM_EPS)


def _adaln(x, g, sc, sh):
    return _rms(x) * (g * (1.0 + sc)) + sh


def _mod_kernel(ct_ref, w_ref, b_ref, o_ref):
    ct = ct_ref[...]
    cond = ct * jax.nn.sigmoid(ct)
    w = w_ref[...]
    for b in range(ct.shape[1]):
        o_ref[b:b + 1, :] = jnp.sum(w * cond[:, b:b + 1], axis=0, keepdims=True) + b_ref[...]


def _mod(c, w_mod, b_mod, tn=1024):
    nb, d = c.shape
    n = w_mod.shape[1]
    return pl.pallas_call(
        _mod_kernel,
        grid=(n // tn,),
        in_specs=[pl.BlockSpec((d, nb), lambda j: (0, 0)),
                  pl.BlockSpec((d, tn), lambda j: (0, j)),
                  pl.BlockSpec((1, tn), lambda j: (0, j))],
        out_specs=pl.BlockSpec((nb, tn), lambda j: (0, j)),
        out_shape=jax.ShapeDtypeStruct((nb, n), F32),
        compiler_params=_params(("arbitrary",)),
        name="mod",
    )(c.T, w_mod, b_mod.reshape(1, n))


def _adaln_matmul_kernel(x_ref, g_ref, sh_ref, sc_ref, *refs, nw, rc, finish):
    w_refs, o_ref, u_bufs = refs[:nw], refs[nw], refs[nw + 1:]
    i, j = pl.program_id(0), pl.program_id(1)
    nchunks = u_bufs[0].shape[0] // rc

    def normalise_into(u_ref):
        r0 = pl.multiple_of(jnp.minimum(j, nchunks - 1) * rc, rc)
        u_ref[pl.ds(r0, rc), :] = _adaln(x_ref[...], g_ref[...], sc_ref[...],
                                         sh_ref[...]).astype(BF16)

    def multiply_from(u_ref):
        u = u_ref[...]
        o_ref[...] = finish(*[jnp.dot(u, w[...].astype(BF16), preferred_element_type=F32)
                              for w in w_refs]).astype(o_ref.dtype)

    @pl.when(i == 0)
    def _():
        normalise_into(u_bufs[0])

    for parity in range(2):
        @pl.when((i > 0) & (i % 2 == parity))
        def _(parity=parity):
            multiply_from(u_bufs[1 - parity])
            normalise_into(u_bufs[parity])


def _adaln_matmul(x2, g, mod4, mod_row, ws, rows_per_batch, finish, name, tm, tn, rc=256):
    t, d = x2.shape
    n = ws[0].shape[1]
    nt, nchunks, tpb = t // tm, tm // rc, rows_per_batch // tm
    tile = lambda i: jnp.minimum(i, nt - 1)
    col = lambda i, j: jnp.where(i > 0, j, 0)
    kern = functools.partial(_adaln_matmul_kernel, nw=len(ws), rc=rc, finish=finish)
    return pl.pallas_call(
        kern,
        grid=(nt + 1, n // tn),
        in_specs=[pl.BlockSpec((rc, d), lambda i, j: (tile(i) * nchunks
                                                      + jnp.minimum(j, nchunks - 1), 0)),
                  pl.BlockSpec((1, d), lambda i, j: (0, 0)),
                  pl.BlockSpec((None, None, 1, d), lambda i, j: (tile(i) // tpb, mod_row, 0, 0)),
                  pl.BlockSpec((None, None, 1, d),
                               lambda i, j: (tile(i) // tpb, mod_row + 1, 0, 0))]
                 + [pl.BlockSpec((d, tn), lambda i, j: (0, col(i, j))) for _ in ws],
        out_specs=pl.BlockSpec((tm, tn), lambda i, j: (jnp.maximum(i - 1, 0), col(i, j))),
        out_shape=jax.ShapeDtypeStruct((t, n), BF16),
        scratch_shapes=[pltpu.VMEM((tm, d), BF16), pltpu.VMEM((tm, d), BF16)],
        compiler_params=_params(("arbitrary", "arbitrary")),
        name=name,
    )(x2, g, mod4, mod4, *ws)


def _in_proj(x2, g, mod4, w, rows_per_batch):
    return _adaln_matmul(x2, g, mod4, 0, [w], rows_per_batch, lambda y: y, "in_proj",
                         tm=2048, tn=1024)


def _col_reduce(x, op, groups=8):
    k, n = x.shape
    if k % (groups * 8) == 0:
        x = op(x.reshape(groups, k // groups, n), axis=0)
    return op(x, axis=0, keepdims=True)


def _attn_kernel(lam_ref, sg_ref, q_ref, k_ref, v_ref, o_ref, vt_ref, acc_ref, m_ref,
                 sa_ref, sb_ref, *, tq, sq, tk, lam_init):
    qi = pl.program_id(2)
    nkb = vt_ref.shape[0]
    nc = tq // sq

    @pl.when(qi == 0)
    def _():
        for j in range(nkb):
            vt_ref[j, 0:HEAD_W, :] = v_ref[j * tk:(j + 1) * tk, :].astype(F32).T.astype(BF16)
            vt_ref[j, HEAD_W:, :] = jnp.ones((vt_ref.shape[1] - HEAD_W, tk), BF16)

    q = (q_ref[...].astype(F32) * (ATT_HALF_DIM ** -0.5 * math.log2(math.e))).astype(BF16)
    lane = lax.broadcasted_iota(jnp.int32, q.shape, 1)
    zero = jnp.zeros_like(q)
    qm = (jnp.where(lane < ATT_HALF_DIM, q, zero), jnp.where(lane >= ATT_HALF_DIM, q, zero))

    m_ref[...] = jnp.full(m_ref.shape, NEG_BIG, F32)
    acc_ref[...] = jnp.zeros(acc_ref.shape, F32)

    def scores(c, mi, k):
        return lax.dot_general(k, qm[mi][c * sq:(c + 1) * sq, :], (((1,), (1,)), ((), ())),
                               preferred_element_type=F32)

    def update(c, mi, vt, s):
        cs = slice(c * sq, (c + 1) * sq)
        m_old = m_ref[mi, :, cs]
        m_new = jnp.maximum(m_old, _col_reduce(s, jnp.max))
        p = jnp.exp2(s - m_new)
        alpha = jnp.exp2(m_old - m_new)
        acc_ref[mi, :, cs] = alpha * acc_ref[mi, :, cs] + jnp.dot(
            vt, p.astype(BF16), preferred_element_type=F32)
        m_ref[mi, :, cs] = m_new

    assert tq == 2 * tk
    chains = [(c, mi) for c in range(nc) for mi in range(2)]

    def stage(j, src, dst):
        vt = vt_ref[j]
        if dst is not None:
            kn = k_ref[pl.ds(pl.multiple_of((j + 1) * tk, tk), tk), :]
        for idx, (c, mi) in enumerate(chains):
            if dst is not None:
                dst[idx] = scores(c, mi, kn)
            update(c, mi, vt, src[idx])

    @pl.when(qi > 0)
    def _():
        k0 = k_ref[0:tk, :]
        for idx, (c, mi) in enumerate(chains):
            sa_ref[idx] = scores(c, mi, k0)

        def pair(pp, carry):
            stage(2 * pp, sa_ref, sb_ref)
            stage(2 * pp + 1, sb_ref, sa_ref)
            return carry

        lax.fori_loop(0, qi - 1, pair, 0)
        stage(2 * qi - 2, sa_ref, sb_ref)
        stage(2 * qi - 1, sb_ref, None)

    krow = lax.broadcasted_iota(jnp.int32, (sq, sq), 0)
    qcol = lax.broadcasted_iota(jnp.int32, (sq, sq), 1)
    dmask = krow // MASK_CHUNK <= qcol // MASK_CHUNK
    vt_tile = jnp.concatenate([vt_ref[qi * 2], vt_ref[qi * 2 + 1]], axis=1)
    pend = []
    for c in reversed(range(nc)):
        nk = (c + 1) * sq
        k = k_ref[pl.ds(pl.multiple_of(qi * tq, tq), nk), :]
        for mi in range(2):
            s = scores(c, mi, k)
            tail = jnp.where(dmask, s[c * sq:, :], NEG_BIG)
            s = tail if c == 0 else jnp.concatenate([s[:c * sq, :], tail], axis=0)
            pend.append((c, mi, vt_tile[:, :nk], s))
    for it in pend:
        update(*it)

    lp = lam_ref[...]
    lam = (jnp.exp(jnp.sum(lp[0:1] * lp[1:2], axis=-1, keepdims=True))
           - jnp.exp(jnp.sum(lp[2:3] * lp[3:4], axis=-1, keepdims=True)) + lam_init)
    o = (acc_ref[0, 0:HEAD_W, :] / acc_ref[0, HEAD_W:HEAD_W + 1, :]
         - lam * (acc_ref[1, 0:HEAD_W, :] / acc_ref[1, HEAD_W:HEAD_W + 1, :]))
    ms = jnp.mean(o * o, axis=0, keepdims=True)
    y = o * lax.rsqrt(ms + NORM_EPS) * sg_ref[...] * (1.0 - lam_init)
    o_ref[...] = y.T.astype(o_ref.dtype)


def _attention(p, lam4, sg_col, nb, s, lam_init, tq=1024, sq=256, tk=512):
    h = ATT_HEADS
    nq = s // tq
    kern = functools.partial(_attn_kernel, tq=tq, sq=sq, tk=tk, lam_init=lam_init)
    nchains = 2 * tq // sq
    return pl.pallas_call(
        kern,
        grid=(nb, h, nq),
        in_specs=[pl.BlockSpec((4, ATT_HALF_DIM), lambda b, hh, i: (0, 0)),
                  pl.BlockSpec((HEAD_W, 1), lambda b, hh, i: (0, 0)),
                  pl.BlockSpec((tq, HEAD_W), lambda b, hh, i: (b * nq + i, hh)),
                  pl.BlockSpec((s, HEAD_W), lambda b, hh, i: (b, h + hh)),
                  pl.BlockSpec((s, HEAD_W), lambda b, hh, i: (b, 2 * h + hh))],
        out_specs=pl.BlockSpec((tq, HEAD_W), lambda b, hh, i: (b * nq + i, hh)),
        out_shape=jax.ShapeDtypeStruct((nb * s, h * HEAD_W), BF16),
        scratch_shapes=[pltpu.VMEM((s // tk, HEAD_W + ONES_ROWS, tk), BF16),
                        pltpu.VMEM((2, HEAD_W + ONES_ROWS, tq), F32),
                        pltpu.VMEM((2, 1, tq), F32),
                        pltpu.VMEM((nchains, tk, sq), F32),
                        pltpu.VMEM((nchains, tk, sq), F32)],
        compiler_params=_params(("parallel", "parallel", "arbitrary")),
        name="diff_attn",
    )(lam4, sg_col, p, p, p)


def _group_row(a, s, grp=8):
    n, w = a.shape
    a3 = a.reshape(n // grp, grp, w)
    return jnp.broadcast_to(a3[:, s:s + 1, :], (n // grp, grp, w)).reshape(n, w)


def _hgrn2_kernel(lbl_ref, gn_ref, q_ref, f_ref, i_ref, g_ref, o_ref, st_ref, *, ch, layer):
    @pl.when(pl.program_id(1) == 0)
    def _():
        st_ref[...] = jnp.zeros(st_ref.shape, F32)

    w = HEAD_W
    lbl = lbl_ref[...]
    mx = jnp.max(lbl, axis=0, keepdims=True)
    e = jnp.exp(lbl - mx)
    lb = (jnp.sum(e[0:layer + 1], axis=0, keepdims=True)
          / jnp.sum(e, axis=0, keepdims=True))

    nt = (((1,), (1,)), ((), ()))
    row = lax.broadcasted_iota(jnp.int32, (ch, w), 0)
    arow = lax.broadcasted_iota(jnp.int32, (ch, ch), 0)
    acol = lax.broadcasted_iota(jnp.int32, (ch, ch), 1)
    tril = (arow >= acol).astype(BF16)
    on_diag = arow == acol
    levels = []
    hs = ch // 2
    while hs >= 1:
        upper = (row % (2 * hs)) >= hs
        levels.append((hs, upper, jnp.where(upper, 1.0, -1.0),
                       (arow // (2 * hs) == acol // (2 * hs))
                       & ((arow % (2 * hs)) >= hs) & ((acol % (2 * hs)) < hs)))
        hs //= 2

    def midpoint_row(b, hs):
        if hs >= 8:
            return jnp.concatenate(
                [jnp.broadcast_to(b[blk * 2 * hs + hs - 1:blk * 2 * hs + hs, :], (2 * hs, w))
                 for blk in range(ch // (2 * hs))], axis=0)
        if hs == 4:
            return _group_row(b, 3)
        return jnp.where((row % 8) < 4, _group_row(b, 1), _group_row(b, 5))

    chunk_vals = {}

    def chunk_prep(c):
        if c not in chunk_vals:
            rs = slice(c * ch, (c + 1) * ch)
            f_all = lb + (1.0 - lb) * jax.nn.sigmoid(f_ref[rs, :].astype(F32))
            g2 = jnp.log2(f_all)
            g2_hi = g2.astype(BF16)
            g2_lo = (g2 - g2_hi.astype(F32)).astype(BF16)
            b_all = (jnp.dot(tril, g2_hi, preferred_element_type=F32)
                     + jnp.dot(tril, g2_lo, preferred_element_type=F32))
            chunk_vals[c] = (f_all, b_all)
        return chunk_vals[c]

    def operands(c, h):
        rs, cs = slice(c * ch, (c + 1) * ch), slice(h * w, (h + 1) * w)
        f_all, b_all = chunk_prep(c)
        b, f = b_all[:, cs], f_all[:, cs]
        kk = 1.0 - f
        q = q_ref[rs, cs].astype(F32)
        blast = b[ch - 1:ch, :]
        ys = []
        for hs, upper, sign, _ in levels:
            if hs == 1:
                y = jnp.where(upper, q * f, kk)
            else:
                y = jnp.where(upper, q, kk) * jnp.exp2((b - midpoint_row(b, hs)) * sign)
            ys.append(y.astype(BF16))
        return dict(rs=rs, cs=cs, h=h, ys=ys,
                    qin=(q * jnp.exp2(b)).astype(BF16),
                    kout=(kk * jnp.exp2(blast - b)).astype(BF16),
                    keep=jnp.exp2(blast),
                    dsum=jnp.sum(q * kk, axis=-1, keepdims=True))

    def level_matmuls(x):
        st = st_ref[x["h"]]
        x["st"] = st
        x["o"] = lax.dot_general(x["qin"], st.astype(BF16), nt, preferred_element_type=F32)
        x["aa"] = [lax.dot_general(y, y, nt, preferred_element_type=F32) for y in x["ys"]]

    def finish(x):
        rs, cs = x["rs"], x["cs"]
        v = i_ref[rs, cs]
        att = jnp.where(on_diag, x["dsum"], 0.0)
        for (_, _, _, amask), a in zip(levels, x["aa"]):
            att = jnp.where(amask, a, att)
        o = x["o"] + jnp.dot(att.astype(BF16), v, preferred_element_type=F32)
        st_ref[x["h"]] = x["keep"] * x["st"] + lax.dot_general(
            v, x["kout"], (((0,), (0,)), ((), ())), preferred_element_type=F32)
        y = _rms(o) * gn_ref[:, cs] * jax.nn.sigmoid(g_ref[rs, cs].astype(F32))
        o_ref[rs, cs] = y.astype(o_ref.dtype)

    items = [(c, h) for c in range(q_ref.shape[0] // ch) for h in range(REC_HEADS)]
    staged = []
    d1, d2 = 1, 3
    for i in range(len(items) + d2):
        if i < len(items):
            staged.append(operands(*items[i]))
        if d1 <= i < len(items) + d1:
            level_matmuls(staged[i - d1])
        if i >= d2:
            finish(staged[i - d2])


def _hgrn2(p, lb_logits, gn, nb, s, layer, rb=256, ch=128):
    hw = REC_HEADS * HEAD_W
    nr = s // rb
    c0 = 3 * ATT_HEADS * HEAD_W // hw
    kern = functools.partial(_hgrn2_kernel, ch=ch, layer=layer)
    spec = lambda off: pl.BlockSpec((rb, hw), lambda b, i: (b * nr + i, c0 + off))
    return pl.pallas_call(
        kern,
        grid=(nb, nr),
        in_specs=[pl.BlockSpec(lb_logits.shape, lambda b, i: (0, 0)),
                  pl.BlockSpec((1, hw), lambda b, i: (0, 0)),
                  spec(0), spec(1), spec(2), spec(3)],
        out_specs=pl.BlockSpec((rb, hw), lambda b, i: (b * nr + i, 0)),
        out_shape=jax.ShapeDtypeStruct((nb * s, hw), BF16),
        scratch_shapes=[pltpu.VMEM((REC_HEADS, HEAD_W, HEAD_W), F32)],
        compiler_params=_params(("parallel", "arbitrary")),
        name="hgrn2",
    )(lb_logits, gn, p, p, p, p)


def _merge_kernel(ya_ref, yr_ref, ga_ref, gr_ref, wa_ref, wr_ref, o_ref, wab_ref, wrb_ref):
    @pl.when(pl.program_id(1) == 0)
    def _():
        wab_ref[...] = wa_ref[...].astype(BF16)
        wrb_ref[...] = wr_ref[...].astype(BF16)

    ta = jnp.dot(ya_ref[...], wab_ref[...], preferred_element_type=F32)
    tr = jnp.dot(yr_ref[...], wrb_ref[...], preferred_element_type=F32)
    m = (jax.nn.sigmoid(ga_ref[...].astype(F32)) * ta
         + jax.nn.sigmoid(gr_ref[...].astype(F32)) * tr)
    o_ref[...] = m.astype(o_ref.dtype)


def _merge(ya, yr, p, wa, wr, ga_col0, tm=1024, tn=1024):
    t, ka = ya.shape
    d = wa.shape[1]
    ga0 = ga_col0 // tn
    gr0 = (ga_col0 + d) // tn
    return pl.pallas_call(
        _merge_kernel,
        grid=(d // tn, t // tm),
        in_specs=[pl.BlockSpec((tm, ka), lambda j, i: (i, 0)),
                  pl.BlockSpec((tm, ka), lambda j, i: (i, 0)),
                  pl.BlockSpec((tm, tn), lambda j, i: (i, ga0 + j)),
                  pl.BlockSpec((tm, tn), lambda j, i: (i, gr0 + j)),
                  pl.BlockSpec((ka, tn), lambda j, i: (0, j)),
                  pl.BlockSpec((ka, tn), lambda j, i: (0, j))],
        out_specs=pl.BlockSpec((tm, tn), lambda j, i: (i, j)),
        out_shape=jax.ShapeDtypeStruct((t, d), BF16),
        scratch_shapes=[pltpu.VMEM((ka, tn), BF16), pltpu.VMEM((ka, tn), BF16)],
        compiler_params=_params(("parallel", "arbitrary")),
        name="merge",
    )(ya, yr, p, p, wa, wr)


def _oproj_kernel(m_ref, x_ref, gt_ref, w_ref, o_ref, wb_ref):
    @pl.when(pl.program_id(1) == 0)
    def _():
        wb_ref[...] = w_ref[...].astype(BF16)

    y = jnp.dot(m_ref[...], wb_ref[...], preferred_element_type=F32)
    o_ref[...] = x_ref[...] + gt_ref[...] * y


def _o_proj(m, x2, mod4, w, rows_per_batch, tm=1024, tn=1024):
    t, d = x2.shape
    tpb = rows_per_batch // tm
    return pl.pallas_call(
        _oproj_kernel,
        grid=(d // tn, t // tm),
        in_specs=[pl.BlockSpec((tm, d), lambda j, i: (i, 0)),
                  pl.BlockSpec((tm, tn), lambda j, i: (i, j)),
                  pl.BlockSpec((None, None, 1, tn), lambda j, i: (i // tpb, 2, 0, j)),
                  pl.BlockSpec((d, tn), lambda j, i: (0, j))],
        out_specs=pl.BlockSpec((tm, tn), lambda j, i: (i, j)),
        out_shape=jax.ShapeDtypeStruct((t, d), F32),
        scratch_shapes=[pltpu.VMEM((d, tn), BF16)],
        compiler_params=_params(("parallel", "arbitrary")),
        name="o_proj",
    )(m, x2, mod4, w)


def _ffn_up(h1, g, mod4, wg, wu, rows_per_batch):
    swiglu = lambda gate, up: gate * jax.nn.sigmoid(gate) * up
    return _adaln_matmul(h1, g, mod4, 3, [wg, wu], rows_per_batch, swiglu, "ffn_up",
                         tm=2048, tn=512)


def _ffn_down_kernel(a_ref, h_ref, gt_ref, fg_ref, w_ref, o_ref, *, tn, final):
    j = pl.program_id(1)
    y = jnp.dot(a_ref[...], w_ref[...].astype(BF16), preferred_element_type=F32)
    o_ref[:, pl.ds(pl.multiple_of(j * tn, tn), tn)] = h_ref[...] + gt_ref[...] * y

    if final:
        @pl.when(j == pl.num_programs(1) - 1)
        def _():
            o_ref[...] = _rms(o_ref[...]) * fg_ref[...]


def _ffn_down(a, h1, mod4, fg, w, rows_per_batch, final, tm=1024, tn=256):
    t, d = h1.shape
    f = a.shape[1]
    tpb = rows_per_batch // tm
    return pl.pallas_call(
        functools.partial(_ffn_down_kernel, tn=tn, final=final),
        grid=(t // tm, d // tn),
        in_specs=[pl.BlockSpec((tm, f), lambda i, j: (i, 0)),
                  pl.BlockSpec((tm, tn), lambda i, j: (i, j)),
                  pl.BlockSpec((None, None, 1, tn), lambda i, j: (i // tpb, 5, 0, j)),
                  pl.BlockSpec((1, d), lambda i, j: (0, 0)),
                  pl.BlockSpec((f, tn), lambda i, j: (0, j))],
        out_specs=pl.BlockSpec((tm, d), lambda i, j: (i, 0)),
        out_shape=jax.ShapeDtypeStruct((t, d), F32),
        compiler_params=_params(("parallel", "arbitrary")),
        name="ffn_down",
    )(a, h1, mod4, fg, w)


def kernel(x, c, w_mod, b_mod, norm1_g, w_in, lambda_q1, lambda_k1, lambda_q2, lambda_k2,
           subln_g, lb_logits, gnorm_g, w_att_out, w_rec_out, w_o, norm2_g,
           w_ffn_gate, w_ffn_up, w_ffn_down, final_g):
    nb, s, d = x.shape
    depth = w_mod.shape[0]
    h = x.reshape(nb * s, d)
    for l in range(depth):
        lam_init = 0.8 - 0.6 * math.exp(-0.3 * l)
        mod4 = _mod(c, w_mod[l], b_mod[l]).reshape(nb, 6, 1, d)
        p = _in_proj(h, norm1_g[l].reshape(1, d), mod4, w_in[l], s)
        lam4 = jnp.stack([lambda_q1[l], lambda_k1[l], lambda_q2[l], lambda_k2[l]])
        ya = _attention(p, lam4, subln_g[l].reshape(HEAD_W, 1), nb, s, lam_init)
        yr = _hgrn2(p, lb_logits, gnorm_g[l].reshape(1, REC_HEADS * HEAD_W), nb, s, l)
        ga_col0 = 3 * ATT_HEADS * HEAD_W + 4 * REC_HEADS * HEAD_W
        m = _merge(ya, yr, p, w_att_out[l], w_rec_out[l], ga_col0)
        h1 = _o_proj(m, h, mod4, w_o[l], s)
        a = _ffn_up(h1, norm2_g[l].reshape(1, d), mod4, w_ffn_gate[l], w_ffn_up[l], s)
        h = _ffn_down(a, h1, mod4, final_g.reshape(1, d), w_ffn_down[l], s, l == depth - 1)
    return h.reshape(nb, s, d)
```

```python
import functools
import math

import jax
import jax.numpy as jnp
from jax import lax
from jax.experimental import pallas as pl
from jax.experimental.pallas import tpu as pltpu

F32 = jnp.float32
BF16 = jnp.bfloat16

NORM_EPS = 1e-6
ATT_HEADS = 8
ATT_HALF_DIM = 64
HEAD_W = 128
REC_HEADS = 8
MASK_CHUNK = 64
NEG_BIG = -1e30
ONES_ROWS = 16

VMEM_LIMIT = 60 * 1024 * 1024


def _params(sem, vmem=VMEM_LIMIT):
    return pltpu.CompilerParams(dimension_semantics=sem, vmem_limit_bytes=vmem)


def _rms(x):
    return x * lax.rsqrt(jnp.mean(x * x, axis=-1, keepdims=True) + NORM_EPS)


def _adaln(x, g, sc, sh):
    return _rms(x) * (g * (1.0 + sc)) + sh


def _mod_kernel(ct_ref, w_ref, b_ref, o_ref):
    ct = ct_ref[...]
    cond = ct * jax.nn.sigmoid(ct)
    w = w_ref[...]
    for b in range(ct.shape[1]):
        o_ref[b:b + 1, :] = jnp.sum(w * cond[:, b:b + 1], axis=0, keepdims=True) + b_ref[...]


def _mod(c, w_mod, b_mod, tn=1024):
    nb, d = c.shape
    n = w_mod.shape[1]
    return pl.pallas_call(
        _mod_kernel,
        grid=(n // tn,),
        in_specs=[pl.BlockSpec((d, nb), lambda j: (0, 0)),
                  pl.BlockSpec((d, tn), lambda j: (0, j)),
                  pl.BlockSpec((1, tn), lambda j: (0, j))],
        out_specs=pl.BlockSpec((nb, tn), lambda j: (0, j)),
        out_shape=jax.ShapeDtypeStruct((nb, n), F32),
        compiler_params=_params(("arbitrary",)),
        name="mod",
    )(c.T, w_mod, b_mod.reshape(1, n))


def _adaln_matmul_kernel(x_ref, g_ref, sh_ref, sc_ref, *refs, nw, rc, finish):
    w_refs, o_ref, u_bufs = refs[:nw], refs[nw], refs[nw + 1:]
    i, j = pl.program_id(0), pl.program_id(1)
    nchunks = u_bufs[0].shape[0] // rc

    def normalise_into(u_ref):
        r0 = pl.multiple_of(jnp.minimum(j, nchunks - 1) * rc, rc)
        u_ref[pl.ds(r0, rc), :] = _adaln(x_ref[...], g_ref[...], sc_ref[...],
                                         sh_ref[...]).astype(BF16)

    def multiply_from(u_ref):
        u = u_ref[...]
        o_ref[...] = finish(*[jnp.dot(u, w[...].astype(BF16), preferred_element_type=F32)
                              for w in w_refs]).astype(o_ref.dtype)

    @pl.when(i == 0)
    def _():
        normalise_into(u_bufs[0])

    for parity in range(2):
        @pl.when((i > 0) & (i % 2 == parity))
        def _(parity=parity):
            multiply_from(u_bufs[1 - parity])
            normalise_into(u_bufs[parity])


def _adaln_matmul(x2, g, mod4, mod_row, ws, rows_per_batch, finish, name, tm, tn, rc=256):
    t, d = x2.shape
    n = ws[0].shape[1]
    nt, nchunks, tpb = t // tm, tm // rc, rows_per_batch // tm
    tile = lambda i: jnp.minimum(i, nt - 1)
    col = lambda i, j: jnp.where(i > 0, j, 0)
    kern = functools.partial(_adaln_matmul_kernel, nw=len(ws), rc=rc, finish=finish)
    return pl.pallas_call(
        kern,
        grid=(nt + 1, n // tn),
        in_specs=[pl.BlockSpec((rc, d), lambda i, j: (tile(i) * nchunks
                                                      + jnp.minimum(j, nchunks - 1), 0)),
                  pl.BlockSpec((1, d), lambda i, j: (0, 0)),
                  pl.BlockSpec((None, None, 1, d), lambda i, j: (tile(i) // tpb, mod_row, 0, 0)),
                  pl.BlockSpec((None, None, 1, d),
                               lambda i, j: (tile(i) // tpb, mod_row + 1, 0, 0))]
                 + [pl.BlockSpec((d, tn), lambda i, j: (0, col(i, j))) for _ in ws],
        out_specs=pl.BlockSpec((tm, tn), lambda i, j: (jnp.maximum(i - 1, 0), col(i, j))),
        out_shape=jax.ShapeDtypeStruct((t, n), BF16),
        scratch_shapes=[pltpu.VMEM((tm, d), BF16), pltpu.VMEM((tm, d), BF16)],
        compiler_params=_params(("arbitrary", "arbitrary")),
        name=name,
    )(x2, g, mod4, mod4, *ws)


def _in_proj(x2, g, mod4, w, rows_per_batch):
    return _adaln_matmul(x2, g, mod4, 0, [w], rows_per_batch, lambda y: y, "in_proj",
                         tm=2048, tn=1024)


def _col_reduce(x, op, groups=8):
    k, n = x.shape
    if k % (groups * 8) == 0:
        x = op(x.reshape(groups, k // groups, n), axis=0)
    return op(x, axis=0, keepdims=True)


def _attn_kernel(lam_ref, sg_ref, q_ref, k_ref, v_ref, o_ref, vt_ref, acc_ref, m_ref,
                 sa_ref, sb_ref, *, tq, sq, tk, lam_init):
    qi = pl.program_id(2)
    nkb = vt_ref.shape[0]
    nc = tq // sq

    @pl.when(qi == 0)
    def _():
        for j in range(nkb):
            vt_ref[j, 0:HEAD_W, :] = v_ref[j * tk:(j + 1) * tk, :].astype(F32).T.astype(BF16)
            vt_ref[j, HEAD_W:, :] = jnp.ones((vt_ref.shape[1] - HEAD_W, tk), BF16)

    q = (q_ref[...].astype(F32) * (ATT_HALF_DIM ** -0.5 * math.log2(math.e))).astype(BF16)
    lane = lax.broadcasted_iota(jnp.int32, q.shape, 1)
    zero = jnp.zeros_like(q)
    qm = (jnp.where(lane < ATT_HALF_DIM, q, zero), jnp.where(lane >= ATT_HALF_DIM, q, zero))

    m_ref[...] = jnp.full(m_ref.shape, NEG_BIG, F32)
    acc_ref[...] = jnp.zeros(acc_ref.shape, F32)

    def scores(c, mi, k):
        return lax.dot_general(k, qm[mi][c * sq:(c + 1) * sq, :], (((1,), (1,)), ((), ())),
                               preferred_element_type=F32)

    def update(c, mi, vt, s):
        cs = slice(c * sq, (c + 1) * sq)
        m_old = m_ref[mi, :, cs]
        m_new = jnp.maximum(m_old, _col_reduce(s, jnp.max))
        p = jnp.exp2(s - m_new)
        alpha = jnp.exp2(m_old - m_new)
        acc_ref[mi, :, cs] = alpha * acc_ref[mi, :, cs] + jnp.dot(
            vt, p.astype(BF16), preferred_element_type=F32)
        m_ref[mi, :, cs] = m_new

    assert tq == 2 * tk
    chains = [(c, mi) for c in range(nc) for mi in range(2)]

    def stage(j, src, dst):
        vt = vt_ref[j]
        if dst is not None:
            kn = k_ref[pl.ds(pl.multiple_of((j + 1) * tk, tk), tk), :]
        for idx, (c, mi) in enumerate(chains):
            if dst is not None:
                dst[idx] = scores(c, mi, kn)
            update(c, mi, vt, src[idx])

    @pl.when(qi > 0)
    def _():
        k0 = k_ref[0:tk, :]
        for idx, (c, mi) in enumerate(chains):
            sa_ref[idx] = scores(c, mi, k0)

        def pair(pp, carry):
            stage(2 * pp, sa_ref, sb_ref)
            stage(2 * pp + 1, sb_ref, sa_ref)
            return carry

        lax.fori_loop(0, qi - 1, pair, 0)
        stage(2 * qi - 2, sa_ref, sb_ref)
        stage(2 * qi - 1, sb_ref, None)

    krow = lax.broadcasted_iota(jnp.int32, (sq, sq), 0)
    qcol = lax.broadcasted_iota(jnp.int32, (sq, sq), 1)
    dmask = krow // MASK_CHUNK <= qcol // MASK_CHUNK
    vt_tile = jnp.concatenate([vt_ref[qi * 2], vt_ref[qi * 2 + 1]], axis=1)
    pend = []
    for c in reversed(range(nc)):
        nk = (c + 1) * sq
        k = k_ref[pl.ds(pl.multiple_of(qi * tq, tq), nk), :]
        for mi in range(2):
            s = scores(c, mi, k)
            tail = jnp.where(dmask, s[c * sq:, :], NEG_BIG)
            s = tail if c == 0 else jnp.concatenate([s[:c * sq, :], tail], axis=0)
            pend.append((c, mi, vt_tile[:, :nk], s))
    for it in pend:
        update(*it)

    lp = lam_ref[...]
    lam = (jnp.exp(jnp.sum(lp[0:1] * lp[1:2], axis=-1, keepdims=True))
           - jnp.exp(jnp.sum(lp[2:3] * lp[3:4], axis=-1, keepdims=True)) + lam_init)
    o = (acc_ref[0, 0:HEAD_W, :] / acc_ref[0, HEAD_W:HEAD_W + 1, :]
         - lam * (acc_ref[1, 0:HEAD_W, :] / acc_ref[1, HEAD_W:HEAD_W + 1, :]))
    ms = jnp.mean(o * o, axis=0, keepdims=True)
    y = o * lax.rsqrt(ms + NORM_EPS) * sg_ref[...] * (1.0 - lam_init)
    o_ref[...] = y.T.astype(o_ref.dtype)


def _attention(p, lam4, sg_col, nb, s, lam_init, tq=1024, sq=256, tk=512):
    h = ATT_HEADS
    nq = s // tq
    kern = functools.partial(_attn_kernel, tq=tq, sq=sq, tk=tk, lam_init=lam_init)
    nchains = 2 * tq // sq
    return pl.pallas_call(
        kern,
        grid=(nb, h, nq),
        in_specs=[pl.BlockSpec((4, ATT_HALF_DIM), lambda b, hh, i: (0, 0)),
                  pl.BlockSpec((HEAD_W, 1), lambda b, hh, i: (0, 0)),
                  pl.BlockSpec((tq, HEAD_W), lambda b, hh, i: (b * nq + i, hh)),
                  pl.BlockSpec((s, HEAD_W), lambda b, hh, i: (b, h + hh)),
                  pl.BlockSpec((s, HEAD_W), lambda b, hh, i: (b, 2 * h + hh))],
        out_specs=pl.BlockSpec((tq, HEAD_W), lambda b, hh, i: (b * nq + i, hh)),
        out_shape=jax.ShapeDtypeStruct((nb * s, h * HEAD_W), BF16),
        scratch_shapes=[pltpu.VMEM((s // tk, HEAD_W + ONES_ROWS, tk), BF16),
                        pltpu.VMEM((2, HEAD_W + ONES_ROWS, tq), F32),
                        pltpu.VMEM((2, 1, tq), F32),
                        pltpu.VMEM((nchains, tk, sq), F32),
                        pltpu.VMEM((nchains, tk, sq), F32)],
        compiler_params=_params(("parallel", "parallel", "arbitrary")),
        name="diff_attn",
    )(lam4, sg_col, p, p, p)


def _group_row(a, s, grp=8):
    n, w = a.shape
    a3 = a.reshape(n // grp, grp, w)
    return jnp.broadcast_to(a3[:, s:s + 1, :], (n // grp, grp, w)).reshape(n, w)


def _hgrn2_kernel(lbl_ref, gn_ref, q_ref, f_ref, i_ref, g_ref, o_ref, st_ref, *, ch, layer):
    @pl.when(pl.program_id(1) == 0)
    def _():
        st_ref[...] = jnp.zeros(st_ref.shape, F32)

    w = HEAD_W
    lbl = lbl_ref[...]
    mx = jnp.max(lbl, axis=0, keepdims=True)
    e = jnp.exp(lbl - mx)
    lb = (jnp.sum(e[0:layer + 1], axis=0, keepdims=True)
          / jnp.sum(e, axis=0, keepdims=True))

    nt = (((1,), (1,)), ((), ()))
    row = lax.broadcasted_iota(jnp.int32, (ch, w), 0)
    arow = lax.broadcasted_iota(jnp.int32, (ch, ch), 0)
    acol = lax.broadcasted_iota(jnp.int32, (ch, ch), 1)
    tril = (arow >= acol).astype(BF16)
    on_diag = arow == acol
    levels = []
    hs = ch // 2
    while hs >= 1:
        upper = (row % (2 * hs)) >= hs
        levels.append((hs, upper, jnp.where(upper, 1.0, -1.0),
                       (arow // (2 * hs) == acol // (2 * hs))
                       & ((arow % (2 * hs)) >= hs) & ((acol % (2 * hs)) < hs)))
        hs //= 2

    def midpoint_row(b, hs):
        if hs >= 8:
            return jnp.concatenate(
                [jnp.broadcast_to(b[blk * 2 * hs + hs - 1:blk * 2 * hs + hs, :], (2 * hs, w))
                 for blk in range(ch // (2 * hs))], axis=0)
        if hs == 4:
            return _group_row(b, 3)
        return jnp.where((row % 8) < 4, _group_row(b, 1), _group_row(b, 5))

    chunk_vals = {}

    def chunk_prep(c):
        if c not in chunk_vals:
            rs = slice(c * ch, (c + 1) * ch)
            f_all = lb + (1.0 - lb) * jax.nn.sigmoid(f_ref[rs, :].astype(F32))
            g2 = jnp.log2(f_all)
            g2_hi = g2.astype(BF16)
            g2_lo = (g2 - g2_hi.astype(F32)).astype(BF16)
            b_all = (jnp.dot(tril, g2_hi, preferred_element_type=F32)
                     + jnp.dot(tril, g2_lo, preferred_element_type=F32))
            chunk_vals[c] = (f_all, b_all)
        return chunk_vals[c]

    def operands(c, h):
        rs, cs = slice(c * ch, (c + 1) * ch), slice(h * w, (h + 1) * w)
        f_all, b_all = chunk_prep(c)
        b, f = b_all[:, cs], f_all[:, cs]
        kk = 1.0 - f
        q = q_ref[rs, cs].astype(F32)
        blast = b[ch - 1:ch, :]
        ys = []
        for hs, upper, sign, _ in levels:
            if hs == 1:
                y = jnp.where(upper, q * f, kk)
            else:
                y = jnp.where(upper, q, kk) * jnp.exp2((b - midpoint_row(b, hs)) * sign)
            ys.append(y.astype(BF16))
        return dict(rs=rs, cs=cs, h=h, ys=ys,
                    qin=(q * jnp.exp2(b)).astype(BF16),
                    kout=(kk * jnp.exp2(blast - b)).astype(BF16),
                    keep=jnp.exp2(blast),
                    dsum=jnp.sum(q * kk, axis=-1, keepdims=True))

    def level_matmuls(x):
        st = st_ref[x["h"]]
        x["st"] = st
        x["o"] = lax.dot_general(x["qin"], st.astype(BF16), nt, preferred_element_type=F32)
        x["aa"] = [lax.dot_general(y, y, nt, preferred_element_type=F32) for y in x["ys"]]

    def finish(x):
        rs, cs = x["rs"], x["cs"]
        v = i_ref[rs, cs]
        att = jnp.where(on_diag, x["dsum"], 0.0)
        for (_, _, _, amask), a in zip(levels, x["aa"]):
            att = jnp.where(amask, a, att)
        o = x["o"] + jnp.dot(att.astype(BF16), v, preferred_element_type=F32)
        st_ref[x["h"]] = x["keep"] * x["st"] + lax.dot_general(
            v, x["kout"], (((0,), (0,)), ((), ())), preferred_element_type=F32)
        y = _rms(o) * gn_ref[:, cs] * jax.nn.sigmoid(g_ref[rs, cs].astype(F32))
        o_ref[rs, cs] = y.astype(o_ref.dtype)

    items = [(c, h) for c in range(q_ref.shape[0] // ch) for h in range(REC_HEADS)]
    staged = []
    d1, d2 = 1, 3
    for i in range(len(items) + d2):
        if i < len(items):
            staged.append(operands(*items[i]))
        if d1 <= i < len(items) + d1:
            level_matmuls(staged[i - d1])
        if i >= d2:
            finish(staged[i - d2])


def _hgrn2(p, lb_logits, gn, nb, s, layer, rb=256, ch=128):
    hw = REC_HEADS * HEAD_W
    nr = s // rb
    c0 = 3 * ATT_HEADS * HEAD_W // hw
    kern = functools.partial(_hgrn2_kernel, ch=ch, layer=layer)
    spec = lambda off: pl.BlockSpec((rb, hw), lambda b, i: (b * nr + i, c0 + off))
    return pl.pallas_call(
        kern,
        grid=(nb, nr),
        in_specs=[pl.BlockSpec(lb_logits.shape, lambda b, i: (0, 0)),
                  pl.BlockSpec((1, hw), lambda b, i: (0, 0)),
                  spec(0), spec(1), spec(2), spec(3)],
        out_specs=pl.BlockSpec((rb, hw), lambda b, i: (b * nr + i, 0)),
        out_shape=jax.ShapeDtypeStruct((nb * s, hw), BF16),
        scratch_shapes=[pltpu.VMEM((REC_HEADS, HEAD_W, HEAD_W), F32)],
        compiler_params=_params(("parallel", "arbitrary")),
        name="hgrn2",
    )(lb_logits, gn, p, p, p, p)


def _merge_kernel(ya_ref, yr_ref, *refs, ng):
    ga_refs, gr_refs = refs[:ng], refs[ng:2 * ng]
    wa_ref, wr_ref, o_ref, wab_ref, wrb_ref = refs[2 * ng:]

    @pl.when(pl.program_id(1) == 0)
    def _():
        wab_ref[...] = wa_ref[...].astype(BF16)
        wrb_ref[...] = wr_ref[...].astype(BF16)

    gw = o_ref.shape[1] // ng
    for g in range(ng):
        cols = slice(g * gw, (g + 1) * gw)
        ta = jnp.dot(ya_ref[...], wab_ref[:, cols], preferred_element_type=F32)
        tr = jnp.dot(yr_ref[...], wrb_ref[:, cols], preferred_element_type=F32)
        m = (jax.nn.sigmoid(ga_refs[g][...].astype(F32)) * ta
             + jax.nn.sigmoid(gr_refs[g][...].astype(F32)) * tr)
        o_ref[:, cols] = m.astype(o_ref.dtype)


def _merge(ya, yr, p, wa, wr, ga_col0, tm=512, tn=2048, gw=1024):
    t, ka = ya.shape
    d = wa.shape[1]
    ng = tn // gw
    ga0 = ga_col0 // gw
    gr0 = (ga_col0 + d) // gw
    wmode = pl.Buffered(1) if tn == d else None
    gate = lambda b0, g: pl.BlockSpec((tm, gw), lambda j, i: (i, b0 + j * ng + g))
    return pl.pallas_call(
        functools.partial(_merge_kernel, ng=ng),
        grid=(d // tn, t // tm),
        in_specs=[pl.BlockSpec((tm, ka), lambda j, i: (i, 0)),
                  pl.BlockSpec((tm, ka), lambda j, i: (i, 0))]
                 + [gate(ga0, g) for g in range(ng)] + [gate(gr0, g) for g in range(ng)]
                 + [pl.BlockSpec((ka, tn), lambda j, i: (0, j), pipeline_mode=wmode),
                    pl.BlockSpec((ka, tn), lambda j, i: (0, j), pipeline_mode=wmode)],
        out_specs=pl.BlockSpec((tm, tn), lambda j, i: (i, j)),
        out_shape=jax.ShapeDtypeStruct((t, d), BF16),
        scratch_shapes=[pltpu.VMEM((ka, tn), BF16), pltpu.VMEM((ka, tn), BF16)],
        compiler_params=_params(("parallel", "arbitrary")),
        name="merge",
    )(ya, yr, *([p] * (2 * ng)), wa, wr)


def _oproj_kernel(m_ref, x_ref, gt_ref, w_ref, o_ref, wb_ref):
    @pl.when(pl.program_id(1) == 0)
    def _():
        wb_ref[...] = w_ref[...].astype(BF16)

    y = jnp.dot(m_ref[...], wb_ref[...], preferred_element_type=F32)
    o_ref[...] = x_ref[...] + gt_ref[...] * y


def _o_proj(m, x2, mod4, w, rows_per_batch, tm=512, tn=2048):
    t, d = x2.shape
    tpb = rows_per_batch // tm
    wmode = pl.Buffered(1) if tn == d else None
    return pl.pallas_call(
        _oproj_kernel,
        grid=(d // tn, t // tm),
        in_specs=[pl.BlockSpec((tm, d), lambda j, i: (i, 0)),
                  pl.BlockSpec((tm, tn), lambda j, i: (i, j)),
                  pl.BlockSpec((None, None, 1, tn), lambda j, i: (i // tpb, 2, 0, j)),
                  pl.BlockSpec((d, tn), lambda j, i: (0, j), pipeline_mode=wmode)],
        out_specs=pl.BlockSpec((tm, tn), lambda j, i: (i, j)),
        out_shape=jax.ShapeDtypeStruct((t, d), F32),
        scratch_shapes=[pltpu.VMEM((d, tn), BF16)],
        compiler_params=_params(("parallel", "arbitrary")),
        name="o_proj",
    )(m, x2, mod4, w)


def _ffn_up(h1, g, mod4, wg, wu, rows_per_batch):
    swiglu = lambda gate, up: gate * jax.nn.sigmoid(gate) * up
    return _adaln_matmul(h1, g, mod4, 3, [wg, wu], rows_per_batch, swiglu, "ffn_up",
                         tm=1024, tn=512, rc=128)


def _ffn_down_kernel(a_ref, h_ref, gt_ref, fg_ref, w_ref, o_ref, *, tn, final):
    j = pl.program_id(1)
    y = jnp.dot(a_ref[...], w_ref[...].astype(BF16), preferred_element_type=F32)
    o_ref[:, pl.ds(pl.multiple_of(j * tn, tn), tn)] = h_ref[...] + gt_ref[...] * y

    if final:
        @pl.when(j == pl.num_programs(1) - 1)
        def _():
            o_ref[...] = _rms(o_ref[...]) * fg_ref[...]


def _ffn_down(a, h1, mod4, fg, w, rows_per_batch, final, tm=1024, tn=256):
    t, d = h1.shape
    f = a.shape[1]
    tpb = rows_per_batch // tm
    return pl.pallas_call(
        functools.partial(_ffn_down_kernel, tn=tn, final=final),
        grid=(t // tm, d // tn),
        in_specs=[pl.BlockSpec((tm, f), lambda i, j: (i, 0)),
                  pl.BlockSpec((tm, tn), lambda i, j: (i, j)),
                  pl.BlockSpec((None, None, 1, tn), lambda i, j: (i // tpb, 5, 0, j)),
                  pl.BlockSpec((1, d), lambda i, j: (0, 0)),
                  pl.BlockSpec((f, tn), lambda i, j: (0, j))],
        out_specs=pl.BlockSpec((tm, d), lambda i, j: (i, 0)),
        out_shape=jax.ShapeDtypeStruct((t, d), F32),
        compiler_params=_params(("parallel", "arbitrary")),
        name="ffn_down",
    )(a, h1, mod4, fg, w)


def kernel(x, c, w_mod, b_mod, norm1_g, w_in, lambda_q1, lambda_k1, lambda_q2, lambda_k2,
           subln_g, lb_logits, gnorm_g, w_att_out, w_rec_out, w_o, norm2_g,
           w_ffn_gate, w_ffn_up, w_ffn_down, final_g):
    nb, s, d = x.shape
    depth = w_mod.shape[0]
    h = x.reshape(nb * s, d)
    for l in range(depth):
        lam_init = 0.8 - 0.6 * math.exp(-0.3 * l)
        mod4 = _mod(c, w_mod[l], b_mod[l]).reshape(nb, 6, 1, d)
        p = _in_proj(h, norm1_g[l].reshape(1, d), mod4, w_in[l], s)
        lam4 = jnp.stack([lambda_q1[l], lambda_k1[l], lambda_q2[l], lambda_k2[l]])
        ya = _attention(p, lam4, subln_g[l].reshape(HEAD_W, 1), nb, s, lam_init)
        yr = _hgrn2(p, lb_logits, gnorm_g[l].reshape(1, REC_HEADS * HEAD_W), nb, s, l)
        ga_col0 = 3 * ATT_HEADS * HEAD_W + 4 * REC_HEADS * HEAD_W
        m = _merge(ya, yr, p, w_att_out[l], w_rec_out[l], ga_col0)
        h1 = _o_proj(m, h, mod4, w_o[l], s)
        a = _ffn_up(h1, norm2_g[l].reshape(1, d), mod4, w_ffn_gate[l], w_ffn_up[l], s)
        h = _ffn_down(a, h1, mod4, final_g.reshape(1, d), w_ffn_down[l], s, l == depth - 1)
    return h.reshape(nb, s, d)
```

```python
import functools
import math

import jax
import jax.numpy as jnp
from jax import lax
from jax.experimental import pallas as pl
from jax.experimental.pallas import tpu as pltpu

F32 = jnp.float32
BF16 = jnp.bfloat16

NORM_EPS = 1e-6
ATT_HEADS = 8
ATT_HALF_DIM = 64
HEAD_W = 128
REC_HEADS = 8
MASK_CHUNK = 64
NEG_BIG = -1e30
ONES_ROWS = 16

VMEM_LIMIT = 60 * 1024 * 1024


def _params(sem, vmem=VMEM_LIMIT):
    return pltpu.CompilerParams(dimension_semantics=sem, vmem_limit_bytes=vmem)


def _rms(x):
    return x * lax.rsqrt(jnp.mean(x * x, axis=-1, keepdims=True) + NORM_EPS)


def _adaln(x, g, sc, sh):
    return _rms(x) * (g * (1.0 + sc)) + sh


def _mod_kernel(ct_ref, w_ref, b_ref, o_ref):
    ct = ct_ref[...]
    cond = ct * jax.nn.sigmoid(ct)
    w = w_ref[...]
    for b in range(ct.shape[1]):
        o_ref[b:b + 1, :] = jnp.sum(w * cond[:, b:b + 1], axis=0, keepdims=True) + b_ref[...]


def _mod(c, w_mod, b_mod, tn=2048):
    nb, d = c.shape
    n = w_mod.shape[1]
    return pl.pallas_call(
        _mod_kernel,
        grid=(n // tn,),
        in_specs=[pl.BlockSpec((d, nb), lambda j: (0, 0)),
                  pl.BlockSpec((d, tn), lambda j: (0, j)),
                  pl.BlockSpec((1, tn), lambda j: (0, j))],
        out_specs=pl.BlockSpec((nb, tn), lambda j: (0, j)),
        out_shape=jax.ShapeDtypeStruct((nb, n), F32),
        compiler_params=_params(("arbitrary",)),
        name="mod",
    )(c.T, w_mod, b_mod.reshape(1, n))


def _adaln_matmul_kernel(x_ref, g_ref, sh_ref, sc_ref, *refs, nw, rc, rows, finish):
    w_refs, o_ref, u_bufs = refs[:nw], refs[nw], refs[nw + 1:]
    i, j = pl.program_id(0), pl.program_id(1)
    nchunks = u_bufs[0].shape[0] // rc

    def normalise_into(u_ref):
        r0 = pl.multiple_of(jnp.minimum(j, nchunks - 1) * rc, rc)
        u_ref[pl.ds(r0, rc), :] = _adaln(x_ref[...], g_ref[...], sc_ref[...],
                                         sh_ref[...]).astype(BF16)

    def multiply_from(u_ref):
        wb = [w[...].astype(BF16) for w in w_refs]
        for r0 in range(0, u_ref.shape[0], rows):
            u = u_ref[r0:r0 + rows, :]
            o_ref[r0:r0 + rows, :] = finish(
                *[jnp.dot(u, w, preferred_element_type=F32) for w in wb]).astype(o_ref.dtype)

    @pl.when(i == 0)
    def _():
        normalise_into(u_bufs[0])

    for parity in range(2):
        @pl.when((i > 0) & (i % 2 == parity))
        def _(parity=parity):
            multiply_from(u_bufs[1 - parity])
            normalise_into(u_bufs[parity])


def _adaln_matmul(x2, g, mod4, mod_row, ws, rows_per_batch, finish, name, tm, tn, rc=256,
                  rows=None):
    t, d = x2.shape
    n = ws[0].shape[1]
    nt, nchunks, tpb = t // tm, tm // rc, rows_per_batch // tm
    assert nchunks <= n // tn
    tile = lambda i: jnp.minimum(i, nt - 1)
    col = lambda i, j: jnp.where(i > 0, j, 0)
    kern = functools.partial(_adaln_matmul_kernel, nw=len(ws), rc=rc, rows=rows or tm,
                             finish=finish)
    return pl.pallas_call(
        kern,
        grid=(nt + 1, n // tn),
        in_specs=[pl.BlockSpec((rc, d), lambda i, j: (tile(i) * nchunks
                                                      + jnp.minimum(j, nchunks - 1), 0)),
                  pl.BlockSpec((1, d), lambda i, j: (0, 0)),
                  pl.BlockSpec((None, None, 1, d), lambda i, j: (tile(i) // tpb, mod_row, 0, 0)),
                  pl.BlockSpec((None, None, 1, d),
                               lambda i, j: (tile(i) // tpb, mod_row + 1, 0, 0))]
                 + [pl.BlockSpec((d, tn), lambda i, j: (0, col(i, j))) for _ in ws],
        out_specs=pl.BlockSpec((tm, tn), lambda i, j: (jnp.maximum(i - 1, 0), col(i, j))),
        out_shape=jax.ShapeDtypeStruct((t, n), BF16),
        scratch_shapes=[pltpu.VMEM((tm, d), BF16), pltpu.VMEM((tm, d), BF16)],
        compiler_params=_params(("arbitrary", "arbitrary")),
        name=name,
    )(x2, g, mod4, mod4, *ws)


def _in_proj(x2, g, mod4, w, rows_per_batch):
    return _adaln_matmul(x2, g, mod4, 0, [w], rows_per_batch, lambda y: y, "in_proj",
                         tm=2048, tn=1024)


def _col_reduce(x, op, groups=8):
    k, n = x.shape
    if k % (groups * 8) == 0:
        x = op(x.reshape(groups, k // groups, n), axis=0)
    return op(x, axis=0, keepdims=True)


def _attn_kernel(lam_ref, sg_ref, q_ref, k_ref, v_ref, o_ref, vt_ref, acc_ref, m_ref,
                 sa_ref, sb_ref, *, tq, sq, tk, lam_init):
    qi = pl.program_id(2)
    nkb = vt_ref.shape[0]
    nc = tq // sq

    @pl.when(qi == 0)
    def _():
        for j in range(nkb):
            vt_ref[j, 0:HEAD_W, :] = v_ref[j * tk:(j + 1) * tk, :].astype(F32).T.astype(BF16)
            vt_ref[j, HEAD_W:, :] = jnp.ones((vt_ref.shape[1] - HEAD_W, tk), BF16)

    q = (q_ref[...].astype(F32) * (ATT_HALF_DIM ** -0.5 * math.log2(math.e))).astype(BF16)
    lane = lax.broadcasted_iota(jnp.int32, q.shape, 1)
    zero = jnp.zeros_like(q)
    qm = (jnp.where(lane < ATT_HALF_DIM, q, zero), jnp.where(lane >= ATT_HALF_DIM, q, zero))

    m_ref[...] = jnp.full(m_ref.shape, NEG_BIG, F32)
    acc_ref[...] = jnp.zeros(acc_ref.shape, F32)

    def scores(c, mi, k):
        return lax.dot_general(k, qm[mi][c * sq:(c + 1) * sq, :], (((1,), (1,)), ((), ())),
                               preferred_element_type=F32)

    def update(c, mi, vt, s):
        cs = slice(c * sq, (c + 1) * sq)
        m_old = m_ref[mi, :, cs]
        m_new = jnp.maximum(m_old, _col_reduce(s, jnp.max))
        p = jnp.exp2(s - m_new)
        alpha = jnp.exp2(m_old - m_new)
        acc_ref[mi, :, cs] = alpha * acc_ref[mi, :, cs] + jnp.dot(
            vt, p.astype(BF16), preferred_element_type=F32)
        m_ref[mi, :, cs] = m_new

    assert tq == 2 * tk
    chains = [(c, mi) for c in range(nc) for mi in range(2)]

    def stage(j, src, dst):
        vt = vt_ref[j]
        if dst is not None:
            kn = k_ref[pl.ds(pl.multiple_of((j + 1) * tk, tk), tk), :]
        for idx, (c, mi) in enumerate(chains):
            if dst is not None:
                dst[idx] = scores(c, mi, kn)
            update(c, mi, vt, src[idx])

    @pl.when(qi > 0)
    def _():
        k0 = k_ref[0:tk, :]
        for idx, (c, mi) in enumerate(chains):
            sa_ref[idx] = scores(c, mi, k0)

        def pair(pp, carry):
            stage(2 * pp, sa_ref, sb_ref)
            stage(2 * pp + 1, sb_ref, sa_ref)
            return carry

        lax.fori_loop(0, qi - 1, pair, 0)
        stage(2 * qi - 2, sa_ref, sb_ref)
        stage(2 * qi - 1, sb_ref, None)

    krow = lax.broadcasted_iota(jnp.int32, (sq, sq), 0)
    qcol = lax.broadcasted_iota(jnp.int32, (sq, sq), 1)
    dmask = krow // MASK_CHUNK <= qcol // MASK_CHUNK
    vt_tile = jnp.concatenate([vt_ref[qi * 2], vt_ref[qi * 2 + 1]], axis=1)
    pend = []
    for c in reversed(range(nc)):
        nk = (c + 1) * sq
        k = k_ref[pl.ds(pl.multiple_of(qi * tq, tq), nk), :]
        for mi in range(2):
            s = scores(c, mi, k)
            tail = jnp.where(dmask, s[c * sq:, :], NEG_BIG)
            s = tail if c == 0 else jnp.concatenate([s[:c * sq, :], tail], axis=0)
            pend.append((c, mi, vt_tile[:, :nk], s))
    for it in pend:
        update(*it)

    lp = lam_ref[...]
    lam = (jnp.exp(jnp.sum(lp[0:1] * lp[1:2], axis=-1, keepdims=True))
           - jnp.exp(jnp.sum(lp[2:3] * lp[3:4], axis=-1, keepdims=True)) + lam_init)
    o = (acc_ref[0, 0:HEAD_W, :] / acc_ref[0, HEAD_W:HEAD_W + 1, :]
         - lam * (acc_ref[1, 0:HEAD_W, :] / acc_ref[1, HEAD_W:HEAD_W + 1, :]))
    ms = jnp.mean(o * o, axis=0, keepdims=True)
    y = o * lax.rsqrt(ms + NORM_EPS) * sg_ref[...] * (1.0 - lam_init)
    o_ref[...] = y.T.astype(o_ref.dtype)


def _attention(p, lam4, sg_col, nb, s, lam_init, tq=1024, sq=256, tk=512):
    h = ATT_HEADS
    nq = s // tq
    kern = functools.partial(_attn_kernel, tq=tq, sq=sq, tk=tk, lam_init=lam_init)
    nchains = 2 * tq // sq
    return pl.pallas_call(
        kern,
        grid=(nb, h, nq),
        in_specs=[pl.BlockSpec((4, ATT_HALF_DIM), lambda b, hh, i: (0, 0)),
                  pl.BlockSpec((HEAD_W, 1), lambda b, hh, i: (0, 0)),
                  pl.BlockSpec((tq, HEAD_W), lambda b, hh, i: (b * nq + i, hh)),
                  pl.BlockSpec((s, HEAD_W), lambda b, hh, i: (b, h + hh)),
                  pl.BlockSpec((s, HEAD_W), lambda b, hh, i: (b, 2 * h + hh))],
        out_specs=pl.BlockSpec((tq, HEAD_W), lambda b, hh, i: (b * nq + i, hh)),
        out_shape=jax.ShapeDtypeStruct((nb * s, h * HEAD_W), BF16),
        scratch_shapes=[pltpu.VMEM((s // tk, HEAD_W + ONES_ROWS, tk), BF16),
                        pltpu.VMEM((2, HEAD_W + ONES_ROWS, tq), F32),
                        pltpu.VMEM((2, 1, tq), F32),
                        pltpu.VMEM((nchains, tk, sq), F32),
                        pltpu.VMEM((nchains, tk, sq), F32)],
        compiler_params=_params(("parallel", "parallel", "arbitrary")),
        name="diff_attn",
    )(lam4, sg_col, p, p, p)


def _group_row(a, s, grp=8):
    n, w = a.shape
    a3 = a.reshape(n // grp, grp, w)
    return jnp.broadcast_to(a3[:, s:s + 1, :], (n // grp, grp, w)).reshape(n, w)


def _hgrn2_kernel(lbl_ref, gn_ref, q_ref, f_ref, i_ref, g_ref, o_ref, st_ref, *, ch, layer):
    @pl.when(pl.program_id(1) == 0)
    def _():
        st_ref[...] = jnp.zeros(st_ref.shape, F32)

    w = HEAD_W
    lbl = lbl_ref[...]
    mx = jnp.max(lbl, axis=0, keepdims=True)
    e = jnp.exp(lbl - mx)
    lb = (jnp.sum(e[0:layer + 1], axis=0, keepdims=True)
          / jnp.sum(e, axis=0, keepdims=True))

    nt = (((1,), (1,)), ((), ()))
    row = lax.broadcasted_iota(jnp.int32, (ch, w), 0)
    arow = lax.broadcasted_iota(jnp.int32, (ch, ch), 0)
    acol = lax.broadcasted_iota(jnp.int32, (ch, ch), 1)
    tril = (arow >= acol).astype(BF16)
    on_diag = arow == acol
    levels = []
    hs = ch // 2
    while hs >= 1:
        upper = (row % (2 * hs)) >= hs
        levels.append((hs, upper, jnp.where(upper, 1.0, -1.0),
                       (arow // (2 * hs) == acol // (2 * hs))
                       & ((arow % (2 * hs)) >= hs) & ((acol % (2 * hs)) < hs)))
        hs //= 2

    def midpoint_row(b, hs):
        if hs >= 8:
            return jnp.concatenate(
                [jnp.broadcast_to(b[blk * 2 * hs + hs - 1:blk * 2 * hs + hs, :], (2 * hs, w))
                 for blk in range(ch // (2 * hs))], axis=0)
        if hs == 4:
            return _group_row(b, 3)
        return jnp.where((row % 8) < 4, _group_row(b, 1), _group_row(b, 5))

    chunk_vals = {}

    def chunk_prep(c):
        if c not in chunk_vals:
            rs = slice(c * ch, (c + 1) * ch)
            f_all = lb + (1.0 - lb) * jax.nn.sigmoid(f_ref[rs, :].astype(F32))
            g2 = jnp.log2(f_all)
            g2_hi = g2.astype(BF16)
            g2_lo = (g2 - g2_hi.astype(F32)).astype(BF16)
            b_all = (jnp.dot(tril, g2_hi, preferred_element_type=F32)
                     + jnp.dot(tril, g2_lo, preferred_element_type=F32))
            chunk_vals[c] = (f_all, b_all)
        return chunk_vals[c]

    def operands(c, h):
        rs, cs = slice(c * ch, (c + 1) * ch), slice(h * w, (h + 1) * w)
        f_all, b_all = chunk_prep(c)
        b, f = b_all[:, cs], f_all[:, cs]
        kk = 1.0 - f
        q = q_ref[rs, cs].astype(F32)
        blast = b[ch - 1:ch, :]
        ys = []
        for hs, upper, sign, _ in levels:
            if hs == 1:
                y = jnp.where(upper, q * f, kk)
            else:
                y = jnp.where(upper, q, kk) * jnp.exp2((b - midpoint_row(b, hs)) * sign)
            ys.append(y.astype(BF16))
        return dict(rs=rs, cs=cs, h=h, ys=ys,
                    qin=(q * jnp.exp2(b)).astype(BF16),
                    kout=(kk * jnp.exp2(blast - b)).astype(BF16),
                    keep=jnp.exp2(blast),
                    dsum=jnp.sum(q * kk, axis=-1, keepdims=True))

    def level_matmuls(x):
        st = st_ref[x["h"]]
        x["st"] = st
        x["o"] = lax.dot_general(x["qin"], st.astype(BF16), nt, preferred_element_type=F32)
        x["aa"] = [lax.dot_general(y, y, nt, preferred_element_type=F32) for y in x["ys"]]

    def finish(x):
        rs, cs = x["rs"], x["cs"]
        v = i_ref[rs, cs]
        att = jnp.where(on_diag, x["dsum"], 0.0)
        for (_, _, _, amask), a in zip(levels, x["aa"]):
            att = jnp.where(amask, a, att)
        o = x["o"] + jnp.dot(att.astype(BF16), v, preferred_element_type=F32)
        st_ref[x["h"]] = x["keep"] * x["st"] + lax.dot_general(
            v, x["kout"], (((0,), (0,)), ((), ())), preferred_element_type=F32)
        y = _rms(o) * gn_ref[:, cs] * jax.nn.sigmoid(g_ref[rs, cs].astype(F32))
        o_ref[rs, cs] = y.astype(o_ref.dtype)

    items = [(c, h) for c in range(q_ref.shape[0] // ch) for h in range(REC_HEADS)]
    staged = []
    d1, d2 = 1, 3
    for i in range(len(items) + d2):
        if i < len(items):
            staged.append(operands(*items[i]))
        if d1 <= i < len(items) + d1:
            level_matmuls(staged[i - d1])
        if i >= d2:
            finish(staged[i - d2])


def _hgrn2(p, lb_logits, gn, nb, s, layer, rb=512, ch=128):
    hw = REC_HEADS * HEAD_W
    nr = s // rb
    c0 = 3 * ATT_HEADS * HEAD_W // hw
    kern = functools.partial(_hgrn2_kernel, ch=ch, layer=layer)
    spec = lambda off: pl.BlockSpec((rb, hw), lambda b, i: (b * nr + i, c0 + off))
    return pl.pallas_call(
        kern,
        grid=(nb, nr),
        in_specs=[pl.BlockSpec(lb_logits.shape, lambda b, i: (0, 0)),
                  pl.BlockSpec((1, hw), lambda b, i: (0, 0)),
                  spec(0), spec(1), spec(2), spec(3)],
        out_specs=pl.BlockSpec((rb, hw), lambda b, i: (b * nr + i, 0)),
        out_shape=jax.ShapeDtypeStruct((nb * s, hw), BF16),
        scratch_shapes=[pltpu.VMEM((REC_HEADS, HEAD_W, HEAD_W), F32)],
        compiler_params=_params(("parallel", "arbitrary")),
        name="hgrn2",
    )(lb_logits, gn, p, p, p, p)


def _merge_kernel(ya_ref, yr_ref, *refs, ng):
    ga_refs, gr_refs = refs[:ng], refs[ng:2 * ng]
    wa_ref, wr_ref, o_ref, wab_ref, wrb_ref = refs[2 * ng:]

    @pl.when(pl.program_id(1) == 0)
    def _():
        wab_ref[...] = wa_ref[...].astype(BF16)
        wrb_ref[...] = wr_ref[...].astype(BF16)

    gw = o_ref.shape[1] // ng
    for g in range(ng):
        cols = slice(g * gw, (g + 1) * gw)
        ta = jnp.dot(ya_ref[...], wab_ref[:, cols], preferred_element_type=F32)
        tr = jnp.dot(yr_ref[...], wrb_ref[:, cols], preferred_element_type=F32)
        m = (jax.nn.sigmoid(ga_refs[g][...].astype(F32)) * ta
             + jax.nn.sigmoid(gr_refs[g][...].astype(F32)) * tr)
        o_ref[:, cols] = m.astype(o_ref.dtype)


def _merge(ya, yr, p, wa, wr, ga_col0, tm=512, tn=2048, gw=1024):
    t, ka = ya.shape
    d = wa.shape[1]
    ng = tn // gw
    ga0 = ga_col0 // gw
    gr0 = (ga_col0 + d) // gw
    wmode = pl.Buffered(1) if tn == d else None
    gate = lambda b0, g: pl.BlockSpec((tm, gw), lambda j, i: (i, b0 + j * ng + g))
    return pl.pallas_call(
        functools.partial(_merge_kernel, ng=ng),
        grid=(d // tn, t // tm),
        in_specs=[pl.BlockSpec((tm, ka), lambda j, i: (i, 0)),
                  pl.BlockSpec((tm, ka), lambda j, i: (i, 0))]
                 + [gate(ga0, g) for g in range(ng)] + [gate(gr0, g) for g in range(ng)]
                 + [pl.BlockSpec((ka, tn), lambda j, i: (0, j), pipeline_mode=wmode),
                    pl.BlockSpec((ka, tn), lambda j, i: (0, j), pipeline_mode=wmode)],
        out_specs=pl.BlockSpec((tm, tn), lambda j, i: (i, j)),
        out_shape=jax.ShapeDtypeStruct((t, d), BF16),
        scratch_shapes=[pltpu.VMEM((ka, tn), BF16), pltpu.VMEM((ka, tn), BF16)],
        compiler_params=_params(("parallel", "arbitrary")),
        name="merge",
    )(ya, yr, *([p] * (2 * ng)), wa, wr)


def _oproj_kernel(m_ref, x_ref, gt_ref, w_ref, o_ref, wb_ref):
    @pl.when(pl.program_id(1) == 0)
    def _():
        wb_ref[...] = w_ref[...].astype(BF16)

    y = jnp.dot(m_ref[...], wb_ref[...], preferred_element_type=F32)
    o_ref[...] = x_ref[...] + gt_ref[...] * y


def _o_proj(m, x2, mod4, w, rows_per_batch, tm=512, tn=2048):
    t, d = x2.shape
    tpb = rows_per_batch // tm
    wmode = pl.Buffered(1) if tn == d else None
    return pl.pallas_call(
        _oproj_kernel,
        grid=(d // tn, t // tm),
        in_specs=[pl.BlockSpec((tm, d), lambda j, i: (i, 0)),
                  pl.BlockSpec((tm, tn), lambda j, i: (i, j)),
                  pl.BlockSpec((None, None, 1, tn), lambda j, i: (i // tpb, 2, 0, j)),
                  pl.BlockSpec((d, tn), lambda j, i: (0, j), pipeline_mode=wmode)],
        out_specs=pl.BlockSpec((tm, tn), lambda j, i: (i, j)),
        out_shape=jax.ShapeDtypeStruct((t, d), F32),
        scratch_shapes=[pltpu.VMEM((d, tn), BF16)],
        compiler_params=_params(("parallel", "arbitrary")),
        name="o_proj",
    )(m, x2, mod4, w)


def _ffn_up(h1, g, mod4, wg, wu, rows_per_batch):
    swiglu = lambda gate, up: gate * jax.nn.sigmoid(gate) * up
    return _adaln_matmul(h1, g, mod4, 3, [wg, wu], rows_per_batch, swiglu, "ffn_up",
                         tm=2048, tn=512, rc=256, rows=1024)


def _ffn_down_kernel(a_ref, h_ref, gt_ref, fg_ref, w_ref, o_ref, *, tn, final):
    j = pl.program_id(1)
    y = jnp.dot(a_ref[...], w_ref[...].astype(BF16), preferred_element_type=F32)
    o_ref[:, pl.ds(pl.multiple_of(j * tn, tn), tn)] = h_ref[...] + gt_ref[...] * y

    if final:
        @pl.when(j == pl.num_programs(1) - 1)
        def _():
            o_ref[...] = _rms(o_ref[...]) * fg_ref[...]


def _ffn_down(a, h1, mod4, fg, w, rows_per_batch, final, tm=1024, tn=256):
    t, d = h1.shape
    f = a.shape[1]
    tpb = rows_per_batch // tm
    return pl.pallas_call(
        functools.partial(_ffn_down_kernel, tn=tn, final=final),
        grid=(t // tm, d // tn),
        in_specs=[pl.BlockSpec((tm, f), lambda i, j: (i, 0)),
                  pl.BlockSpec((tm, tn), lambda i, j: (i, j)),
                  pl.BlockSpec((None, None, 1, tn), lambda i, j: (i // tpb, 5, 0, j)),
                  pl.BlockSpec((1, d), lambda i, j: (0, 0)),
                  pl.BlockSpec((f, tn), lambda i, j: (0, j))],
        out_specs=pl.BlockSpec((tm, d), lambda i, j: (i, 0)),
        out_shape=jax.ShapeDtypeStruct((t, d), F32),
        compiler_params=_params(("parallel", "arbitrary")),
        name="ffn_down",
    )(a, h1, mod4, fg, w)


def kernel(x, c, w_mod, b_mod, norm1_g, w_in, lambda_q1, lambda_k1, lambda_q2, lambda_k2,
           subln_g, lb_logits, gnorm_g, w_att_out, w_rec_out, w_o, norm2_g,
           w_ffn_gate, w_ffn_up, w_ffn_down, final_g):
    nb, s, d = x.shape
    depth = w_mod.shape[0]
    h = x.reshape(nb * s, d)
    for l in range(depth):
        lam_init = 0.8 - 0.6 * math.exp(-0.3 * l)
        mod4 = _mod(c, w_mod[l], b_mod[l]).reshape(nb, 6, 1, d)
        p = _in_proj(h, norm1_g[l].reshape(1, d), mod4, w_in[l], s)
        lam4 = jnp.stack([lambda_q1[l], lambda_k1[l], lambda_q2[l], lambda_k2[l]])
        ya = _attention(p, lam4, subln_g[l].reshape(HEAD_W, 1), nb, s, lam_init)
        yr = _hgrn2(p, lb_logits, gnorm_g[l].reshape(1, REC_HEADS * HEAD_W), nb, s, l)
        ga_col0 = 3 * ATT_HEADS * HEAD_W + 4 * REC_HEADS * HEAD_W
        m = _merge(ya, yr, p, w_att_out[l], w_rec_out[l], ga_col0)
        h1 = _o_proj(m, h, mod4, w_o[l], s)
        a = _ffn_up(h1, norm2_g[l].reshape(1, d), mod4, w_ffn_gate[l], w_ffn_up[l], s)
        h = _ffn_down(a, h1, mod4, final_g.reshape(1, d), w_ffn_down[l], s, l == depth - 1)
    return h.reshape(nb, s, d)
```

```python
import functools
import math

import jax
import jax.numpy as jnp
from jax import lax
from jax.experimental import pallas as pl
from jax.experimental.pallas import tpu as pltpu

F32 = jnp.float32
BF16 = jnp.bfloat16

NORM_EPS = 1e-6
ATT_HEADS = 8
ATT_HALF_DIM = 64
HEAD_W = 128
REC_HEADS = 8
MASK_CHUNK = 64
NEG_BIG = -1e30
ONES_ROWS = 16

VMEM_LIMIT = 60 * 1024 * 1024


def _params(sem, vmem=VMEM_LIMIT):
    return pltpu.CompilerParams(dimension_semantics=sem, vmem_limit_bytes=vmem)


def _rms(x):
    return x * lax.rsqrt(jnp.mean(x * x, axis=-1, keepdims=True) + NORM_EPS)


def _adaln(x, g, sc, sh):
    return _rms(x) * (g * (1.0 + sc)) + sh


def _mod_kernel(ct_ref, w_ref, b_ref, o_ref):
    ct = ct_ref[...]
    cond = ct * jax.nn.sigmoid(ct)
    w = w_ref[...]
    for b in range(ct.shape[1]):
        o_ref[b:b + 1, :] = jnp.sum(w * cond[:, b:b + 1], axis=0, keepdims=True) + b_ref[...]


def _mod(c, w_mod, b_mod, tn=1024):
    nb, d = c.shape
    n = w_mod.shape[1]
    return pl.pallas_call(
        _mod_kernel,
        grid=(n // tn,),
        in_specs=[pl.BlockSpec((d, nb), lambda j: (0, 0)),
                  pl.BlockSpec((d, tn), lambda j: (0, j)),
                  pl.BlockSpec((1, tn), lambda j: (0, j))],
        out_specs=pl.BlockSpec((nb, tn), lambda j: (0, j)),
        out_shape=jax.ShapeDtypeStruct((nb, n), F32),
        compiler_params=_params(("arbitrary",)),
        name="mod",
    )(c.T, w_mod, b_mod.reshape(1, n))


def _adaln_matmul_kernel(x_ref, g_ref, sh_ref, sc_ref, *refs, nw, rc, rows, finish):
    w_refs, o_ref, u_bufs = refs[:nw], refs[nw], refs[nw + 1:]
    i, j = pl.program_id(0), pl.program_id(1)
    nchunks = u_bufs[0].shape[0] // rc

    def normalise_into(u_ref):
        r0 = pl.multiple_of(jnp.minimum(j, nchunks - 1) * rc, rc)
        u_ref[pl.ds(r0, rc), :] = _adaln(x_ref[...], g_ref[...], sc_ref[...],
                                         sh_ref[...]).astype(BF16)

    def multiply_from(u_ref):
        wb = [w[...].astype(BF16) for w in w_refs]
        for r0 in range(0, u_ref.shape[0], rows):
            u = u_ref[r0:r0 + rows, :]
            o_ref[r0:r0 + rows, :] = finish(
                *[jnp.dot(u, w, preferred_element_type=F32) for w in wb]).astype(o_ref.dtype)

    @pl.when(i == 0)
    def _():
        normalise_into(u_bufs[0])

    for parity in range(2):
        @pl.when((i > 0) & (i % 2 == parity))
        def _(parity=parity):
            multiply_from(u_bufs[1 - parity])
            normalise_into(u_bufs[parity])


def _adaln_matmul(x2, g, mod4, mod_row, ws, rows_per_batch, finish, name, tm, tn, rc=256,
                  rows=None):
    t, d = x2.shape
    n = ws[0].shape[1]
    nt, nchunks, tpb = t // tm, tm // rc, rows_per_batch // tm
    assert nchunks <= n // tn
    tile = lambda i: jnp.minimum(i, nt - 1)
    col = lambda i, j: jnp.where(i > 0, j, 0)
    kern = functools.partial(_adaln_matmul_kernel, nw=len(ws), rc=rc, rows=rows or tm,
                             finish=finish)
    return pl.pallas_call(
        kern,
        grid=(nt + 1, n // tn),
        in_specs=[pl.BlockSpec((rc, d), lambda i, j: (tile(i) * nchunks
                                                      + jnp.minimum(j, nchunks - 1), 0)),
                  pl.BlockSpec((1, d), lambda i, j: (0, 0)),
                  pl.BlockSpec((None, None, 1, d), lambda i, j: (tile(i) // tpb, mod_row, 0, 0)),
                  pl.BlockSpec((None, None, 1, d),
                               lambda i, j: (tile(i) // tpb, mod_row + 1, 0, 0))]
                 + [pl.BlockSpec((d, tn), lambda i, j: (0, col(i, j))) for _ in ws],
        out_specs=pl.BlockSpec((tm, tn), lambda i, j: (jnp.maximum(i - 1, 0), col(i, j))),
        out_shape=jax.ShapeDtypeStruct((t, n), BF16),
        scratch_shapes=[pltpu.VMEM((tm, d), BF16), pltpu.VMEM((tm, d), BF16)],
        compiler_params=_params(("arbitrary", "arbitrary")),
        name=name,
    )(x2, g, mod4, mod4, *ws)


def _in_proj(x2, g, mod4, w, rows_per_batch):
    return _adaln_matmul(x2, g, mod4, 0, [w], rows_per_batch, lambda y: y, "in_proj",
                         tm=2048, tn=1024)


def _col_reduce(x, op, groups=8):
    k, n = x.shape
    if k % (groups * 8) == 0:
        x = op(x.reshape(groups, k // groups, n), axis=0)
    return op(x, axis=0, keepdims=True)


def _attn_kernel(lam_ref, sg_ref, q_ref, k_ref, v_ref, o_ref, vt_ref, acc_ref, m_ref,
                 sa_ref, sb_ref, *, tq, sq, tk, lam_init):
    nkb = vt_ref.shape[0]
    nc = tq // sq

    for j in range(nkb):
        vt_ref[j, 0:HEAD_W, :] = v_ref[j * tk:(j + 1) * tk, :].astype(F32).T.astype(BF16)
        vt_ref[j, HEAD_W:, :] = jnp.ones((vt_ref.shape[1] - HEAD_W, tk), BF16)

    lax.fori_loop(0, q_ref.shape[0] // tq,
                  functools.partial(_attn_q_tile, lam_ref, sg_ref, q_ref, k_ref, o_ref, vt_ref,
                                    acc_ref, m_ref, sa_ref, sb_ref, tq, sq, tk, nc, lam_init), 0)


def _attn_q_tile(lam_ref, sg_ref, q_ref, k_ref, o_ref, vt_ref, acc_ref, m_ref, sa_ref, sb_ref,
                 tq, sq, tk, nc, lam_init, qi, carry):
    q_rows = pl.ds(pl.multiple_of(qi * tq, tq), tq)
    q = (q_ref[q_rows, :].astype(F32) * (ATT_HALF_DIM ** -0.5 * math.log2(math.e))).astype(BF16)
    lane = lax.broadcasted_iota(jnp.int32, q.shape, 1)
    zero = jnp.zeros_like(q)
    qm = (jnp.where(lane < ATT_HALF_DIM, q, zero), jnp.where(lane >= ATT_HALF_DIM, q, zero))

    m_ref[...] = jnp.full(m_ref.shape, NEG_BIG, F32)
    acc_ref[...] = jnp.zeros(acc_ref.shape, F32)

    def scores(c, mi, k):
        return lax.dot_general(k, qm[mi][c * sq:(c + 1) * sq, :], (((1,), (1,)), ((), ())),
                               preferred_element_type=F32)

    def update(c, mi, vt, s):
        cs = slice(c * sq, (c + 1) * sq)
        m_old = m_ref[mi, :, cs]
        m_new = jnp.maximum(m_old, _col_reduce(s, jnp.max))
        p = jnp.exp2(s - m_new)
        alpha = jnp.exp2(m_old - m_new)
        acc_ref[mi, :, cs] = alpha * acc_ref[mi, :, cs] + jnp.dot(
            vt, p.astype(BF16), preferred_element_type=F32)
        m_ref[mi, :, cs] = m_new

    assert tq == 2 * tk
    chains = [(c, mi) for c in range(nc) for mi in range(2)]

    def stage(j, src, dst):
        vt = vt_ref[j]
        if dst is not None:
            kn = k_ref[pl.ds(pl.multiple_of((j + 1) * tk, tk), tk), :]
        for idx, (c, mi) in enumerate(chains):
            if dst is not None:
                dst[idx] = scores(c, mi, kn)
            update(c, mi, vt, src[idx])

    @pl.when(qi > 0)
    def _():
        k0 = k_ref[0:tk, :]
        for idx, (c, mi) in enumerate(chains):
            sa_ref[idx] = scores(c, mi, k0)

        def pair(pp, carry):
            stage(2 * pp, sa_ref, sb_ref)
            stage(2 * pp + 1, sb_ref, sa_ref)
            return carry

        lax.fori_loop(0, qi - 1, pair, 0)
        stage(2 * qi - 2, sa_ref, sb_ref)
        stage(2 * qi - 1, sb_ref, None)

    krow = lax.broadcasted_iota(jnp.int32, (sq, sq), 0)
    qcol = lax.broadcasted_iota(jnp.int32, (sq, sq), 1)
    dmask = krow // MASK_CHUNK <= qcol // MASK_CHUNK
    vt_tile = jnp.concatenate([vt_ref[qi * 2], vt_ref[qi * 2 + 1]], axis=1)
    pend = []
    for c in reversed(range(nc)):
        nk = (c + 1) * sq
        k = k_ref[pl.ds(pl.multiple_of(qi * tq, tq), nk), :]
        for mi in range(2):
            s = scores(c, mi, k)
            tail = jnp.where(dmask, s[c * sq:, :], NEG_BIG)
            s = tail if c == 0 else jnp.concatenate([s[:c * sq, :], tail], axis=0)
            pend.append((c, mi, vt_tile[:, :nk], s))
    for it in pend:
        update(*it)

    lp = lam_ref[...]
    lam = (jnp.exp(jnp.sum(lp[0:1] * lp[1:2], axis=-1, keepdims=True))
           - jnp.exp(jnp.sum(lp[2:3] * lp[3:4], axis=-1, keepdims=True)) + lam_init)
    o = (acc_ref[0, 0:HEAD_W, :] / acc_ref[0, HEAD_W:HEAD_W + 1, :]
         - lam * (acc_ref[1, 0:HEAD_W, :] / acc_ref[1, HEAD_W:HEAD_W + 1, :]))
    ms = jnp.mean(o * o, axis=0, keepdims=True)
    y = o * lax.rsqrt(ms + NORM_EPS) * sg_ref[...] * (1.0 - lam_init)
    o_ref[q_rows, :] = y.T.astype(o_ref.dtype)
    return carry


def _attention(p, lam4, sg_col, nb, s, lam_init, tq=1024, sq=256, tk=512):
    h = ATT_HEADS
    kern = functools.partial(_attn_kernel, tq=tq, sq=sq, tk=tk, lam_init=lam_init)
    nchains = 2 * tq // sq
    return pl.pallas_call(
        kern,
        grid=(nb, h),
        in_specs=[pl.BlockSpec((4, ATT_HALF_DIM), lambda b, hh: (0, 0)),
                  pl.BlockSpec((HEAD_W, 1), lambda b, hh: (0, 0)),
                  pl.BlockSpec((s, HEAD_W), lambda b, hh: (b, hh)),
                  pl.BlockSpec((s, HEAD_W), lambda b, hh: (b, h + hh)),
                  pl.BlockSpec((s, HEAD_W), lambda b, hh: (b, 2 * h + hh))],
        out_specs=pl.BlockSpec((s, HEAD_W), lambda b, hh: (b, hh)),
        out_shape=jax.ShapeDtypeStruct((nb * s, h * HEAD_W), BF16),
        scratch_shapes=[pltpu.VMEM((s // tk, HEAD_W + ONES_ROWS, tk), BF16),
                        pltpu.VMEM((2, HEAD_W + ONES_ROWS, tq), F32),
                        pltpu.VMEM((2, 1, tq), F32),
                        pltpu.VMEM((nchains, tk, sq), F32),
                        pltpu.VMEM((nchains, tk, sq), F32)],
        compiler_params=_params(("parallel", "parallel")),
        name="diff_attn",
    )(lam4, sg_col, p, p, p)


def _group_row(a, s, grp=8):
    n, w = a.shape
    a3 = a.reshape(n // grp, grp, w)
    return jnp.broadcast_to(a3[:, s:s + 1, :], (n // grp, grp, w)).reshape(n, w)


def _hgrn2_kernel(lbl_ref, gn_ref, q_ref, f_ref, i_ref, g_ref, o_ref, st_ref, *, ch, layer):
    @pl.when(pl.program_id(1) == 0)
    def _():
        st_ref[...] = jnp.zeros(st_ref.shape, F32)

    w = HEAD_W
    lbl = lbl_ref[...]
    mx = jnp.max(lbl, axis=0, keepdims=True)
    e = jnp.exp(lbl - mx)
    lb = (jnp.sum(e[0:layer + 1], axis=0, keepdims=True)
          / jnp.sum(e, axis=0, keepdims=True))

    nt = (((1,), (1,)), ((), ()))
    row = lax.broadcasted_iota(jnp.int32, (ch, w), 0)
    arow = lax.broadcasted_iota(jnp.int32, (ch, ch), 0)
    acol = lax.broadcasted_iota(jnp.int32, (ch, ch), 1)
    tril = (arow >= acol).astype(BF16)
    on_diag = arow == acol
    levels = []
    hs = ch // 2
    while hs >= 1:
        upper = (row % (2 * hs)) >= hs
        levels.append((hs, upper, jnp.where(upper, 1.0, -1.0),
                       (arow // (2 * hs) == acol // (2 * hs))
                       & ((arow % (2 * hs)) >= hs) & ((acol % (2 * hs)) < hs)))
        hs //= 2

    def midpoint_row(b, hs):
        if hs >= 8:
            return jnp.concatenate(
                [jnp.broadcast_to(b[blk * 2 * hs + hs - 1:blk * 2 * hs + hs, :], (2 * hs, w))
                 for blk in range(ch // (2 * hs))], axis=0)
        if hs == 4:
            return _group_row(b, 3)
        return jnp.where((row % 8) < 4, _group_row(b, 1), _group_row(b, 5))

    chunk_vals = {}

    def chunk_prep(c):
        if c not in chunk_vals:
            rs = slice(c * ch, (c + 1) * ch)
            f_all = lb + (1.0 - lb) * jax.nn.sigmoid(f_ref[rs, :].astype(F32))
            g2 = jnp.log2(f_all)
            g2_hi = g2.astype(BF16)
            g2_lo = (g2 - g2_hi.astype(F32)).astype(BF16)
            b_all = (jnp.dot(tril, g2_hi, preferred_element_type=F32)
                     + jnp.dot(tril, g2_lo, preferred_element_type=F32))
            chunk_vals[c] = (f_all, b_all)
        return chunk_vals[c]

    def operands(c, h):
        rs, cs = slice(c * ch, (c + 1) * ch), slice(h * w, (h + 1) * w)
        f_all, b_all = chunk_prep(c)
        b, f = b_all[:, cs], f_all[:, cs]
        kk = 1.0 - f
        q = q_ref[rs, cs].astype(F32)
        blast = b[ch - 1:ch, :]
        ys = []
        for hs, upper, sign, _ in levels:
            if hs == 1:
                y = jnp.where(upper, q * f, kk)
            else:
                y = jnp.where(upper, q, kk) * jnp.exp2((b - midpoint_row(b, hs)) * sign)
            ys.append(y.astype(BF16))
        return dict(rs=rs, cs=cs, h=h, ys=ys,
                    qin=(q * jnp.exp2(b)).astype(BF16),
                    kout=(kk * jnp.exp2(blast - b)).astype(BF16),
                    keep=jnp.exp2(blast),
                    dsum=jnp.sum(q * kk, axis=-1, keepdims=True))

    def level_matmuls(x):
        st = st_ref[x["h"]]
        x["st"] = st
        x["o"] = lax.dot_general(x["qin"], st.astype(BF16), nt, preferred_element_type=F32)
        x["aa"] = [lax.dot_general(y, y, nt, preferred_element_type=F32) for y in x["ys"]]

    def finish(x):
        rs, cs = x["rs"], x["cs"]
        v = i_ref[rs, cs]
        att = jnp.where(on_diag, x["dsum"], 0.0)
        for (_, _, _, amask), a in zip(levels, x["aa"]):
            att = jnp.where(amask, a, att)
        o = x["o"] + jnp.dot(att.astype(BF16), v, preferred_element_type=F32)
        st_ref[x["h"]] = x["keep"] * x["st"] + lax.dot_general(
            v, x["kout"], (((0,), (0,)), ((), ())), preferred_element_type=F32)
        y = _rms(o) * gn_ref[:, cs] * jax.nn.sigmoid(g_ref[rs, cs].astype(F32))
        o_ref[rs, cs] = y.astype(o_ref.dtype)

    items = [(c, h) for c in range(q_ref.shape[0] // ch) for h in range(REC_HEADS)]
    staged = []
    d1, d2 = 1, 3
    for i in range(len(items) + d2):
        if i < len(items):
            staged.append(operands(*items[i]))
        if d1 <= i < len(items) + d1:
            level_matmuls(staged[i - d1])
        if i >= d2:
            finish(staged[i - d2])


def _hgrn2(p, lb_logits, gn, nb, s, layer, rb=512, ch=128):
    hw = REC_HEADS * HEAD_W
    nr = s // rb
    c0 = 3 * ATT_HEADS * HEAD_W // hw
    kern = functools.partial(_hgrn2_kernel, ch=ch, layer=layer)
    spec = lambda off: pl.BlockSpec((rb, hw), lambda b, i: (b * nr + i, c0 + off))
    return pl.pallas_call(
        kern,
        grid=(nb, nr),
        in_specs=[pl.BlockSpec(lb_logits.shape, lambda b, i: (0, 0)),
                  pl.BlockSpec((1, hw), lambda b, i: (0, 0)),
                  spec(0), spec(1), spec(2), spec(3)],
        out_specs=pl.BlockSpec((rb, hw), lambda b, i: (b * nr + i, 0)),
        out_shape=jax.ShapeDtypeStruct((nb * s, hw), BF16),
        scratch_shapes=[pltpu.VMEM((REC_HEADS, HEAD_W, HEAD_W), F32)],
        compiler_params=_params(("parallel", "arbitrary")),
        name="hgrn2",
    )(lb_logits, gn, p, p, p, p)


def _merge_kernel(ya_ref, yr_ref, *refs, ng):
    ga_refs, gr_refs = refs[:ng], refs[ng:2 * ng]
    wa_ref, wr_ref, o_ref, wab_ref, wrb_ref = refs[2 * ng:]

    @pl.when(pl.program_id(1) == 0)
    def _():
        wab_ref[...] = wa_ref[...].astype(BF16)
        wrb_ref[...] = wr_ref[...].astype(BF16)

    gw = o_ref.shape[1] // ng
    for g in range(ng):
        cols = slice(g * gw, (g + 1) * gw)
        ta = jnp.dot(ya_ref[...], wab_ref[:, cols], preferred_element_type=F32)
        tr = jnp.dot(yr_ref[...], wrb_ref[:, cols], preferred_element_type=F32)
        m = (jax.nn.sigmoid(ga_refs[g][...].astype(F32)) * ta
             + jax.nn.sigmoid(gr_refs[g][...].astype(F32)) * tr)
        o_ref[:, cols] = m.astype(o_ref.dtype)


def _merge(ya, yr, p, wa, wr, ga_col0, tm=512, tn=2048, gw=1024):
    t, ka = ya.shape
    d = wa.shape[1]
    ng = tn // gw
    ga0 = ga_col0 // gw
    gr0 = (ga_col0 + d) // gw
    wmode = pl.Buffered(1) if tn == d else None
    gate = lambda b0, g: pl.BlockSpec((tm, gw), lambda j, i: (i, b0 + j * ng + g))
    return pl.pallas_call(
        functools.partial(_merge_kernel, ng=ng),
        grid=(d // tn, t // tm),
        in_specs=[pl.BlockSpec((tm, ka), lambda j, i: (i, 0)),
                  pl.BlockSpec((tm, ka), lambda j, i: (i, 0))]
                 + [gate(ga0, g) for g in range(ng)] + [gate(gr0, g) for g in range(ng)]
                 + [pl.BlockSpec((ka, tn), lambda j, i: (0, j), pipeline_mode=wmode),
                    pl.BlockSpec((ka, tn), lambda j, i: (0, j), pipeline_mode=wmode)],
        out_specs=pl.BlockSpec((tm, tn), lambda j, i: (i, j)),
        out_shape=jax.ShapeDtypeStruct((t, d), BF16),
        scratch_shapes=[pltpu.VMEM((ka, tn), BF16), pltpu.VMEM((ka, tn), BF16)],
        compiler_params=_params(("parallel", "arbitrary")),
        name="merge",
    )(ya, yr, *([p] * (2 * ng)), wa, wr)


def _oproj_kernel(m_ref, x_ref, gt_ref, w_ref, o_ref, wb_ref):
    @pl.when(pl.program_id(1) == 0)
    def _():
        wb_ref[...] = w_ref[...].astype(BF16)

    y = jnp.dot(m_ref[...], wb_ref[...], preferred_element_type=F32)
    o_ref[...] = x_ref[...] + gt_ref[...] * y


def _o_proj(m, x2, mod4, w, rows_per_batch, tm=512, tn=2048):
    t, d = x2.shape
    tpb = rows_per_batch // tm
    wmode = pl.Buffered(1) if tn == d else None
    return pl.pallas_call(
        _oproj_kernel,
        grid=(d // tn, t // tm),
        in_specs=[pl.BlockSpec((tm, d), lambda j, i: (i, 0)),
                  pl.BlockSpec((tm, tn), lambda j, i: (i, j)),
                  pl.BlockSpec((None, None, 1, tn), lambda j, i: (i // tpb, 2, 0, j)),
                  pl.BlockSpec((d, tn), lambda j, i: (0, j), pipeline_mode=wmode)],
        out_specs=pl.BlockSpec((tm, tn), lambda j, i: (i, j)),
        out_shape=jax.ShapeDtypeStruct((t, d), F32),
        scratch_shapes=[pltpu.VMEM((d, tn), BF16)],
        compiler_params=_params(("parallel", "arbitrary")),
        name="o_proj",
    )(m, x2, mod4, w)


def _ffn_up(h1, g, mod4, wg, wu, rows_per_batch):
    swiglu = lambda gate, up: gate * jax.nn.sigmoid(gate) * up
    return _adaln_matmul(h1, g, mod4, 3, [wg, wu], rows_per_batch, swiglu, "ffn_up",
                         tm=2048, tn=512, rc=256, rows=1024)


def _ffn_down_kernel(a_ref, h_ref, gt_ref, fg_ref, w_ref, o_ref, *, tn, final):
    j = pl.program_id(1)
    y = jnp.dot(a_ref[...], w_ref[...].astype(BF16), preferred_element_type=F32)
    o_ref[:, pl.ds(pl.multiple_of(j * tn, tn), tn)] = h_ref[...] + gt_ref[...] * y

    if final:
        @pl.when(j == pl.num_programs(1) - 1)
        def _():
            o_ref[...] = _rms(o_ref[...]) * fg_ref[...]


def _ffn_down(a, h1, mod4, fg, w, rows_per_batch, final, tm=1024, tn=256):
    t, d = h1.shape
    f = a.shape[1]
    tpb = rows_per_batch // tm
    return pl.pallas_call(
        functools.partial(_ffn_down_kernel, tn=tn, final=final),
        grid=(t // tm, d // tn),
        in_specs=[pl.BlockSpec((tm, f), lambda i, j: (i, 0)),
                  pl.BlockSpec((tm, tn), lambda i, j: (i, j)),
                  pl.BlockSpec((None, None, 1, tn), lambda i, j: (i // tpb, 5, 0, j)),
                  pl.BlockSpec((1, d), lambda i, j: (0, 0)),
                  pl.BlockSpec((f, tn), lambda i, j: (0, j))],
        out_specs=pl.BlockSpec((tm, d), lambda i, j: (i, 0)),
        out_shape=jax.ShapeDtypeStruct((t, d), F32),
        compiler_params=_params(("parallel", "arbitrary")),
        name="ffn_down",
    )(a, h1, mod4, fg, w)


def kernel(x, c, w_mod, b_mod, norm1_g, w_in, lambda_q1, lambda_k1, lambda_q2, lambda_k2,
           subln_g, lb_logits, gnorm_g, w_att_out, w_rec_out, w_o, norm2_g,
           w_ffn_gate, w_ffn_up, w_ffn_down, final_g):
    nb, s, d = x.shape
    depth = w_mod.shape[0]
    h = x.reshape(nb * s, d)
    for l in range(depth):
        lam_init = 0.8 - 0.6 * math.exp(-0.3 * l)
        mod4 = _mod(c, w_mod[l], b_mod[l]).reshape(nb, 6, 1, d)
        p = _in_proj(h, norm1_g[l].reshape(1, d), mod4, w_in[l], s)
        lam4 = jnp.stack([lambda_q1[l], lambda_k1[l], lambda_q2[l], lambda_k2[l]])
        ya = _attention(p, lam4, subln_g[l].reshape(HEAD_W, 1), nb, s, lam_init)
        yr = _hgrn2(p, lb_logits, gnorm_g[l].reshape(1, REC_HEADS * HEAD_W), nb, s, l)
        ga_col0 = 3 * ATT_HEADS * HEAD_W + 4 * REC_HEADS * HEAD_W
        m = _merge(ya, yr, p, w_att_out[l], w_rec_out[l], ga_col0)
        h1 = _o_proj(m, h, mod4, w_o[l], s)
        a = _ffn_up(h1, norm2_g[l].reshape(1, d), mod4, w_ffn_gate[l], w_ffn_up[l], s)
        h = _ffn_down(a, h1, mod4, final_g.reshape(1, d), w_ffn_down[l], s, l == depth - 1)
    return h.reshape(nb, s, d)
```

```python
import functools
import math

import jax
import jax.numpy as jnp
from jax import lax
from jax.experimental import pallas as pl
from jax.experimental.pallas import tpu as pltpu

F32 = jnp.float32
BF16 = jnp.bfloat16

NORM_EPS = 1e-6
ATT_HEADS = 8
ATT_HALF_DIM = 64
HEAD_W = 128
REC_HEADS = 8
MASK_CHUNK = 64
NEG_BIG = -1e30
ONES_ROWS = 16

VMEM_LIMIT = 60 * 1024 * 1024


def _params(sem, vmem=VMEM_LIMIT):
    return pltpu.CompilerParams(dimension_semantics=sem, vmem_limit_bytes=vmem)


def _rms(x):
    return x * lax.rsqrt(jnp.mean(x * x, axis=-1, keepdims=True) + NORM_EPS)


def _adaln(x, g, sc, sh):
    return _rms(x) * (g * (1.0 + sc)) + sh


def _mod_kernel(ct_ref, w_ref, b_ref, o_ref):
    ct = ct_ref[...]
    cond = ct * jax.nn.sigmoid(ct)
    w = w_ref[...]
    for b in range(ct.shape[1]):
        o_ref[b:b + 1, :] = jnp.sum(w * cond[:, b:b + 1], axis=0, keepdims=True) + b_ref[...]


def _mod(c, w_mod, b_mod, tn=1024):
    nb, d = c.shape
    n = w_mod.shape[1]
    return pl.pallas_call(
        _mod_kernel,
        grid=(n // tn,),
        in_specs=[pl.BlockSpec((d, nb), lambda j: (0, 0)),
                  pl.BlockSpec((d, tn), lambda j: (0, j)),
                  pl.BlockSpec((1, tn), lambda j: (0, j))],
        out_specs=pl.BlockSpec((nb, tn), lambda j: (0, j)),
        out_shape=jax.ShapeDtypeStruct((nb, n), F32),
        compiler_params=_params(("arbitrary",)),
        name="mod",
    )(c.T, w_mod, b_mod.reshape(1, n))


def _adaln_matmul_kernel(x_ref, g_ref, sh_ref, sc_ref, *refs, nw, rc, rows, finish):
    w_refs, o_ref, u_bufs = refs[:nw], refs[nw], refs[nw + 1:]
    i, j = pl.program_id(0), pl.program_id(1)
    nchunks = u_bufs[0].shape[0] // rc

    def normalise_into(u_ref):
        r0 = pl.multiple_of(jnp.minimum(j, nchunks - 1) * rc, rc)
        u_ref[pl.ds(r0, rc), :] = _adaln(x_ref[...], g_ref[...], sc_ref[...],
                                         sh_ref[...]).astype(BF16)

    def multiply_from(u_ref):
        wb = [w[...].astype(BF16) for w in w_refs]
        for r0 in range(0, u_ref.shape[0], rows):
            u = u_ref[r0:r0 + rows, :]
            o_ref[r0:r0 + rows, :] = finish(
                *[jnp.dot(u, w, preferred_element_type=F32) for w in wb]).astype(o_ref.dtype)

    @pl.when(i == 0)
    def _():
        normalise_into(u_bufs[0])

    for parity in range(2):
        @pl.when((i > 0) & (i % 2 == parity))
        def _(parity=parity):
            multiply_from(u_bufs[1 - parity])
            normalise_into(u_bufs[parity])


def _adaln_matmul(x2, g, mod4, mod_row, ws, rows_per_batch, finish, name, tm, tn, rc=256,
                  rows=None):
    t, d = x2.shape
    n = ws[0].shape[1]
    nt, nchunks, tpb = t // tm, tm // rc, rows_per_batch // tm
    assert nchunks <= n // tn
    tile = lambda i: jnp.minimum(i, nt - 1)
    col = lambda i, j: jnp.where(i > 0, j, 0)
    kern = functools.partial(_adaln_matmul_kernel, nw=len(ws), rc=rc, rows=rows or tm,
                             finish=finish)
    return pl.pallas_call(
        kern,
        grid=(nt + 1, n // tn),
        in_specs=[pl.BlockSpec((rc, d), lambda i, j: (tile(i) * nchunks
                                                      + jnp.minimum(j, nchunks - 1), 0)),
                  pl.BlockSpec((1, d), lambda i, j: (0, 0)),
                  pl.BlockSpec((None, None, 1, d), lambda i, j: (tile(i) // tpb, mod_row, 0, 0)),
                  pl.BlockSpec((None, None, 1, d),
                               lambda i, j: (tile(i) // tpb, mod_row + 1, 0, 0))]
                 + [pl.BlockSpec((d, tn), lambda i, j: (0, col(i, j))) for _ in ws],
        out_specs=pl.BlockSpec((tm, tn), lambda i, j: (jnp.maximum(i - 1, 0), col(i, j))),
        out_shape=jax.ShapeDtypeStruct((t, n), BF16),
        scratch_shapes=[pltpu.VMEM((tm, d), BF16), pltpu.VMEM((tm, d), BF16)],
        compiler_params=_params(("arbitrary", "arbitrary")),
        name=name,
    )(x2, g, mod4, mod4, *ws)


def _in_proj(x2, g, mod4, w, rows_per_batch):
    return _adaln_matmul(x2, g, mod4, 0, [w], rows_per_batch, lambda y: y, "in_proj",
                         tm=2048, tn=1024)


def _col_reduce(x, op, groups=8):
    k, n = x.shape
    if k % (groups * 8) == 0:
        x = op(x.reshape(groups, k // groups, n), axis=0)
    return op(x, axis=0, keepdims=True)


def _attn_kernel(*refs, ncast, tq, sq, tk, lam_init):
    lam_ref, sg_ref, q_ref, k_ref, v_ref = refs[:5]
    cast_in, o_ref, cast_out = refs[5:5 + ncast], refs[5 + ncast], refs[6 + ncast:6 + 2 * ncast]
    vt_ref, acc_ref, m_ref, sa_ref, sb_ref = refs[6 + 2 * ncast:]
    nkb = vt_ref.shape[0]
    nc = tq // sq

    for w_ref, wb_ref in zip(cast_in, cast_out):
        wb_ref[...] = w_ref[...].astype(BF16)

    for j in range(nkb):
        vt_ref[j, 0:HEAD_W, :] = v_ref[j * tk:(j + 1) * tk, :].astype(F32).T.astype(BF16)
        vt_ref[j, HEAD_W:, :] = jnp.ones((vt_ref.shape[1] - HEAD_W, tk), BF16)

    lax.fori_loop(0, q_ref.shape[0] // tq,
                  functools.partial(_attn_q_tile, lam_ref, sg_ref, q_ref, k_ref, o_ref, vt_ref,
                                    acc_ref, m_ref, sa_ref, sb_ref, tq, sq, tk, nc, lam_init), 0)


def _attn_q_tile(lam_ref, sg_ref, q_ref, k_ref, o_ref, vt_ref, acc_ref, m_ref, sa_ref, sb_ref,
                 tq, sq, tk, nc, lam_init, qi, carry):
    q_rows = pl.ds(pl.multiple_of(qi * tq, tq), tq)
    q = (q_ref[q_rows, :].astype(F32) * (ATT_HALF_DIM ** -0.5 * math.log2(math.e))).astype(BF16)
    lane = lax.broadcasted_iota(jnp.int32, q.shape, 1)
    zero = jnp.zeros_like(q)
    qm = (jnp.where(lane < ATT_HALF_DIM, q, zero), jnp.where(lane >= ATT_HALF_DIM, q, zero))

    m_ref[...] = jnp.full(m_ref.shape, NEG_BIG, F32)
    acc_ref[...] = jnp.zeros(acc_ref.shape, F32)

    def scores(c, mi, k):
        return lax.dot_general(k, qm[mi][c * sq:(c + 1) * sq, :], (((1,), (1,)), ((), ())),
                               preferred_element_type=F32)

    def update(c, mi, vt, s):
        cs = slice(c * sq, (c + 1) * sq)
        m_old = m_ref[mi, :, cs]
        m_new = jnp.maximum(m_old, _col_reduce(s, jnp.max))
        p = jnp.exp2(s - m_new)
        alpha = jnp.exp2(m_old - m_new)
        acc_ref[mi, :, cs] = alpha * acc_ref[mi, :, cs] + jnp.dot(
            vt, p.astype(BF16), preferred_element_type=F32)
        m_ref[mi, :, cs] = m_new

    assert tq == 2 * tk
    chains = [(c, mi) for c in range(nc) for mi in range(2)]

    def stage(j, src, dst):
        vt = vt_ref[j]
        if dst is not None:
            kn = k_ref[pl.ds(pl.multiple_of((j + 1) * tk, tk), tk), :]
        for idx, (c, mi) in enumerate(chains):
            if dst is not None:
                dst[idx] = scores(c, mi, kn)
            update(c, mi, vt, src[idx])

    @pl.when(qi > 0)
    def _():
        k0 = k_ref[0:tk, :]
        for idx, (c, mi) in enumerate(chains):
            sa_ref[idx] = scores(c, mi, k0)

        def pair(pp, carry):
            stage(2 * pp, sa_ref, sb_ref)
            stage(2 * pp + 1, sb_ref, sa_ref)
            return carry

        lax.fori_loop(0, qi - 1, pair, 0)
        stage(2 * qi - 2, sa_ref, sb_ref)
        stage(2 * qi - 1, sb_ref, None)

    krow = lax.broadcasted_iota(jnp.int32, (sq, sq), 0)
    qcol = lax.broadcasted_iota(jnp.int32, (sq, sq), 1)
    dmask = krow // MASK_CHUNK <= qcol // MASK_CHUNK
    vt_tile = jnp.concatenate([vt_ref[qi * 2], vt_ref[qi * 2 + 1]], axis=1)
    pend = []
    for c in reversed(range(nc)):
        nk = (c + 1) * sq
        k = k_ref[pl.ds(pl.multiple_of(qi * tq, tq), nk), :]
        for mi in range(2):
            s = scores(c, mi, k)
            tail = jnp.where(dmask, s[c * sq:, :], NEG_BIG)
            s = tail if c == 0 else jnp.concatenate([s[:c * sq, :], tail], axis=0)
            pend.append((c, mi, vt_tile[:, :nk], s))
    for it in pend:
        update(*it)

    lp = lam_ref[...]
    lam = (jnp.exp(jnp.sum(lp[0:1] * lp[1:2], axis=-1, keepdims=True))
           - jnp.exp(jnp.sum(lp[2:3] * lp[3:4], axis=-1, keepdims=True)) + lam_init)
    o = (acc_ref[0, 0:HEAD_W, :] / acc_ref[0, HEAD_W:HEAD_W + 1, :]
         - lam * (acc_ref[1, 0:HEAD_W, :] / acc_ref[1, HEAD_W:HEAD_W + 1, :]))
    ms = jnp.mean(o * o, axis=0, keepdims=True)
    y = o * lax.rsqrt(ms + NORM_EPS) * sg_ref[...] * (1.0 - lam_init)
    o_ref[q_rows, :] = y.T.astype(o_ref.dtype)
    return carry


def _attention(p, lam4, sg_col, nb, s, lam_init, cast_ws=(), tq=1024, sq=256, tk=512):
    h = ATT_HEADS
    kern = functools.partial(_attn_kernel, ncast=len(cast_ws), tq=tq, sq=sq, tk=tk,
                             lam_init=lam_init)
    nchains = 2 * tq // sq
    slab = lambda w: pl.BlockSpec((w.shape[0] // (nb * h), w.shape[1]),
                                  lambda b, hh: (b * h + hh, 0))
    outs = pl.pallas_call(
        kern,
        grid=(nb, h),
        in_specs=[pl.BlockSpec((4, ATT_HALF_DIM), lambda b, hh: (0, 0)),
                  pl.BlockSpec((HEAD_W, 1), lambda b, hh: (0, 0)),
                  pl.BlockSpec((s, HEAD_W), lambda b, hh: (b, hh)),
                  pl.BlockSpec((s, HEAD_W), lambda b, hh: (b, h + hh)),
                  pl.BlockSpec((s, HEAD_W), lambda b, hh: (b, 2 * h + hh))]
                 + [slab(w) for w in cast_ws],
        out_specs=[pl.BlockSpec((s, HEAD_W), lambda b, hh: (b, hh))] + [slab(w) for w in cast_ws],
        out_shape=[jax.ShapeDtypeStruct((nb * s, h * HEAD_W), BF16)]
                  + [jax.ShapeDtypeStruct(w.shape, BF16) for w in cast_ws],
        scratch_shapes=[pltpu.VMEM((s // tk, HEAD_W + ONES_ROWS, tk), BF16),
                        pltpu.VMEM((2, HEAD_W + ONES_ROWS, tq), F32),
                        pltpu.VMEM((2, 1, tq), F32),
                        pltpu.VMEM((nchains, tk, sq), F32),
                        pltpu.VMEM((nchains, tk, sq), F32)],
        compiler_params=_params(("parallel", "parallel")),
        name="diff_attn",
    )(lam4, sg_col, p, p, p, *cast_ws)
    return outs[0], outs[1:]


def _group_row(a, s, grp=8):
    n, w = a.shape
    a3 = a.reshape(n // grp, grp, w)
    return jnp.broadcast_to(a3[:, s:s + 1, :], (n // grp, grp, w)).reshape(n, w)


def _hgrn2_kernel(lbl_ref, gn_ref, q_ref, f_ref, i_ref, g_ref, o_ref, st_ref, *, ch, layer):
    @pl.when(pl.program_id(1) == 0)
    def _():
        st_ref[...] = jnp.zeros(st_ref.shape, F32)

    w = HEAD_W
    lbl = lbl_ref[...]
    mx = jnp.max(lbl, axis=0, keepdims=True)
    e = jnp.exp(lbl - mx)
    lb = (jnp.sum(e[0:layer + 1], axis=0, keepdims=True)
          / jnp.sum(e, axis=0, keepdims=True))

    nt = (((1,), (1,)), ((), ()))
    row = lax.broadcasted_iota(jnp.int32, (ch, w), 0)
    arow = lax.broadcasted_iota(jnp.int32, (ch, ch), 0)
    acol = lax.broadcasted_iota(jnp.int32, (ch, ch), 1)
    tril = (arow >= acol).astype(BF16)
    on_diag = arow == acol
    levels = []
    hs = ch // 2
    while hs >= 1:
        upper = (row % (2 * hs)) >= hs
        levels.append((hs, upper, jnp.where(upper, 1.0, -1.0),
                       (arow // (2 * hs) == acol // (2 * hs))
                       & ((arow % (2 * hs)) >= hs) & ((acol % (2 * hs)) < hs)))
        hs //= 2

    def midpoint_row(b, hs):
        if hs >= 8:
            return jnp.concatenate(
                [jnp.broadcast_to(b[blk * 2 * hs + hs - 1:blk * 2 * hs + hs, :], (2 * hs, w))
                 for blk in range(ch // (2 * hs))], axis=0)
        if hs == 4:
            return _group_row(b, 3)
        return jnp.where((row % 8) < 4, _group_row(b, 1), _group_row(b, 5))

    chunk_vals = {}

    def chunk_prep(c):
        if c not in chunk_vals:
            rs = slice(c * ch, (c + 1) * ch)
            f_all = lb + (1.0 - lb) * jax.nn.sigmoid(f_ref[rs, :].astype(F32))
            g2 = jnp.log2(f_all)
            g2_hi = g2.astype(BF16)
            g2_lo = (g2 - g2_hi.astype(F32)).astype(BF16)
            b_all = (jnp.dot(tril, g2_hi, preferred_element_type=F32)
                     + jnp.dot(tril, g2_lo, preferred_element_type=F32))
            chunk_vals[c] = (f_all, b_all)
        return chunk_vals[c]

    def operands(c, h):
        rs, cs = slice(c * ch, (c + 1) * ch), slice(h * w, (h + 1) * w)
        f_all, b_all = chunk_prep(c)
        b, f = b_all[:, cs], f_all[:, cs]
        kk = 1.0 - f
        q = q_ref[rs, cs].astype(F32)
        blast = b[ch - 1:ch, :]
        ys = []
        for hs, upper, sign, _ in levels:
            if hs == 1:
                y = jnp.where(upper, q * f, kk)
            else:
                y = jnp.where(upper, q, kk) * jnp.exp2((b - midpoint_row(b, hs)) * sign)
            ys.append(y.astype(BF16))
        return dict(rs=rs, cs=cs, h=h, ys=ys,
                    qin=(q * jnp.exp2(b)).astype(BF16),
                    kout=(kk * jnp.exp2(blast - b)).astype(BF16),
                    keep=jnp.exp2(blast),
                    dsum=jnp.sum(q * kk, axis=-1, keepdims=True))

    def level_matmuls(x):
        st = st_ref[x["h"]]
        x["st"] = st
        x["o"] = lax.dot_general(x["qin"], st.astype(BF16), nt, preferred_element_type=F32)
        x["aa"] = [lax.dot_general(y, y, nt, preferred_element_type=F32) for y in x["ys"]]

    def finish(x):
        rs, cs = x["rs"], x["cs"]
        v = i_ref[rs, cs]
        att = jnp.where(on_diag, x["dsum"], 0.0)
        for (_, _, _, amask), a in zip(levels, x["aa"]):
            att = jnp.where(amask, a, att)
        o = x["o"] + jnp.dot(att.astype(BF16), v, preferred_element_type=F32)
        st_ref[x["h"]] = x["keep"] * x["st"] + lax.dot_general(
            v, x["kout"], (((0,), (0,)), ((), ())), preferred_element_type=F32)
        y = _rms(o) * gn_ref[:, cs] * jax.nn.sigmoid(g_ref[rs, cs].astype(F32))
        o_ref[rs, cs] = y.astype(o_ref.dtype)

    items = [(c, h) for c in range(q_ref.shape[0] // ch) for h in range(REC_HEADS)]
    staged = []
    d1, d2 = 1, 3
    for i in range(len(items) + d2):
        if i < len(items):
            staged.append(operands(*items[i]))
        if d1 <= i < len(items) + d1:
            level_matmuls(staged[i - d1])
        if i >= d2:
            finish(staged[i - d2])


def _hgrn2(p, lb_logits, gn, nb, s, layer, rb=512, ch=128):
    hw = REC_HEADS * HEAD_W
    nr = s // rb
    c0 = 3 * ATT_HEADS * HEAD_W // hw
    kern = functools.partial(_hgrn2_kernel, ch=ch, layer=layer)
    spec = lambda off: pl.BlockSpec((rb, hw), lambda b, i: (b * nr + i, c0 + off))
    return pl.pallas_call(
        kern,
        grid=(nb, nr),
        in_specs=[pl.BlockSpec(lb_logits.shape, lambda b, i: (0, 0)),
                  pl.BlockSpec((1, hw), lambda b, i: (0, 0)),
                  spec(0), spec(1), spec(2), spec(3)],
        out_specs=pl.BlockSpec((rb, hw), lambda b, i: (b * nr + i, 0)),
        out_shape=jax.ShapeDtypeStruct((nb * s, hw), BF16),
        scratch_shapes=[pltpu.VMEM((REC_HEADS, HEAD_W, HEAD_W), F32)],
        compiler_params=_params(("parallel", "arbitrary")),
        name="hgrn2",
    )(lb_logits, gn, p, p, p, p)


def _merge_kernel(ya_ref, yr_ref, *refs, ng):
    ga_refs, gr_refs = refs[:ng], refs[ng:2 * ng]
    wa_ref, wr_ref, o_ref, wab_ref, wrb_ref = refs[2 * ng:]

    @pl.when(pl.program_id(1) == 0)
    def _():
        wab_ref[...] = wa_ref[...].astype(BF16)
        wrb_ref[...] = wr_ref[...].astype(BF16)

    gw = o_ref.shape[1] // ng
    for g in range(ng):
        cols = slice(g * gw, (g + 1) * gw)
        ta = jnp.dot(ya_ref[...], wab_ref[:, cols], preferred_element_type=F32)
        tr = jnp.dot(yr_ref[...], wrb_ref[:, cols], preferred_element_type=F32)
        m = (jax.nn.sigmoid(ga_refs[g][...].astype(F32)) * ta
             + jax.nn.sigmoid(gr_refs[g][...].astype(F32)) * tr)
        o_ref[:, cols] = m.astype(o_ref.dtype)


def _merge(ya, yr, p, wa, wr, ga_col0, tm=512, tn=2048, gw=1024):
    t, ka = ya.shape
    d = wa.shape[1]
    ng = tn // gw
    ga0 = ga_col0 // gw
    gr0 = (ga_col0 + d) // gw
    wmode = pl.Buffered(1) if tn == d else None
    gate = lambda b0, g: pl.BlockSpec((tm, gw), lambda j, i: (i, b0 + j * ng + g))
    return pl.pallas_call(
        functools.partial(_merge_kernel, ng=ng),
        grid=(d // tn, t // tm),
        in_specs=[pl.BlockSpec((tm, ka), lambda j, i: (i, 0)),
                  pl.BlockSpec((tm, ka), lambda j, i: (i, 0))]
                 + [gate(ga0, g) for g in range(ng)] + [gate(gr0, g) for g in range(ng)]
                 + [pl.BlockSpec((ka, tn), lambda j, i: (0, j), pipeline_mode=wmode),
                    pl.BlockSpec((ka, tn), lambda j, i: (0, j), pipeline_mode=wmode)],
        out_specs=pl.BlockSpec((tm, tn), lambda j, i: (i, j)),
        out_shape=jax.ShapeDtypeStruct((t, d), BF16),
        scratch_shapes=[pltpu.VMEM((ka, tn), BF16), pltpu.VMEM((ka, tn), BF16)],
        compiler_params=_params(("parallel", "arbitrary")),
        name="merge",
    )(ya, yr, *([p] * (2 * ng)), wa, wr)


def _oproj_kernel(m_ref, x_ref, gt_ref, w_ref, o_ref, wb_ref):
    @pl.when(pl.program_id(1) == 0)
    def _():
        wb_ref[...] = w_ref[...].astype(BF16)

    y = jnp.dot(m_ref[...], wb_ref[...], preferred_element_type=F32)
    o_ref[...] = x_ref[...] + gt_ref[...] * y


def _o_proj(m, x2, mod4, w, rows_per_batch, tm=512, tn=2048):
    t, d = x2.shape
    tpb = rows_per_batch // tm
    wmode = pl.Buffered(1) if tn == d else None
    return pl.pallas_call(
        _oproj_kernel,
        grid=(d // tn, t // tm),
        in_specs=[pl.BlockSpec((tm, d), lambda j, i: (i, 0)),
                  pl.BlockSpec((tm, tn), lambda j, i: (i, j)),
                  pl.BlockSpec((None, None, 1, tn), lambda j, i: (i // tpb, 2, 0, j)),
                  pl.BlockSpec((d, tn), lambda j, i: (0, j), pipeline_mode=wmode)],
        out_specs=pl.BlockSpec((tm, tn), lambda j, i: (i, j)),
        out_shape=jax.ShapeDtypeStruct((t, d), F32),
        scratch_shapes=[pltpu.VMEM((d, tn), BF16)],
        compiler_params=_params(("parallel", "arbitrary")),
        name="o_proj",
    )(m, x2, mod4, w)


def _ffn_up(h1, g, mod4, wg, wu, rows_per_batch):
    swiglu = lambda gate, up: gate * jax.nn.sigmoid(gate) * up
    return _adaln_matmul(h1, g, mod4, 3, [wg, wu], rows_per_batch, swiglu, "ffn_up",
                         tm=2048, tn=512, rc=256, rows=1024)


def _ffn_down_kernel(a_ref, h_ref, gt_ref, fg_ref, w_ref, o_ref, *, tn, final):
    j = pl.program_id(1)
    y = jnp.dot(a_ref[...], w_ref[...], preferred_element_type=F32)
    o_ref[:, pl.ds(pl.multiple_of(j * tn, tn), tn)] = h_ref[...] + gt_ref[...] * y

    if final:
        @pl.when(j == pl.num_programs(1) - 1)
        def _():
            o_ref[...] = _rms(o_ref[...]) * fg_ref[...]


def _ffn_down(a, h1, mod4, fg, w, rows_per_batch, final, tm=1024, tn=512):
    t, d = h1.shape
    f = a.shape[1]
    tpb = rows_per_batch // tm
    return pl.pallas_call(
        functools.partial(_ffn_down_kernel, tn=tn, final=final),
        grid=(t // tm, d // tn),
        in_specs=[pl.BlockSpec((tm, f), lambda i, j: (i, 0)),
                  pl.BlockSpec((tm, tn), lambda i, j: (i, j)),
                  pl.BlockSpec((None, None, 1, tn), lambda i, j: (i // tpb, 5, 0, j)),
                  pl.BlockSpec((1, d), lambda i, j: (0, 0)),
                  pl.BlockSpec((f, tn), lambda i, j: (0, j))],
        out_specs=pl.BlockSpec((tm, d), lambda i, j: (i, 0)),
        out_shape=jax.ShapeDtypeStruct((t, d), F32),
        compiler_params=_params(("parallel", "arbitrary")),
        name="ffn_down",
    )(a, h1, mod4, fg, w)


def kernel(x, c, w_mod, b_mod, norm1_g, w_in, lambda_q1, lambda_k1, lambda_q2, lambda_k2,
           subln_g, lb_logits, gnorm_g, w_att_out, w_rec_out, w_o, norm2_g,
           w_ffn_gate, w_ffn_up, w_ffn_down, final_g):
    nb, s, d = x.shape
    depth = w_mod.shape[0]
    h = x.reshape(nb * s, d)
    for l in range(depth):
        lam_init = 0.8 - 0.6 * math.exp(-0.3 * l)
        mod4 = _mod(c, w_mod[l], b_mod[l]).reshape(nb, 6, 1, d)
        p = _in_proj(h, norm1_g[l].reshape(1, d), mod4, w_in[l], s)
        lam4 = jnp.stack([lambda_q1[l], lambda_k1[l], lambda_q2[l], lambda_k2[l]])
        ya, (wd_bf16,) = _attention(p, lam4, subln_g[l].reshape(HEAD_W, 1), nb, s, lam_init,
                                    cast_ws=(w_ffn_down[l],))
        yr = _hgrn2(p, lb_logits, gnorm_g[l].reshape(1, REC_HEADS * HEAD_W), nb, s, l)
        ga_col0 = 3 * ATT_HEADS * HEAD_W + 4 * REC_HEADS * HEAD_W
        m = _merge(ya, yr, p, w_att_out[l], w_rec_out[l], ga_col0)
        h1 = _o_proj(m, h, mod4, w_o[l], s)
        a = _ffn_up(h1, norm2_g[l].reshape(1, d), mod4, w_ffn_gate[l], w_ffn_up[l], s)
        h = _ffn_down(a, h1, mod4, final_g.reshape(1, d), wd_bf16, s, l == depth - 1)
    return h.reshape(nb, s, d)
```

```python
import functools
import math

import jax
import jax.numpy as jnp
from jax import lax
from jax.experimental import pallas as pl
from jax.experimental.pallas import tpu as pltpu

F32 = jnp.float32
BF16 = jnp.bfloat16

NORM_EPS = 1e-6
ATT_HEADS = 8
ATT_HALF_DIM = 64
HEAD_W = 128
REC_HEADS = 8
MASK_CHUNK = 64
NEG_BIG = -1e30
BF16_SUBLANES = 16
ONES_ROWS = BF16_SUBLANES

VMEM_LIMIT = 60 * 1024 * 1024


def _params(sem, vmem=VMEM_LIMIT):
    return pltpu.CompilerParams(dimension_semantics=sem, vmem_limit_bytes=vmem)


def _rms(x):
    return x * lax.rsqrt(jnp.mean(x * x, axis=-1, keepdims=True) + NORM_EPS)


def _adaln(x, g, sc, sh):
    return _rms(x) * (g * (1.0 + sc)) + sh


def _mod_kernel(ct_ref, w_ref, b_ref, o_ref):
    ct = ct_ref[...]
    cond = ct * jax.nn.sigmoid(ct)
    w = w_ref[...]
    for b in range(ct.shape[1]):
        o_ref[b:b + 1, :] = jnp.sum(w * cond[:, b:b + 1], axis=0, keepdims=True) + b_ref[...]


def _mod(c, w_mod, b_mod, tn=1024):
    nb, d = c.shape
    n = w_mod.shape[1]
    return pl.pallas_call(
        _mod_kernel,
        grid=(n // tn,),
        in_specs=[pl.BlockSpec((d, nb), lambda j: (0, 0)),
                  pl.BlockSpec((d, tn), lambda j: (0, j)),
                  pl.BlockSpec((1, tn), lambda j: (0, j))],
        out_specs=pl.BlockSpec((nb, tn), lambda j: (0, j)),
        out_shape=jax.ShapeDtypeStruct((nb, n), F32),
        compiler_params=_params(("arbitrary",)),
        name="mod",
    )(c.T, w_mod, b_mod.reshape(1, n))


def _adaln_matmul_kernel(x_ref, g_ref, sh_ref, sc_ref, *refs, nw, ncast, rc, rows, finish):
    w_refs, cast_in = refs[:nw], refs[nw:nw + ncast]
    o_ref, cast_out = refs[nw + ncast], refs[nw + ncast + 1:nw + 2 * ncast + 1]
    u_bufs = refs[nw + 2 * ncast + 1:]
    i, j = pl.program_id(0), pl.program_id(1)
    nchunks = u_bufs[0].shape[0] // rc

    def normalise_into(u_ref):
        r0 = pl.multiple_of(jnp.minimum(j, nchunks - 1) * rc, rc)
        u_ref[pl.ds(r0, rc), :] = _adaln(x_ref[...], g_ref[...], sc_ref[...],
                                         sh_ref[...]).astype(BF16)

    def multiply_from(u_ref):
        wb = [w[...].astype(BF16) for w in w_refs]
        for r0 in range(0, u_ref.shape[0], rows):
            u = u_ref[r0:r0 + rows, :]
            o_ref[r0:r0 + rows, :] = finish(
                *[jnp.dot(u, w, preferred_element_type=F32) for w in wb]).astype(o_ref.dtype)

    @pl.when(i == 0)
    def _():
        normalise_into(u_bufs[0])

    for parity in range(2):
        @pl.when((i > 0) & (i % 2 == parity))
        def _(parity=parity):
            multiply_from(u_bufs[1 - parity])
            normalise_into(u_bufs[parity])
            for w_ref, wb_ref in zip(cast_in, cast_out):
                wb_ref[...] = w_ref[...].astype(BF16)


def _adaln_matmul(x2, g, mod4, mod_row, ws, rows_per_batch, finish, name, tm, tn, rc=256,
                  rows=None, cast_ws=()):
    t, d = x2.shape
    n = ws[0].shape[1]
    nt, nj, nchunks, tpb = t // tm, n // tn, tm // rc, rows_per_batch // tm
    assert nchunks <= nj
    tile = lambda i: jnp.minimum(i, nt - 1)
    col = lambda i, j: jnp.where(i > 0, j, 0)
    def slab(w):
        k = w.shape[0]
        ns = max(c for c in range(1, nt * nj + 1) if k % (c * BF16_SUBLANES) == 0)
        return pl.BlockSpec((k // ns, w.shape[1]),
                            lambda i, j: (jnp.clip((i - 1) * nj + j, 0, ns - 1), 0))

    kern = functools.partial(_adaln_matmul_kernel, nw=len(ws), ncast=len(cast_ws), rc=rc,
                             rows=rows or tm, finish=finish)
    outs = pl.pallas_call(
        kern,
        grid=(nt + 1, nj),
        in_specs=[pl.BlockSpec((rc, d), lambda i, j: (tile(i) * nchunks
                                                      + jnp.minimum(j, nchunks - 1), 0)),
                  pl.BlockSpec((1, d), lambda i, j: (0, 0)),
                  pl.BlockSpec((None, None, 1, d), lambda i, j: (tile(i) // tpb, mod_row, 0, 0)),
                  pl.BlockSpec((None, None, 1, d),
                               lambda i, j: (tile(i) // tpb, mod_row + 1, 0, 0))]
                 + [pl.BlockSpec((d, tn), lambda i, j: (0, col(i, j))) for _ in ws]
                 + [slab(w) for w in cast_ws],
        out_specs=[pl.BlockSpec((tm, tn), lambda i, j: (jnp.maximum(i - 1, 0), col(i, j)))]
                  + [slab(w) for w in cast_ws],
        out_shape=[jax.ShapeDtypeStruct((t, n), BF16)]
                  + [jax.ShapeDtypeStruct(w.shape, BF16) for w in cast_ws],
        scratch_shapes=[pltpu.VMEM((tm, d), BF16), pltpu.VMEM((tm, d), BF16)],
        compiler_params=_params(("arbitrary", "arbitrary")),
        name=name,
    )(x2, g, mod4, mod4, *ws, *cast_ws)
    return outs[0], outs[1:]


def _in_proj(x2, g, mod4, w, rows_per_batch, cast_ws):
    return _adaln_matmul(x2, g, mod4, 0, [w], rows_per_batch, lambda y: y, "in_proj",
                         tm=2048, tn=1024, cast_ws=cast_ws)


def _col_reduce(x, op, groups=8):
    k, n = x.shape
    if k % (groups * 8) == 0:
        x = op(x.reshape(groups, k // groups, n), axis=0)
    return op(x, axis=0, keepdims=True)


def _attn_kernel(*refs, ncast, tq, sq, tk, lam_init):
    lam_ref, sg_ref, q_ref, k_ref, v_ref = refs[:5]
    cast_in, o_ref, cast_out = refs[5:5 + ncast], refs[5 + ncast], refs[6 + ncast:6 + 2 * ncast]
    vt_ref, acc_ref, m_ref, sa_ref, sb_ref = refs[6 + 2 * ncast:]
    nkb = vt_ref.shape[0]
    nc = tq // sq

    for w_ref, wb_ref in zip(cast_in, cast_out):
        wb_ref[...] = w_ref[...].astype(BF16)

    for j in range(nkb):
        vt_ref[j, 0:HEAD_W, :] = v_ref[j * tk:(j + 1) * tk, :].astype(F32).T.astype(BF16)
        vt_ref[j, HEAD_W:, :] = jnp.ones((vt_ref.shape[1] - HEAD_W, tk), BF16)

    lax.fori_loop(0, q_ref.shape[0] // tq,
                  functools.partial(_attn_q_tile, lam_ref, sg_ref, q_ref, k_ref, o_ref, vt_ref,
                                    acc_ref, m_ref, sa_ref, sb_ref, tq, sq, tk, nc, lam_init), 0)


def _attn_q_tile(lam_ref, sg_ref, q_ref, k_ref, o_ref, vt_ref, acc_ref, m_ref, sa_ref, sb_ref,
                 tq, sq, tk, nc, lam_init, qi, carry):
    q_rows = pl.ds(pl.multiple_of(qi * tq, tq), tq)
    q = (q_ref[q_rows, :].astype(F32) * (ATT_HALF_DIM ** -0.5 * math.log2(math.e))).astype(BF16)
    lane = lax.broadcasted_iota(jnp.int32, q.shape, 1)
    zero = jnp.zeros_like(q)
    qm = (jnp.where(lane < ATT_HALF_DIM, q, zero), jnp.where(lane >= ATT_HALF_DIM, q, zero))

    m_ref[...] = jnp.full(m_ref.shape, NEG_BIG, F32)
    acc_ref[...] = jnp.zeros(acc_ref.shape, F32)

    def scores(c, mi, k):
        return lax.dot_general(k, qm[mi][c * sq:(c + 1) * sq, :], (((1,), (1,)), ((), ())),
                               preferred_element_type=F32)

    def update(c, mi, vt, s):
        cs = slice(c * sq, (c + 1) * sq)
        m_old = m_ref[mi, :, cs]
        m_new = jnp.maximum(m_old, _col_reduce(s, jnp.max))
        p = jnp.exp2(s - m_new)
        alpha = jnp.exp2(m_old - m_new)
        acc_ref[mi, :, cs] = alpha * acc_ref[mi, :, cs] + jnp.dot(
            vt, p.astype(BF16), preferred_element_type=F32)
        m_ref[mi, :, cs] = m_new

    assert tq == 2 * tk
    chains = [(c, mi) for c in range(nc) for mi in range(2)]

    def stage(j, src, dst):
        vt = vt_ref[j]
        if dst is not None:
            kn = k_ref[pl.ds(pl.multiple_of((j + 1) * tk, tk), tk), :]
        for idx, (c, mi) in enumerate(chains):
            if dst is not None:
                dst[idx] = scores(c, mi, kn)
            update(c, mi, vt, src[idx])

    @pl.when(qi > 0)
    def _():
        k0 = k_ref[0:tk, :]
        for idx, (c, mi) in enumerate(chains):
            sa_ref[idx] = scores(c, mi, k0)

        def pair(pp, carry):
            stage(2 * pp, sa_ref, sb_ref)
            stage(2 * pp + 1, sb_ref, sa_ref)
            return carry

        lax.fori_loop(0, qi - 1, pair, 0)
        stage(2 * qi - 2, sa_ref, sb_ref)
        stage(2 * qi - 1, sb_ref, None)

    krow = lax.broadcasted_iota(jnp.int32, (sq, sq), 0)
    qcol = lax.broadcasted_iota(jnp.int32, (sq, sq), 1)
    dmask = krow // MASK_CHUNK <= qcol // MASK_CHUNK
    vt_tile = jnp.concatenate([vt_ref[qi * 2], vt_ref[qi * 2 + 1]], axis=1)
    pend = []
    for c in reversed(range(nc)):
        nk = (c + 1) * sq
        k = k_ref[pl.ds(pl.multiple_of(qi * tq, tq), nk), :]
        for mi in range(2):
            s = scores(c, mi, k)
            tail = jnp.where(dmask, s[c * sq:, :], NEG_BIG)
            s = tail if c == 0 else jnp.concatenate([s[:c * sq, :], tail], axis=0)
            pend.append((c, mi, vt_tile[:, :nk], s))
    for it in pend:
        update(*it)

    lp = lam_ref[...]
    lam = (jnp.exp(jnp.sum(lp[0:1] * lp[1:2], axis=-1, keepdims=True))
           - jnp.exp(jnp.sum(lp[2:3] * lp[3:4], axis=-1, keepdims=True)) + lam_init)
    o = (acc_ref[0, 0:HEAD_W, :] / acc_ref[0, HEAD_W:HEAD_W + 1, :]
         - lam * (acc_ref[1, 0:HEAD_W, :] / acc_ref[1, HEAD_W:HEAD_W + 1, :]))
    ms = jnp.mean(o * o, axis=0, keepdims=True)
    y = o * lax.rsqrt(ms + NORM_EPS) * sg_ref[...] * (1.0 - lam_init)
    o_ref[q_rows, :] = y.T.astype(o_ref.dtype)
    return carry


def _attention(p, lam4, sg_col, nb, s, lam_init, cast_ws=(), tq=1024, sq=256, tk=512):
    h = ATT_HEADS
    kern = functools.partial(_attn_kernel, ncast=len(cast_ws), tq=tq, sq=sq, tk=tk,
                             lam_init=lam_init)
    nchains = 2 * tq // sq
    slab = lambda w: pl.BlockSpec((w.shape[0] // (nb * h), w.shape[1]),
                                  lambda b, hh: (b * h + hh, 0))
    outs = pl.pallas_call(
        kern,
        grid=(nb, h),
        in_specs=[pl.BlockSpec((4, ATT_HALF_DIM), lambda b, hh: (0, 0)),
                  pl.BlockSpec((HEAD_W, 1), lambda b, hh: (0, 0)),
                  pl.BlockSpec((s, HEAD_W), lambda b, hh: (b, hh)),
                  pl.BlockSpec((s, HEAD_W), lambda b, hh: (b, h + hh)),
                  pl.BlockSpec((s, HEAD_W), lambda b, hh: (b, 2 * h + hh))]
                 + [slab(w) for w in cast_ws],
        out_specs=[pl.BlockSpec((s, HEAD_W), lambda b, hh: (b, hh))] + [slab(w) for w in cast_ws],
        out_shape=[jax.ShapeDtypeStruct((nb * s, h * HEAD_W), BF16)]
                  + [jax.ShapeDtypeStruct(w.shape, BF16) for w in cast_ws],
        scratch_shapes=[pltpu.VMEM((s // tk, HEAD_W + ONES_ROWS, tk), BF16),
                        pltpu.VMEM((2, HEAD_W + ONES_ROWS, tq), F32),
                        pltpu.VMEM((2, 1, tq), F32),
                        pltpu.VMEM((nchains, tk, sq), F32),
                        pltpu.VMEM((nchains, tk, sq), F32)],
        compiler_params=_params(("parallel", "parallel")),
        name="diff_attn",
    )(lam4, sg_col, p, p, p, *cast_ws)
    return outs[0], outs[1:]


def _group_row(a, s, grp=8):
    n, w = a.shape
    a3 = a.reshape(n // grp, grp, w)
    return jnp.broadcast_to(a3[:, s:s + 1, :], (n // grp, grp, w)).reshape(n, w)


def _hgrn2_kernel(lbl_ref, gn_ref, q_ref, f_ref, i_ref, g_ref, o_ref, st_ref, *, ch, layer):
    @pl.when(pl.program_id(1) == 0)
    def _():
        st_ref[...] = jnp.zeros(st_ref.shape, F32)

    w = HEAD_W
    lbl = lbl_ref[...]
    mx = jnp.max(lbl, axis=0, keepdims=True)
    e = jnp.exp(lbl - mx)
    lb = (jnp.sum(e[0:layer + 1], axis=0, keepdims=True)
          / jnp.sum(e, axis=0, keepdims=True))

    nt = (((1,), (1,)), ((), ()))
    row = lax.broadcasted_iota(jnp.int32, (ch, w), 0)
    arow = lax.broadcasted_iota(jnp.int32, (ch, ch), 0)
    acol = lax.broadcasted_iota(jnp.int32, (ch, ch), 1)
    tril = (arow >= acol).astype(BF16)
    on_diag = arow == acol
    levels = []
    hs = ch // 2
    while hs >= 1:
        upper = (row % (2 * hs)) >= hs
        levels.append((hs, upper, jnp.where(upper, 1.0, -1.0),
                       (arow // (2 * hs) == acol // (2 * hs))
                       & ((arow % (2 * hs)) >= hs) & ((acol % (2 * hs)) < hs)))
        hs //= 2

    def midpoint_row(b, hs):
        if hs >= 8:
            return jnp.concatenate(
                [jnp.broadcast_to(b[blk * 2 * hs + hs - 1:blk * 2 * hs + hs, :], (2 * hs, w))
                 for blk in range(ch // (2 * hs))], axis=0)
        if hs == 4:
            return _group_row(b, 3)
        return jnp.where((row % 8) < 4, _group_row(b, 1), _group_row(b, 5))

    chunk_vals = {}

    def chunk_prep(c):
        if c not in chunk_vals:
            rs = slice(c * ch, (c + 1) * ch)
            f_all = lb + (1.0 - lb) * jax.nn.sigmoid(f_ref[rs, :].astype(F32))
            g2 = jnp.log2(f_all)
            g2_hi = g2.astype(BF16)
            g2_lo = (g2 - g2_hi.astype(F32)).astype(BF16)
            b_all = (jnp.dot(tril, g2_hi, preferred_element_type=F32)
                     + jnp.dot(tril, g2_lo, preferred_element_type=F32))
            chunk_vals[c] = (f_all, b_all)
        return chunk_vals[c]

    def operands(c, h):
        rs, cs = slice(c * ch, (c + 1) * ch), slice(h * w, (h + 1) * w)
        f_all, b_all = chunk_prep(c)
        b, f = b_all[:, cs], f_all[:, cs]
        kk = 1.0 - f
        q = q_ref[rs, cs].astype(F32)
        blast = b[ch - 1:ch, :]
        ys = []
        for hs, upper, sign, _ in levels:
            if hs == 1:
                y = jnp.where(upper, q * f, kk)
            else:
                y = jnp.where(upper, q, kk) * jnp.exp2((b - midpoint_row(b, hs)) * sign)
            ys.append(y.astype(BF16))
        return dict(rs=rs, cs=cs, h=h, ys=ys,
                    qin=(q * jnp.exp2(b)).astype(BF16),
                    kout=(kk * jnp.exp2(blast - b)).astype(BF16),
                    keep=jnp.exp2(blast),
                    dsum=jnp.sum(q * kk, axis=-1, keepdims=True))

    def level_matmuls(x):
        st = st_ref[x["h"]]
        x["st"] = st
        x["o"] = lax.dot_general(x["qin"], st.astype(BF16), nt, preferred_element_type=F32)
        x["aa"] = [lax.dot_general(y, y, nt, preferred_element_type=F32) for y in x["ys"]]

    def finish(x):
        rs, cs = x["rs"], x["cs"]
        v = i_ref[rs, cs]
        att = jnp.where(on_diag, x["dsum"], 0.0)
        for (_, _, _, amask), a in zip(levels, x["aa"]):
            att = jnp.where(amask, a, att)
        o = x["o"] + jnp.dot(att.astype(BF16), v, preferred_element_type=F32)
        st_ref[x["h"]] = x["keep"] * x["st"] + lax.dot_general(
            v, x["kout"], (((0,), (0,)), ((), ())), preferred_element_type=F32)
        y = _rms(o) * gn_ref[:, cs] * jax.nn.sigmoid(g_ref[rs, cs].astype(F32))
        o_ref[rs, cs] = y.astype(o_ref.dtype)

    items = [(c, h) for c in range(q_ref.shape[0] // ch) for h in range(REC_HEADS)]
    staged = []
    d1, d2 = 1, 3
    for i in range(len(items) + d2):
        if i < len(items):
            staged.append(operands(*items[i]))
        if d1 <= i < len(items) + d1:
            level_matmuls(staged[i - d1])
        if i >= d2:
            finish(staged[i - d2])


def _hgrn2(p, lb_logits, gn, nb, s, layer, rb=512, ch=128):
    hw = REC_HEADS * HEAD_W
    nr = s // rb
    c0 = 3 * ATT_HEADS * HEAD_W // hw
    kern = functools.partial(_hgrn2_kernel, ch=ch, layer=layer)
    spec = lambda off: pl.BlockSpec((rb, hw), lambda b, i: (b * nr + i, c0 + off))
    return pl.pallas_call(
        kern,
        grid=(nb, nr),
        in_specs=[pl.BlockSpec(lb_logits.shape, lambda b, i: (0, 0)),
                  pl.BlockSpec((1, hw), lambda b, i: (0, 0)),
                  spec(0), spec(1), spec(2), spec(3)],
        out_specs=pl.BlockSpec((rb, hw), lambda b, i: (b * nr + i, 0)),
        out_shape=jax.ShapeDtypeStruct((nb * s, hw), BF16),
        scratch_shapes=[pltpu.VMEM((REC_HEADS, HEAD_W, HEAD_W), F32)],
        compiler_params=_params(("parallel", "arbitrary")),
        name="hgrn2",
    )(lb_logits, gn, p, p, p, p)


def _merge_oproj_kernel(ya_ref, yr_ref, *refs, ng):
    ga_refs, gr_refs = refs[:ng], refs[ng:2 * ng]
    x_ref, gt_ref, wa_ref, wr_ref, wo_ref, o_ref = refs[2 * ng:]
    gw = wa_ref.shape[1] // ng
    parts = []
    for g in range(ng):
        cols = slice(g * gw, (g + 1) * gw)
        ta = jnp.dot(ya_ref[...], wa_ref[:, cols], preferred_element_type=F32)
        tr = jnp.dot(yr_ref[...], wr_ref[:, cols], preferred_element_type=F32)
        parts.append((jax.nn.sigmoid(ga_refs[g][...].astype(F32)) * ta
                      + jax.nn.sigmoid(gr_refs[g][...].astype(F32)) * tr).astype(BF16))
    m = jnp.concatenate(parts, axis=1)
    o_ref[...] = x_ref[...] + gt_ref[...] * jnp.dot(m, wo_ref[...], preferred_element_type=F32)


def _merge_oproj(ya, yr, p, x2, mod4, wa, wr, wo, ga_col0, rows_per_batch, tm=512, gw=1024):
    t, ka = ya.shape
    d = wa.shape[1]
    ng = d // gw
    ga0 = ga_col0 // gw
    gr0 = (ga_col0 + d) // gw
    tpb = rows_per_batch // tm
    gate = lambda b0, g: pl.BlockSpec((tm, gw), lambda i: (i, b0 + g))
    resident = lambda w: pl.BlockSpec(w.shape, lambda i: (0, 0), pipeline_mode=pl.Buffered(1))
    return pl.pallas_call(
        functools.partial(_merge_oproj_kernel, ng=ng),
        grid=(t // tm,),
        in_specs=[pl.BlockSpec((tm, ka), lambda i: (i, 0)),
                  pl.BlockSpec((tm, ka), lambda i: (i, 0))]
                 + [gate(ga0, g) for g in range(ng)] + [gate(gr0, g) for g in range(ng)]
                 + [pl.BlockSpec((tm, d), lambda i: (i, 0)),
                    pl.BlockSpec((None, None, 1, d), lambda i: (i // tpb, 2, 0, 0)),
                    resident(wa), resident(wr), resident(wo)],
        out_specs=pl.BlockSpec((tm, d), lambda i: (i, 0)),
        out_shape=jax.ShapeDtypeStruct((t, d), F32),
        compiler_params=_params(("parallel",)),
        name="merge_oproj",
    )(ya, yr, *([p] * (2 * ng)), x2, mod4, wa, wr, wo)


def _ffn_up(h1, g, mod4, wg, wu, rows_per_batch, cast_ws):
    swiglu = lambda gate, up: gate * jax.nn.sigmoid(gate) * up
    return _adaln_matmul(h1, g, mod4, 3, [wg, wu], rows_per_batch, swiglu, "ffn_up",
                         tm=2048, tn=512, rc=256, rows=1024, cast_ws=cast_ws)


def _ffn_down_kernel(a_ref, h_ref, gt_ref, fg_ref, w_ref, o_ref, *, tn, final):
    j = pl.program_id(1)
    y = jnp.dot(a_ref[...], w_ref[...], preferred_element_type=F32)
    o_ref[:, pl.ds(pl.multiple_of(j * tn, tn), tn)] = h_ref[...] + gt_ref[...] * y

    if final:
        @pl.when(j == pl.num_programs(1) - 1)
        def _():
            o_ref[...] = _rms(o_ref[...]) * fg_ref[...]


def _ffn_down(a, h1, mod4, fg, w, rows_per_batch, final, tm=1024, tn=512):
    t, d = h1.shape
    f = a.shape[1]
    tpb = rows_per_batch // tm
    return pl.pallas_call(
        functools.partial(_ffn_down_kernel, tn=tn, final=final),
        grid=(t // tm, d // tn),
        in_specs=[pl.BlockSpec((tm, f), lambda i, j: (i, 0)),
                  pl.BlockSpec((tm, tn), lambda i, j: (i, j)),
                  pl.BlockSpec((None, None, 1, tn), lambda i, j: (i // tpb, 5, 0, j)),
                  pl.BlockSpec((1, d), lambda i, j: (0, 0)),
                  pl.BlockSpec((f, tn), lambda i, j: (0, j))],
        out_specs=pl.BlockSpec((tm, d), lambda i, j: (i, 0)),
        out_shape=jax.ShapeDtypeStruct((t, d), F32),
        compiler_params=_params(("parallel", "arbitrary")),
        name="ffn_down",
    )(a, h1, mod4, fg, w)


def kernel(x, c, w_mod, b_mod, norm1_g, w_in, lambda_q1, lambda_k1, lambda_q2, lambda_k2,
           subln_g, lb_logits, gnorm_g, w_att_out, w_rec_out, w_o, norm2_g,
           w_ffn_gate, w_ffn_up, w_ffn_down, final_g):
    nb, s, d = x.shape
    depth = w_mod.shape[0]
    h = x.reshape(nb * s, d)
    for l in range(depth):
        lam_init = 0.8 - 0.6 * math.exp(-0.3 * l)
        mod4 = _mod(c, w_mod[l], b_mod[l]).reshape(nb, 6, 1, d)
        p, (wa, wr, wo) = _in_proj(h, norm1_g[l].reshape(1, d), mod4, w_in[l], s,
                                   cast_ws=(w_att_out[l], w_rec_out[l], w_o[l]))
        lam4 = jnp.stack([lambda_q1[l], lambda_k1[l], lambda_q2[l], lambda_k2[l]])
        ya, _ = _attention(p, lam4, subln_g[l].reshape(HEAD_W, 1), nb, s, lam_init)
        yr = _hgrn2(p, lb_logits, gnorm_g[l].reshape(1, REC_HEADS * HEAD_W), nb, s, l)
        ga_col0 = 3 * ATT_HEADS * HEAD_W + 4 * REC_HEADS * HEAD_W
        h1 = _merge_oproj(ya, yr, p, h, mod4, wa, wr, wo, ga_col0, s)
        a, (wd,) = _ffn_up(h1, norm2_g[l].reshape(1, d), mod4, w_ffn_gate[l], w_ffn_up[l], s,
                           cast_ws=(w_ffn_down[l],))
        h = _ffn_down(a, h1, mod4, final_g.reshape(1, d), wd, s, l == depth - 1)
    return h.reshape(nb, s, d)
```

```python
import functools
import math

import jax
import jax.numpy as jnp
from jax import lax
from jax.experimental import pallas as pl
from jax.experimental.pallas import tpu as pltpu

F32 = jnp.float32
BF16 = jnp.bfloat16

NORM_EPS = 1e-6
ATT_HEADS = 8
ATT_HALF_DIM = 64
HEAD_W = 128
REC_HEADS = 8
MASK_CHUNK = 64
NEG_BIG = -1e30
BF16_SUBLANES = 16
ONES_ROWS = BF16_SUBLANES

VMEM_LIMIT = 60 * 1024 * 1024


def _params(sem, vmem=VMEM_LIMIT):
    return pltpu.CompilerParams(dimension_semantics=sem, vmem_limit_bytes=vmem)


def _rms(x):
    return x * lax.rsqrt(jnp.mean(x * x, axis=-1, keepdims=True) + NORM_EPS)


def _adaln(x, g, sc, sh):
    return _rms(x) * (g * (1.0 + sc)) + sh


def _mod_kernel(ct_ref, w_ref, b_ref, o_ref):
    ct = ct_ref[...]
    cond = ct * jax.nn.sigmoid(ct)
    w = w_ref[...]
    for b in range(ct.shape[1]):
        o_ref[b:b + 1, :] = jnp.sum(w * cond[:, b:b + 1], axis=0, keepdims=True) + b_ref[...]


def _mod(ct, w_mod, b_row, n, tn=1024):
    d, nb = ct.shape
    return pl.pallas_call(
        _mod_kernel,
        grid=(n // tn,),
        in_specs=[pl.BlockSpec((d, nb), lambda j: (0, 0)),
                  pl.BlockSpec((d, tn), lambda j: (0, j)),
                  pl.BlockSpec((1, tn), lambda j: (0, j))],
        out_specs=pl.BlockSpec((nb, tn), lambda j: (0, j)),
        out_shape=jax.ShapeDtypeStruct((nb, n), F32),
        compiler_params=_params(("arbitrary",)),
        name="mod",
    )(ct, w_mod, b_row)


def _adaln_matmul_kernel(x_ref, g_ref, sh_ref, sc_ref, *refs, nw, ncast, rc, rows, finish):
    w_refs, cast_in = refs[:nw], refs[nw:nw + ncast]
    o_ref, cast_out = refs[nw + ncast], refs[nw + ncast + 1:nw + 2 * ncast + 1]
    u_bufs = refs[nw + 2 * ncast + 1:]
    i, j = pl.program_id(0), pl.program_id(1)
    nchunks = u_bufs[0].shape[0] // rc

    def normalise_into(u_ref):
        r0 = pl.multiple_of(jnp.minimum(j, nchunks - 1) * rc, rc)
        u_ref[pl.ds(r0, rc), :] = _adaln(x_ref[...], g_ref[...], sc_ref[...],
                                         sh_ref[...]).astype(BF16)

    def multiply_from(u_ref):
        wb = [w[...].astype(BF16) for w in w_refs]
        for r0 in range(0, u_ref.shape[0], rows):
            u = u_ref[r0:r0 + rows, :]
            o_ref[r0:r0 + rows, :] = finish(
                *[jnp.dot(u, w, preferred_element_type=F32) for w in wb]).astype(o_ref.dtype)

    @pl.when(i == 0)
    def _():
        normalise_into(u_bufs[0])

    for parity in range(2):
        @pl.when((i > 0) & (i % 2 == parity))
        def _(parity=parity):
            multiply_from(u_bufs[1 - parity])
            normalise_into(u_bufs[parity])
            for w_ref, wb_ref in zip(cast_in, cast_out):
                wb_ref[...] = w_ref[...].astype(BF16)


def _adaln_matmul(x2, g, mod4, mod_row, ws, rows_per_batch, finish, name, tm, tn, rc=256,
                  rows=None, cast_ws=()):
    t, d = x2.shape
    n = ws[0].shape[1]
    nt, nj, nchunks, tpb = t // tm, n // tn, tm // rc, rows_per_batch // tm
    assert nchunks <= nj
    tile = lambda i: jnp.minimum(i, nt - 1)
    col = lambda i, j: jnp.where(i > 0, j, 0)
    def slab(w):
        k = w.shape[0]
        ns = max(c for c in range(1, nt * nj + 1) if k % (c * BF16_SUBLANES) == 0)
        return pl.BlockSpec((k // ns, w.shape[1]),
                            lambda i, j: (jnp.clip((i - 1) * nj + j, 0, ns - 1), 0))

    kern = functools.partial(_adaln_matmul_kernel, nw=len(ws), ncast=len(cast_ws), rc=rc,
                             rows=rows or tm, finish=finish)
    outs = pl.pallas_call(
        kern,
        grid=(nt + 1, nj),
        in_specs=[pl.BlockSpec((rc, d), lambda i, j: (tile(i) * nchunks
                                                      + jnp.minimum(j, nchunks - 1), 0)),
                  pl.BlockSpec((1, d), lambda i, j: (0, 0)),
                  pl.BlockSpec((None, None, 1, d), lambda i, j: (tile(i) // tpb, mod_row, 0, 0)),
                  pl.BlockSpec((None, None, 1, d),
                               lambda i, j: (tile(i) // tpb, mod_row + 1, 0, 0))]
                 + [pl.BlockSpec((d, tn), lambda i, j: (0, col(i, j))) for _ in ws]
                 + [slab(w) for w in cast_ws],
        out_specs=[pl.BlockSpec((tm, tn), lambda i, j: (jnp.maximum(i - 1, 0), col(i, j)))]
                  + [slab(w) for w in cast_ws],
        out_shape=[jax.ShapeDtypeStruct((t, n), BF16)]
                  + [jax.ShapeDtypeStruct(w.shape, BF16) for w in cast_ws],
        scratch_shapes=[pltpu.VMEM((tm, d), BF16), pltpu.VMEM((tm, d), BF16)],
        compiler_params=_params(("arbitrary", "arbitrary")),
        name=name,
    )(x2, g, mod4, mod4, *ws, *cast_ws)
    return outs[0], outs[1:]


def _in_proj(x2, g, mod4, w, rows_per_batch, cast_ws):
    return _adaln_matmul(x2, g, mod4, 0, [w], rows_per_batch, lambda y: y, "in_proj",
                         tm=2048, tn=1024, cast_ws=cast_ws)


def _col_reduce(x, op, groups=8):
    k, n = x.shape
    if k % (groups * 8) == 0:
        x = op(x.reshape(groups, k // groups, n), axis=0)
    return op(x, axis=0, keepdims=True)


def _attn_kernel(lam_ref, sg_ref, q_ref, k_ref, v_ref, ct_ref, mw_ref, mb_ref, o_ref, mvo_ref,
                 vt_ref, acc_ref, m_ref, sa_ref, sb_ref, *, tq, sq, tk, lam_init):
    nkb = vt_ref.shape[0]
    nc = tq // sq

    _mod_kernel(ct_ref, mw_ref, mb_ref, mvo_ref)

    for j in range(nkb):
        vt_ref[j, 0:HEAD_W, :] = v_ref[j * tk:(j + 1) * tk, :].astype(F32).T.astype(BF16)
        vt_ref[j, HEAD_W:, :] = jnp.ones((vt_ref.shape[1] - HEAD_W, tk), BF16)

    lax.fori_loop(0, q_ref.shape[0] // tq,
                  functools.partial(_attn_q_tile, lam_ref, sg_ref, q_ref, k_ref, o_ref, vt_ref,
                                    acc_ref, m_ref, sa_ref, sb_ref, tq, sq, tk, nc, lam_init), 0)


def _attn_q_tile(lam_ref, sg_ref, q_ref, k_ref, o_ref, vt_ref, acc_ref, m_ref, sa_ref, sb_ref,
                 tq, sq, tk, nc, lam_init, qi, carry):
    q_rows = pl.ds(pl.multiple_of(qi * tq, tq), tq)
    q = (q_ref[q_rows, :].astype(F32) * (ATT_HALF_DIM ** -0.5 * math.log2(math.e))).astype(BF16)
    lane = lax.broadcasted_iota(jnp.int32, q.shape, 1)
    zero = jnp.zeros_like(q)
    qm = (jnp.where(lane < ATT_HALF_DIM, q, zero), jnp.where(lane >= ATT_HALF_DIM, q, zero))

    m_ref[...] = jnp.full(m_ref.shape, NEG_BIG, F32)
    acc_ref[...] = jnp.zeros(acc_ref.shape, F32)

    def scores(c, mi, k):
        return lax.dot_general(k, qm[mi][c * sq:(c + 1) * sq, :], (((1,), (1,)), ((), ())),
                               preferred_element_type=F32)

    def update(c, mi, vt, s):
        cs = slice(c * sq, (c + 1) * sq)
        m_old = m_ref[mi, :, cs]
        m_new = jnp.maximum(m_old, _col_reduce(s, jnp.max))
        p = jnp.exp2(s - m_new)
        alpha = jnp.exp2(m_old - m_new)
        acc_ref[mi, :, cs] = alpha * acc_ref[mi, :, cs] + jnp.dot(
            vt, p.astype(BF16), preferred_element_type=F32)
        m_ref[mi, :, cs] = m_new

    assert tq == 2 * tk
    chains = [(c, mi) for c in range(nc) for mi in range(2)]

    def stage(j, src, dst):
        vt = vt_ref[j]
        if dst is not None:
            kn = k_ref[pl.ds(pl.multiple_of((j + 1) * tk, tk), tk), :]
        for idx, (c, mi) in enumerate(chains):
            if dst is not None:
                dst[idx] = scores(c, mi, kn)
            update(c, mi, vt, src[idx])

    @pl.when(qi > 0)
    def _():
        k0 = k_ref[0:tk, :]
        for idx, (c, mi) in enumerate(chains):
            sa_ref[idx] = scores(c, mi, k0)

        def pair(pp, carry):
            stage(2 * pp, sa_ref, sb_ref)
            stage(2 * pp + 1, sb_ref, sa_ref)
            return carry

        lax.fori_loop(0, qi - 1, pair, 0)
        stage(2 * qi - 2, sa_ref, sb_ref)
        stage(2 * qi - 1, sb_ref, None)

    krow = lax.broadcasted_iota(jnp.int32, (sq, sq), 0)
    qcol = lax.broadcasted_iota(jnp.int32, (sq, sq), 1)
    dmask = krow // MASK_CHUNK <= qcol // MASK_CHUNK
    vt_tile = jnp.concatenate([vt_ref[qi * 2], vt_ref[qi * 2 + 1]], axis=1)
    pend = []
    for c in reversed(range(nc)):
        nk = (c + 1) * sq
        k = k_ref[pl.ds(pl.multiple_of(qi * tq, tq), nk), :]
        for mi in range(2):
            s = scores(c, mi, k)
            tail = jnp.where(dmask, s[c * sq:, :], NEG_BIG)
            s = tail if c == 0 else jnp.concatenate([s[:c * sq, :], tail], axis=0)
            pend.append((c, mi, vt_tile[:, :nk], s))
    for it in pend:
        update(*it)

    lp = lam_ref[...]
    lam = (jnp.exp(jnp.sum(lp[0:1] * lp[1:2], axis=-1, keepdims=True))
           - jnp.exp(jnp.sum(lp[2:3] * lp[3:4], axis=-1, keepdims=True)) + lam_init)
    o = (acc_ref[0, 0:HEAD_W, :] / acc_ref[0, HEAD_W:HEAD_W + 1, :]
         - lam * (acc_ref[1, 0:HEAD_W, :] / acc_ref[1, HEAD_W:HEAD_W + 1, :]))
    ms = jnp.mean(o * o, axis=0, keepdims=True)
    y = o * lax.rsqrt(ms + NORM_EPS) * sg_ref[...] * (1.0 - lam_init)
    o_ref[q_rows, :] = y.T.astype(o_ref.dtype)
    return carry


def _attention(p, lam4, sg_col, nb, s, lam_init, ct, w_mod, b_row, mod_col0, tq=1024, sq=256,
               tk=512):
    h = ATT_HEADS
    kern = functools.partial(_attn_kernel, tq=tq, sq=sq, tk=tk, lam_init=lam_init)
    nchains = 2 * tq // sq
    d = ct.shape[0]
    sw = (w_mod.shape[1] - mod_col0) // (nb * h)
    mslab = lambda rows, off: pl.BlockSpec((rows, sw), lambda b, hh: (0, off + b * h + hh))
    return pl.pallas_call(
        kern,
        grid=(nb, h),
        in_specs=[pl.BlockSpec((4, ATT_HALF_DIM), lambda b, hh: (0, 0)),
                  pl.BlockSpec((HEAD_W, 1), lambda b, hh: (0, 0)),
                  pl.BlockSpec((s, HEAD_W), lambda b, hh: (b, hh)),
                  pl.BlockSpec((s, HEAD_W), lambda b, hh: (b, h + hh)),
                  pl.BlockSpec((s, HEAD_W), lambda b, hh: (b, 2 * h + hh)),
                  pl.BlockSpec(ct.shape, lambda b, hh: (0, 0)),
                  mslab(d, mod_col0 // sw), mslab(1, mod_col0 // sw)],
        out_specs=[pl.BlockSpec((s, HEAD_W), lambda b, hh: (b, hh)), mslab(nb, 0)],
        out_shape=[jax.ShapeDtypeStruct((nb * s, h * HEAD_W), BF16),
                   jax.ShapeDtypeStruct((nb, w_mod.shape[1] - mod_col0), F32)],
        scratch_shapes=[pltpu.VMEM((s // tk, HEAD_W + ONES_ROWS, tk), BF16),
                        pltpu.VMEM((2, HEAD_W + ONES_ROWS, tq), F32),
                        pltpu.VMEM((2, 1, tq), F32),
                        pltpu.VMEM((nchains, tk, sq), F32),
                        pltpu.VMEM((nchains, tk, sq), F32)],
        compiler_params=_params(("parallel", "parallel")),
        name="diff_attn",
    )(lam4, sg_col, p, p, p, ct, w_mod, b_row)


def _group_row(a, s, grp=8):
    n, w = a.shape
    a3 = a.reshape(n // grp, grp, w)
    return jnp.broadcast_to(a3[:, s:s + 1, :], (n // grp, grp, w)).reshape(n, w)


def _hgrn2_kernel(lbl_ref, gn_ref, q_ref, f_ref, i_ref, g_ref, o_ref, st_ref, *, ch, layer):
    @pl.when(pl.program_id(1) == 0)
    def _():
        st_ref[...] = jnp.zeros(st_ref.shape, F32)

    w = HEAD_W
    lbl = lbl_ref[...]
    mx = jnp.max(lbl, axis=0, keepdims=True)
    e = jnp.exp(lbl - mx)
    lb = (jnp.sum(e[0:layer + 1], axis=0, keepdims=True)
          / jnp.sum(e, axis=0, keepdims=True))

    nt = (((1,), (1,)), ((), ()))
    row = lax.broadcasted_iota(jnp.int32, (ch, w), 0)
    arow = lax.broadcasted_iota(jnp.int32, (ch, ch), 0)
    acol = lax.broadcasted_iota(jnp.int32, (ch, ch), 1)
    tril = (arow >= acol).astype(BF16)
    on_diag = arow == acol
    levels = []
    hs = ch // 2
    while hs >= 1:
        upper = (row % (2 * hs)) >= hs
        levels.append((hs, upper, jnp.where(upper, 1.0, -1.0),
                       (arow // (2 * hs) == acol // (2 * hs))
                       & ((arow % (2 * hs)) >= hs) & ((acol % (2 * hs)) < hs)))
        hs //= 2

    def midpoint_row(b, hs):
        if hs >= 8:
            return jnp.concatenate(
                [jnp.broadcast_to(b[blk * 2 * hs + hs - 1:blk * 2 * hs + hs, :], (2 * hs, w))
                 for blk in range(ch // (2 * hs))], axis=0)
        if hs == 4:
            return _group_row(b, 3)
        return jnp.where((row % 8) < 4, _group_row(b, 1), _group_row(b, 5))

    chunk_vals = {}

    def chunk_prep(c):
        if c not in chunk_vals:
            rs = slice(c * ch, (c + 1) * ch)
            f_all = lb + (1.0 - lb) * jax.nn.sigmoid(f_ref[rs, :].astype(F32))
            g2 = jnp.log2(f_all)
            g2_hi = g2.astype(BF16)
            g2_lo = (g2 - g2_hi.astype(F32)).astype(BF16)
            b_all = (jnp.dot(tril, g2_hi, preferred_element_type=F32)
                     + jnp.dot(tril, g2_lo, preferred_element_type=F32))
            chunk_vals[c] = (f_all, b_all)
        return chunk_vals[c]

    def operands(c, h):
        rs, cs = slice(c * ch, (c + 1) * ch), slice(h * w, (h + 1) * w)
        f_all, b_all = chunk_prep(c)
        b, f = b_all[:, cs], f_all[:, cs]
        kk = 1.0 - f
        q = q_ref[rs, cs].astype(F32)
        blast = b[ch - 1:ch, :]
        ys = []
        for hs, upper, sign, _ in levels:
            if hs == 1:
                y = jnp.where(upper, q * f, kk)
            else:
                y = jnp.where(upper, q, kk) * jnp.exp2((b - midpoint_row(b, hs)) * sign)
            ys.append(y.astype(BF16))
        return dict(rs=rs, cs=cs, h=h, ys=ys,
                    qin=(q * jnp.exp2(b)).astype(BF16),
                    kout=(kk * jnp.exp2(blast - b)).astype(BF16),
                    keep=jnp.exp2(blast),
                    dsum=jnp.sum(q * kk, axis=-1, keepdims=True))

    def level_matmuls(x):
        st = st_ref[x["h"]]
        x["st"] = st
        x["o"] = lax.dot_general(x["qin"], st.astype(BF16), nt, preferred_element_type=F32)
        x["aa"] = [lax.dot_general(y, y, nt, preferred_element_type=F32) for y in x["ys"]]

    def finish(x):
        rs, cs = x["rs"], x["cs"]
        v = i_ref[rs, cs]
        att = jnp.where(on_diag, x["dsum"], 0.0).astype(BF16)
        for (_, _, _, amask), a in zip(levels, x["aa"]):
            att = jnp.where(amask, a.astype(BF16), att)
        o = x["o"] + jnp.dot(att, v, preferred_element_type=F32)
        st_ref[x["h"]] = x["keep"] * x["st"] + lax.dot_general(
            v, x["kout"], (((0,), (0,)), ((), ())), preferred_element_type=F32)
        y = _rms(o) * gn_ref[:, cs] * jax.nn.sigmoid(g_ref[rs, cs].astype(F32))
        o_ref[rs, cs] = y.astype(o_ref.dtype)

    items = [(c, h) for c in range(q_ref.shape[0] // ch) for h in range(REC_HEADS)]
    staged = []
    d1, d2 = 1, 2
    for i in range(len(items) + d2):
        if i < len(items):
            staged.append(operands(*items[i]))
        if d1 <= i < len(items) + d1:
            level_matmuls(staged[i - d1])
        if i >= d2:
            finish(staged[i - d2])


def _hgrn2(p, lb_logits, gn, nb, s, layer, rb=512, ch=128):
    hw = REC_HEADS * HEAD_W
    nr = s // rb
    c0 = 3 * ATT_HEADS * HEAD_W // hw
    kern = functools.partial(_hgrn2_kernel, ch=ch, layer=layer)
    spec = lambda off: pl.BlockSpec((rb, hw), lambda b, i: (b * nr + i, c0 + off))
    return pl.pallas_call(
        kern,
        grid=(nb, nr),
        in_specs=[pl.BlockSpec(lb_logits.shape, lambda b, i: (0, 0)),
                  pl.BlockSpec((1, hw), lambda b, i: (0, 0)),
                  spec(0), spec(1), spec(2), spec(3)],
        out_specs=pl.BlockSpec((rb, hw), lambda b, i: (b * nr + i, 0)),
        out_shape=jax.ShapeDtypeStruct((nb * s, hw), BF16),
        scratch_shapes=[pltpu.VMEM((REC_HEADS, HEAD_W, HEAD_W), F32)],
        compiler_params=_params(("parallel", "arbitrary")),
        name="hgrn2",
    )(lb_logits, gn, p, p, p, p)


def _merge_oproj_kernel(ya_ref, yr_ref, *refs, ng):
    ga_refs, gr_refs = refs[:ng], refs[ng:2 * ng]
    x_ref, gt_ref, wa_ref, wr_ref, wo_ref, o_ref = refs[2 * ng:]
    gw = wa_ref.shape[1] // ng
    parts = []
    for g in range(ng):
        cols = slice(g * gw, (g + 1) * gw)
        ta = jnp.dot(ya_ref[...], wa_ref[:, cols], preferred_element_type=F32)
        tr = jnp.dot(yr_ref[...], wr_ref[:, cols], preferred_element_type=F32)
        parts.append((jax.nn.sigmoid(ga_refs[g][...].astype(F32)) * ta
                      + jax.nn.sigmoid(gr_refs[g][...].astype(F32)) * tr).astype(BF16))
    m = jnp.concatenate(parts, axis=1)
    o_ref[...] = x_ref[...] + gt_ref[...] * jnp.dot(m, wo_ref[...], preferred_element_type=F32)


def _merge_oproj(ya, yr, p, x2, mod4, wa, wr, wo, ga_col0, rows_per_batch, tm=512, gw=1024):
    t, ka = ya.shape
    d = wa.shape[1]
    ng = d // gw
    ga0 = ga_col0 // gw
    gr0 = (ga_col0 + d) // gw
    tpb = rows_per_batch // tm
    gate = lambda b0, g: pl.BlockSpec((tm, gw), lambda i: (i, b0 + g))
    resident = lambda w: pl.BlockSpec(w.shape, lambda i: (0, 0), pipeline_mode=pl.Buffered(1))
    return pl.pallas_call(
        functools.partial(_merge_oproj_kernel, ng=ng),
        grid=(t // tm,),
        in_specs=[pl.BlockSpec((tm, ka), lambda i: (i, 0)),
                  pl.BlockSpec((tm, ka), lambda i: (i, 0))]
                 + [gate(ga0, g) for g in range(ng)] + [gate(gr0, g) for g in range(ng)]
                 + [pl.BlockSpec((tm, d), lambda i: (i, 0)),
                    pl.BlockSpec((None, None, 1, d), lambda i: (i // tpb, 2, 0, 0)),
                    resident(wa), resident(wr), resident(wo)],
        out_specs=pl.BlockSpec((tm, d), lambda i: (i, 0)),
        out_shape=jax.ShapeDtypeStruct((t, d), F32),
        compiler_params=_params(("parallel",)),
        name="merge_oproj",
    )(ya, yr, *([p] * (2 * ng)), x2, mod4, wa, wr, wo)


def _ffn_up(h1, g, mod4, wg, wu, rows_per_batch, cast_ws):
    swiglu = lambda gate, up: gate * jax.nn.sigmoid(gate) * up
    return _adaln_matmul(h1, g, mod4, 3, [wg, wu], rows_per_batch, swiglu, "ffn_up",
                         tm=2048, tn=512, rc=256, rows=1024, cast_ws=cast_ws)


def _ffn_down_kernel(a_ref, h_ref, gt_ref, fg_ref, w_ref, o_ref, *, tn, final):
    j = pl.program_id(1)
    y = jnp.dot(a_ref[...], w_ref[...], preferred_element_type=F32)
    o_ref[:, pl.ds(pl.multiple_of(j * tn, tn), tn)] = h_ref[...] + gt_ref[...] * y

    if final:
        @pl.when(j == pl.num_programs(1) - 1)
        def _():
            o_ref[...] = _rms(o_ref[...]) * fg_ref[...]


def _ffn_down(a, h1, mod4, fg, w, rows_per_batch, final, tm=1024, tn=512):
    t, d = h1.shape
    f = a.shape[1]
    tpb = rows_per_batch // tm
    return pl.pallas_call(
        functools.partial(_ffn_down_kernel, tn=tn, final=final),
        grid=(t // tm, d // tn),
        in_specs=[pl.BlockSpec((tm, f), lambda i, j: (i, 0)),
                  pl.BlockSpec((tm, tn), lambda i, j: (i, j)),
                  pl.BlockSpec((None, None, 1, tn), lambda i, j: (i // tpb, 5, 0, j)),
                  pl.BlockSpec((1, d), lambda i, j: (0, 0)),
                  pl.BlockSpec((f, tn), lambda i, j: (0, j))],
        out_specs=pl.BlockSpec((tm, d), lambda i, j: (i, 0)),
        out_shape=jax.ShapeDtypeStruct((t, d), F32),
        compiler_params=_params(("parallel", "arbitrary")),
        name="ffn_down",
    )(a, h1, mod4, fg, w)


def kernel(x, c, w_mod, b_mod, norm1_g, w_in, lambda_q1, lambda_k1, lambda_q2, lambda_k2,
           subln_g, lb_logits, gnorm_g, w_att_out, w_rec_out, w_o, norm2_g,
           w_ffn_gate, w_ffn_up, w_ffn_down, final_g):
    nb, s, d = x.shape
    depth = w_mod.shape[0]
    h = x.reshape(nb * s, d)
    for l in range(depth):
        lam_init = 0.8 - 0.6 * math.exp(-0.3 * l)
        ct, b_row = c.T, b_mod[l].reshape(1, 6 * d)
        mod_head = _mod(ct, w_mod[l], b_row, 2 * d)
        p, (wa, wr, wo) = _in_proj(h, norm1_g[l].reshape(1, d), mod_head.reshape(nb, 2, 1, d),
                                   w_in[l], s, cast_ws=(w_att_out[l], w_rec_out[l], w_o[l]))
        lam4 = jnp.stack([lambda_q1[l], lambda_k1[l], lambda_q2[l], lambda_k2[l]])
        ya, mod_tail = _attention(p, lam4, subln_g[l].reshape(HEAD_W, 1), nb, s, lam_init,
                                  ct, w_mod[l], b_row, 2 * d)
        mod4 = jnp.concatenate([mod_head, mod_tail], axis=1).reshape(nb, 6, 1, d)
        yr = _hgrn2(p, lb_logits, gnorm_g[l].reshape(1, REC_HEADS * HEAD_W), nb, s, l)
        ga_col0 = 3 * ATT_HEADS * HEAD_W + 4 * REC_HEADS * HEAD_W
        h1 = _merge_oproj(ya, yr, p, h, mod4, wa, wr, wo, ga_col0, s)
        a, (wd,) = _ffn_up(h1, norm2_g[l].reshape(1, d), mod4, w_ffn_gate[l], w_ffn_up[l], s,
                           cast_ws=(w_ffn_down[l],))
        h = _ffn_down(a, h1, mod4, final_g.reshape(1, d), wd, s, l == depth - 1)
    return h.reshape(nb, s, d)
```

```python
import functools
import math

import jax
import jax.numpy as jnp
from jax import lax
from jax.experimental import pallas as pl
from jax.experimental.pallas import tpu as pltpu

F32 = jnp.float32
BF16 = jnp.bfloat16

NORM_EPS = 1e-6
ATT_HEADS = 8
ATT_HALF_DIM = 64
HEAD_W = 128
REC_HEADS = 8
MASK_CHUNK = 64
NEG_BIG = -1e30
BF16_SUBLANES = 16
ONES_ROWS = BF16_SUBLANES

VMEM_LIMIT = 60 * 1024 * 1024


def _params(sem, vmem=VMEM_LIMIT):
    return pltpu.CompilerParams(dimension_semantics=sem, vmem_limit_bytes=vmem)


def _rms(x):
    return x * lax.rsqrt(jnp.mean(x * x, axis=-1, keepdims=True) + NORM_EPS)


def _adaln(x, g, sc, sh):
    return _rms(x) * (g * (1.0 + sc)) + sh


def _mod_kernel(ct_ref, w_ref, b_ref, o_ref):
    ct = ct_ref[...]
    cond = ct * jax.nn.sigmoid(ct)
    w = w_ref[...]
    for b in range(ct.shape[1]):
        o_ref[b:b + 1, :] = jnp.sum(w * cond[:, b:b + 1], axis=0, keepdims=True) + b_ref[...]


def _mod(ct, w_mod, b_row, n, tn=1024):
    d, nb = ct.shape
    return pl.pallas_call(
        _mod_kernel,
        grid=(n // tn,),
        in_specs=[pl.BlockSpec((d, nb), lambda j: (0, 0)),
                  pl.BlockSpec((d, tn), lambda j: (0, j)),
                  pl.BlockSpec((1, tn), lambda j: (0, j))],
        out_specs=pl.BlockSpec((nb, tn), lambda j: (0, j)),
        out_shape=jax.ShapeDtypeStruct((nb, n), F32),
        compiler_params=_params(("arbitrary",)),
        name="mod",
    )(ct, w_mod, b_row)


def _adaln_matmul_kernel(x_ref, g_ref, sh_ref, sc_ref, *refs, nw, ncast, rc, rows, finish):
    w_refs, cast_in = refs[:nw], refs[nw:nw + ncast]
    o_ref, cast_out = refs[nw + ncast], refs[nw + ncast + 1:nw + 2 * ncast + 1]
    u_bufs = refs[nw + 2 * ncast + 1:]
    i, j = pl.program_id(0), pl.program_id(1)
    nchunks = u_bufs[0].shape[0] // rc

    def normalise_into(u_ref):
        r0 = pl.multiple_of(jnp.minimum(j, nchunks - 1) * rc, rc)
        u_ref[pl.ds(r0, rc), :] = _adaln(x_ref[...], g_ref[...], sc_ref[...],
                                         sh_ref[...]).astype(BF16)

    def multiply_from(u_ref):
        wb = [w[...].astype(BF16) for w in w_refs]
        for r0 in range(0, u_ref.shape[0], rows):
            u = u_ref[r0:r0 + rows, :]
            o_ref[r0:r0 + rows, :] = finish(
                *[jnp.dot(u, w, preferred_element_type=F32) for w in wb]).astype(o_ref.dtype)

    @pl.when(i == 0)
    def _():
        normalise_into(u_bufs[0])

    for parity in range(2):
        @pl.when((i > 0) & (i % 2 == parity))
        def _(parity=parity):
            multiply_from(u_bufs[1 - parity])
            normalise_into(u_bufs[parity])
            for w_ref, wb_ref in zip(cast_in, cast_out):
                wb_ref[...] = w_ref[...].astype(BF16)


def _adaln_matmul(x2, g, mod4, mod_row, ws, rows_per_batch, finish, name, tm, tn, rc=256,
                  rows=None, cast_ws=()):
    t, d = x2.shape
    n = ws[0].shape[1]
    nt, nj, nchunks, tpb = t // tm, n // tn, tm // rc, rows_per_batch // tm
    assert nchunks <= nj
    tile = lambda i: jnp.minimum(i, nt - 1)
    col = lambda i, j: jnp.where(i > 0, j, 0)
    def slab(w):
        k = w.shape[0]
        ns = max(c for c in range(1, nt * nj + 1) if k % (c * BF16_SUBLANES) == 0)
        return pl.BlockSpec((k // ns, w.shape[1]),
                            lambda i, j: (jnp.clip((i - 1) * nj + j, 0, ns - 1), 0))

    kern = functools.partial(_adaln_matmul_kernel, nw=len(ws), ncast=len(cast_ws), rc=rc,
                             rows=rows or tm, finish=finish)
    outs = pl.pallas_call(
        kern,
        grid=(nt + 1, nj),
        in_specs=[pl.BlockSpec((rc, d), lambda i, j: (tile(i) * nchunks
                                                      + jnp.minimum(j, nchunks - 1), 0)),
                  pl.BlockSpec((1, d), lambda i, j: (0, 0)),
                  pl.BlockSpec((None, None, 1, d), lambda i, j: (tile(i) // tpb, mod_row, 0, 0)),
                  pl.BlockSpec((None, None, 1, d),
                               lambda i, j: (tile(i) // tpb, mod_row + 1, 0, 0))]
                 + [pl.BlockSpec((d, tn), lambda i, j: (0, col(i, j))) for _ in ws]
                 + [slab(w) for w in cast_ws],
        out_specs=[pl.BlockSpec((tm, tn), lambda i, j: (jnp.maximum(i - 1, 0), col(i, j)))]
                  + [slab(w) for w in cast_ws],
        out_shape=[jax.ShapeDtypeStruct((t, n), BF16)]
                  + [jax.ShapeDtypeStruct(w.shape, BF16) for w in cast_ws],
        scratch_shapes=[pltpu.VMEM((tm, d), BF16), pltpu.VMEM((tm, d), BF16)],
        compiler_params=_params(("arbitrary", "arbitrary")),
        name=name,
    )(x2, g, mod4, mod4, *ws, *cast_ws)
    return outs[0], outs[1:]


def _in_proj(x2, g, mod4, w, rows_per_batch, cast_ws):
    return _adaln_matmul(x2, g, mod4, 0, [w], rows_per_batch, lambda y: y, "in_proj",
                         tm=2048, tn=1024, rows=512, cast_ws=cast_ws)


def _col_reduce(x, op, groups=8):
    k, n = x.shape
    if k % (groups * 8) == 0:
        x = op(x.reshape(groups, k // groups, n), axis=0)
    return op(x, axis=0, keepdims=True)


def _attn_kernel(lam_ref, sg_ref, q_ref, k_ref, v_ref, ct_ref, mw_ref, mb_ref, o_ref, mvo_ref,
                 vt_ref, acc_ref, m_ref, sa_ref, sb_ref, *, tq, sq, tk, lam_init):
    nkb = vt_ref.shape[0]
    nc = tq // sq

    _mod_kernel(ct_ref, mw_ref, mb_ref, mvo_ref)

    for j in range(nkb):
        vt_ref[j, 0:HEAD_W, :] = v_ref[j * tk:(j + 1) * tk, :].astype(F32).T.astype(BF16)
        vt_ref[j, HEAD_W:, :] = jnp.ones((vt_ref.shape[1] - HEAD_W, tk), BF16)

    lax.fori_loop(0, q_ref.shape[0] // tq,
                  functools.partial(_attn_q_tile, lam_ref, sg_ref, q_ref, k_ref, o_ref, vt_ref,
                                    acc_ref, m_ref, sa_ref, sb_ref, tq, sq, tk, nc, lam_init), 0)


def _attn_q_tile(lam_ref, sg_ref, q_ref, k_ref, o_ref, vt_ref, acc_ref, m_ref, sa_ref, sb_ref,
                 tq, sq, tk, nc, lam_init, qi, carry):
    q_rows = pl.ds(pl.multiple_of(qi * tq, tq), tq)
    q = (q_ref[q_rows, :].astype(F32) * (ATT_HALF_DIM ** -0.5 * math.log2(math.e))).astype(BF16)
    lane = lax.broadcasted_iota(jnp.int32, q.shape, 1)
    zero = jnp.zeros_like(q)
    qm = (jnp.where(lane < ATT_HALF_DIM, q, zero), jnp.where(lane >= ATT_HALF_DIM, q, zero))

    m_ref[...] = jnp.full(m_ref.shape, NEG_BIG, F32)
    acc_ref[...] = jnp.zeros(acc_ref.shape, F32)

    def scores(c, mi, k):
        return lax.dot_general(k, qm[mi][c * sq:(c + 1) * sq, :], (((1,), (1,)), ((), ())),
                               preferred_element_type=F32)

    def update(c, mi, vt, s):
        cs = slice(c * sq, (c + 1) * sq)
        m_old = m_ref[mi, :, cs]
        m_new = jnp.maximum(m_old, _col_reduce(s, jnp.max))
        p = jnp.exp2(s - m_new)
        alpha = jnp.exp2(m_old - m_new)
        acc_ref[mi, :, cs] = alpha * acc_ref[mi, :, cs] + jnp.dot(
            vt, p.astype(BF16), preferred_element_type=F32)
        m_ref[mi, :, cs] = m_new

    assert tq == 2 * tk
    chains = [(c, mi) for c in range(nc) for mi in range(2)]

    def stage(j, src, dst):
        vt = vt_ref[j]
        if dst is not None:
            kn = k_ref[pl.ds(pl.multiple_of((j + 1) * tk, tk), tk), :]
        for idx, (c, mi) in enumerate(chains):
            if dst is not None:
                dst[idx] = scores(c, mi, kn)
            update(c, mi, vt, src[idx])

    @pl.when(qi > 0)
    def _():
        k0 = k_ref[0:tk, :]
        for idx, (c, mi) in enumerate(chains):
            sa_ref[idx] = scores(c, mi, k0)

        def pair(pp, carry):
            stage(2 * pp, sa_ref, sb_ref)
            stage(2 * pp + 1, sb_ref, sa_ref)
            return carry

        lax.fori_loop(0, qi - 1, pair, 0)
        stage(2 * qi - 2, sa_ref, sb_ref)
        stage(2 * qi - 1, sb_ref, None)

    krow = lax.broadcasted_iota(jnp.int32, (sq, sq), 0)
    qcol = lax.broadcasted_iota(jnp.int32, (sq, sq), 1)
    dmask = krow // MASK_CHUNK <= qcol // MASK_CHUNK
    vt_tile = jnp.concatenate([vt_ref[qi * 2], vt_ref[qi * 2 + 1]], axis=1)
    pend = []
    for c in reversed(range(nc)):
        nk = (c + 1) * sq
        k = k_ref[pl.ds(pl.multiple_of(qi * tq, tq), nk), :]
        for mi in range(2):
            s = scores(c, mi, k)
            tail = jnp.where(dmask, s[c * sq:, :], NEG_BIG)
            s = tail if c == 0 else jnp.concatenate([s[:c * sq, :], tail], axis=0)
            pend.append((c, mi, vt_tile[:, :nk], s))
    for it in pend:
        update(*it)

    lp = lam_ref[...]
    lam = (jnp.exp(jnp.sum(lp[0:1] * lp[1:2], axis=-1, keepdims=True))
           - jnp.exp(jnp.sum(lp[2:3] * lp[3:4], axis=-1, keepdims=True)) + lam_init)
    o = (acc_ref[0, 0:HEAD_W, :] / acc_ref[0, HEAD_W:HEAD_W + 1, :]
         - lam * (acc_ref[1, 0:HEAD_W, :] / acc_ref[1, HEAD_W:HEAD_W + 1, :]))
    ms = jnp.mean(o * o, axis=0, keepdims=True)
    y = o * lax.rsqrt(ms + NORM_EPS) * sg_ref[...] * (1.0 - lam_init)
    o_ref[q_rows, :] = y.T.astype(o_ref.dtype)
    return carry


def _attention(p, lam4, sg_col, nb, s, lam_init, ct, w_mod, b_row, mod_col0, tq=1024, sq=256,
               tk=512):
    h = ATT_HEADS
    kern = functools.partial(_attn_kernel, tq=tq, sq=sq, tk=tk, lam_init=lam_init)
    nchains = 2 * tq // sq
    d = ct.shape[0]
    sw = (w_mod.shape[1] - mod_col0) // (nb * h)
    mslab = lambda rows, off: pl.BlockSpec((rows, sw), lambda b, hh: (0, off + b * h + hh))
    return pl.pallas_call(
        kern,
        grid=(nb, h),
        in_specs=[pl.BlockSpec((4, ATT_HALF_DIM), lambda b, hh: (0, 0)),
                  pl.BlockSpec((HEAD_W, 1), lambda b, hh: (0, 0)),
                  pl.BlockSpec((s, HEAD_W), lambda b, hh: (b, hh)),
                  pl.BlockSpec((s, HEAD_W), lambda b, hh: (b, h + hh)),
                  pl.BlockSpec((s, HEAD_W), lambda b, hh: (b, 2 * h + hh)),
                  pl.BlockSpec(ct.shape, lambda b, hh: (0, 0)),
                  mslab(d, mod_col0 // sw), mslab(1, mod_col0 // sw)],
        out_specs=[pl.BlockSpec((s, HEAD_W), lambda b, hh: (b, hh)), mslab(nb, 0)],
        out_shape=[jax.ShapeDtypeStruct((nb * s, h * HEAD_W), BF16),
                   jax.ShapeDtypeStruct((nb, w_mod.shape[1] - mod_col0), F32)],
        scratch_shapes=[pltpu.VMEM((s // tk, HEAD_W + ONES_ROWS, tk), BF16),
                        pltpu.VMEM((2, HEAD_W + ONES_ROWS, tq), F32),
                        pltpu.VMEM((2, 1, tq), F32),
                        pltpu.VMEM((nchains, tk, sq), F32),
                        pltpu.VMEM((nchains, tk, sq), F32)],
        compiler_params=_params(("parallel", "parallel")),
        name="diff_attn",
    )(lam4, sg_col, p, p, p, ct, w_mod, b_row)


def _group_row(a, s, grp=8):
    n, w = a.shape
    a3 = a.reshape(n // grp, grp, w)
    return jnp.broadcast_to(a3[:, s:s + 1, :], (n // grp, grp, w)).reshape(n, w)


def _hgrn2_kernel(lbl_ref, gn_ref, q_ref, f_ref, i_ref, g_ref, o_ref, st_ref, *, ch, layer):
    @pl.when(pl.program_id(1) == 0)
    def _():
        st_ref[...] = jnp.zeros(st_ref.shape, F32)

    w = HEAD_W
    lbl = lbl_ref[...]
    mx = jnp.max(lbl, axis=0, keepdims=True)
    e = jnp.exp(lbl - mx)
    lb = (jnp.sum(e[0:layer + 1], axis=0, keepdims=True)
          / jnp.sum(e, axis=0, keepdims=True))

    nt = (((1,), (1,)), ((), ()))
    row = lax.broadcasted_iota(jnp.int32, (ch, w), 0)
    arow = lax.broadcasted_iota(jnp.int32, (ch, ch), 0)
    acol = lax.broadcasted_iota(jnp.int32, (ch, ch), 1)
    tril = (arow >= acol).astype(BF16)
    on_diag = arow == acol
    levels = []
    hs = ch // 2
    while hs >= 1:
        upper = (row % (2 * hs)) >= hs
        levels.append((hs, upper, jnp.where(upper, 1.0, -1.0),
                       (arow // (2 * hs) == acol // (2 * hs))
                       & ((arow % (2 * hs)) >= hs) & ((acol % (2 * hs)) < hs)))
        hs //= 2

    def midpoint_row(b, hs):
        if hs >= 8:
            return jnp.concatenate(
                [jnp.broadcast_to(b[blk * 2 * hs + hs - 1:blk * 2 * hs + hs, :], (2 * hs, w))
                 for blk in range(ch // (2 * hs))], axis=0)
        if hs == 4:
            return _group_row(b, 3)
        return jnp.where((row % 8) < 4, _group_row(b, 1), _group_row(b, 5))

    chunk_vals = {}

    def chunk_prep(c):
        if c not in chunk_vals:
            rs = slice(c * ch, (c + 1) * ch)
            f_all = lb + (1.0 - lb) * jax.nn.sigmoid(f_ref[rs, :].astype(F32))
            g2 = jnp.log2(f_all)
            g2_hi = g2.astype(BF16)
            g2_lo = (g2 - g2_hi.astype(F32)).astype(BF16)
            b_all = (jnp.dot(tril, g2_hi, preferred_element_type=F32)
                     + jnp.dot(tril, g2_lo, preferred_element_type=F32))
            chunk_vals[c] = (f_all, b_all)
        return chunk_vals[c]

    def operands(c, h):
        rs, cs = slice(c * ch, (c + 1) * ch), slice(h * w, (h + 1) * w)
        f_all, b_all = chunk_prep(c)
        b, f = b_all[:, cs], f_all[:, cs]
        kk = 1.0 - f
        q = q_ref[rs, cs].astype(F32)
        blast = b[ch - 1:ch, :]
        ys = []
        for hs, upper, sign, _ in levels:
            if hs == 1:
                y = jnp.where(upper, q * f, kk)
            else:
                y = jnp.where(upper, q, kk) * jnp.exp2((b - midpoint_row(b, hs)) * sign)
            ys.append(y.astype(BF16))
        return dict(rs=rs, cs=cs, h=h, ys=ys,
                    qin=(q * jnp.exp2(b)).astype(BF16),
                    kout=(kk * jnp.exp2(blast - b)).astype(BF16),
                    keep=jnp.exp2(blast),
                    dsum=jnp.sum(q * kk, axis=-1, keepdims=True))

    def level_matmuls(x):
        st = st_ref[x["h"]]
        x["st"] = st
        x["o"] = lax.dot_general(x["qin"], st.astype(BF16), nt, preferred_element_type=F32)
        x["aa"] = [lax.dot_general(y, y, nt, preferred_element_type=F32) for y in x["ys"]]

    def finish(x):
        rs, cs = x["rs"], x["cs"]
        v = i_ref[rs, cs]
        att = jnp.where(on_diag, x["dsum"], 0.0).astype(BF16)
        for (_, _, _, amask), a in zip(levels, x["aa"]):
            att = jnp.where(amask, a.astype(BF16), att)
        o = x["o"] + jnp.dot(att, v, preferred_element_type=F32)
        st_ref[x["h"]] = x["keep"] * x["st"] + lax.dot_general(
            v, x["kout"], (((0,), (0,)), ((), ())), preferred_element_type=F32)
        y = _rms(o) * gn_ref[:, cs] * jax.nn.sigmoid(g_ref[rs, cs].astype(F32))
        o_ref[rs, cs] = y.astype(o_ref.dtype)

    items = [(c, h) for c in range(q_ref.shape[0] // ch) for h in range(REC_HEADS)]
    staged = []
    d1, d2 = 1, 2
    for i in range(len(items) + d2):
        if i < len(items):
            staged.append(operands(*items[i]))
        if d1 <= i < len(items) + d1:
            level_matmuls(staged[i - d1])
        if i >= d2:
            finish(staged[i - d2])


def _hgrn2(p, lb_logits, gn, nb, s, layer, rb=512, ch=128):
    hw = REC_HEADS * HEAD_W
    nr = s // rb
    c0 = 3 * ATT_HEADS * HEAD_W // hw
    kern = functools.partial(_hgrn2_kernel, ch=ch, layer=layer)
    spec = lambda off: pl.BlockSpec((rb, hw), lambda b, i: (b * nr + i, c0 + off))
    return pl.pallas_call(
        kern,
        grid=(nb, nr),
        in_specs=[pl.BlockSpec(lb_logits.shape, lambda b, i: (0, 0)),
                  pl.BlockSpec((1, hw), lambda b, i: (0, 0)),
                  spec(0), spec(1), spec(2), spec(3)],
        out_specs=pl.BlockSpec((rb, hw), lambda b, i: (b * nr + i, 0)),
        out_shape=jax.ShapeDtypeStruct((nb * s, hw), BF16),
        scratch_shapes=[pltpu.VMEM((REC_HEADS, HEAD_W, HEAD_W), F32)],
        compiler_params=_params(("parallel", "arbitrary")),
        name="hgrn2",
    )(lb_logits, gn, p, p, p, p)


def _merge_oproj_kernel(ya_ref, yr_ref, *refs, ng):
    ga_refs, gr_refs = refs[:ng], refs[ng:2 * ng]
    x_ref, gt_ref, wa_ref, wr_ref, wo_ref, o_ref = refs[2 * ng:]
    gw = wa_ref.shape[1] // ng
    parts = []
    for g in range(ng):
        cols = slice(g * gw, (g + 1) * gw)
        ta = jnp.dot(ya_ref[...], wa_ref[:, cols], preferred_element_type=F32)
        tr = jnp.dot(yr_ref[...], wr_ref[:, cols], preferred_element_type=F32)
        parts.append((jax.nn.sigmoid(ga_refs[g][...].astype(F32)) * ta
                      + jax.nn.sigmoid(gr_refs[g][...].astype(F32)) * tr).astype(BF16))
    m = jnp.concatenate(parts, axis=1)
    o_ref[...] = x_ref[...] + gt_ref[...] * jnp.dot(m, wo_ref[...], preferred_element_type=F32)


def _merge_oproj(ya, yr, p, x2, mod4, wa, wr, wo, ga_col0, rows_per_batch, tm=512, gw=1024):
    t, ka = ya.shape
    d = wa.shape[1]
    ng = d // gw
    ga0 = ga_col0 // gw
    gr0 = (ga_col0 + d) // gw
    tpb = rows_per_batch // tm
    gate = lambda b0, g: pl.BlockSpec((tm, gw), lambda i: (i, b0 + g))
    resident = lambda w: pl.BlockSpec(w.shape, lambda i: (0, 0), pipeline_mode=pl.Buffered(1))
    return pl.pallas_call(
        functools.partial(_merge_oproj_kernel, ng=ng),
        grid=(t // tm,),
        in_specs=[pl.BlockSpec((tm, ka), lambda i: (i, 0)),
                  pl.BlockSpec((tm, ka), lambda i: (i, 0))]
                 + [gate(ga0, g) for g in range(ng)] + [gate(gr0, g) for g in range(ng)]
                 + [pl.BlockSpec((tm, d), lambda i: (i, 0)),
                    pl.BlockSpec((None, None, 1, d), lambda i: (i // tpb, 2, 0, 0)),
                    resident(wa), resident(wr), resident(wo)],
        out_specs=pl.BlockSpec((tm, d), lambda i: (i, 0)),
        out_shape=jax.ShapeDtypeStruct((t, d), F32),
        compiler_params=_params(("parallel",)),
        name="merge_oproj",
    )(ya, yr, *([p] * (2 * ng)), x2, mod4, wa, wr, wo)


def _ffn_up(h1, g, mod4, wg, wu, rows_per_batch, cast_ws):
    swiglu = lambda gate, up: gate * jax.nn.sigmoid(gate) * up
    return _adaln_matmul(h1, g, mod4, 3, [wg, wu], rows_per_batch, swiglu, "ffn_up",
                         tm=2048, tn=512, rc=256, rows=512, cast_ws=cast_ws)


def _ffn_down_kernel(a_ref, h_ref, gt_ref, fg_ref, w_ref, o_ref, *, tn, final):
    j = pl.program_id(1)
    y = jnp.dot(a_ref[...], w_ref[...], preferred_element_type=F32)
    o_ref[:, pl.ds(pl.multiple_of(j * tn, tn), tn)] = h_ref[...] + gt_ref[...] * y

    if final:
        @pl.when(j == pl.num_programs(1) - 1)
        def _():
            o_ref[...] = _rms(o_ref[...]) * fg_ref[...]


def _ffn_down(a, h1, mod4, fg, w, rows_per_batch, final, tm=1024, tn=512):
    t, d = h1.shape
    f = a.shape[1]
    tpb = rows_per_batch // tm
    return pl.pallas_call(
        functools.partial(_ffn_down_kernel, tn=tn, final=final),
        grid=(t // tm, d // tn),
        in_specs=[pl.BlockSpec((tm, f), lambda i, j: (i, 0)),
                  pl.BlockSpec((tm, tn), lambda i, j: (i, j)),
                  pl.BlockSpec((None, None, 1, tn), lambda i, j: (i // tpb, 5, 0, j)),
                  pl.BlockSpec((1, d), lambda i, j: (0, 0)),
                  pl.BlockSpec((f, tn), lambda i, j: (0, j))],
        out_specs=pl.BlockSpec((tm, d), lambda i, j: (i, 0)),
        out_shape=jax.ShapeDtypeStruct((t, d), F32),
        compiler_params=_params(("parallel", "arbitrary")),
        name="ffn_down",
    )(a, h1, mod4, fg, w)


def kernel(x, c, w_mod, b_mod, norm1_g, w_in, lambda_q1, lambda_k1, lambda_q2, lambda_k2,
           subln_g, lb_logits, gnorm_g, w_att_out, w_rec_out, w_o, norm2_g,
           w_ffn_gate, w_ffn_up, w_ffn_down, final_g):
    nb, s, d = x.shape
    depth = w_mod.shape[0]
    h = x.reshape(nb * s, d)
    for l in range(depth):
        lam_init = 0.8 - 0.6 * math.exp(-0.3 * l)
        ct, b_row = c.T, b_mod[l].reshape(1, 6 * d)
        mod_head = _mod(ct, w_mod[l], b_row, 2 * d)
        p, (wa, wr, wo) = _in_proj(h, norm1_g[l].reshape(1, d), mod_head.reshape(nb, 2, 1, d),
                                   w_in[l], s, cast_ws=(w_att_out[l], w_rec_out[l], w_o[l]))
        lam4 = jnp.stack([lambda_q1[l], lambda_k1[l], lambda_q2[l], lambda_k2[l]])
        ya, mod_tail = _attention(p, lam4, subln_g[l].reshape(HEAD_W, 1), nb, s, lam_init,
                                  ct, w_mod[l], b_row, 2 * d)
        mod4 = jnp.concatenate([mod_head, mod_tail], axis=1).reshape(nb, 6, 1, d)
        yr = _hgrn2(p, lb_logits, gnorm_g[l].reshape(1, REC_HEADS * HEAD_W), nb, s, l)
        ga_col0 = 3 * ATT_HEADS * HEAD_W + 4 * REC_HEADS * HEAD_W
        h1 = _merge_oproj(ya, yr, p, h, mod4, wa, wr, wo, ga_col0, s)
        a, (wd,) = _ffn_up(h1, norm2_g[l].reshape(1, d), mod4, w_ffn_gate[l], w_ffn_up[l], s,
                           cast_ws=(w_ffn_down[l],))
        h = _ffn_down(a, h1, mod4, final_g.reshape(1, d), wd, s, l == depth - 1)
    return h.reshape(nb, s, d)
```

```python
import functools
import math

import jax
import jax.numpy as jnp
from jax import lax
from jax.experimental import pallas as pl
from jax.experimental.pallas import tpu as pltpu

F32 = jnp.float32
BF16 = jnp.bfloat16

NORM_EPS = 1e-6
ATT_HEADS = 8
ATT_HALF_DIM = 64
HEAD_W = 128
REC_HEADS = 8
MASK_CHUNK = 64
NEG_BIG = -1e30
BF16_SUBLANES = 16
ONES_ROWS = BF16_SUBLANES

VMEM_LIMIT = 60 * 1024 * 1024


def _params(sem, vmem=VMEM_LIMIT):
    return pltpu.CompilerParams(dimension_semantics=sem, vmem_limit_bytes=vmem)


def _rms(x):
    return x * lax.rsqrt(jnp.mean(x * x, axis=-1, keepdims=True) + NORM_EPS)


def _adaln(x, g, sc, sh):
    return _rms(x) * (g * (1.0 + sc)) + sh


def _mod_kernel(ct_ref, w_ref, b_ref, o_ref):
    ct = ct_ref[...]
    cond = ct * jax.nn.sigmoid(ct)
    w = w_ref[...]
    for b in range(ct.shape[1]):
        o_ref[b:b + 1, :] = jnp.sum(w * cond[:, b:b + 1], axis=0, keepdims=True) + b_ref[...]


def _mod(ct, w_mod, b_row, n, tn=1024):
    d, nb = ct.shape
    return pl.pallas_call(
        _mod_kernel,
        grid=(n // tn,),
        in_specs=[pl.BlockSpec((d, nb), lambda j: (0, 0)),
                  pl.BlockSpec((d, tn), lambda j: (0, j)),
                  pl.BlockSpec((1, tn), lambda j: (0, j))],
        out_specs=pl.BlockSpec((nb, tn), lambda j: (0, j)),
        out_shape=jax.ShapeDtypeStruct((nb, n), F32),
        compiler_params=_params(("arbitrary",)),
        name="mod",
    )(ct, w_mod, b_row)


def _adaln_matmul_kernel(x_ref, g_ref, sh_ref, sc_ref, *refs, nw, ncast, rc, rows, finish):
    w_refs, cast_in = refs[:nw], refs[nw:nw + ncast]
    o_ref, cast_out = refs[nw + ncast], refs[nw + ncast + 1:nw + 2 * ncast + 1]
    u_bufs = refs[nw + 2 * ncast + 1:]
    i, j = pl.program_id(0), pl.program_id(1)
    nchunks = u_bufs[0].shape[0] // rc

    def normalise_into(u_ref):
        r0 = pl.multiple_of(jnp.minimum(j, nchunks - 1) * rc, rc)
        u_ref[pl.ds(r0, rc), :] = _adaln(x_ref[...], g_ref[...], sc_ref[...],
                                         sh_ref[...]).astype(BF16)

    def multiply_from(u_ref):
        wb = [w[...].astype(BF16) for w in w_refs]
        for r0 in range(0, u_ref.shape[0], rows):
            u = u_ref[r0:r0 + rows, :]
            o_ref[r0:r0 + rows, :] = finish(
                *[jnp.dot(u, w, preferred_element_type=F32) for w in wb]).astype(o_ref.dtype)

    @pl.when(i == 0)
    def _():
        normalise_into(u_bufs[0])

    for parity in range(2):
        @pl.when((i > 0) & (i % 2 == parity))
        def _(parity=parity):
            multiply_from(u_bufs[1 - parity])
            normalise_into(u_bufs[parity])
            for w_ref, wb_ref in zip(cast_in, cast_out):
                wb_ref[...] = w_ref[...].astype(BF16)


def _adaln_matmul(x2, g, mod4, mod_row, ws, rows_per_batch, finish, name, tm, tn, rc=256,
                  rows=None, cast_ws=()):
    t, d = x2.shape
    n = ws[0].shape[1]
    nt, nj, nchunks, tpb = t // tm, n // tn, tm // rc, rows_per_batch // tm
    assert nchunks <= nj
    tile = lambda i: jnp.minimum(i, nt - 1)
    col = lambda i, j: jnp.where(i > 0, j, 0)
    def slab(w):
        k = w.shape[0]
        ns = max(c for c in range(1, nt * nj + 1) if k % (c * BF16_SUBLANES) == 0)
        return pl.BlockSpec((k // ns, w.shape[1]),
                            lambda i, j: (jnp.clip((i - 1) * nj + j, 0, ns - 1), 0))

    kern = functools.partial(_adaln_matmul_kernel, nw=len(ws), ncast=len(cast_ws), rc=rc,
                             rows=rows or tm, finish=finish)
    outs = pl.pallas_call(
        kern,
        grid=(nt + 1, nj),
        in_specs=[pl.BlockSpec((rc, d), lambda i, j: (tile(i) * nchunks
                                                      + jnp.minimum(j, nchunks - 1), 0)),
                  pl.BlockSpec((1, d), lambda i, j: (0, 0)),
                  pl.BlockSpec((None, None, 1, d), lambda i, j: (tile(i) // tpb, mod_row, 0, 0)),
                  pl.BlockSpec((None, None, 1, d),
                               lambda i, j: (tile(i) // tpb, mod_row + 1, 0, 0))]
                 + [pl.BlockSpec((d, tn), lambda i, j: (0, col(i, j))) for _ in ws]
                 + [slab(w) for w in cast_ws],
        out_specs=[pl.BlockSpec((tm, tn), lambda i, j: (jnp.maximum(i - 1, 0), col(i, j)))]
                  + [slab(w) for w in cast_ws],
        out_shape=[jax.ShapeDtypeStruct((t, n), BF16)]
                  + [jax.ShapeDtypeStruct(w.shape, BF16) for w in cast_ws],
        scratch_shapes=[pltpu.VMEM((tm, d), BF16), pltpu.VMEM((tm, d), BF16)],
        compiler_params=_params(("arbitrary", "arbitrary")),
        name=name,
    )(x2, g, mod4, mod4, *ws, *cast_ws)
    return outs[0], outs[1:]


def _in_proj(x2, g, mod4, w, rows_per_batch, cast_ws):
    return _adaln_matmul(x2, g, mod4, 0, [w], rows_per_batch, lambda y: y, "in_proj",
                         tm=2048, tn=1024, rows=512, cast_ws=cast_ws)


def _col_reduce(x, op, groups=8):
    k, n = x.shape
    if k % (groups * 8) == 0:
        x = op(x.reshape(groups, k // groups, n), axis=0)
    return op(x, axis=0, keepdims=True)


def _attn_kernel(lam_ref, sg_ref, q_ref, k_ref, v_ref, ct_ref, mw_ref, mb_ref, o_ref, mvo_ref,
                 vt_ref, acc_ref, m_ref, sa_ref, sb_ref, *, tq, sq, tk, lam_init):
    nkb = vt_ref.shape[0]
    nc = tq // sq

    _mod_kernel(ct_ref, mw_ref, mb_ref, mvo_ref)

    for j in range(nkb):
        vt_ref[j, 0:HEAD_W, :] = v_ref[j * tk:(j + 1) * tk, :].T
        vt_ref[j, HEAD_W:, :] = jnp.ones((vt_ref.shape[1] - HEAD_W, tk), BF16)

    lax.fori_loop(0, q_ref.shape[0] // tq,
                  functools.partial(_attn_q_tile, lam_ref, sg_ref, q_ref, k_ref, o_ref, vt_ref,
                                    acc_ref, m_ref, sa_ref, sb_ref, tq, sq, tk, nc, lam_init), 0)


def _attn_q_tile(lam_ref, sg_ref, q_ref, k_ref, o_ref, vt_ref, acc_ref, m_ref, sa_ref, sb_ref,
                 tq, sq, tk, nc, lam_init, qi, carry):
    q_rows = pl.ds(pl.multiple_of(qi * tq, tq), tq)
    q = (q_ref[q_rows, :].astype(F32) * (ATT_HALF_DIM ** -0.5 * math.log2(math.e))).astype(BF16)
    lane = lax.broadcasted_iota(jnp.int32, q.shape, 1)
    zero = jnp.zeros_like(q)
    qm = (jnp.where(lane < ATT_HALF_DIM, q, zero), jnp.where(lane >= ATT_HALF_DIM, q, zero))

    m_ref[...] = jnp.full(m_ref.shape, NEG_BIG, F32)
    acc_ref[...] = jnp.zeros(acc_ref.shape, F32)

    def scores(c, mi, k):
        return lax.dot_general(k, qm[mi][c * sq:(c + 1) * sq, :], (((1,), (1,)), ((), ())),
                               preferred_element_type=F32)

    def update(c, mi, vt, s):
        cs = slice(c * sq, (c + 1) * sq)
        m_old = m_ref[mi, :, cs]
        m_new = jnp.maximum(m_old, _col_reduce(s, jnp.max))
        p = jnp.exp2(s - m_new)
        alpha = jnp.exp2(m_old - m_new)
        acc_ref[mi, :, cs] = alpha * acc_ref[mi, :, cs] + jnp.dot(
            vt, p.astype(BF16), preferred_element_type=F32)
        m_ref[mi, :, cs] = m_new

    assert tq == 2 * tk
    chains = [(c, mi) for c in range(nc) for mi in range(2)]

    def stage(j, src, dst):
        vt = vt_ref[j]
        if dst is not None:
            kn = k_ref[pl.ds(pl.multiple_of((j + 1) * tk, tk), tk), :]
        for idx, (c, mi) in enumerate(chains):
            if dst is not None:
                dst[idx] = scores(c, mi, kn)
            update(c, mi, vt, src[idx])

    @pl.when(qi > 0)
    def _():
        k0 = k_ref[0:tk, :]
        for idx, (c, mi) in enumerate(chains):
            sa_ref[idx] = scores(c, mi, k0)

        def pair(pp, carry):
            stage(2 * pp, sa_ref, sb_ref)
            stage(2 * pp + 1, sb_ref, sa_ref)
            return carry

        lax.fori_loop(0, qi - 1, pair, 0)
        stage(2 * qi - 2, sa_ref, sb_ref)
        stage(2 * qi - 1, sb_ref, None)

    krow = lax.broadcasted_iota(jnp.int32, (sq, sq), 0)
    qcol = lax.broadcasted_iota(jnp.int32, (sq, sq), 1)
    dmask = krow // MASK_CHUNK <= qcol // MASK_CHUNK
    vt_tile = jnp.concatenate([vt_ref[qi * 2], vt_ref[qi * 2 + 1]], axis=1)
    pend = []
    for c in reversed(range(nc)):
        nk = (c + 1) * sq
        k = k_ref[pl.ds(pl.multiple_of(qi * tq, tq), nk), :]
        for mi in range(2):
            s = scores(c, mi, k)
            tail = jnp.where(dmask, s[c * sq:, :], NEG_BIG)
            s = tail if c == 0 else jnp.concatenate([s[:c * sq, :], tail], axis=0)
            pend.append((c, mi, vt_tile[:, :nk], s))
    for it in pend:
        update(*it)

    lp = lam_ref[...]
    lam = (jnp.exp(jnp.sum(lp[0:1] * lp[1:2], axis=-1, keepdims=True))
           - jnp.exp(jnp.sum(lp[2:3] * lp[3:4], axis=-1, keepdims=True)) + lam_init)
    o = (acc_ref[0, 0:HEAD_W, :] / acc_ref[0, HEAD_W:HEAD_W + 1, :]
         - lam * (acc_ref[1, 0:HEAD_W, :] / acc_ref[1, HEAD_W:HEAD_W + 1, :]))
    ms = jnp.mean(o * o, axis=0, keepdims=True)
    y = o * lax.rsqrt(ms + NORM_EPS) * sg_ref[...] * (1.0 - lam_init)
    o_ref[q_rows, :] = y.T.astype(o_ref.dtype)
    return carry


def _attention(p, lam4, sg_col, nb, s, lam_init, ct, w_mod, b_row, mod_col0, tq=1024, sq=256,
               tk=512):
    h = ATT_HEADS
    kern = functools.partial(_attn_kernel, tq=tq, sq=sq, tk=tk, lam_init=lam_init)
    nchains = 2 * tq // sq
    d = ct.shape[0]
    sw = (w_mod.shape[1] - mod_col0) // (nb * h)
    mslab = lambda rows, off: pl.BlockSpec((rows, sw), lambda b, hh: (0, off + b * h + hh))
    return pl.pallas_call(
        kern,
        grid=(nb, h),
        in_specs=[pl.BlockSpec((4, ATT_HALF_DIM), lambda b, hh: (0, 0)),
                  pl.BlockSpec((HEAD_W, 1), lambda b, hh: (0, 0)),
                  pl.BlockSpec((s, HEAD_W), lambda b, hh: (b, hh)),
                  pl.BlockSpec((s, HEAD_W), lambda b, hh: (b, h + hh)),
                  pl.BlockSpec((s, HEAD_W), lambda b, hh: (b, 2 * h + hh)),
                  pl.BlockSpec(ct.shape, lambda b, hh: (0, 0)),
                  mslab(d, mod_col0 // sw), mslab(1, mod_col0 // sw)],
        out_specs=[pl.BlockSpec((s, HEAD_W), lambda b, hh: (b, hh)), mslab(nb, 0)],
        out_shape=[jax.ShapeDtypeStruct((nb * s, h * HEAD_W), BF16),
                   jax.ShapeDtypeStruct((nb, w_mod.shape[1] - mod_col0), F32)],
        scratch_shapes=[pltpu.VMEM((s // tk, HEAD_W + ONES_ROWS, tk), BF16),
                        pltpu.VMEM((2, HEAD_W + ONES_ROWS, tq), F32),
                        pltpu.VMEM((2, 1, tq), F32),
                        pltpu.VMEM((nchains, tk, sq), F32),
                        pltpu.VMEM((nchains, tk, sq), F32)],
        compiler_params=_params(("parallel", "parallel")),
        name="diff_attn",
    )(lam4, sg_col, p, p, p, ct, w_mod, b_row)


def _group_row(a, s, grp=8):
    n, w = a.shape
    a3 = a.reshape(n // grp, grp, w)
    return jnp.broadcast_to(a3[:, s:s + 1, :], (n // grp, grp, w)).reshape(n, w)


def _hgrn2_kernel(lbl_ref, gn_ref, q_ref, f_ref, i_ref, g_ref, o_ref, st_ref, *, ch, layer):
    @pl.when(pl.program_id(1) == 0)
    def _():
        st_ref[...] = jnp.zeros(st_ref.shape, F32)

    w = HEAD_W
    lbl = lbl_ref[...]
    mx = jnp.max(lbl, axis=0, keepdims=True)
    e = jnp.exp(lbl - mx)
    lb = (jnp.sum(e[0:layer + 1], axis=0, keepdims=True)
          / jnp.sum(e, axis=0, keepdims=True))

    nt = (((1,), (1,)), ((), ()))
    row = lax.broadcasted_iota(jnp.int32, (ch, w), 0)
    arow = lax.broadcasted_iota(jnp.int32, (ch, ch), 0)
    acol = lax.broadcasted_iota(jnp.int32, (ch, ch), 1)
    tril = (arow >= acol).astype(BF16)
    on_diag = arow == acol
    levels = []
    hs = ch // 2
    while hs >= 1:
        upper = (row % (2 * hs)) >= hs
        levels.append((hs, upper, jnp.where(upper, 1.0, -1.0),
                       (arow // (2 * hs) == acol // (2 * hs))
                       & ((arow % (2 * hs)) >= hs) & ((acol % (2 * hs)) < hs)))
        hs //= 2

    def midpoint_row(b, hs):
        if hs >= 8:
            return jnp.concatenate(
                [jnp.broadcast_to(b[blk * 2 * hs + hs - 1:blk * 2 * hs + hs, :], (2 * hs, w))
                 for blk in range(ch // (2 * hs))], axis=0)
        if hs == 4:
            return _group_row(b, 3)
        return jnp.where((row % 8) < 4, _group_row(b, 1), _group_row(b, 5))

    chunk_vals = {}

    def chunk_prep(c):
        if c not in chunk_vals:
            rs = slice(c * ch, (c + 1) * ch)
            f_all = lb + (1.0 - lb) * jax.nn.sigmoid(f_ref[rs, :].astype(F32))
            g2 = jnp.log2(f_all)
            g2_hi = g2.astype(BF16)
            g2_lo = (g2 - g2_hi.astype(F32)).astype(BF16)
            b_all = (jnp.dot(tril, g2_hi, preferred_element_type=F32)
                     + jnp.dot(tril, g2_lo, preferred_element_type=F32))
            chunk_vals[c] = (f_all, b_all)
        return chunk_vals[c]

    def operands(c, h):
        rs, cs = slice(c * ch, (c + 1) * ch), slice(h * w, (h + 1) * w)
        f_all, b_all = chunk_prep(c)
        b, f = b_all[:, cs], f_all[:, cs]
        kk = 1.0 - f
        q = q_ref[rs, cs].astype(F32)
        blast = b[ch - 1:ch, :]
        ys = []
        for hs, upper, sign, _ in levels:
            if hs == 1:
                y = jnp.where(upper, q * f, kk)
            else:
                y = jnp.where(upper, q, kk) * jnp.exp2((b - midpoint_row(b, hs)) * sign)
            ys.append(y.astype(BF16))
        return dict(rs=rs, cs=cs, h=h, ys=ys,
                    qin=(q * jnp.exp2(b)).astype(BF16),
                    kout=(kk * jnp.exp2(blast - b)).astype(BF16),
                    keep=jnp.exp2(blast),
                    dsum=jnp.sum(q * kk, axis=-1, keepdims=True))

    def level_matmuls(x):
        st = st_ref[x["h"]]
        x["st"] = st
        x["o"] = lax.dot_general(x["qin"], st.astype(BF16), nt, preferred_element_type=F32)
        x["aa"] = [lax.dot_general(y, y, nt, preferred_element_type=F32) for y in x["ys"]]

    def finish(x):
        rs, cs = x["rs"], x["cs"]
        v = i_ref[rs, cs]
        att = jnp.where(on_diag, x["dsum"], 0.0).astype(BF16)
        for (_, _, _, amask), a in zip(levels, x["aa"]):
            att = jnp.where(amask, a.astype(BF16), att)
        o = x["o"] + jnp.dot(att, v, preferred_element_type=F32)
        st_ref[x["h"]] = x["keep"] * x["st"] + lax.dot_general(
            v, x["kout"], (((0,), (0,)), ((), ())), preferred_element_type=F32)
        y = _rms(o) * gn_ref[:, cs] * jax.nn.sigmoid(g_ref[rs, cs].astype(F32))
        o_ref[rs, cs] = y.astype(o_ref.dtype)

    items = [(c, h) for c in range(q_ref.shape[0] // ch) for h in range(REC_HEADS)]
    staged = []
    d1, d2 = 1, 2
    for i in range(len(items) + d2):
        if i < len(items):
            staged.append(operands(*items[i]))
        if d1 <= i < len(items) + d1:
            level_matmuls(staged[i - d1])
        if i >= d2:
            finish(staged[i - d2])


def _hgrn2(p, lb_logits, gn, nb, s, layer, rb=1024, ch=128):
    hw = REC_HEADS * HEAD_W
    nr = s // rb
    c0 = 3 * ATT_HEADS * HEAD_W // hw
    kern = functools.partial(_hgrn2_kernel, ch=ch, layer=layer)
    spec = lambda off: pl.BlockSpec((rb, hw), lambda b, i: (b * nr + i, c0 + off))
    return pl.pallas_call(
        kern,
        grid=(nb, nr),
        in_specs=[pl.BlockSpec(lb_logits.shape, lambda b, i: (0, 0)),
                  pl.BlockSpec((1, hw), lambda b, i: (0, 0)),
                  spec(0), spec(1), spec(2), spec(3)],
        out_specs=pl.BlockSpec((rb, hw), lambda b, i: (b * nr + i, 0)),
        out_shape=jax.ShapeDtypeStruct((nb * s, hw), BF16),
        scratch_shapes=[pltpu.VMEM((REC_HEADS, HEAD_W, HEAD_W), F32)],
        compiler_params=_params(("parallel", "arbitrary")),
        name="hgrn2",
    )(lb_logits, gn, p, p, p, p)


def _merge_oproj_kernel(ya_ref, yr_ref, *refs, ng):
    ga_refs, gr_refs = refs[:ng], refs[ng:2 * ng]
    x_ref, gt_ref, wa_ref, wr_ref, wo_ref, o_ref = refs[2 * ng:]
    gw = wa_ref.shape[1] // ng
    parts = []
    for g in range(ng):
        cols = slice(g * gw, (g + 1) * gw)
        ta = jnp.dot(ya_ref[...], wa_ref[:, cols], preferred_element_type=F32)
        tr = jnp.dot(yr_ref[...], wr_ref[:, cols], preferred_element_type=F32)
        parts.append((jax.nn.sigmoid(ga_refs[g][...].astype(F32)) * ta
                      + jax.nn.sigmoid(gr_refs[g][...].astype(F32)) * tr).astype(BF16))
    m = jnp.concatenate(parts, axis=1)
    o_ref[...] = x_ref[...] + gt_ref[...] * jnp.dot(m, wo_ref[...], preferred_element_type=F32)


def _merge_oproj(ya, yr, p, x2, mod4, wa, wr, wo, ga_col0, rows_per_batch, tm=512, gw=1024):
    t, ka = ya.shape
    d = wa.shape[1]
    ng = d // gw
    ga0 = ga_col0 // gw
    gr0 = (ga_col0 + d) // gw
    tpb = rows_per_batch // tm
    gate = lambda b0, g: pl.BlockSpec((tm, gw), lambda i: (i, b0 + g))
    resident = lambda w: pl.BlockSpec(w.shape, lambda i: (0, 0), pipeline_mode=pl.Buffered(1))
    return pl.pallas_call(
        functools.partial(_merge_oproj_kernel, ng=ng),
        grid=(t // tm,),
        in_specs=[pl.BlockSpec((tm, ka), lambda i: (i, 0)),
                  pl.BlockSpec((tm, ka), lambda i: (i, 0))]
                 + [gate(ga0, g) for g in range(ng)] + [gate(gr0, g) for g in range(ng)]
                 + [pl.BlockSpec((tm, d), lambda i: (i, 0)),
                    pl.BlockSpec((None, None, 1, d), lambda i: (i // tpb, 2, 0, 0)),
                    resident(wa), resident(wr), resident(wo)],
        out_specs=pl.BlockSpec((tm, d), lambda i: (i, 0)),
        out_shape=jax.ShapeDtypeStruct((t, d), F32),
        compiler_params=_params(("parallel",)),
        name="merge_oproj",
    )(ya, yr, *([p] * (2 * ng)), x2, mod4, wa, wr, wo)


def _ffn_up(h1, g, mod4, wg, wu, rows_per_batch, cast_ws):
    swiglu = lambda gate, up: gate * jax.nn.sigmoid(gate) * up
    return _adaln_matmul(h1, g, mod4, 3, [wg, wu], rows_per_batch, swiglu, "ffn_up",
                         tm=2048, tn=512, rc=256, rows=512, cast_ws=cast_ws)


def _ffn_down_kernel(a_ref, h_ref, gt_ref, fg_ref, w_ref, o_ref, *, tn, final):
    j = pl.program_id(1)
    y = jnp.dot(a_ref[...], w_ref[...], preferred_element_type=F32)
    o_ref[:, pl.ds(pl.multiple_of(j * tn, tn), tn)] = h_ref[...] + gt_ref[...] * y

    if final:
        @pl.when(j == pl.num_programs(1) - 1)
        def _():
            o_ref[...] = _rms(o_ref[...]) * fg_ref[...]


def _ffn_down(a, h1, mod4, fg, w, rows_per_batch, final, tm=1024, tn=512):
    t, d = h1.shape
    f = a.shape[1]
    tpb = rows_per_batch // tm
    return pl.pallas_call(
        functools.partial(_ffn_down_kernel, tn=tn, final=final),
        grid=(t // tm, d // tn),
        in_specs=[pl.BlockSpec((tm, f), lambda i, j: (i, 0)),
                  pl.BlockSpec((tm, tn), lambda i, j: (i, j)),
                  pl.BlockSpec((None, None, 1, tn), lambda i, j: (i // tpb, 5, 0, j)),
                  pl.BlockSpec((1, d), lambda i, j: (0, 0)),
                  pl.BlockSpec((f, tn), lambda i, j: (0, j))],
        out_specs=pl.BlockSpec((tm, d), lambda i, j: (i, 0)),
        out_shape=jax.ShapeDtypeStruct((t, d), F32),
        compiler_params=_params(("parallel", "arbitrary")),
        name="ffn_down",
    )(a, h1, mod4, fg, w)


def kernel(x, c, w_mod, b_mod, norm1_g, w_in, lambda_q1, lambda_k1, lambda_q2, lambda_k2,
           subln_g, lb_logits, gnorm_g, w_att_out, w_rec_out, w_o, norm2_g,
           w_ffn_gate, w_ffn_up, w_ffn_down, final_g):
    nb, s, d = x.shape
    depth = w_mod.shape[0]
    h = x.reshape(nb * s, d)
    for l in range(depth):
        lam_init = 0.8 - 0.6 * math.exp(-0.3 * l)
        ct, b_row = c.T, b_mod[l].reshape(1, 6 * d)
        mod_head = _mod(ct, w_mod[l], b_row, 2 * d)
        p, (wa, wr, wo) = _in_proj(h, norm1_g[l].reshape(1, d), mod_head.reshape(nb, 2, 1, d),
                                   w_in[l], s, cast_ws=(w_att_out[l], w_rec_out[l], w_o[l]))
        lam4 = jnp.stack([lambda_q1[l], lambda_k1[l], lambda_q2[l], lambda_k2[l]])
        ya, mod_tail = _attention(p, lam4, subln_g[l].reshape(HEAD_W, 1), nb, s, lam_init,
                                  ct, w_mod[l], b_row, 2 * d)
        mod4 = jnp.concatenate([mod_head, mod_tail], axis=1).reshape(nb, 6, 1, d)
        yr = _hgrn2(p, lb_logits, gnorm_g[l].reshape(1, REC_HEADS * HEAD_W), nb, s, l)
        ga_col0 = 3 * ATT_HEADS * HEAD_W + 4 * REC_HEADS * HEAD_W
        h1 = _merge_oproj(ya, yr, p, h, mod4, wa, wr, wo, ga_col0, s)
        a, (wd,) = _ffn_up(h1, norm2_g[l].reshape(1, d), mod4, w_ffn_gate[l], w_ffn_up[l], s,
                           cast_ws=(w_ffn_down[l],))
        h = _ffn_down(a, h1, mod4, final_g.reshape(1, d), wd, s, l == depth - 1)
    return h.reshape(nb, s, d)
```

```python
import functools
import math

import jax
import jax.numpy as jnp
from jax import lax
from jax.experimental import pallas as pl
from jax.experimental.pallas import tpu as pltpu

F32 = jnp.float32
BF16 = jnp.bfloat16

NORM_EPS = 1e-6
ATT_HEADS = 8
ATT_HALF_DIM = 64
HEAD_W = 128
REC_HEADS = 8
MASK_CHUNK = 64
NEG_BIG = -1e30
BF16_SUBLANES = 16
ONES_ROWS = BF16_SUBLANES

VMEM_LIMIT = 60 * 1024 * 1024


def _params(sem, vmem=VMEM_LIMIT):
    return pltpu.CompilerParams(dimension_semantics=sem, vmem_limit_bytes=vmem)


def _rms(x):
    return x * lax.rsqrt(jnp.mean(x * x, axis=-1, keepdims=True) + NORM_EPS)


def _adaln(x, g, sc, sh):
    return _rms(x) * (g * (1.0 + sc)) + sh


def _mod_kernel(ct_ref, w_ref, b_ref, o_ref):
    ct = ct_ref[...]
    cond = ct * jax.nn.sigmoid(ct)
    w = w_ref[...]
    for b in range(ct.shape[1]):
        o_ref[b:b + 1, :] = jnp.sum(w * cond[:, b:b + 1], axis=0, keepdims=True) + b_ref[...]


def _mod(ct, w_mod, b_row, n, tn=1024):
    d, nb = ct.shape
    return pl.pallas_call(
        _mod_kernel,
        grid=(n // tn,),
        in_specs=[pl.BlockSpec((d, nb), lambda j: (0, 0)),
                  pl.BlockSpec((d, tn), lambda j: (0, j)),
                  pl.BlockSpec((1, tn), lambda j: (0, j))],
        out_specs=pl.BlockSpec((nb, tn), lambda j: (0, j)),
        out_shape=jax.ShapeDtypeStruct((nb, n), F32),
        compiler_params=_params(("arbitrary",)),
        name="mod",
    )(ct, w_mod, b_row)


def _adaln_matmul_kernel(x_ref, g_ref, sh_ref, sc_ref, *refs, nw, ncast, rc, rows, finish):
    w_refs, cast_in = refs[:nw], refs[nw:nw + ncast]
    o_ref, cast_out = refs[nw + ncast], refs[nw + ncast + 1:nw + 2 * ncast + 1]
    u_bufs = refs[nw + 2 * ncast + 1:]
    i, j = pl.program_id(0), pl.program_id(1)
    nchunks = u_bufs[0].shape[0] // rc

    def normalise_into(u_ref):
        r0 = pl.multiple_of(jnp.minimum(j, nchunks - 1) * rc, rc)
        u_ref[pl.ds(r0, rc), :] = _adaln(x_ref[...], g_ref[...], sc_ref[...],
                                         sh_ref[...]).astype(BF16)

    def multiply_from(u_ref):
        wb = [w[...].astype(BF16) for w in w_refs]
        for r0 in range(0, u_ref.shape[0], rows):
            u = u_ref[r0:r0 + rows, :]
            o_ref[r0:r0 + rows, :] = finish(
                *[jnp.dot(u, w, preferred_element_type=F32) for w in wb]).astype(o_ref.dtype)

    @pl.when(i == 0)
    def _():
        normalise_into(u_bufs[0])

    for parity in range(2):
        @pl.when((i > 0) & (i % 2 == parity))
        def _(parity=parity):
            multiply_from(u_bufs[1 - parity])
            normalise_into(u_bufs[parity])
            for w_ref, wb_ref in zip(cast_in, cast_out):
                wb_ref[...] = w_ref[...].astype(BF16)


def _adaln_matmul(x2, g, mod4, mod_row, ws, rows_per_batch, finish, name, tm, tn, rc=256,
                  rows=None, cast_ws=()):
    t, d = x2.shape
    n = ws[0].shape[1]
    nt, nj, nchunks, tpb = t // tm, n // tn, tm // rc, rows_per_batch // tm
    assert nchunks <= nj
    tile = lambda i: jnp.minimum(i, nt - 1)
    col = lambda i, j: jnp.where(i > 0, j, 0)
    def slab(w):
        k = w.shape[0]
        ns = max(c for c in range(1, nt * nj + 1) if k % (c * BF16_SUBLANES) == 0)
        return pl.BlockSpec((k // ns, w.shape[1]),
                            lambda i, j: (jnp.clip((i - 1) * nj + j, 0, ns - 1), 0))

    kern = functools.partial(_adaln_matmul_kernel, nw=len(ws), ncast=len(cast_ws), rc=rc,
                             rows=rows or tm, finish=finish)
    outs = pl.pallas_call(
        kern,
        grid=(nt + 1, nj),
        in_specs=[pl.BlockSpec((rc, d), lambda i, j: (tile(i) * nchunks
                                                      + jnp.minimum(j, nchunks - 1), 0)),
                  pl.BlockSpec((1, d), lambda i, j: (0, 0)),
                  pl.BlockSpec((None, None, 1, d), lambda i, j: (tile(i) // tpb, mod_row, 0, 0)),
                  pl.BlockSpec((None, None, 1, d),
                               lambda i, j: (tile(i) // tpb, mod_row + 1, 0, 0))]
                 + [pl.BlockSpec((d, tn), lambda i, j: (0, col(i, j))) for _ in ws]
                 + [slab(w) for w in cast_ws],
        out_specs=[pl.BlockSpec((tm, tn), lambda i, j: (jnp.maximum(i - 1, 0), col(i, j)))]
                  + [slab(w) for w in cast_ws],
        out_shape=[jax.ShapeDtypeStruct((t, n), BF16)]
                  + [jax.ShapeDtypeStruct(w.shape, BF16) for w in cast_ws],
        scratch_shapes=[pltpu.VMEM((tm, d), BF16), pltpu.VMEM((tm, d), BF16)],
        compiler_params=_params(("arbitrary", "arbitrary")),
        name=name,
    )(x2, g, mod4, mod4, *ws, *cast_ws)
    return outs[0], outs[1:]


def _in_proj(x2, g, mod4, w, rows_per_batch, cast_ws):
    return _adaln_matmul(x2, g, mod4, 0, [w], rows_per_batch, lambda y: y, "in_proj",
                         tm=2048, tn=1024, rows=512, cast_ws=cast_ws)


def _col_reduce(x, op, groups=8):
    k, n = x.shape
    if k % (groups * 8) == 0:
        x = op(x.reshape(groups, k // groups, n), axis=0)
    return op(x, axis=0, keepdims=True)


def _attn_kernel(lam_ref, sg_ref, q_ref, k_ref, v_ref, ct_ref, mw_ref, mb_ref, o_ref, mvo_ref,
                 vt_ref, acc_ref, m_ref, sa_ref, sb_ref, xa_ref, xb_ref, *, tq, sq, tk,
                 lam_init):
    nkb = vt_ref.shape[0]
    nc = tq // sq

    _mod_kernel(ct_ref, mw_ref, mb_ref, mvo_ref)

    for j in range(nkb):
        vt_ref[j, 0:HEAD_W, :] = v_ref[j * tk:(j + 1) * tk, :].astype(F32).T.astype(BF16)
        vt_ref[j, HEAD_W:, :] = jnp.ones((vt_ref.shape[1] - HEAD_W, tk), BF16)

    lax.fori_loop(0, q_ref.shape[0] // tq,
                  functools.partial(_attn_q_tile, lam_ref, sg_ref, q_ref, k_ref, o_ref, vt_ref,
                                    acc_ref, m_ref, sa_ref, sb_ref, xa_ref, xb_ref,
                                    tq, sq, tk, nc, lam_init), 0)


def _attn_q_tile(lam_ref, sg_ref, q_ref, k_ref, o_ref, vt_ref, acc_ref, m_ref, sa_ref, sb_ref,
                 xa_ref, xb_ref, tq, sq, tk, nc, lam_init, qi, carry):
    q_rows = pl.ds(pl.multiple_of(qi * tq, tq), tq)
    q = (q_ref[q_rows, :].astype(F32) * (ATT_HALF_DIM ** -0.5 * math.log2(math.e))).astype(BF16)
    lane = lax.broadcasted_iota(jnp.int32, q.shape, 1)
    zero = jnp.zeros_like(q)
    qm = (jnp.where(lane < ATT_HALF_DIM, q, zero), jnp.where(lane >= ATT_HALF_DIM, q, zero))

    m_ref[...] = jnp.full(m_ref.shape, NEG_BIG, F32)
    acc_ref[...] = jnp.zeros(acc_ref.shape, F32)

    def scores(c, mi, k):
        return lax.dot_general(k, qm[mi][c * sq:(c + 1) * sq, :], (((1,), (1,)), ((), ())),
                               preferred_element_type=F32)

    def update(c, mi, vt, s, smax=None):
        cs = slice(c * sq, (c + 1) * sq)
        m_old = m_ref[mi, :, cs]
        m_new = jnp.maximum(m_old, _col_reduce(s, jnp.max) if smax is None else smax)
        p = jnp.exp2(s - m_new)
        alpha = jnp.exp2(m_old - m_new)
        acc_ref[mi, :, cs] = alpha * acc_ref[mi, :, cs] + jnp.dot(
            vt, p.astype(BF16), preferred_element_type=F32)
        m_ref[mi, :, cs] = m_new

    assert tq == 2 * tk
    chains = [(c, mi) for c in range(nc) for mi in range(2)]

    bufs = ((sa_ref, xa_ref), (sb_ref, xb_ref))

    def produce(idx, k, dst):
        c, mi = chains[idx]
        s = scores(c, mi, k)
        dst[0][idx] = s
        dst[1][idx] = _col_reduce(s, jnp.max)

    def stage(j, src, dst):
        vt = vt_ref[j]
        if dst is not None:
            kn = k_ref[pl.ds(pl.multiple_of((j + 1) * tk, tk), tk), :]
        for idx, (c, mi) in enumerate(chains):
            if dst is not None:
                produce(idx, kn, dst)
            update(c, mi, vt, src[0][idx], src[1][idx])

    @pl.when(qi > 0)
    def _():
        k0 = k_ref[0:tk, :]
        for idx in range(len(chains)):
            produce(idx, k0, bufs[0])

        def pair(pp, carry):
            stage(2 * pp, bufs[0], bufs[1])
            stage(2 * pp + 1, bufs[1], bufs[0])
            return carry

        lax.fori_loop(0, qi - 1, pair, 0)
        stage(2 * qi - 2, bufs[0], bufs[1])
        stage(2 * qi - 1, bufs[1], None)

    krow = lax.broadcasted_iota(jnp.int32, (sq, sq), 0)
    qcol = lax.broadcasted_iota(jnp.int32, (sq, sq), 1)
    dmask = krow // MASK_CHUNK <= qcol // MASK_CHUNK
    vt_tile = jnp.concatenate([vt_ref[qi * 2], vt_ref[qi * 2 + 1]], axis=1)
    pend = []
    for c in reversed(range(nc)):
        nk = (c + 1) * sq
        k = k_ref[pl.ds(pl.multiple_of(qi * tq, tq), nk), :]
        for mi in range(2):
            s = scores(c, mi, k)
            tail = jnp.where(dmask, s[c * sq:, :], NEG_BIG)
            s = tail if c == 0 else jnp.concatenate([s[:c * sq, :], tail], axis=0)
            pend.append((c, mi, vt_tile[:, :nk], s))
    for it in pend:
        update(*it)

    lp = lam_ref[...]
    lam = (jnp.exp(jnp.sum(lp[0:1] * lp[1:2], axis=-1, keepdims=True))
           - jnp.exp(jnp.sum(lp[2:3] * lp[3:4], axis=-1, keepdims=True)) + lam_init)
    o = (acc_ref[0, 0:HEAD_W, :] / acc_ref[0, HEAD_W:HEAD_W + 1, :]
         - lam * (acc_ref[1, 0:HEAD_W, :] / acc_ref[1, HEAD_W:HEAD_W + 1, :]))
    ms = jnp.mean(o * o, axis=0, keepdims=True)
    y = o * lax.rsqrt(ms + NORM_EPS) * sg_ref[...] * (1.0 - lam_init)
    o_ref[q_rows, :] = y.T.astype(o_ref.dtype)
    return carry


def _attention(p, lam4, sg_col, nb, s, lam_init, ct, w_mod, b_row, mod_col0, tq=1024, sq=256,
               tk=512):
    h = ATT_HEADS
    kern = functools.partial(_attn_kernel, tq=tq, sq=sq, tk=tk, lam_init=lam_init)
    nchains = 2 * tq // sq
    d = ct.shape[0]
    sw = (w_mod.shape[1] - mod_col0) // (nb * h)
    mslab = lambda rows, off: pl.BlockSpec((rows, sw), lambda b, hh: (0, off + b * h + hh))
    return pl.pallas_call(
        kern,
        grid=(nb, h),
        in_specs=[pl.BlockSpec((4, ATT_HALF_DIM), lambda b, hh: (0, 0)),
                  pl.BlockSpec((HEAD_W, 1), lambda b, hh: (0, 0)),
                  pl.BlockSpec((s, HEAD_W), lambda b, hh: (b, hh)),
                  pl.BlockSpec((s, HEAD_W), lambda b, hh: (b, h + hh)),
                  pl.BlockSpec((s, HEAD_W), lambda b, hh: (b, 2 * h + hh)),
                  pl.BlockSpec(ct.shape, lambda b, hh: (0, 0)),
                  mslab(d, mod_col0 // sw), mslab(1, mod_col0 // sw)],
        out_specs=[pl.BlockSpec((s, HEAD_W), lambda b, hh: (b, hh)), mslab(nb, 0)],
        out_shape=[jax.ShapeDtypeStruct((nb * s, h * HEAD_W), BF16),
                   jax.ShapeDtypeStruct((nb, w_mod.shape[1] - mod_col0), F32)],
        scratch_shapes=[pltpu.VMEM((s // tk, HEAD_W + ONES_ROWS, tk), BF16),
                        pltpu.VMEM((2, HEAD_W + ONES_ROWS, tq), F32),
                        pltpu.VMEM((2, 1, tq), F32),
                        pltpu.VMEM((nchains, tk, sq), F32),
                        pltpu.VMEM((nchains, tk, sq), F32),
                        pltpu.VMEM((nchains, 1, sq), F32),
                        pltpu.VMEM((nchains, 1, sq), F32)],
        compiler_params=_params(("parallel", "parallel")),
        name="diff_attn",
    )(lam4, sg_col, p, p, p, ct, w_mod, b_row)


def _group_row(a, s, grp=8):
    n, w = a.shape
    a3 = a.reshape(n // grp, grp, w)
    return jnp.broadcast_to(a3[:, s:s + 1, :], (n // grp, grp, w)).reshape(n, w)


def _hgrn2_kernel(lbl_ref, gn_ref, q_ref, f_ref, i_ref, g_ref, o_ref, st_ref, *, ch, layer):
    @pl.when(pl.program_id(1) == 0)
    def _():
        st_ref[...] = jnp.zeros(st_ref.shape, F32)

    w = HEAD_W
    lbl = lbl_ref[...]
    mx = jnp.max(lbl, axis=0, keepdims=True)
    e = jnp.exp(lbl - mx)
    lb = (jnp.sum(e[0:layer + 1], axis=0, keepdims=True)
          / jnp.sum(e, axis=0, keepdims=True))

    nt = (((1,), (1,)), ((), ()))
    row = lax.broadcasted_iota(jnp.int32, (ch, w), 0)
    arow = lax.broadcasted_iota(jnp.int32, (ch, ch), 0)
    acol = lax.broadcasted_iota(jnp.int32, (ch, ch), 1)
    tril = (arow >= acol).astype(BF16)
    on_diag = arow == acol
    levels = []
    hs = ch // 2
    while hs >= 1:
        upper = (row % (2 * hs)) >= hs
        levels.append((hs, upper, jnp.where(upper, 1.0, -1.0),
                       (arow // (2 * hs) == acol // (2 * hs))
                       & ((arow % (2 * hs)) >= hs) & ((acol % (2 * hs)) < hs)))
        hs //= 2

    def midpoint_row(b, hs):
        if hs >= 8:
            return jnp.concatenate(
                [jnp.broadcast_to(b[blk * 2 * hs + hs - 1:blk * 2 * hs + hs, :], (2 * hs, w))
                 for blk in range(ch // (2 * hs))], axis=0)
        if hs == 4:
            return _group_row(b, 3)
        return jnp.where((row % 8) < 4, _group_row(b, 1), _group_row(b, 5))

    chunk_vals = {}

    def chunk_prep(c):
        if c not in chunk_vals:
            rs = slice(c * ch, (c + 1) * ch)
            f_all = lb + (1.0 - lb) * jax.nn.sigmoid(f_ref[rs, :].astype(F32))
            g2 = jnp.log2(f_all)
            g2_hi = g2.astype(BF16)
            g2_lo = (g2 - g2_hi.astype(F32)).astype(BF16)
            b_all = (jnp.dot(tril, g2_hi, preferred_element_type=F32)
                     + jnp.dot(tril, g2_lo, preferred_element_type=F32))
            chunk_vals[c] = (f_all, b_all)
        return chunk_vals[c]

    def operands(c, h):
        rs, cs = slice(c * ch, (c + 1) * ch), slice(h * w, (h + 1) * w)
        f_all, b_all = chunk_prep(c)
        b, f = b_all[:, cs], f_all[:, cs]
        kk = 1.0 - f
        q = q_ref[rs, cs].astype(F32)
        blast = b[ch - 1:ch, :]
        ys = []
        for hs, upper, sign, _ in levels:
            if hs == 1:
                y = jnp.where(upper, q * f, kk)
            else:
                y = jnp.where(upper, q, kk) * jnp.exp2((b - midpoint_row(b, hs)) * sign)
            ys.append(y.astype(BF16))
        return dict(rs=rs, cs=cs, h=h, ys=ys,
                    qin=(q * jnp.exp2(b)).astype(BF16),
                    kout=(kk * jnp.exp2(blast - b)).astype(BF16),
                    keep=jnp.exp2(blast),
                    dsum=jnp.sum(q * kk, axis=-1, keepdims=True))

    def level_matmuls(x):
        st = st_ref[x["h"]]
        x["st"] = st
        x["o"] = lax.dot_general(x["qin"], st.astype(BF16), nt, preferred_element_type=F32)
        x["aa"] = [lax.dot_general(y, y, nt, preferred_element_type=F32) for y in x["ys"]]

    def finish(x):
        rs, cs = x["rs"], x["cs"]
        v = i_ref[rs, cs]
        att = jnp.where(on_diag, x["dsum"], 0.0).astype(BF16)
        for (_, _, _, amask), a in zip(levels, x["aa"]):
            att = jnp.where(amask, a.astype(BF16), att)
        o = x["o"] + jnp.dot(att, v, preferred_element_type=F32)
        st_ref[x["h"]] = x["keep"] * x["st"] + lax.dot_general(
            v, x["kout"], (((0,), (0,)), ((), ())), preferred_element_type=F32)
        y = _rms(o) * gn_ref[:, cs] * jax.nn.sigmoid(g_ref[rs, cs].astype(F32))
        o_ref[rs, cs] = y.astype(o_ref.dtype)

    items = [(c, h) for c in range(q_ref.shape[0] // ch) for h in range(REC_HEADS)]
    staged = []
    d1, d2 = 1, 2
    for i in range(len(items) + d2):
        if i < len(items):
            staged.append(operands(*items[i]))
        if d1 <= i < len(items) + d1:
            level_matmuls(staged[i - d1])
        if i >= d2:
            finish(staged[i - d2])


def _hgrn2(p, lb_logits, gn, nb, s, layer, rb=512, ch=128):
    hw = REC_HEADS * HEAD_W
    nr = s // rb
    c0 = 3 * ATT_HEADS * HEAD_W // hw
    kern = functools.partial(_hgrn2_kernel, ch=ch, layer=layer)
    spec = lambda off: pl.BlockSpec((rb, hw), lambda b, i: (b * nr + i, c0 + off))
    return pl.pallas_call(
        kern,
        grid=(nb, nr),
        in_specs=[pl.BlockSpec(lb_logits.shape, lambda b, i: (0, 0)),
                  pl.BlockSpec((1, hw), lambda b, i: (0, 0)),
                  spec(0), spec(1), spec(2), spec(3)],
        out_specs=pl.BlockSpec((rb, hw), lambda b, i: (b * nr + i, 0)),
        out_shape=jax.ShapeDtypeStruct((nb * s, hw), BF16),
        scratch_shapes=[pltpu.VMEM((REC_HEADS, HEAD_W, HEAD_W), F32)],
        compiler_params=_params(("parallel", "arbitrary")),
        name="hgrn2",
    )(lb_logits, gn, p, p, p, p)


def _merge_oproj_kernel(ya_ref, yr_ref, *refs, ng):
    ga_refs, gr_refs = refs[:ng], refs[ng:2 * ng]
    x_ref, gt_ref, wa_ref, wr_ref, wo_ref, o_ref = refs[2 * ng:]
    gw = wa_ref.shape[1] // ng
    parts = []
    for g in range(ng):
        cols = slice(g * gw, (g + 1) * gw)
        ta = jnp.dot(ya_ref[...], wa_ref[:, cols], preferred_element_type=F32)
        tr = jnp.dot(yr_ref[...], wr_ref[:, cols], preferred_element_type=F32)
        parts.append((jax.nn.sigmoid(ga_refs[g][...].astype(F32)) * ta
                      + jax.nn.sigmoid(gr_refs[g][...].astype(F32)) * tr).astype(BF16))
    m = jnp.concatenate(parts, axis=1)
    o_ref[...] = x_ref[...] + gt_ref[...] * jnp.dot(m, wo_ref[...], preferred_element_type=F32)


def _merge_oproj(ya, yr, p, x2, mod4, wa, wr, wo, ga_col0, rows_per_batch, tm=512, gw=1024):
    t, ka = ya.shape
    d = wa.shape[1]
    ng = d // gw
    ga0 = ga_col0 // gw
    gr0 = (ga_col0 + d) // gw
    tpb = rows_per_batch // tm
    gate = lambda b0, g: pl.BlockSpec((tm, gw), lambda i: (i, b0 + g))
    resident = lambda w: pl.BlockSpec(w.shape, lambda i: (0, 0), pipeline_mode=pl.Buffered(1))
    return pl.pallas_call(
        functools.partial(_merge_oproj_kernel, ng=ng),
        grid=(t // tm,),
        in_specs=[pl.BlockSpec((tm, ka), lambda i: (i, 0)),
                  pl.BlockSpec((tm, ka), lambda i: (i, 0))]
                 + [gate(ga0, g) for g in range(ng)] + [gate(gr0, g) for g in range(ng)]
                 + [pl.BlockSpec((tm, d), lambda i: (i, 0)),
                    pl.BlockSpec((None, None, 1, d), lambda i: (i // tpb, 2, 0, 0)),
                    resident(wa), resident(wr), resident(wo)],
        out_specs=pl.BlockSpec((tm, d), lambda i: (i, 0)),
        out_shape=jax.ShapeDtypeStruct((t, d), F32),
        compiler_params=_params(("parallel",)),
        name="merge_oproj",
    )(ya, yr, *([p] * (2 * ng)), x2, mod4, wa, wr, wo)


def _ffn_up(h1, g, mod4, wg, wu, rows_per_batch, cast_ws):
    swiglu = lambda gate, up: gate * jax.nn.sigmoid(gate) * up
    return _adaln_matmul(h1, g, mod4, 3, [wg, wu], rows_per_batch, swiglu, "ffn_up",
                         tm=2048, tn=512, rc=256, rows=512, cast_ws=cast_ws)


def _ffn_down_kernel(a_ref, h_ref, gt_ref, fg_ref, w_ref, o_ref, *, tn, final):
    j = pl.program_id(1)
    y = jnp.dot(a_ref[...], w_ref[...], preferred_element_type=F32)
    o_ref[:, pl.ds(pl.multiple_of(j * tn, tn), tn)] = h_ref[...] + gt_ref[...] * y

    if final:
        @pl.when(j == pl.num_programs(1) - 1)
        def _():
            o_ref[...] = _rms(o_ref[...]) * fg_ref[...]


def _ffn_down(a, h1, mod4, fg, w, rows_per_batch, final, tm=1024, tn=512):
    t, d = h1.shape
    f = a.shape[1]
    tpb = rows_per_batch // tm
    return pl.pallas_call(
        functools.partial(_ffn_down_kernel, tn=tn, final=final),
        grid=(t // tm, d // tn),
        in_specs=[pl.BlockSpec((tm, f), lambda i, j: (i, 0)),
                  pl.BlockSpec((tm, tn), lambda i, j: (i, j)),
                  pl.BlockSpec((None, None, 1, tn), lambda i, j: (i // tpb, 5, 0, j)),
                  pl.BlockSpec((1, d), lambda i, j: (0, 0)),
                  pl.BlockSpec((f, tn), lambda i, j: (0, j))],
        out_specs=pl.BlockSpec((tm, d), lambda i, j: (i, 0)),
        out_shape=jax.ShapeDtypeStruct((t, d), F32),
        compiler_params=_params(("parallel", "arbitrary")),
        name="ffn_down",
    )(a, h1, mod4, fg, w)


def kernel(x, c, w_mod, b_mod, norm1_g, w_in, lambda_q1, lambda_k1, lambda_q2, lambda_k2,
           subln_g, lb_logits, gnorm_g, w_att_out, w_rec_out, w_o, norm2_g,
           w_ffn_gate, w_ffn_up, w_ffn_down, final_g):
    nb, s, d = x.shape
    depth = w_mod.shape[0]
    h = x.reshape(nb * s, d)
    for l in range(depth):
        lam_init = 0.8 - 0.6 * math.exp(-0.3 * l)
        ct, b_row = c.T, b_mod[l].reshape(1, 6 * d)
        mod_head = _mod(ct, w_mod[l], b_row, 2 * d)
        p, (wa, wr, wo) = _in_proj(h, norm1_g[l].reshape(1, d), mod_head.reshape(nb, 2, 1, d),
                                   w_in[l], s, cast_ws=(w_att_out[l], w_rec_out[l], w_o[l]))
        lam4 = jnp.stack([lambda_q1[l], lambda_k1[l], lambda_q2[l], lambda_k2[l]])
        ya, mod_tail = _attention(p, lam4, subln_g[l].reshape(HEAD_W, 1), nb, s, lam_init,
                                  ct, w_mod[l], b_row, 2 * d)
        mod4 = jnp.concatenate([mod_head, mod_tail], axis=1).reshape(nb, 6, 1, d)
        yr = _hgrn2(p, lb_logits, gnorm_g[l].reshape(1, REC_HEADS * HEAD_W), nb, s, l)
        ga_col0 = 3 * ATT_HEADS * HEAD_W + 4 * REC_HEADS * HEAD_W
        h1 = _merge_oproj(ya, yr, p, h, mod4, wa, wr, wo, ga_col0, s)
        a, (wd,) = _ffn_up(h1, norm2_g[l].reshape(1, d), mod4, w_ffn_gate[l], w_ffn_up[l], s,
                           cast_ws=(w_ffn_down[l],))
        h = _ffn_down(a, h1, mod4, final_g.reshape(1, d), wd, s, l == depth - 1)
    return h.reshape(nb, s, d)
```

```python
import functools
import math

import jax
import jax.numpy as jnp
from jax import lax
from jax.experimental import pallas as pl
from jax.experimental.pallas import tpu as pltpu

F32 = jnp.float32
BF16 = jnp.bfloat16

NORM_EPS = 1e-6
ATT_HEADS = 8
ATT_HALF_DIM = 64
HEAD_W = 128
REC_HEADS = 8
MASK_CHUNK = 64
NEG_BIG = -1e30
BF16_SUBLANES = 16
ONES_ROWS = BF16_SUBLANES
DIAG_AHEAD = 4

VMEM_LIMIT = 60 * 1024 * 1024


def _params(sem, vmem=VMEM_LIMIT):
    return pltpu.CompilerParams(dimension_semantics=sem, vmem_limit_bytes=vmem)


def _rms(x):
    return x * lax.rsqrt(jnp.mean(x * x, axis=-1, keepdims=True) + NORM_EPS)


def _adaln(x, g, sc, sh):
    return _rms(x) * (g * (1.0 + sc)) + sh


def _mod_kernel(ct_ref, w_ref, b_ref, o_ref):
    ct = ct_ref[...]
    cond = ct * jax.nn.sigmoid(ct)
    w = w_ref[...]
    for b in range(ct.shape[1]):
        o_ref[b:b + 1, :] = jnp.sum(w * cond[:, b:b + 1], axis=0, keepdims=True) + b_ref[...]


def _mod(ct, w_mod, b_row, n, tn=1024):
    d, nb = ct.shape
    return pl.pallas_call(
        _mod_kernel,
        grid=(n // tn,),
        in_specs=[pl.BlockSpec((d, nb), lambda j: (0, 0)),
                  pl.BlockSpec((d, tn), lambda j: (0, j)),
                  pl.BlockSpec((1, tn), lambda j: (0, j))],
        out_specs=pl.BlockSpec((nb, tn), lambda j: (0, j)),
        out_shape=jax.ShapeDtypeStruct((nb, n), F32),
        compiler_params=_params(("arbitrary",)),
        name="mod",
    )(ct, w_mod, b_row)


def _adaln_matmul_kernel(x_ref, g_ref, sh_ref, sc_ref, *refs, nw, ncast, rc, rows, finish):
    w_refs, cast_in = refs[:nw], refs[nw:nw + ncast]
    o_ref, cast_out = refs[nw + ncast], refs[nw + ncast + 1:nw + 2 * ncast + 1]
    u_bufs = refs[nw + 2 * ncast + 1:]
    i, j = pl.program_id(0), pl.program_id(1)
    nchunks = u_bufs[0].shape[0] // rc

    def normalise_into(u_ref):
        r0 = pl.multiple_of(jnp.minimum(j, nchunks - 1) * rc, rc)
        u_ref[pl.ds(r0, rc), :] = _adaln(x_ref[...], g_ref[...], sc_ref[...],
                                         sh_ref[...]).astype(BF16)

    def multiply_from(u_ref):
        wb = [w[...].astype(BF16) for w in w_refs]
        for r0 in range(0, u_ref.shape[0], rows):
            u = u_ref[r0:r0 + rows, :]
            o_ref[r0:r0 + rows, :] = finish(
                *[jnp.dot(u, w, preferred_element_type=F32) for w in wb]).astype(o_ref.dtype)

    @pl.when(i == 0)
    def _():
        normalise_into(u_bufs[0])

    for parity in range(2):
        @pl.when((i > 0) & (i % 2 == parity))
        def _(parity=parity):
            multiply_from(u_bufs[1 - parity])
            normalise_into(u_bufs[parity])
            for w_ref, wb_ref in zip(cast_in, cast_out):
                wb_ref[...] = w_ref[...].astype(BF16)


def _adaln_matmul(x2, g, mod4, mod_row, ws, rows_per_batch, finish, name, tm, tn, rc=256,
                  rows=None, cast_ws=()):
    t, d = x2.shape
    n = ws[0].shape[1]
    nt, nj, nchunks, tpb = t // tm, n // tn, tm // rc, rows_per_batch // tm
    assert nchunks <= nj
    tile = lambda i: jnp.minimum(i, nt - 1)
    col = lambda i, j: jnp.where(i > 0, j, 0)
    def slab(w):
        k = w.shape[0]
        ns = max(c for c in range(1, nt * nj + 1) if k % (c * BF16_SUBLANES) == 0)
        return pl.BlockSpec((k // ns, w.shape[1]),
                            lambda i, j: (jnp.clip((i - 1) * nj + j, 0, ns - 1), 0))

    kern = functools.partial(_adaln_matmul_kernel, nw=len(ws), ncast=len(cast_ws), rc=rc,
                             rows=rows or tm, finish=finish)
    outs = pl.pallas_call(
        kern,
        grid=(nt + 1, nj),
        in_specs=[pl.BlockSpec((rc, d), lambda i, j: (tile(i) * nchunks
                                                      + jnp.minimum(j, nchunks - 1), 0)),
                  pl.BlockSpec((1, d), lambda i, j: (0, 0)),
                  pl.BlockSpec((None, None, 1, d), lambda i, j: (tile(i) // tpb, mod_row, 0, 0)),
                  pl.BlockSpec((None, None, 1, d),
                               lambda i, j: (tile(i) // tpb, mod_row + 1, 0, 0))]
                 + [pl.BlockSpec((d, tn), lambda i, j: (0, col(i, j))) for _ in ws]
                 + [slab(w) for w in cast_ws],
        out_specs=[pl.BlockSpec((tm, tn), lambda i, j: (jnp.maximum(i - 1, 0), col(i, j)))]
                  + [slab(w) for w in cast_ws],
        out_shape=[jax.ShapeDtypeStruct((t, n), BF16)]
                  + [jax.ShapeDtypeStruct(w.shape, BF16) for w in cast_ws],
        scratch_shapes=[pltpu.VMEM((tm, d), BF16), pltpu.VMEM((tm, d), BF16)],
        compiler_params=_params(("arbitrary", "arbitrary")),
        name=name,
    )(x2, g, mod4, mod4, *ws, *cast_ws)
    return outs[0], outs[1:]


def _in_proj(x2, g, mod4, w, rows_per_batch, cast_ws):
    return _adaln_matmul(x2, g, mod4, 0, [w], rows_per_batch, lambda y: y, "in_proj",
                         tm=2048, tn=1024, rows=512, cast_ws=cast_ws)


def _col_reduce(x, op, groups=8):
    k, n = x.shape
    if k % (groups * 8) == 0:
        x = op(x.reshape(groups, k // groups, n), axis=0)
    return op(x, axis=0, keepdims=True)


def _attn_kernel(lam_ref, sg_ref, q_ref, k_ref, v_ref, ct_ref, mw_ref, mb_ref, o_ref, mvo_ref,
                 vt_ref, acc_ref, m_ref, sa_ref, sb_ref, xa_ref, xb_ref, *, tq, sq, tk,
                 lam_init):
    nkb = vt_ref.shape[0]
    nc = tq // sq

    _mod_kernel(ct_ref, mw_ref, mb_ref, mvo_ref)

    for j in range(nkb):
        vt_ref[j, 0:HEAD_W, :] = v_ref[j * tk:(j + 1) * tk, :].astype(F32).T.astype(BF16)
        vt_ref[j, HEAD_W:, :] = jnp.ones((vt_ref.shape[1] - HEAD_W, tk), BF16)

    lax.fori_loop(0, q_ref.shape[0] // tq,
                  functools.partial(_attn_q_tile, lam_ref, sg_ref, q_ref, k_ref, o_ref, vt_ref,
                                    acc_ref, m_ref, sa_ref, sb_ref, xa_ref, xb_ref,
                                    tq, sq, tk, nc, lam_init), 0)


def _attn_q_tile(lam_ref, sg_ref, q_ref, k_ref, o_ref, vt_ref, acc_ref, m_ref, sa_ref, sb_ref,
                 xa_ref, xb_ref, tq, sq, tk, nc, lam_init, qi, carry):
    q_rows = pl.ds(pl.multiple_of(qi * tq, tq), tq)
    q = (q_ref[q_rows, :].astype(F32) * (ATT_HALF_DIM ** -0.5 * math.log2(math.e))).astype(BF16)
    lane = lax.broadcasted_iota(jnp.int32, q.shape, 1)
    zero = jnp.zeros_like(q)
    qm = (jnp.where(lane < ATT_HALF_DIM, q, zero), jnp.where(lane >= ATT_HALF_DIM, q, zero))

    m_ref[...] = jnp.full(m_ref.shape, NEG_BIG, F32)
    acc_ref[...] = jnp.zeros(acc_ref.shape, F32)

    def scores(c, mi, k):
        return lax.dot_general(k, qm[mi][c * sq:(c + 1) * sq, :], (((1,), (1,)), ((), ())),
                               preferred_element_type=F32)

    def update(c, mi, vt, s, smax=None):
        cs = slice(c * sq, (c + 1) * sq)
        m_old = m_ref[mi, :, cs]
        m_new = jnp.maximum(m_old, _col_reduce(s, jnp.max) if smax is None else smax)
        p = jnp.exp2(s - m_new)
        alpha = jnp.exp2(m_old - m_new)
        acc_ref[mi, :, cs] = alpha * acc_ref[mi, :, cs] + jnp.dot(
            vt, p.astype(BF16), preferred_element_type=F32)
        m_ref[mi, :, cs] = m_new

    assert tq == 2 * tk
    chains = [(c, mi) for c in range(nc) for mi in range(2)]

    bufs = ((sa_ref, xa_ref), (sb_ref, xb_ref))

    def produce(idx, k, dst):
        c, mi = chains[idx]
        s = scores(c, mi, k)
        dst[0][idx] = s
        dst[1][idx] = _col_reduce(s, jnp.max)

    def stage(j, src, dst):
        vt = vt_ref[j]
        if dst is not None:
            kn = k_ref[pl.ds(pl.multiple_of((j + 1) * tk, tk), tk), :]
        for idx, (c, mi) in enumerate(chains):
            if dst is not None:
                produce(idx, kn, dst)
            update(c, mi, vt, src[0][idx], src[1][idx])

    @pl.when(qi > 0)
    def _():
        k0 = k_ref[0:tk, :]
        for idx in range(len(chains)):
            produce(idx, k0, bufs[0])

        def pair(pp, carry):
            stage(2 * pp, bufs[0], bufs[1])
            stage(2 * pp + 1, bufs[1], bufs[0])
            return carry

        lax.fori_loop(0, qi - 1, pair, 0)
        stage(2 * qi - 2, bufs[0], bufs[1])
        stage(2 * qi - 1, bufs[1], None)

    krow = lax.broadcasted_iota(jnp.int32, (sq, sq), 0)
    qcol = lax.broadcasted_iota(jnp.int32, (sq, sq), 1)
    dmask = krow // MASK_CHUNK <= qcol // MASK_CHUNK
    vt_tile = jnp.concatenate([vt_ref[qi * 2], vt_ref[qi * 2 + 1]], axis=1)
    pend = []
    for c in reversed(range(nc)):
        nk = (c + 1) * sq
        k = k_ref[pl.ds(pl.multiple_of(qi * tq, tq), nk), :]
        for mi in range(2):
            s = scores(c, mi, k)
            tail = jnp.where(dmask, s[c * sq:, :], NEG_BIG)
            s = tail if c == 0 else jnp.concatenate([s[:c * sq, :], tail], axis=0)
            pend.append((c, mi, vt_tile[:, :nk], s))
            if len(pend) > DIAG_AHEAD:
                update(*pend.pop(0))
    for it in pend:
        update(*it)

    lp = lam_ref[...]
    lam = (jnp.exp(jnp.sum(lp[0:1] * lp[1:2], axis=-1, keepdims=True))
           - jnp.exp(jnp.sum(lp[2:3] * lp[3:4], axis=-1, keepdims=True)) + lam_init)
    o = (acc_ref[0, 0:HEAD_W, :] / acc_ref[0, HEAD_W:HEAD_W + 1, :]
         - lam * (acc_ref[1, 0:HEAD_W, :] / acc_ref[1, HEAD_W:HEAD_W + 1, :]))
    ms = jnp.mean(o * o, axis=0, keepdims=True)
    y = o * lax.rsqrt(ms + NORM_EPS) * sg_ref[...] * (1.0 - lam_init)
    o_ref[q_rows, :] = y.T.astype(o_ref.dtype)
    return carry


def _attention(p, lam4, sg_col, nb, s, lam_init, ct, w_mod, b_row, mod_col0, tq=1024, sq=256,
               tk=512):
    h = ATT_HEADS
    kern = functools.partial(_attn_kernel, tq=tq, sq=sq, tk=tk, lam_init=lam_init)
    nchains = 2 * tq // sq
    d = ct.shape[0]
    sw = (w_mod.shape[1] - mod_col0) // (nb * h)
    mslab = lambda rows, off: pl.BlockSpec((rows, sw), lambda b, hh: (0, off + b * h + hh))
    return pl.pallas_call(
        kern,
        grid=(nb, h),
        in_specs=[pl.BlockSpec((4, ATT_HALF_DIM), lambda b, hh: (0, 0)),
                  pl.BlockSpec((HEAD_W, 1), lambda b, hh: (0, 0)),
                  pl.BlockSpec((s, HEAD_W), lambda b, hh: (b, hh)),
                  pl.BlockSpec((s, HEAD_W), lambda b, hh: (b, h + hh)),
                  pl.BlockSpec((s, HEAD_W), lambda b, hh: (b, 2 * h + hh)),
                  pl.BlockSpec(ct.shape, lambda b, hh: (0, 0)),
                  mslab(d, mod_col0 // sw), mslab(1, mod_col0 // sw)],
        out_specs=[pl.BlockSpec((s, HEAD_W), lambda b, hh: (b, hh)), mslab(nb, 0)],
        out_shape=[jax.ShapeDtypeStruct((nb * s, h * HEAD_W), BF16),
                   jax.ShapeDtypeStruct((nb, w_mod.shape[1] - mod_col0), F32)],
        scratch_shapes=[pltpu.VMEM((s // tk, HEAD_W + ONES_ROWS, tk), BF16),
                        pltpu.VMEM((2, HEAD_W + ONES_ROWS, tq), F32),
                        pltpu.VMEM((2, 1, tq), F32),
                        pltpu.VMEM((nchains, tk, sq), F32),
                        pltpu.VMEM((nchains, tk, sq), F32),
                        pltpu.VMEM((nchains, 1, sq), F32),
                        pltpu.VMEM((nchains, 1, sq), F32)],
        compiler_params=_params(("parallel", "parallel")),
        name="diff_attn",
    )(lam4, sg_col, p, p, p, ct, w_mod, b_row)


def _group_row(a, s, grp=8):
    n, w = a.shape
    a3 = a.reshape(n // grp, grp, w)
    return jnp.broadcast_to(a3[:, s:s + 1, :], (n // grp, grp, w)).reshape(n, w)


def _hgrn2_kernel(lbl_ref, gn_ref, q_ref, f_ref, i_ref, g_ref, o_ref, st_ref, *, ch, layer):
    @pl.when(pl.program_id(1) == 0)
    def _():
        st_ref[...] = jnp.zeros(st_ref.shape, F32)

    w = HEAD_W
    lbl = lbl_ref[...]
    mx = jnp.max(lbl, axis=0, keepdims=True)
    e = jnp.exp(lbl - mx)
    lb = (jnp.sum(e[0:layer + 1], axis=0, keepdims=True)
          / jnp.sum(e, axis=0, keepdims=True))

    nt = (((1,), (1,)), ((), ()))
    row = lax.broadcasted_iota(jnp.int32, (ch, w), 0)
    arow = lax.broadcasted_iota(jnp.int32, (ch, ch), 0)
    acol = lax.broadcasted_iota(jnp.int32, (ch, ch), 1)
    tril = (arow >= acol).astype(BF16)
    on_diag = arow == acol
    levels = []
    hs = ch // 2
    while hs >= 1:
        upper = (row % (2 * hs)) >= hs
        levels.append((hs, upper, jnp.where(upper, 1.0, -1.0),
                       (arow // (2 * hs) == acol // (2 * hs))
                       & ((arow % (2 * hs)) >= hs) & ((acol % (2 * hs)) < hs)))
        hs //= 2

    def midpoint_row(b, hs):
        if hs >= 8:
            return jnp.concatenate(
                [jnp.broadcast_to(b[blk * 2 * hs + hs - 1:blk * 2 * hs + hs, :], (2 * hs, w))
                 for blk in range(ch // (2 * hs))], axis=0)
        if hs == 4:
            return _group_row(b, 3)
        return jnp.where((row % 8) < 4, _group_row(b, 1), _group_row(b, 5))

    chunk_vals = {}

    def chunk_prep(c):
        if c not in chunk_vals:
            rs = slice(c * ch, (c + 1) * ch)
            f_all = lb + (1.0 - lb) * jax.nn.sigmoid(f_ref[rs, :].astype(F32))
            g2 = jnp.log2(f_all)
            g2_hi = g2.astype(BF16)
            g2_lo = (g2 - g2_hi.astype(F32)).astype(BF16)
            b_all = (jnp.dot(tril, g2_hi, preferred_element_type=F32)
                     + jnp.dot(tril, g2_lo, preferred_element_type=F32))
            chunk_vals[c] = (f_all, b_all)
        return chunk_vals[c]

    def operands(c, h):
        rs, cs = slice(c * ch, (c + 1) * ch), slice(h * w, (h + 1) * w)
        f_all, b_all = chunk_prep(c)
        b, f = b_all[:, cs], f_all[:, cs]
        kk = 1.0 - f
        q = q_ref[rs, cs].astype(F32)
        blast = b[ch - 1:ch, :]
        ys = []
        for hs, upper, sign, _ in levels:
            if hs == 1:
                y = jnp.where(upper, q * f, kk)
            else:
                y = jnp.where(upper, q, kk) * jnp.exp2((b - midpoint_row(b, hs)) * sign)
            ys.append(y.astype(BF16))
        return dict(rs=rs, cs=cs, h=h, ys=ys,
                    qin=(q * jnp.exp2(b)).astype(BF16),
                    kout=(kk * jnp.exp2(blast - b)).astype(BF16),
                    keep=jnp.exp2(blast),
                    dsum=jnp.sum(q * kk, axis=-1, keepdims=True))

    def level_matmuls(x):
        st = st_ref[x["h"]]
        x["st"] = st
        x["o"] = lax.dot_general(x["qin"], st.astype(BF16), nt, preferred_element_type=F32)
        x["aa"] = [lax.dot_general(y, y, nt, preferred_element_type=F32) for y in x["ys"]]

    def finish(x):
        rs, cs = x["rs"], x["cs"]
        v = i_ref[rs, cs]
        att = jnp.where(on_diag, x["dsum"], 0.0).astype(BF16)
        for (_, _, _, amask), a in zip(levels, x["aa"]):
            att = jnp.where(amask, a.astype(BF16), att)
        o = x["o"] + jnp.dot(att, v, preferred_element_type=F32)
        st_ref[x["h"]] = x["keep"] * x["st"] + lax.dot_general(
            v, x["kout"], (((0,), (0,)), ((), ())), preferred_element_type=F32)
        y = _rms(o) * gn_ref[:, cs] * jax.nn.sigmoid(g_ref[rs, cs].astype(F32))
        o_ref[rs, cs] = y.astype(o_ref.dtype)

    items = [(c, h) for c in range(q_ref.shape[0] // ch) for h in range(REC_HEADS)]
    staged = []
    d1, d2 = 1, 2
    for i in range(len(items) + d2):
        if i < len(items):
            staged.append(operands(*items[i]))
        if d1 <= i < len(items) + d1:
            level_matmuls(staged[i - d1])
        if i >= d2:
            finish(staged[i - d2])


def _hgrn2(p, lb_logits, gn, nb, s, layer, rb=512, ch=128):
    hw = REC_HEADS * HEAD_W
    nr = s // rb
    c0 = 3 * ATT_HEADS * HEAD_W // hw
    kern = functools.partial(_hgrn2_kernel, ch=ch, layer=layer)
    spec = lambda off: pl.BlockSpec((rb, hw), lambda b, i: (b * nr + i, c0 + off))
    return pl.pallas_call(
        kern,
        grid=(nb, nr),
        in_specs=[pl.BlockSpec(lb_logits.shape, lambda b, i: (0, 0)),
                  pl.BlockSpec((1, hw), lambda b, i: (0, 0)),
                  spec(0), spec(1), spec(2), spec(3)],
        out_specs=pl.BlockSpec((rb, hw), lambda b, i: (b * nr + i, 0)),
        out_shape=jax.ShapeDtypeStruct((nb * s, hw), BF16),
        scratch_shapes=[pltpu.VMEM((REC_HEADS, HEAD_W, HEAD_W), F32)],
        compiler_params=_params(("parallel", "arbitrary")),
        name="hgrn2",
    )(lb_logits, gn, p, p, p, p)


def _merge_oproj_kernel(ya_ref, yr_ref, *refs, ng):
    ga_refs, gr_refs = refs[:ng], refs[ng:2 * ng]
    x_ref, gt_ref, wa_ref, wr_ref, wo_ref, o_ref = refs[2 * ng:]
    gw = wa_ref.shape[1] // ng
    parts = []
    for g in range(ng):
        cols = slice(g * gw, (g + 1) * gw)
        ta = jnp.dot(ya_ref[...], wa_ref[:, cols], preferred_element_type=F32)
        tr = jnp.dot(yr_ref[...], wr_ref[:, cols], preferred_element_type=F32)
        parts.append((jax.nn.sigmoid(ga_refs[g][...].astype(F32)) * ta
                      + jax.nn.sigmoid(gr_refs[g][...].astype(F32)) * tr).astype(BF16))
    m = jnp.concatenate(parts, axis=1)
    o_ref[...] = x_ref[...] + gt_ref[...] * jnp.dot(m, wo_ref[...], preferred_element_type=F32)


def _merge_oproj(ya, yr, p, x2, mod4, wa, wr, wo, ga_col0, rows_per_batch, tm=512, gw=1024):
    t, ka = ya.shape
    d = wa.shape[1]
    ng = d // gw
    ga0 = ga_col0 // gw
    gr0 = (ga_col0 + d) // gw
    tpb = rows_per_batch // tm
    gate = lambda b0, g: pl.BlockSpec((tm, gw), lambda i: (i, b0 + g))
    resident = lambda w: pl.BlockSpec(w.shape, lambda i: (0, 0), pipeline_mode=pl.Buffered(1))
    return pl.pallas_call(
        functools.partial(_merge_oproj_kernel, ng=ng),
        grid=(t // tm,),
        in_specs=[pl.BlockSpec((tm, ka), lambda i: (i, 0)),
                  pl.BlockSpec((tm, ka), lambda i: (i, 0))]
                 + [gate(ga0, g) for g in range(ng)] + [gate(gr0, g) for g in range(ng)]
                 + [pl.BlockSpec((tm, d), lambda i: (i, 0)),
                    pl.BlockSpec((None, None, 1, d), lambda i: (i // tpb, 2, 0, 0)),
                    resident(wa), resident(wr), resident(wo)],
        out_specs=pl.BlockSpec((tm, d), lambda i: (i, 0)),
        out_shape=jax.ShapeDtypeStruct((t, d), F32),
        compiler_params=_params(("parallel",)),
        name="merge_oproj",
    )(ya, yr, *([p] * (2 * ng)), x2, mod4, wa, wr, wo)


def _ffn_up(h1, g, mod4, wg, wu, rows_per_batch, cast_ws):
    swiglu = lambda gate, up: gate * jax.nn.sigmoid(gate) * up
    return _adaln_matmul(h1, g, mod4, 3, [wg, wu], rows_per_batch, swiglu, "ffn_up",
                         tm=2048, tn=512, rc=256, rows=256, cast_ws=cast_ws)


def _ffn_down_kernel(a_ref, h_ref, gt_ref, fg_ref, w_ref, o_ref, *, tn, final):
    j = pl.program_id(1)
    y = jnp.dot(a_ref[...], w_ref[...], preferred_element_type=F32)
    o_ref[:, pl.ds(pl.multiple_of(j * tn, tn), tn)] = h_ref[...] + gt_ref[...] * y

    if final:
        @pl.when(j == pl.num_programs(1) - 1)
        def _():
            o_ref[...] = _rms(o_ref[...]) * fg_ref[...]


def _ffn_down(a, h1, mod4, fg, w, rows_per_batch, final, tm=1024, tn=512):
    t, d = h1.shape
    f = a.shape[1]
    tpb = rows_per_batch // tm
    return pl.pallas_call(
        functools.partial(_ffn_down_kernel, tn=tn, final=final),
        grid=(t // tm, d // tn),
        in_specs=[pl.BlockSpec((tm, f), lambda i, j: (i, 0)),
                  pl.BlockSpec((tm, tn), lambda i, j: (i, j)),
                  pl.BlockSpec((None, None, 1, tn), lambda i, j: (i // tpb, 5, 0, j)),
                  pl.BlockSpec((1, d), lambda i, j: (0, 0)),
                  pl.BlockSpec((f, tn), lambda i, j: (0, j))],
        out_specs=pl.BlockSpec((tm, d), lambda i, j: (i, 0)),
        out_shape=jax.ShapeDtypeStruct((t, d), F32),
        compiler_params=_params(("parallel", "arbitrary")),
        name="ffn_down",
    )(a, h1, mod4, fg, w)


def kernel(x, c, w_mod, b_mod, norm1_g, w_in, lambda_q1, lambda_k1, lambda_q2, lambda_k2,
           subln_g, lb_logits, gnorm_g, w_att_out, w_rec_out, w_o, norm2_g,
           w_ffn_gate, w_ffn_up, w_ffn_down, final_g):
    nb, s, d = x.shape
    depth = w_mod.shape[0]
    h = x.reshape(nb * s, d)
    for l in range(depth):
        lam_init = 0.8 - 0.6 * math.exp(-0.3 * l)
        ct, b_row = c.T, b_mod[l].reshape(1, 6 * d)
        mod_head = _mod(ct, w_mod[l], b_row, 2 * d)
        p, (wa, wr, wo) = _in_proj(h, norm1_g[l].reshape(1, d), mod_head.reshape(nb, 2, 1, d),
                                   w_in[l], s, cast_ws=(w_att_out[l], w_rec_out[l], w_o[l]))
        lam4 = jnp.stack([lambda_q1[l], lambda_k1[l], lambda_q2[l], lambda_k2[l]])
        ya, mod_tail = _attention(p, lam4, subln_g[l].reshape(HEAD_W, 1), nb, s, lam_init,
                                  ct, w_mod[l], b_row, 2 * d)
        mod4 = jnp.concatenate([mod_head, mod_tail], axis=1).reshape(nb, 6, 1, d)
        yr = _hgrn2(p, lb_logits, gnorm_g[l].reshape(1, REC_HEADS * HEAD_W), nb, s, l)
        ga_col0 = 3 * ATT_HEADS * HEAD_W + 4 * REC_HEADS * HEAD_W
        h1 = _merge_oproj(ya, yr, p, h, mod4, wa, wr, wo, ga_col0, s)
        a, (wd,) = _ffn_up(h1, norm2_g[l].reshape(1, d), mod4, w_ffn_gate[l], w_ffn_up[l], s,
                           cast_ws=(w_ffn_down[l],))
        h = _ffn_down(a, h1, mod4, final_g.reshape(1, d), wd, s, l == depth - 1)
    return h.reshape(nb, s, d)
```

```python
import functools
import math

import jax
import jax.numpy as jnp
from jax import lax
from jax.experimental import pallas as pl
from jax.experimental.pallas import tpu as pltpu

F32 = jnp.float32
BF16 = jnp.bfloat16

NORM_EPS = 1e-6
ATT_HEADS = 8
ATT_HALF_DIM = 64
HEAD_W = 128
REC_HEADS = 8
MASK_CHUNK = 64
NEG_BIG = -1e30
BF16_SUBLANES = 16
ONES_ROWS = BF16_SUBLANES
DIAG_AHEAD = 4

VMEM_LIMIT = 60 * 1024 * 1024


def _params(sem, vmem=VMEM_LIMIT):
    return pltpu.CompilerParams(dimension_semantics=sem, vmem_limit_bytes=vmem)


def _rms(x):
    return x * lax.rsqrt(jnp.mean(x * x, axis=-1, keepdims=True) + NORM_EPS)


def _adaln(x, g, sc, sh):
    return _rms(x) * (g * (1.0 + sc)) + sh


def _mod_kernel(ct_ref, w_ref, b_ref, o_ref):
    ct = ct_ref[...]
    cond = ct * jax.nn.sigmoid(ct)
    w = w_ref[...]
    for b in range(ct.shape[1]):
        o_ref[b:b + 1, :] = jnp.sum(w * cond[:, b:b + 1], axis=0, keepdims=True) + b_ref[...]


def _mod(ct, w_mod, b_row, n, tn=1024):
    d, nb = ct.shape
    return pl.pallas_call(
        _mod_kernel,
        grid=(n // tn,),
        in_specs=[pl.BlockSpec((d, nb), lambda j: (0, 0)),
                  pl.BlockSpec((d, tn), lambda j: (0, j)),
                  pl.BlockSpec((1, tn), lambda j: (0, j))],
        out_specs=pl.BlockSpec((nb, tn), lambda j: (0, j)),
        out_shape=jax.ShapeDtypeStruct((nb, n), F32),
        compiler_params=_params(("arbitrary",)),
        name="mod",
    )(ct, w_mod, b_row)


def _adaln_matmul_kernel(x_ref, g_ref, sh_ref, sc_ref, *refs, nw, ncast, rc, rows, finish):
    w_refs, cast_in = refs[:nw], refs[nw:nw + ncast]
    o_ref, cast_out = refs[nw + ncast], refs[nw + ncast + 1:nw + 2 * ncast + 1]
    u_bufs = refs[nw + 2 * ncast + 1:]
    i, j = pl.program_id(0), pl.program_id(1)
    nchunks = u_bufs[0].shape[0] // rc

    def normalise_into(u_ref):
        r0 = pl.multiple_of(jnp.minimum(j, nchunks - 1) * rc, rc)
        u_ref[pl.ds(r0, rc), :] = _adaln(x_ref[...], g_ref[...], sc_ref[...],
                                         sh_ref[...]).astype(BF16)

    def multiply_from(u_ref):
        wb = [w[...].astype(BF16) for w in w_refs]
        for r0 in range(0, u_ref.shape[0], rows):
            u = u_ref[r0:r0 + rows, :]
            o_ref[r0:r0 + rows, :] = finish(
                *[jnp.dot(u, w, preferred_element_type=F32) for w in wb]).astype(o_ref.dtype)

    @pl.when(i == 0)
    def _():
        normalise_into(u_bufs[0])

    for parity in range(2):
        @pl.when((i > 0) & (i % 2 == parity))
        def _(parity=parity):
            multiply_from(u_bufs[1 - parity])
            normalise_into(u_bufs[parity])
            for w_ref, wb_ref in zip(cast_in, cast_out):
                wb_ref[...] = w_ref[...].astype(BF16)


def _adaln_matmul(x2, g, mod4, mod_row, ws, rows_per_batch, finish, name, tm, tn, rc=256,
                  rows=None, cast_ws=()):
    t, d = x2.shape
    n = ws[0].shape[1]
    nt, nj, nchunks, tpb = t // tm, n // tn, tm // rc, rows_per_batch // tm
    assert nchunks <= nj
    tile = lambda i: jnp.minimum(i, nt - 1)
    col = lambda i, j: jnp.where(i > 0, j, 0)
    def slab(w):
        k = w.shape[0]
        ns = max(c for c in range(1, nt * nj + 1) if k % (c * BF16_SUBLANES) == 0)
        return pl.BlockSpec((k // ns, w.shape[1]),
                            lambda i, j: (jnp.clip((i - 1) * nj + j, 0, ns - 1), 0))

    kern = functools.partial(_adaln_matmul_kernel, nw=len(ws), ncast=len(cast_ws), rc=rc,
                             rows=rows or tm, finish=finish)
    outs = pl.pallas_call(
        kern,
        grid=(nt + 1, nj),
        in_specs=[pl.BlockSpec((rc, d), lambda i, j: (tile(i) * nchunks
                                                      + jnp.minimum(j, nchunks - 1), 0)),
                  pl.BlockSpec((1, d), lambda i, j: (0, 0)),
                  pl.BlockSpec((None, None, 1, d), lambda i, j: (tile(i) // tpb, mod_row, 0, 0)),
                  pl.BlockSpec((None, None, 1, d),
                               lambda i, j: (tile(i) // tpb, mod_row + 1, 0, 0))]
                 + [pl.BlockSpec((d, tn), lambda i, j: (0, col(i, j))) for _ in ws]
                 + [slab(w) for w in cast_ws],
        out_specs=[pl.BlockSpec((tm, tn), lambda i, j: (jnp.maximum(i - 1, 0), col(i, j)))]
                  + [slab(w) for w in cast_ws],
        out_shape=[jax.ShapeDtypeStruct((t, n), BF16)]
                  + [jax.ShapeDtypeStruct(w.shape, BF16) for w in cast_ws],
        scratch_shapes=[pltpu.VMEM((tm, d), BF16), pltpu.VMEM((tm, d), BF16)],
        compiler_params=_params(("arbitrary", "arbitrary")),
        name=name,
    )(x2, g, mod4, mod4, *ws, *cast_ws)
    return outs[0], outs[1:]


def _in_proj(x2, g, mod4, w, rows_per_batch, cast_ws):
    return _adaln_matmul(x2, g, mod4, 0, [w], rows_per_batch, lambda y: y, "in_proj",
                         tm=2048, tn=1024, rows=512, cast_ws=cast_ws)


def _col_reduce(x, op, groups=8):
    k, n = x.shape
    if k % (groups * 8) == 0:
        x = op(x.reshape(groups, k // groups, n), axis=0)
    return op(x, axis=0, keepdims=True)


def _attn_kernel(lam_ref, sg_ref, q_ref, k_ref, v_ref, ct_ref, mw_ref, mb_ref, o_ref, mvo_ref,
                 vt_ref, acc_ref, m_ref, sa_ref, sb_ref, xa_ref, xb_ref, *, tq, sq, tk,
                 lam_init):
    nkb = vt_ref.shape[0]
    nc = tq // sq

    _mod_kernel(ct_ref, mw_ref, mb_ref, mvo_ref)

    for j in range(nkb):
        vt_ref[j, 0:HEAD_W, :] = v_ref[j * tk:(j + 1) * tk, :].astype(F32).T.astype(BF16)
        vt_ref[j, HEAD_W:, :] = jnp.ones((vt_ref.shape[1] - HEAD_W, tk), BF16)

    lax.fori_loop(0, q_ref.shape[0] // tq,
                  functools.partial(_attn_q_tile, lam_ref, sg_ref, q_ref, k_ref, o_ref, vt_ref,
                                    acc_ref, m_ref, sa_ref, sb_ref, xa_ref, xb_ref,
                                    tq, sq, tk, nc, lam_init), 0)


def _attn_q_tile(lam_ref, sg_ref, q_ref, k_ref, o_ref, vt_ref, acc_ref, m_ref, sa_ref, sb_ref,
                 xa_ref, xb_ref, tq, sq, tk, nc, lam_init, qi, carry):
    q_rows = pl.ds(pl.multiple_of(qi * tq, tq), tq)
    q = (q_ref[q_rows, :].astype(F32) * (ATT_HALF_DIM ** -0.5 * math.log2(math.e))).astype(BF16)
    lane = lax.broadcasted_iota(jnp.int32, q.shape, 1)
    zero = jnp.zeros_like(q)
    qm = (jnp.where(lane < ATT_HALF_DIM, q, zero), jnp.where(lane >= ATT_HALF_DIM, q, zero))

    m_ref[...] = jnp.full(m_ref.shape, NEG_BIG, F32)
    acc_ref[...] = jnp.zeros(acc_ref.shape, F32)

    def scores(c, mi, k):
        return lax.dot_general(k, qm[mi][c * sq:(c + 1) * sq, :], (((1,), (1,)), ((), ())),
                               preferred_element_type=F32)

    def update(c, mi, vt, s, smax=None):
        cs = slice(c * sq, (c + 1) * sq)
        m_old = m_ref[mi, :, cs]
        m_new = jnp.maximum(m_old, _col_reduce(s, jnp.max) if smax is None else smax)
        p = jnp.exp2(s - m_new)
        alpha = jnp.exp2(m_old - m_new)
        acc_ref[mi, :, cs] = alpha * acc_ref[mi, :, cs] + jnp.dot(
            vt, p.astype(BF16), preferred_element_type=F32)
        m_ref[mi, :, cs] = m_new

    assert tq == 2 * tk
    chains = [(c, mi) for c in range(nc) for mi in range(2)]

    bufs = ((sa_ref, xa_ref), (sb_ref, xb_ref))

    def produce(idx, k, dst):
        c, mi = chains[idx]
        s = scores(c, mi, k)
        dst[0][idx] = s
        dst[1][idx] = _col_reduce(s, jnp.max)

    def stage(j, src, dst):
        vt = vt_ref[j]
        if dst is not None:
            kn = k_ref[pl.ds(pl.multiple_of((j + 1) * tk, tk), tk), :]
        for idx, (c, mi) in enumerate(chains):
            if dst is not None:
                produce(idx, kn, dst)
            update(c, mi, vt, src[0][idx], src[1][idx])

    @pl.when(qi > 0)
    def _():
        k0 = k_ref[0:tk, :]
        for idx in range(len(chains)):
            produce(idx, k0, bufs[0])

        def pair(pp, carry):
            stage(2 * pp, bufs[0], bufs[1])
            stage(2 * pp + 1, bufs[1], bufs[0])
            return carry

        lax.fori_loop(0, qi - 1, pair, 0)
        stage(2 * qi - 2, bufs[0], bufs[1])
        stage(2 * qi - 1, bufs[1], None)

    krow = lax.broadcasted_iota(jnp.int32, (sq, sq), 0)
    qcol = lax.broadcasted_iota(jnp.int32, (sq, sq), 1)
    dmask = krow // MASK_CHUNK <= qcol // MASK_CHUNK
    vt_tile = jnp.concatenate([vt_ref[qi * 2], vt_ref[qi * 2 + 1]], axis=1)

    def diag_slot(n, piece):
        slot = 2 * n + piece
        return bufs[slot // len(chains)], slot % len(chains)

    def diag_produce(n, c, mi):
        nk = (c + 1) * sq
        s = scores(c, mi, k_ref[pl.ds(pl.multiple_of(qi * tq, tq), nk), :])
        tail = jnp.where(dmask, s[c * sq:, :], NEG_BIG)
        s = tail if c == 0 else jnp.concatenate([s[:c * sq, :], tail], axis=0)
        for piece, r0 in enumerate(range(0, nk, tk)):
            buf, slot = diag_slot(n, piece)
            rows = min(tk, nk - r0)
            buf[0][slot, 0:rows, :] = s[r0:r0 + rows, :]
        buf, slot = diag_slot(n, 0)
        buf[1][slot] = _col_reduce(s, jnp.max)

    def diag_consume(n, c, mi):
        nk = (c + 1) * sq
        parts = []
        for piece, r0 in enumerate(range(0, nk, tk)):
            buf, slot = diag_slot(n, piece)
            parts.append(buf[0][slot, 0:min(tk, nk - r0), :])
        buf, slot = diag_slot(n, 0)
        update(c, mi, vt_tile[:, :nk], jnp.concatenate(parts, axis=0), buf[1][slot])

    order = [(c, mi) for c in reversed(range(nc)) for mi in range(2)]
    for n in range(len(order) + DIAG_AHEAD):
        if n < len(order):
            diag_produce(n, *order[n])
        if n >= DIAG_AHEAD:
            diag_consume(n - DIAG_AHEAD, *order[n - DIAG_AHEAD])

    lp = lam_ref[...]
    lam = (jnp.exp(jnp.sum(lp[0:1] * lp[1:2], axis=-1, keepdims=True))
           - jnp.exp(jnp.sum(lp[2:3] * lp[3:4], axis=-1, keepdims=True)) + lam_init)
    o = (acc_ref[0, 0:HEAD_W, :] / acc_ref[0, HEAD_W:HEAD_W + 1, :]
         - lam * (acc_ref[1, 0:HEAD_W, :] / acc_ref[1, HEAD_W:HEAD_W + 1, :]))
    ms = jnp.mean(o * o, axis=0, keepdims=True)
    y = o * lax.rsqrt(ms + NORM_EPS) * sg_ref[...] * (1.0 - lam_init)
    o_ref[q_rows, :] = y.T.astype(o_ref.dtype)
    return carry


def _attention(p, lam4, sg_col, nb, s, lam_init, ct, w_mod, b_row, mod_col0, tq=1024, sq=256,
               tk=512):
    h = ATT_HEADS
    kern = functools.partial(_attn_kernel, tq=tq, sq=sq, tk=tk, lam_init=lam_init)
    nchains = 2 * tq // sq
    d = ct.shape[0]
    sw = (w_mod.shape[1] - mod_col0) // (nb * h)
    mslab = lambda rows, off: pl.BlockSpec((rows, sw), lambda b, hh: (0, off + b * h + hh))
    return pl.pallas_call(
        kern,
        grid=(nb, h),
        in_specs=[pl.BlockSpec((4, ATT_HALF_DIM), lambda b, hh: (0, 0)),
                  pl.BlockSpec((HEAD_W, 1), lambda b, hh: (0, 0)),
                  pl.BlockSpec((s, HEAD_W), lambda b, hh: (b, hh)),
                  pl.BlockSpec((s, HEAD_W), lambda b, hh: (b, h + hh)),
                  pl.BlockSpec((s, HEAD_W), lambda b, hh: (b, 2 * h + hh)),
                  pl.BlockSpec(ct.shape, lambda b, hh: (0, 0)),
                  mslab(d, mod_col0 // sw), mslab(1, mod_col0 // sw)],
        out_specs=[pl.BlockSpec((s, HEAD_W), lambda b, hh: (b, hh)), mslab(nb, 0)],
        out_shape=[jax.ShapeDtypeStruct((nb * s, h * HEAD_W), BF16),
                   jax.ShapeDtypeStruct((nb, w_mod.shape[1] - mod_col0), F32)],
        scratch_shapes=[pltpu.VMEM((s // tk, HEAD_W + ONES_ROWS, tk), BF16),
                        pltpu.VMEM((2, HEAD_W + ONES_ROWS, tq), F32),
                        pltpu.VMEM((2, 1, tq), F32),
                        pltpu.VMEM((nchains, tk, sq), F32),
                        pltpu.VMEM((nchains, tk, sq), F32),
                        pltpu.VMEM((nchains, 1, sq), F32),
                        pltpu.VMEM((nchains, 1, sq), F32)],
        compiler_params=_params(("parallel", "parallel")),
        name="diff_attn",
    )(lam4, sg_col, p, p, p, ct, w_mod, b_row)


def _group_row(a, s, grp=8):
    n, w = a.shape
    a3 = a.reshape(n // grp, grp, w)
    return jnp.broadcast_to(a3[:, s:s + 1, :], (n // grp, grp, w)).reshape(n, w)


def _hgrn2_kernel(lbl_ref, gn_ref, q_ref, f_ref, i_ref, g_ref, o_ref, st_ref, *, ch, layer):
    @pl.when(pl.program_id(1) == 0)
    def _():
        st_ref[...] = jnp.zeros(st_ref.shape, F32)

    w = HEAD_W
    lbl = lbl_ref[...]
    mx = jnp.max(lbl, axis=0, keepdims=True)
    e = jnp.exp(lbl - mx)
    lb = (jnp.sum(e[0:layer + 1], axis=0, keepdims=True)
          / jnp.sum(e, axis=0, keepdims=True))

    nt = (((1,), (1,)), ((), ()))
    row = lax.broadcasted_iota(jnp.int32, (ch, w), 0)
    arow = lax.broadcasted_iota(jnp.int32, (ch, ch), 0)
    acol = lax.broadcasted_iota(jnp.int32, (ch, ch), 1)
    tril = (arow >= acol).astype(BF16)
    on_diag = arow == acol
    levels = []
    hs = ch // 2
    while hs >= 1:
        upper = (row % (2 * hs)) >= hs
        levels.append((hs, upper, jnp.where(upper, 1.0, -1.0),
                       (arow // (2 * hs) == acol // (2 * hs))
                       & ((arow % (2 * hs)) >= hs) & ((acol % (2 * hs)) < hs)))
        hs //= 2

    def midpoint_row(b, hs):
        if hs >= 8:
            return jnp.concatenate(
                [jnp.broadcast_to(b[blk * 2 * hs + hs - 1:blk * 2 * hs + hs, :], (2 * hs, w))
                 for blk in range(ch // (2 * hs))], axis=0)
        if hs == 4:
            return _group_row(b, 3)
        return jnp.where((row % 8) < 4, _group_row(b, 1), _group_row(b, 5))

    chunk_vals = {}

    def chunk_prep(c):
        if c not in chunk_vals:
            rs = slice(c * ch, (c + 1) * ch)
            f_all = lb + (1.0 - lb) * jax.nn.sigmoid(f_ref[rs, :].astype(F32))
            g2 = jnp.log2(f_all)
            g2_hi = g2.astype(BF16)
            g2_lo = (g2 - g2_hi.astype(F32)).astype(BF16)
            b_all = (jnp.dot(tril, g2_hi, preferred_element_type=F32)
                     + jnp.dot(tril, g2_lo, preferred_element_type=F32))
            chunk_vals[c] = (f_all, b_all)
        return chunk_vals[c]

    def operands(c, h):
        rs, cs = slice(c * ch, (c + 1) * ch), slice(h * w, (h + 1) * w)
        f_all, b_all = chunk_prep(c)
        b, f = b_all[:, cs], f_all[:, cs]
        kk = 1.0 - f
        q = q_ref[rs, cs].astype(F32)
        blast = b[ch - 1:ch, :]
        ys = []
        for hs, upper, sign, _ in levels:
            if hs == 1:
                y = jnp.where(upper, q * f, kk)
            else:
                y = jnp.where(upper, q, kk) * jnp.exp2((b - midpoint_row(b, hs)) * sign)
            ys.append(y.astype(BF16))
        return dict(rs=rs, cs=cs, h=h, ys=ys,
                    qin=(q * jnp.exp2(b)).astype(BF16),
                    kout=(kk * jnp.exp2(blast - b)).astype(BF16),
                    keep=jnp.exp2(blast),
                    dsum=jnp.sum(q * kk, axis=-1, keepdims=True))

    def level_matmuls(x):
        st = st_ref[x["h"]]
        x["st"] = st
        x["o"] = lax.dot_general(x["qin"], st.astype(BF16), nt, preferred_element_type=F32)
        x["aa"] = [lax.dot_general(y, y, nt, preferred_element_type=F32) for y in x["ys"]]

    def finish(x):
        rs, cs = x["rs"], x["cs"]
        v = i_ref[rs, cs]
        att = jnp.where(on_diag, x["dsum"], 0.0).astype(BF16)
        for (_, _, _, amask), a in zip(levels, x["aa"]):
            att = jnp.where(amask, a.astype(BF16), att)
        o = x["o"] + jnp.dot(att, v, preferred_element_type=F32)
        st_ref[x["h"]] = x["keep"] * x["st"] + lax.dot_general(
            v, x["kout"], (((0,), (0,)), ((), ())), preferred_element_type=F32)
        y = _rms(o) * gn_ref[:, cs] * jax.nn.sigmoid(g_ref[rs, cs].astype(F32))
        o_ref[rs, cs] = y.astype(o_ref.dtype)

    items = [(c, h) for c in range(q_ref.shape[0] // ch) for h in range(REC_HEADS)]
    staged = []
    d1, d2 = 1, 2
    for i in range(len(items) + d2):
        if i < len(items):
            staged.append(operands(*items[i]))
        if d1 <= i < len(items) + d1:
            level_matmuls(staged[i - d1])
        if i >= d2:
            finish(staged[i - d2])


def _hgrn2(p, lb_logits, gn, nb, s, layer, rb=512, ch=128):
    hw = REC_HEADS * HEAD_W
    nr = s // rb
    c0 = 3 * ATT_HEADS * HEAD_W // hw
    kern = functools.partial(_hgrn2_kernel, ch=ch, layer=layer)
    spec = lambda off: pl.BlockSpec((rb, hw), lambda b, i: (b * nr + i, c0 + off))
    return pl.pallas_call(
        kern,
        grid=(nb, nr),
        in_specs=[pl.BlockSpec(lb_logits.shape, lambda b, i: (0, 0)),
                  pl.BlockSpec((1, hw), lambda b, i: (0, 0)),
                  spec(0), spec(1), spec(2), spec(3)],
        out_specs=pl.BlockSpec((rb, hw), lambda b, i: (b * nr + i, 0)),
        out_shape=jax.ShapeDtypeStruct((nb * s, hw), BF16),
        scratch_shapes=[pltpu.VMEM((REC_HEADS, HEAD_W, HEAD_W), F32)],
        compiler_params=_params(("parallel", "arbitrary")),
        name="hgrn2",
    )(lb_logits, gn, p, p, p, p)


def _merge_oproj_kernel(ya_ref, yr_ref, *refs, ng):
    ga_refs, gr_refs = refs[:ng], refs[ng:2 * ng]
    x_ref, gt_ref, wa_ref, wr_ref, wo_ref, o_ref = refs[2 * ng:]
    gw = wa_ref.shape[1] // ng
    parts = []
    for g in range(ng):
        cols = slice(g * gw, (g + 1) * gw)
        ta = jnp.dot(ya_ref[...], wa_ref[:, cols], preferred_element_type=F32)
        tr = jnp.dot(yr_ref[...], wr_ref[:, cols], preferred_element_type=F32)
        parts.append((jax.nn.sigmoid(ga_refs[g][...].astype(F32)) * ta
                      + jax.nn.sigmoid(gr_refs[g][...].astype(F32)) * tr).astype(BF16))
    m = jnp.concatenate(parts, axis=1)
    o_ref[...] = x_ref[...] + gt_ref[...] * jnp.dot(m, wo_ref[...], preferred_element_type=F32)


def _merge_oproj(ya, yr, p, x2, mod4, wa, wr, wo, ga_col0, rows_per_batch, tm=512, gw=1024):
    t, ka = ya.shape
    d = wa.shape[1]
    ng = d // gw
    ga0 = ga_col0 // gw
    gr0 = (ga_col0 + d) // gw
    tpb = rows_per_batch // tm
    gate = lambda b0, g: pl.BlockSpec((tm, gw), lambda i: (i, b0 + g))
    resident = lambda w: pl.BlockSpec(w.shape, lambda i: (0, 0), pipeline_mode=pl.Buffered(1))
    return pl.pallas_call(
        functools.partial(_merge_oproj_kernel, ng=ng),
        grid=(t // tm,),
        in_specs=[pl.BlockSpec((tm, ka), lambda i: (i, 0)),
                  pl.BlockSpec((tm, ka), lambda i: (i, 0))]
                 + [gate(ga0, g) for g in range(ng)] + [gate(gr0, g) for g in range(ng)]
                 + [pl.BlockSpec((tm, d), lambda i: (i, 0)),
                    pl.BlockSpec((None, None, 1, d), lambda i: (i // tpb, 2, 0, 0)),
                    resident(wa), resident(wr), resident(wo)],
        out_specs=pl.BlockSpec((tm, d), lambda i: (i, 0)),
        out_shape=jax.ShapeDtypeStruct((t, d), F32),
        compiler_params=_params(("parallel",)),
        name="merge_oproj",
    )(ya, yr, *([p] * (2 * ng)), x2, mod4, wa, wr, wo)


def _ffn_up(h1, g, mod4, wg, wu, rows_per_batch, cast_ws):
    swiglu = lambda gate, up: gate * jax.nn.sigmoid(gate) * up
    return _adaln_matmul(h1, g, mod4, 3, [wg, wu], rows_per_batch, swiglu, "ffn_up",
                         tm=2048, tn=512, rc=256, rows=256, cast_ws=cast_ws)


def _ffn_down_kernel(a_ref, h_ref, gt_ref, fg_ref, w_ref, o_ref, *, tn, final):
    j = pl.program_id(1)
    y = jnp.dot(a_ref[...], w_ref[...], preferred_element_type=F32)
    o_ref[:, pl.ds(pl.multiple_of(j * tn, tn), tn)] = h_ref[...] + gt_ref[...] * y

    if final:
        @pl.when(j == pl.num_programs(1) - 1)
        def _():
            o_ref[...] = _rms(o_ref[...]) * fg_ref[...]


def _ffn_down(a, h1, mod4, fg, w, rows_per_batch, final, tm=1024, tn=512):
    t, d = h1.shape
    f = a.shape[1]
    tpb = rows_per_batch // tm
    return pl.pallas_call(
        functools.partial(_ffn_down_kernel, tn=tn, final=final),
        grid=(t // tm, d // tn),
        in_specs=[pl.BlockSpec((tm, f), lambda i, j: (i, 0)),
                  pl.BlockSpec((tm, tn), lambda i, j: (i, j)),
                  pl.BlockSpec((None, None, 1, tn), lambda i, j: (i // tpb, 5, 0, j)),
                  pl.BlockSpec((1, d), lambda i, j: (0, 0)),
                  pl.BlockSpec((f, tn), lambda i, j: (0, j))],
        out_specs=pl.BlockSpec((tm, d), lambda i, j: (i, 0)),
        out_shape=jax.ShapeDtypeStruct((t, d), F32),
        compiler_params=_params(("parallel", "arbitrary")),
        name="ffn_down",
    )(a, h1, mod4, fg, w)


def kernel(x, c, w_mod, b_mod, norm1_g, w_in, lambda_q1, lambda_k1, lambda_q2, lambda_k2,
           subln_g, lb_logits, gnorm_g, w_att_out, w_rec_out, w_o, norm2_g,
           w_ffn_gate, w_ffn_up, w_ffn_down, final_g):
    nb, s, d = x.shape
    depth = w_mod.shape[0]
    h = x.reshape(nb * s, d)
    for l in range(depth):
        lam_init = 0.8 - 0.6 * math.exp(-0.3 * l)
        ct, b_row = c.T, b_mod[l].reshape(1, 6 * d)
        mod_head = _mod(ct, w_mod[l], b_row, 2 * d)
        p, (wa, wr, wo) = _in_proj(h, norm1_g[l].reshape(1, d), mod_head.reshape(nb, 2, 1, d),
                                   w_in[l], s, cast_ws=(w_att_out[l], w_rec_out[l], w_o[l]))
        lam4 = jnp.stack([lambda_q1[l], lambda_k1[l], lambda_q2[l], lambda_k2[l]])
        ya, mod_tail = _attention(p, lam4, subln_g[l].reshape(HEAD_W, 1), nb, s, lam_init,
                                  ct, w_mod[l], b_row, 2 * d)
        mod4 = jnp.concatenate([mod_head, mod_tail], axis=1).reshape(nb, 6, 1, d)
        yr = _hgrn2(p, lb_logits, gnorm_g[l].reshape(1, REC_HEADS * HEAD_W), nb, s, l)
        ga_col0 = 3 * ATT_HEADS * HEAD_W + 4 * REC_HEADS * HEAD_W
        h1 = _merge_oproj(ya, yr, p, h, mod4, wa, wr, wo, ga_col0, s)
        a, (wd,) = _ffn_up(h1, norm2_g[l].reshape(1, d), mod4, w_ffn_gate[l], w_ffn_up[l], s,
                           cast_ws=(w_ffn_down[l],))
        h = _ffn_down(a, h1, mod4, final_g.reshape(1, d), wd, s, l == depth - 1)
    return h.reshape(nb, s, d)
```

```python
import functools
import math

import jax
import jax.numpy as jnp
from jax import lax
from jax.experimental import pallas as pl
from jax.experimental.pallas import tpu as pltpu

F32 = jnp.float32
BF16 = jnp.bfloat16

NORM_EPS = 1e-6
ATT_HEADS = 8
ATT_HALF_DIM = 64
HEAD_W = 128
REC_HEADS = 8
MASK_CHUNK = 64
NEG_BIG = -1e30
BF16_SUBLANES = 16
ONES_ROWS = BF16_SUBLANES
DIAG_AHEAD = 4

VMEM_LIMIT = 60 * 1024 * 1024


def _params(sem, vmem=VMEM_LIMIT):
    return pltpu.CompilerParams(dimension_semantics=sem, vmem_limit_bytes=vmem)


def _rms(x):
    return x * lax.rsqrt(jnp.mean(x * x, axis=-1, keepdims=True) + NORM_EPS)


def _adaln(x, g, sc, sh):
    return _rms(x) * (g * (1.0 + sc)) + sh


def _mod_kernel(ct_ref, w_ref, b_ref, o_ref):
    ct = ct_ref[...]
    cond = ct * jax.nn.sigmoid(ct)
    w = w_ref[...]
    for b in range(ct.shape[1]):
        o_ref[b:b + 1, :] = jnp.sum(w * cond[:, b:b + 1], axis=0, keepdims=True) + b_ref[...]


def _mod(ct, w_mod, b_row, n, tn=1024):
    d, nb = ct.shape
    return pl.pallas_call(
        _mod_kernel,
        grid=(n // tn,),
        in_specs=[pl.BlockSpec((d, nb), lambda j: (0, 0)),
                  pl.BlockSpec((d, tn), lambda j: (0, j)),
                  pl.BlockSpec((1, tn), lambda j: (0, j))],
        out_specs=pl.BlockSpec((nb, tn), lambda j: (0, j)),
        out_shape=jax.ShapeDtypeStruct((nb, n), F32),
        compiler_params=_params(("arbitrary",)),
        name="mod",
    )(ct, w_mod, b_row)


def _adaln_matmul_kernel(x_ref, g_ref, sh_ref, sc_ref, *refs, nw, ncast, rc, rows, finish):
    w_refs, cast_in = refs[:nw], refs[nw:nw + ncast]
    o_ref, cast_out = refs[nw + ncast], refs[nw + ncast + 1:nw + 2 * ncast + 1]
    u_bufs = refs[nw + 2 * ncast + 1:]
    i, j = pl.program_id(0), pl.program_id(1)
    nchunks = u_bufs[0].shape[0] // rc

    def normalise_into(u_ref):
        r0 = pl.multiple_of(j * rc, rc)
        u_ref[pl.ds(r0, rc), :] = _adaln(x_ref[...], g_ref[...], sc_ref[...],
                                         sh_ref[...]).astype(BF16)

    def multiply_from(u_ref):
        wb = [w[...].astype(BF16) for w in w_refs]
        for r0 in range(0, u_ref.shape[0], rows):
            u = u_ref[r0:r0 + rows, :]
            o_ref[r0:r0 + rows, :] = finish(
                *[jnp.dot(u, w, preferred_element_type=F32) for w in wb]).astype(o_ref.dtype)

    for parity in range(2):
        @pl.when((i > 0) & (i % 2 == parity))
        def _(parity=parity):
            multiply_from(u_bufs[1 - parity])
            for w_ref, wb_ref in zip(cast_in, cast_out):
                wb_ref[...] = w_ref[...].astype(BF16)

        @pl.when((i < pl.num_programs(0) - 1) & (j < nchunks) & (i % 2 == parity))
        def _(parity=parity):
            normalise_into(u_bufs[parity])


def _adaln_matmul(x2, g, mod4, mod_row, ws, rows_per_batch, finish, name, tm, tn, rc=256,
                  rows=None, cast_ws=()):
    t, d = x2.shape
    n = ws[0].shape[1]
    nt, nj, nchunks, tpb = t // tm, n // tn, tm // rc, rows_per_batch // tm
    assert nchunks <= nj
    tile = lambda i: jnp.minimum(i, nt - 1)
    col = lambda i, j: jnp.where(i > 0, j, 0)
    def slab(w):
        k = w.shape[0]
        ns = max(c for c in range(1, nt * nj + 1) if k % (c * BF16_SUBLANES) == 0)
        return pl.BlockSpec((k // ns, w.shape[1]),
                            lambda i, j: (jnp.clip((i - 1) * nj + j, 0, ns - 1), 0))

    kern = functools.partial(_adaln_matmul_kernel, nw=len(ws), ncast=len(cast_ws), rc=rc,
                             rows=rows or tm, finish=finish)
    outs = pl.pallas_call(
        kern,
        grid=(nt + 1, nj),
        in_specs=[pl.BlockSpec((rc, d), lambda i, j: (tile(i) * nchunks
                                                      + jnp.minimum(j, nchunks - 1), 0)),
                  pl.BlockSpec((1, d), lambda i, j: (0, 0)),
                  pl.BlockSpec((None, None, 1, d), lambda i, j: (tile(i) // tpb, mod_row, 0, 0)),
                  pl.BlockSpec((None, None, 1, d),
                               lambda i, j: (tile(i) // tpb, mod_row + 1, 0, 0))]
                 + [pl.BlockSpec((d, tn), lambda i, j: (0, col(i, j))) for _ in ws]
                 + [slab(w) for w in cast_ws],
        out_specs=[pl.BlockSpec((tm, tn), lambda i, j: (jnp.maximum(i - 1, 0), col(i, j)))]
                  + [slab(w) for w in cast_ws],
        out_shape=[jax.ShapeDtypeStruct((t, n), BF16)]
                  + [jax.ShapeDtypeStruct(w.shape, BF16) for w in cast_ws],
        scratch_shapes=[pltpu.VMEM((tm, d), BF16), pltpu.VMEM((tm, d), BF16)],
        compiler_params=_params(("arbitrary", "arbitrary")),
        name=name,
    )(x2, g, mod4, mod4, *ws, *cast_ws)
    return outs[0], outs[1:]


def _in_proj(x2, g, mod4, w, rows_per_batch, cast_ws):
    return _adaln_matmul(x2, g, mod4, 0, [w], rows_per_batch, lambda y: y, "in_proj",
                         tm=2048, tn=1024, rows=512, cast_ws=cast_ws)


def _col_reduce(x, op, groups=8):
    k, n = x.shape
    if k % (groups * 8) == 0:
        x = op(x.reshape(groups, k // groups, n), axis=0)
    return op(x, axis=0, keepdims=True)


def _attn_kernel(lam_ref, sg_ref, q_ref, k_ref, v_ref, ct_ref, mw_ref, mb_ref, o_ref, mvo_ref,
                 vt_ref, acc_ref, m_ref, sa_ref, sb_ref, xa_ref, xb_ref, *, tq, sq, tk,
                 lam_init):
    nkb = vt_ref.shape[0]
    nc = tq // sq

    _mod_kernel(ct_ref, mw_ref, mb_ref, mvo_ref)

    for j in range(nkb):
        vt_ref[j, 0:HEAD_W, :] = v_ref[j * tk:(j + 1) * tk, :].astype(F32).T.astype(BF16)
        vt_ref[j, HEAD_W:, :] = jnp.ones((vt_ref.shape[1] - HEAD_W, tk), BF16)

    lax.fori_loop(0, q_ref.shape[0] // tq,
                  functools.partial(_attn_q_tile, lam_ref, sg_ref, q_ref, k_ref, o_ref, vt_ref,
                                    acc_ref, m_ref, sa_ref, sb_ref, xa_ref, xb_ref,
                                    tq, sq, tk, nc, lam_init), 0)


def _attn_q_tile(lam_ref, sg_ref, q_ref, k_ref, o_ref, vt_ref, acc_ref, m_ref, sa_ref, sb_ref,
                 xa_ref, xb_ref, tq, sq, tk, nc, lam_init, qi, carry):
    q_rows = pl.ds(pl.multiple_of(qi * tq, tq), tq)
    q = (q_ref[q_rows, :].astype(F32) * (ATT_HALF_DIM ** -0.5 * math.log2(math.e))).astype(BF16)
    lane = lax.broadcasted_iota(jnp.int32, q.shape, 1)
    zero = jnp.zeros_like(q)
    qm = (jnp.where(lane < ATT_HALF_DIM, q, zero), jnp.where(lane >= ATT_HALF_DIM, q, zero))

    m_ref[...] = jnp.full(m_ref.shape, NEG_BIG, F32)
    acc_ref[...] = jnp.zeros(acc_ref.shape, F32)

    def scores(c, mi, k):
        return lax.dot_general(k, qm[mi][c * sq:(c + 1) * sq, :], (((1,), (1,)), ((), ())),
                               preferred_element_type=F32)

    def update(c, mi, vt, s, smax=None):
        cs = slice(c * sq, (c + 1) * sq)
        m_old = m_ref[mi, :, cs]
        m_new = jnp.maximum(m_old, _col_reduce(s, jnp.max) if smax is None else smax)
        p = jnp.exp2(s - m_new)
        alpha = jnp.exp2(m_old - m_new)
        acc_ref[mi, :, cs] = alpha * acc_ref[mi, :, cs] + jnp.dot(
            vt, p.astype(BF16), preferred_element_type=F32)
        m_ref[mi, :, cs] = m_new

    assert tq == 2 * tk
    chains = [(c, mi) for c in range(nc) for mi in range(2)]

    bufs = ((sa_ref, xa_ref), (sb_ref, xb_ref))

    def produce(idx, k, dst):
        c, mi = chains[idx]
        s = scores(c, mi, k)
        dst[0][idx] = s
        dst[1][idx] = _col_reduce(s, jnp.max)

    def stage(j, src, dst):
        vt = vt_ref[j]
        if dst is not None:
            kn = k_ref[pl.ds(pl.multiple_of((j + 1) * tk, tk), tk), :]
        for idx, (c, mi) in enumerate(chains):
            if dst is not None:
                produce(idx, kn, dst)
            update(c, mi, vt, src[0][idx], src[1][idx])

    @pl.when(qi > 0)
    def _():
        k0 = k_ref[0:tk, :]
        for idx in range(len(chains)):
            produce(idx, k0, bufs[0])

        def pair(pp, carry):
            stage(2 * pp, bufs[0], bufs[1])
            stage(2 * pp + 1, bufs[1], bufs[0])
            return carry

        lax.fori_loop(0, qi - 1, pair, 0)
        stage(2 * qi - 2, bufs[0], bufs[1])
        stage(2 * qi - 1, bufs[1], None)

    krow = lax.broadcasted_iota(jnp.int32, (sq, sq), 0)
    qcol = lax.broadcasted_iota(jnp.int32, (sq, sq), 1)
    dmask = krow // MASK_CHUNK <= qcol // MASK_CHUNK
    vt_tile = jnp.concatenate([vt_ref[qi * 2], vt_ref[qi * 2 + 1]], axis=1)

    def diag_slot(n, piece):
        slot = 2 * n + piece
        return bufs[slot // len(chains)], slot % len(chains)

    def diag_produce(n, c, mi):
        nk = (c + 1) * sq
        s = scores(c, mi, k_ref[pl.ds(pl.multiple_of(qi * tq, tq), nk), :])
        tail = jnp.where(dmask, s[c * sq:, :], NEG_BIG)
        s = tail if c == 0 else jnp.concatenate([s[:c * sq, :], tail], axis=0)
        for piece, r0 in enumerate(range(0, nk, tk)):
            buf, slot = diag_slot(n, piece)
            rows = min(tk, nk - r0)
            buf[0][slot, 0:rows, :] = s[r0:r0 + rows, :]
        buf, slot = diag_slot(n, 0)
        buf[1][slot] = _col_reduce(s, jnp.max)

    def diag_consume(n, c, mi):
        nk = (c + 1) * sq
        parts = []
        for piece, r0 in enumerate(range(0, nk, tk)):
            buf, slot = diag_slot(n, piece)
            parts.append(buf[0][slot, 0:min(tk, nk - r0), :])
        buf, slot = diag_slot(n, 0)
        update(c, mi, vt_tile[:, :nk], jnp.concatenate(parts, axis=0), buf[1][slot])

    order = [(c, mi) for c in reversed(range(nc)) for mi in range(2)]
    for n in range(len(order) + DIAG_AHEAD):
        if n < len(order):
            diag_produce(n, *order[n])
        if n >= DIAG_AHEAD:
            diag_consume(n - DIAG_AHEAD, *order[n - DIAG_AHEAD])

    lp = lam_ref[...]
    lam = (jnp.exp(jnp.sum(lp[0:1] * lp[1:2], axis=-1, keepdims=True))
           - jnp.exp(jnp.sum(lp[2:3] * lp[3:4], axis=-1, keepdims=True)) + lam_init)
    o = (acc_ref[0, 0:HEAD_W, :] / acc_ref[0, HEAD_W:HEAD_W + 1, :]
         - lam * (acc_ref[1, 0:HEAD_W, :] / acc_ref[1, HEAD_W:HEAD_W + 1, :]))
    ms = jnp.mean(o * o, axis=0, keepdims=True)
    y = o * lax.rsqrt(ms + NORM_EPS) * sg_ref[...] * (1.0 - lam_init)
    o_ref[q_rows, :] = y.T.astype(o_ref.dtype)
    return carry


def _attention(p, lam4, sg_col, nb, s, lam_init, ct, w_mod, b_row, mod_col0, tq=1024, sq=256,
               tk=512):
    h = ATT_HEADS
    kern = functools.partial(_attn_kernel, tq=tq, sq=sq, tk=tk, lam_init=lam_init)
    nchains = 2 * tq // sq
    d = ct.shape[0]
    sw = (w_mod.shape[1] - mod_col0) // (nb * h)
    mslab = lambda rows, off: pl.BlockSpec((rows, sw), lambda b, hh: (0, off + b * h + hh))
    return pl.pallas_call(
        kern,
        grid=(nb, h),
        in_specs=[pl.BlockSpec((4, ATT_HALF_DIM), lambda b, hh: (0, 0)),
                  pl.BlockSpec((HEAD_W, 1), lambda b, hh: (0, 0)),
                  pl.BlockSpec((s, HEAD_W), lambda b, hh: (b, hh)),
                  pl.BlockSpec((s, HEAD_W), lambda b, hh: (b, h + hh)),
                  pl.BlockSpec((s, HEAD_W), lambda b, hh: (b, 2 * h + hh)),
                  pl.BlockSpec(ct.shape, lambda b, hh: (0, 0)),
                  mslab(d, mod_col0 // sw), mslab(1, mod_col0 // sw)],
        out_specs=[pl.BlockSpec((s, HEAD_W), lambda b, hh: (b, hh)), mslab(nb, 0)],
        out_shape=[jax.ShapeDtypeStruct((nb * s, h * HEAD_W), BF16),
                   jax.ShapeDtypeStruct((nb, w_mod.shape[1] - mod_col0), F32)],
        scratch_shapes=[pltpu.VMEM((s // tk, HEAD_W + ONES_ROWS, tk), BF16),
                        pltpu.VMEM((2, HEAD_W + ONES_ROWS, tq), F32),
                        pltpu.VMEM((2, 1, tq), F32),
                        pltpu.VMEM((nchains, tk, sq), F32),
                        pltpu.VMEM((nchains, tk, sq), F32),
                        pltpu.VMEM((nchains, 1, sq), F32),
                        pltpu.VMEM((nchains, 1, sq), F32)],
        compiler_params=_params(("parallel", "parallel")),
        name="diff_attn",
    )(lam4, sg_col, p, p, p, ct, w_mod, b_row)


def _group_row(a, s, grp=8):
    n, w = a.shape
    a3 = a.reshape(n // grp, grp, w)
    return jnp.broadcast_to(a3[:, s:s + 1, :], (n // grp, grp, w)).reshape(n, w)


def _hgrn2_kernel(lbl_ref, gn_ref, q_ref, f_ref, i_ref, g_ref, o_ref, st_ref, *, ch, layer):
    @pl.when(pl.program_id(1) == 0)
    def _():
        st_ref[...] = jnp.zeros(st_ref.shape, F32)

    w = HEAD_W
    lbl = lbl_ref[...]
    mx = jnp.max(lbl, axis=0, keepdims=True)
    e = jnp.exp(lbl - mx)
    lb = (jnp.sum(e[0:layer + 1], axis=0, keepdims=True)
          / jnp.sum(e, axis=0, keepdims=True))

    nt = (((1,), (1,)), ((), ()))
    row = lax.broadcasted_iota(jnp.int32, (ch, w), 0)
    arow = lax.broadcasted_iota(jnp.int32, (ch, ch), 0)
    acol = lax.broadcasted_iota(jnp.int32, (ch, ch), 1)
    tril = (arow >= acol).astype(BF16)
    on_diag = arow == acol
    levels = []
    hs = ch // 2
    while hs >= 1:
        upper = (row % (2 * hs)) >= hs
        levels.append((hs, upper, jnp.where(upper, 1.0, -1.0),
                       (arow // (2 * hs) == acol // (2 * hs))
                       & ((arow % (2 * hs)) >= hs) & ((acol % (2 * hs)) < hs)))
        hs //= 2

    def midpoint_row(b, hs):
        if hs >= 8:
            return jnp.concatenate(
                [jnp.broadcast_to(b[blk * 2 * hs + hs - 1:blk * 2 * hs + hs, :], (2 * hs, w))
                 for blk in range(ch // (2 * hs))], axis=0)
        if hs == 4:
            return _group_row(b, 3)
        return jnp.where((row % 8) < 4, _group_row(b, 1), _group_row(b, 5))

    chunk_vals = {}

    def chunk_prep(c):
        if c not in chunk_vals:
            rs = slice(c * ch, (c + 1) * ch)
            f_all = lb + (1.0 - lb) * jax.nn.sigmoid(f_ref[rs, :].astype(F32))
            g2 = jnp.log2(f_all)
            g2_hi = g2.astype(BF16)
            g2_lo = (g2 - g2_hi.astype(F32)).astype(BF16)
            b_all = (jnp.dot(tril, g2_hi, preferred_element_type=F32)
                     + jnp.dot(tril, g2_lo, preferred_element_type=F32))
            chunk_vals[c] = (f_all, b_all)
        return chunk_vals[c]

    def operands(c, h):
        rs, cs = slice(c * ch, (c + 1) * ch), slice(h * w, (h + 1) * w)
        f_all, b_all = chunk_prep(c)
        b, f = b_all[:, cs], f_all[:, cs]
        kk = 1.0 - f
        q = q_ref[rs, cs].astype(F32)
        blast = b[ch - 1:ch, :]
        ys = []
        for hs, upper, sign, _ in levels:
            if hs == 1:
                y = jnp.where(upper, q * f, kk)
            else:
                y = jnp.where(upper, q, kk) * jnp.exp2((b - midpoint_row(b, hs)) * sign)
            ys.append(y.astype(BF16))
        return dict(rs=rs, cs=cs, h=h, ys=ys,
                    qin=(q * jnp.exp2(b)).astype(BF16),
                    kout=(kk * jnp.exp2(blast - b)).astype(BF16),
                    keep=jnp.exp2(blast),
                    dsum=jnp.sum(q * kk, axis=-1, keepdims=True))

    def level_matmuls(x):
        st = st_ref[x["h"]]
        x["st"] = st
        x["o"] = lax.dot_general(x["qin"], st.astype(BF16), nt, preferred_element_type=F32)
        x["aa"] = [lax.dot_general(y, y, nt, preferred_element_type=F32) for y in x["ys"]]

    def finish(x):
        rs, cs = x["rs"], x["cs"]
        v = i_ref[rs, cs]
        att = jnp.where(on_diag, x["dsum"], 0.0).astype(BF16)
        for (_, _, _, amask), a in zip(levels, x["aa"]):
            att = jnp.where(amask, a.astype(BF16), att)
        o = x["o"] + jnp.dot(att, v, preferred_element_type=F32)
        st_ref[x["h"]] = x["keep"] * x["st"] + lax.dot_general(
            v, x["kout"], (((0,), (0,)), ((), ())), preferred_element_type=F32)
        y = _rms(o) * gn_ref[:, cs] * jax.nn.sigmoid(g_ref[rs, cs].astype(F32))
        o_ref[rs, cs] = y.astype(o_ref.dtype)

    items = [(c, h) for c in range(q_ref.shape[0] // ch) for h in range(REC_HEADS)]
    staged = []
    d1, d2 = 1, 2
    for i in range(len(items) + d2):
        if i < len(items):
            staged.append(operands(*items[i]))
        if d1 <= i < len(items) + d1:
            level_matmuls(staged[i - d1])
        if i >= d2:
            finish(staged[i - d2])


def _hgrn2(p, lb_logits, gn, nb, s, layer, rb=512, ch=128):
    hw = REC_HEADS * HEAD_W
    nr = s // rb
    c0 = 3 * ATT_HEADS * HEAD_W // hw
    kern = functools.partial(_hgrn2_kernel, ch=ch, layer=layer)
    spec = lambda off: pl.BlockSpec((rb, hw), lambda b, i: (b * nr + i, c0 + off))
    return pl.pallas_call(
        kern,
        grid=(nb, nr),
        in_specs=[pl.BlockSpec(lb_logits.shape, lambda b, i: (0, 0)),
                  pl.BlockSpec((1, hw), lambda b, i: (0, 0)),
                  spec(0), spec(1), spec(2), spec(3)],
        out_specs=pl.BlockSpec((rb, hw), lambda b, i: (b * nr + i, 0)),
        out_shape=jax.ShapeDtypeStruct((nb * s, hw), BF16),
        scratch_shapes=[pltpu.VMEM((REC_HEADS, HEAD_W, HEAD_W), F32)],
        compiler_params=_params(("parallel", "arbitrary")),
        name="hgrn2",
    )(lb_logits, gn, p, p, p, p)


def _merge_oproj_kernel(ya_ref, yr_ref, *refs, ng):
    ga_refs, gr_refs = refs[:ng], refs[ng:2 * ng]
    x_ref, gt_ref, wa_ref, wr_ref, wo_ref, o_ref = refs[2 * ng:]
    gw = wa_ref.shape[1] // ng
    parts = []
    for g in range(ng):
        cols = slice(g * gw, (g + 1) * gw)
        ta = jnp.dot(ya_ref[...], wa_ref[:, cols], preferred_element_type=F32)
        tr = jnp.dot(yr_ref[...], wr_ref[:, cols], preferred_element_type=F32)
        parts.append((jax.nn.sigmoid(ga_refs[g][...].astype(F32)) * ta
                      + jax.nn.sigmoid(gr_refs[g][...].astype(F32)) * tr).astype(BF16))
    m = jnp.concatenate(parts, axis=1)
    o_ref[...] = x_ref[...] + gt_ref[...] * jnp.dot(m, wo_ref[...], preferred_element_type=F32)


def _merge_oproj(ya, yr, p, x2, mod4, wa, wr, wo, ga_col0, rows_per_batch, tm=512, gw=1024):
    t, ka = ya.shape
    d = wa.shape[1]
    ng = d // gw
    ga0 = ga_col0 // gw
    gr0 = (ga_col0 + d) // gw
    tpb = rows_per_batch // tm
    gate = lambda b0, g: pl.BlockSpec((tm, gw), lambda i: (i, b0 + g))
    resident = lambda w: pl.BlockSpec(w.shape, lambda i: (0, 0), pipeline_mode=pl.Buffered(1))
    return pl.pallas_call(
        functools.partial(_merge_oproj_kernel, ng=ng),
        grid=(t // tm,),
        in_specs=[pl.BlockSpec((tm, ka), lambda i: (i, 0)),
                  pl.BlockSpec((tm, ka), lambda i: (i, 0))]
                 + [gate(ga0, g) for g in range(ng)] + [gate(gr0, g) for g in range(ng)]
                 + [pl.BlockSpec((tm, d), lambda i: (i, 0)),
                    pl.BlockSpec((None, None, 1, d), lambda i: (i // tpb, 2, 0, 0)),
                    resident(wa), resident(wr), resident(wo)],
        out_specs=pl.BlockSpec((tm, d), lambda i: (i, 0)),
        out_shape=jax.ShapeDtypeStruct((t, d), F32),
        compiler_params=_params(("parallel",)),
        name="merge_oproj",
    )(ya, yr, *([p] * (2 * ng)), x2, mod4, wa, wr, wo)


def _ffn_up(h1, g, mod4, wg, wu, rows_per_batch, cast_ws):
    swiglu = lambda gate, up: gate * jax.nn.sigmoid(gate) * up
    return _adaln_matmul(h1, g, mod4, 3, [wg, wu], rows_per_batch, swiglu, "ffn_up",
                         tm=2048, tn=512, rc=256, rows=256, cast_ws=cast_ws)


def _ffn_down_kernel(a_ref, h_ref, gt_ref, fg_ref, w_ref, o_ref, *, tn, final):
    j = pl.program_id(1)
    y = jnp.dot(a_ref[...], w_ref[...], preferred_element_type=F32)
    o_ref[:, pl.ds(pl.multiple_of(j * tn, tn), tn)] = h_ref[...] + gt_ref[...] * y

    if final:
        @pl.when(j == pl.num_programs(1) - 1)
        def _():
            o_ref[...] = _rms(o_ref[...]) * fg_ref[...]


def _ffn_down(a, h1, mod4, fg, w, rows_per_batch, final, tm=1024, tn=512):
    t, d = h1.shape
    f = a.shape[1]
    tpb = rows_per_batch // tm
    return pl.pallas_call(
        functools.partial(_ffn_down_kernel, tn=tn, final=final),
        grid=(t // tm, d // tn),
        in_specs=[pl.BlockSpec((tm, f), lambda i, j: (i, 0)),
                  pl.BlockSpec((tm, tn), lambda i, j: (i, j)),
                  pl.BlockSpec((None, None, 1, tn), lambda i, j: (i // tpb, 5, 0, j)),
                  pl.BlockSpec((1, d), lambda i, j: (0, 0)),
                  pl.BlockSpec((f, tn), lambda i, j: (0, j))],
        out_specs=pl.BlockSpec((tm, d), lambda i, j: (i, 0)),
        out_shape=jax.ShapeDtypeStruct((t, d), F32),
        compiler_params=_params(("parallel", "arbitrary")),
        name="ffn_down",
    )(a, h1, mod4, fg, w)


def kernel(x, c, w_mod, b_mod, norm1_g, w_in, lambda_q1, lambda_k1, lambda_q2, lambda_k2,
           subln_g, lb_logits, gnorm_g, w_att_out, w_rec_out, w_o, norm2_g,
           w_ffn_gate, w_ffn_up, w_ffn_down, final_g):
    nb, s, d = x.shape
    depth = w_mod.shape[0]
    h = x.reshape(nb * s, d)
    for l in range(depth):
        lam_init = 0.8 - 0.6 * math.exp(-0.3 * l)
        ct, b_row = c.T, b_mod[l].reshape(1, 6 * d)
        mod_head = _mod(ct, w_mod[l], b_row, 2 * d)
        p, (wa, wr, wo) = _in_proj(h, norm1_g[l].reshape(1, d), mod_head.reshape(nb, 2, 1, d),
                                   w_in[l], s, cast_ws=(w_att_out[l], w_rec_out[l], w_o[l]))
        lam4 = jnp.stack([lambda_q1[l], lambda_k1[l], lambda_q2[l], lambda_k2[l]])
        ya, mod_tail = _attention(p, lam4, subln_g[l].reshape(HEAD_W, 1), nb, s, lam_init,
                                  ct, w_mod[l], b_row, 2 * d)
        mod4 = jnp.concatenate([mod_head, mod_tail], axis=1).reshape(nb, 6, 1, d)
        yr = _hgrn2(p, lb_logits, gnorm_g[l].reshape(1, REC_HEADS * HEAD_W), nb, s, l)
        ga_col0 = 3 * ATT_HEADS * HEAD_W + 4 * REC_HEADS * HEAD_W
        h1 = _merge_oproj(ya, yr, p, h, mod4, wa, wr, wo, ga_col0, s)
        a, (wd,) = _ffn_up(h1, norm2_g[l].reshape(1, d), mod4, w_ffn_gate[l], w_ffn_up[l], s,
                           cast_ws=(w_ffn_down[l],))
        h = _ffn_down(a, h1, mod4, final_g.reshape(1, d), wd, s, l == depth - 1)
    return h.reshape(nb, s, d)
```

```python
import functools
import math

import jax
import jax.numpy as jnp
from jax import lax
from jax.experimental import pallas as pl
from jax.experimental.pallas import tpu as pltpu

F32 = jnp.float32
BF16 = jnp.bfloat16

NORM_EPS = 1e-6
ATT_HEADS = 8
ATT_HALF_DIM = 64
HEAD_W = 128
REC_HEADS = 8
MASK_CHUNK = 64
NEG_BIG = -1e30
BF16_SUBLANES = 16
ONES_ROWS = BF16_SUBLANES
DIAG_AHEAD = 4

VMEM_LIMIT = 60 * 1024 * 1024


def _params(sem, vmem=VMEM_LIMIT):
    return pltpu.CompilerParams(dimension_semantics=sem, vmem_limit_bytes=vmem)


def _rms(x):
    return x * lax.rsqrt(jnp.mean(x * x, axis=-1, keepdims=True) + NORM_EPS)


def _adaln(x, g, sc, sh):
    return _rms(x) * (g * (1.0 + sc)) + sh


def _mod_kernel(ct_ref, w_ref, b_ref, o_ref):
    ct = ct_ref[...]
    cond = ct * jax.nn.sigmoid(ct)
    w = w_ref[...]
    for b in range(ct.shape[1]):
        o_ref[b:b + 1, :] = jnp.sum(w * cond[:, b:b + 1], axis=0, keepdims=True) + b_ref[...]


def _mod(ct, w_mod, b_row, n, tn=1024):
    d, nb = ct.shape
    return pl.pallas_call(
        _mod_kernel,
        grid=(n // tn,),
        in_specs=[pl.BlockSpec((d, nb), lambda j: (0, 0)),
                  pl.BlockSpec((d, tn), lambda j: (0, j)),
                  pl.BlockSpec((1, tn), lambda j: (0, j))],
        out_specs=pl.BlockSpec((nb, tn), lambda j: (0, j)),
        out_shape=jax.ShapeDtypeStruct((nb, n), F32),
        compiler_params=_params(("arbitrary",)),
        name="mod",
    )(ct, w_mod, b_row)


def _adaln_matmul_kernel(x_ref, g_ref, sh_ref, sc_ref, *refs, nw, ncast, rc, rows, finish):
    w_refs, cast_in = refs[:nw], refs[nw:nw + ncast]
    o_ref, cast_out = refs[nw + ncast], refs[nw + ncast + 1:nw + 2 * ncast + 1]
    u_bufs = refs[nw + 2 * ncast + 1:]
    i, j = pl.program_id(0), pl.program_id(1)
    nchunks = u_bufs[0].shape[0] // rc

    def normalise_into(u_ref):
        normed = _adaln(x_ref[...], g_ref[...], sc_ref[...], sh_ref[...]).astype(BF16)
        r0 = pl.multiple_of(jnp.minimum(j, nchunks - 1) * rc, rc)
        u_ref[pl.ds(r0, rc), :] = normed
        return normed

    def multiply_from(u_ref, anchor):
        wb = [w[...].astype(BF16) for w in w_refs]
        nsub = u_ref.shape[0] // rows
        for k in range(nsub):
            u = u_ref[k * rows:(k + 1) * rows, :]
            if k == nsub - 1:
                head = jnp.where(j >= 0, u[:rc, :], anchor)
                u = head if rows == rc else jnp.concatenate([head, u[rc:, :]], axis=0)
            o_ref[k * rows:(k + 1) * rows, :] = finish(
                *[jnp.dot(u, w, preferred_element_type=F32) for w in wb]).astype(o_ref.dtype)

    @pl.when(i == 0)
    def _():
        normalise_into(u_bufs[0])

    for parity in range(2):
        @pl.when((i > 0) & (i % 2 == parity))
        def _(parity=parity):
            multiply_from(u_bufs[1 - parity], normalise_into(u_bufs[parity]))
            for w_ref, wb_ref in zip(cast_in, cast_out):
                wb_ref[...] = w_ref[...].astype(BF16)


def _adaln_matmul(x2, g, mod4, mod_row, ws, rows_per_batch, finish, name, tm, tn, rc=256,
                  rows=None, cast_ws=()):
    t, d = x2.shape
    n = ws[0].shape[1]
    nt, nj, nchunks, tpb = t // tm, n // tn, tm // rc, rows_per_batch // tm
    assert nchunks <= nj
    tile = lambda i: jnp.minimum(i, nt - 1)
    col = lambda i, j: jnp.where(i > 0, j, 0)
    def slab(w):
        k = w.shape[0]
        ns = max(c for c in range(1, nt * nj + 1) if k % (c * BF16_SUBLANES) == 0)
        return pl.BlockSpec((k // ns, w.shape[1]),
                            lambda i, j: (jnp.clip((i - 1) * nj + j, 0, ns - 1), 0))

    kern = functools.partial(_adaln_matmul_kernel, nw=len(ws), ncast=len(cast_ws), rc=rc,
                             rows=rows or tm, finish=finish)
    outs = pl.pallas_call(
        kern,
        grid=(nt + 1, nj),
        in_specs=[pl.BlockSpec((rc, d), lambda i, j: (tile(i) * nchunks
                                                      + jnp.minimum(j, nchunks - 1), 0)),
                  pl.BlockSpec((1, d), lambda i, j: (0, 0)),
                  pl.BlockSpec((None, None, 1, d), lambda i, j: (tile(i) // tpb, mod_row, 0, 0)),
                  pl.BlockSpec((None, None, 1, d),
                               lambda i, j: (tile(i) // tpb, mod_row + 1, 0, 0))]
                 + [pl.BlockSpec((d, tn), lambda i, j: (0, col(i, j))) for _ in ws]
                 + [slab(w) for w in cast_ws],
        out_specs=[pl.BlockSpec((tm, tn), lambda i, j: (jnp.maximum(i - 1, 0), col(i, j)))]
                  + [slab(w) for w in cast_ws],
        out_shape=[jax.ShapeDtypeStruct((t, n), BF16)]
                  + [jax.ShapeDtypeStruct(w.shape, BF16) for w in cast_ws],
        scratch_shapes=[pltpu.VMEM((tm, d), BF16), pltpu.VMEM((tm, d), BF16)],
        compiler_params=_params(("arbitrary", "arbitrary")),
        name=name,
    )(x2, g, mod4, mod4, *ws, *cast_ws)
    return outs[0], outs[1:]


def _in_proj(x2, g, mod4, w, rows_per_batch, cast_ws):
    return _adaln_matmul(x2, g, mod4, 0, [w], rows_per_batch, lambda y: y, "in_proj",
                         tm=2048, tn=1024, rows=512, cast_ws=cast_ws)


def _col_reduce(x, op, groups=8):
    k, n = x.shape
    if k % (groups * 8) == 0:
        x = op(x.reshape(groups, k // groups, n), axis=0)
    return op(x, axis=0, keepdims=True)


def _attn_kernel(lam_ref, sg_ref, q_ref, k_ref, v_ref, ct_ref, mw_ref, mb_ref, o_ref, mvo_ref,
                 vt_ref, acc_ref, m_ref, sa_ref, sb_ref, xa_ref, xb_ref, *, tq, sq, tk,
                 lam_init):
    nkb = vt_ref.shape[0]
    nc = tq // sq

    _mod_kernel(ct_ref, mw_ref, mb_ref, mvo_ref)

    for j in range(nkb):
        vt_ref[j, 0:HEAD_W, :] = v_ref[j * tk:(j + 1) * tk, :].astype(F32).T.astype(BF16)
        vt_ref[j, HEAD_W:, :] = jnp.ones((vt_ref.shape[1] - HEAD_W, tk), BF16)

    lax.fori_loop(0, q_ref.shape[0] // tq,
                  functools.partial(_attn_q_tile, lam_ref, sg_ref, q_ref, k_ref, o_ref, vt_ref,
                                    acc_ref, m_ref, sa_ref, sb_ref, xa_ref, xb_ref,
                                    tq, sq, tk, nc, lam_init), 0)


def _attn_q_tile(lam_ref, sg_ref, q_ref, k_ref, o_ref, vt_ref, acc_ref, m_ref, sa_ref, sb_ref,
                 xa_ref, xb_ref, tq, sq, tk, nc, lam_init, qi, carry):
    q_rows = pl.ds(pl.multiple_of(qi * tq, tq), tq)
    q = (q_ref[q_rows, :].astype(F32) * (ATT_HALF_DIM ** -0.5 * math.log2(math.e))).astype(BF16)
    lane = lax.broadcasted_iota(jnp.int32, q.shape, 1)
    zero = jnp.zeros_like(q)
    qm = (jnp.where(lane < ATT_HALF_DIM, q, zero), jnp.where(lane >= ATT_HALF_DIM, q, zero))

    m_ref[...] = jnp.full(m_ref.shape, NEG_BIG, F32)
    acc_ref[...] = jnp.zeros(acc_ref.shape, F32)

    def scores(c, mi, k):
        return lax.dot_general(k, qm[mi][c * sq:(c + 1) * sq, :], (((1,), (1,)), ((), ())),
                               preferred_element_type=F32)

    def update(c, mi, vt, s, smax=None):
        cs = slice(c * sq, (c + 1) * sq)
        m_old = m_ref[mi, :, cs]
        m_new = jnp.maximum(m_old, _col_reduce(s, jnp.max) if smax is None else smax)
        p = jnp.exp2(s - m_new)
        alpha = jnp.exp2(m_old - m_new)
        acc_ref[mi, :, cs] = alpha * acc_ref[mi, :, cs] + jnp.dot(
            vt, p.astype(BF16), preferred_element_type=F32)
        m_ref[mi, :, cs] = m_new

    assert tq == 2 * tk
    chains = [(c, mi) for c in range(nc) for mi in range(2)]

    bufs = ((sa_ref, xa_ref), (sb_ref, xb_ref))

    def produce(idx, k, dst):
        c, mi = chains[idx]
        s = scores(c, mi, k)
        dst[0][idx] = s
        dst[1][idx] = _col_reduce(s, jnp.max)

    def stage(j, src, dst):
        vt = vt_ref[j]
        if dst is not None:
            kn = k_ref[pl.ds(pl.multiple_of((j + 1) * tk, tk), tk), :]
        for idx, (c, mi) in enumerate(chains):
            if dst is not None:
                produce(idx, kn, dst)
            update(c, mi, vt, src[0][idx], src[1][idx])

    @pl.when(qi > 0)
    def _():
        k0 = k_ref[0:tk, :]
        for idx in range(len(chains)):
            produce(idx, k0, bufs[0])

        def pair(pp, carry):
            stage(2 * pp, bufs[0], bufs[1])
            stage(2 * pp + 1, bufs[1], bufs[0])
            return carry

        lax.fori_loop(0, qi - 1, pair, 0)
        stage(2 * qi - 2, bufs[0], bufs[1])
        stage(2 * qi - 1, bufs[1], None)

    krow = lax.broadcasted_iota(jnp.int32, (sq, sq), 0)
    qcol = lax.broadcasted_iota(jnp.int32, (sq, sq), 1)
    dmask = krow // MASK_CHUNK <= qcol // MASK_CHUNK
    vt_tile = jnp.concatenate([vt_ref[qi * 2], vt_ref[qi * 2 + 1]], axis=1)

    def diag_slot(n, piece):
        slot = 2 * n + piece
        return bufs[slot // len(chains)], slot % len(chains)

    def diag_produce(n, c, mi):
        nk = (c + 1) * sq
        s = scores(c, mi, k_ref[pl.ds(pl.multiple_of(qi * tq, tq), nk), :])
        tail = jnp.where(dmask, s[c * sq:, :], NEG_BIG)
        s = tail if c == 0 else jnp.concatenate([s[:c * sq, :], tail], axis=0)
        for piece, r0 in enumerate(range(0, nk, tk)):
            buf, slot = diag_slot(n, piece)
            rows = min(tk, nk - r0)
            buf[0][slot, 0:rows, :] = s[r0:r0 + rows, :]
        buf, slot = diag_slot(n, 0)
        buf[1][slot] = _col_reduce(s, jnp.max)

    def diag_consume(n, c, mi):
        nk = (c + 1) * sq
        parts = []
        for piece, r0 in enumerate(range(0, nk, tk)):
            buf, slot = diag_slot(n, piece)
            parts.append(buf[0][slot, 0:min(tk, nk - r0), :])
        buf, slot = diag_slot(n, 0)
        update(c, mi, vt_tile[:, :nk], jnp.concatenate(parts, axis=0), buf[1][slot])

    order = [(c, mi) for c in reversed(range(nc)) for mi in range(2)]
    for n in range(len(order) + DIAG_AHEAD):
        if n < len(order):
            diag_produce(n, *order[n])
        if n >= DIAG_AHEAD:
            diag_consume(n - DIAG_AHEAD, *order[n - DIAG_AHEAD])

    lp = lam_ref[...]
    lam = (jnp.exp(jnp.sum(lp[0:1] * lp[1:2], axis=-1, keepdims=True))
           - jnp.exp(jnp.sum(lp[2:3] * lp[3:4], axis=-1, keepdims=True)) + lam_init)
    o = (acc_ref[0, 0:HEAD_W, :] / acc_ref[0, HEAD_W:HEAD_W + 1, :]
         - lam * (acc_ref[1, 0:HEAD_W, :] / acc_ref[1, HEAD_W:HEAD_W + 1, :]))
    ms = jnp.mean(o * o, axis=0, keepdims=True)
    y = o * lax.rsqrt(ms + NORM_EPS) * sg_ref[...] * (1.0 - lam_init)
    o_ref[q_rows, :] = y.T.astype(o_ref.dtype)
    return carry


def _attention(p, lam4, sg_col, nb, s, lam_init, ct, w_mod, b_row, mod_col0, tq=1024, sq=256,
               tk=512):
    h = ATT_HEADS
    kern = functools.partial(_attn_kernel, tq=tq, sq=sq, tk=tk, lam_init=lam_init)
    nchains = 2 * tq // sq
    d = ct.shape[0]
    sw = (w_mod.shape[1] - mod_col0) // (nb * h)
    mslab = lambda rows, off: pl.BlockSpec((rows, sw), lambda b, hh: (0, off + b * h + hh))
    return pl.pallas_call(
        kern,
        grid=(nb, h),
        in_specs=[pl.BlockSpec((4, ATT_HALF_DIM), lambda b, hh: (0, 0)),
                  pl.BlockSpec((HEAD_W, 1), lambda b, hh: (0, 0)),
                  pl.BlockSpec((s, HEAD_W), lambda b, hh: (b, hh)),
                  pl.BlockSpec((s, HEAD_W), lambda b, hh: (b, h + hh)),
                  pl.BlockSpec((s, HEAD_W), lambda b, hh: (b, 2 * h + hh)),
                  pl.BlockSpec(ct.shape, lambda b, hh: (0, 0)),
                  mslab(d, mod_col0 // sw), mslab(1, mod_col0 // sw)],
        out_specs=[pl.BlockSpec((s, HEAD_W), lambda b, hh: (b, hh)), mslab(nb, 0)],
        out_shape=[jax.ShapeDtypeStruct((nb * s, h * HEAD_W), BF16),
                   jax.ShapeDtypeStruct((nb, w_mod.shape[1] - mod_col0), F32)],
        scratch_shapes=[pltpu.VMEM((s // tk, HEAD_W + ONES_ROWS, tk), BF16),
                        pltpu.VMEM((2, HEAD_W + ONES_ROWS, tq), F32),
                        pltpu.VMEM((2, 1, tq), F32),
                        pltpu.VMEM((nchains, tk, sq), F32),
                        pltpu.VMEM((nchains, tk, sq), F32),
                        pltpu.VMEM((nchains, 1, sq), F32),
                        pltpu.VMEM((nchains, 1, sq), F32)],
        compiler_params=_params(("parallel", "parallel")),
        name="diff_attn",
    )(lam4, sg_col, p, p, p, ct, w_mod, b_row)


def _group_row(a, s, grp=8):
    n, w = a.shape
    a3 = a.reshape(n // grp, grp, w)
    return jnp.broadcast_to(a3[:, s:s + 1, :], (n // grp, grp, w)).reshape(n, w)


def _hgrn2_kernel(lbl_ref, gn_ref, q_ref, f_ref, i_ref, g_ref, o_ref, st_ref, *, ch, layer):
    @pl.when(pl.program_id(1) == 0)
    def _():
        st_ref[...] = jnp.zeros(st_ref.shape, F32)

    w = HEAD_W
    lbl = lbl_ref[...]
    mx = jnp.max(lbl, axis=0, keepdims=True)
    e = jnp.exp(lbl - mx)
    lb = (jnp.sum(e[0:layer + 1], axis=0, keepdims=True)
          / jnp.sum(e, axis=0, keepdims=True))

    nt = (((1,), (1,)), ((), ()))
    row = lax.broadcasted_iota(jnp.int32, (ch, w), 0)
    arow = lax.broadcasted_iota(jnp.int32, (ch, ch), 0)
    acol = lax.broadcasted_iota(jnp.int32, (ch, ch), 1)
    tril = (arow >= acol).astype(BF16)
    on_diag = arow == acol
    levels = []
    hs = ch // 2
    while hs >= 1:
        upper = (row % (2 * hs)) >= hs
        levels.append((hs, upper, jnp.where(upper, 1.0, -1.0),
                       (arow // (2 * hs) == acol // (2 * hs))
                       & ((arow % (2 * hs)) >= hs) & ((acol % (2 * hs)) < hs)))
        hs //= 2

    def midpoint_row(b, hs):
        if hs >= 8:
            return jnp.concatenate(
                [jnp.broadcast_to(b[blk * 2 * hs + hs - 1:blk * 2 * hs + hs, :], (2 * hs, w))
                 for blk in range(ch // (2 * hs))], axis=0)
        if hs == 4:
            return _group_row(b, 3)
        return jnp.where((row % 8) < 4, _group_row(b, 1), _group_row(b, 5))

    chunk_vals = {}

    def chunk_prep(c):
        if c not in chunk_vals:
            rs = slice(c * ch, (c + 1) * ch)
            f_all = lb + (1.0 - lb) * jax.nn.sigmoid(f_ref[rs, :].astype(F32))
            g2 = jnp.log2(f_all)
            g2_hi = g2.astype(BF16)
            g2_lo = (g2 - g2_hi.astype(F32)).astype(BF16)
            b_all = (jnp.dot(tril, g2_hi, preferred_element_type=F32)
                     + jnp.dot(tril, g2_lo, preferred_element_type=F32))
            chunk_vals[c] = (f_all, b_all)
        return chunk_vals[c]

    def operands(c, h):
        rs, cs = slice(c * ch, (c + 1) * ch), slice(h * w, (h + 1) * w)
        f_all, b_all = chunk_prep(c)
        b, f = b_all[:, cs], f_all[:, cs]
        kk = 1.0 - f
        q = q_ref[rs, cs].astype(F32)
        blast = b[ch - 1:ch, :]
        ys = []
        for hs, upper, sign, _ in levels:
            if hs == 1:
                y = jnp.where(upper, q * f, kk)
            else:
                y = jnp.where(upper, q, kk) * jnp.exp2((b - midpoint_row(b, hs)) * sign)
            ys.append(y.astype(BF16))
        return dict(rs=rs, cs=cs, h=h, ys=ys,
                    qin=(q * jnp.exp2(b)).astype(BF16),
                    kout=(kk * jnp.exp2(blast - b)).astype(BF16),
                    keep=jnp.exp2(blast),
                    dsum=jnp.sum(q * kk, axis=-1, keepdims=True))

    def level_matmuls(x):
        st = st_ref[x["h"]]
        x["st"] = st
        x["o"] = lax.dot_general(x["qin"], st.astype(BF16), nt, preferred_element_type=F32)
        x["aa"] = [lax.dot_general(y, y, nt, preferred_element_type=F32) for y in x["ys"]]

    def finish(x):
        rs, cs = x["rs"], x["cs"]
        v = i_ref[rs, cs]
        att = jnp.where(on_diag, x["dsum"], 0.0).astype(BF16)
        for (_, _, _, amask), a in zip(levels, x["aa"]):
            att = jnp.where(amask, a.astype(BF16), att)
        o = x["o"] + jnp.dot(att, v, preferred_element_type=F32)
        st_ref[x["h"]] = x["keep"] * x["st"] + lax.dot_general(
            v, x["kout"], (((0,), (0,)), ((), ())), preferred_element_type=F32)
        y = _rms(o) * gn_ref[:, cs] * jax.nn.sigmoid(g_ref[rs, cs].astype(F32))
        o_ref[rs, cs] = y.astype(o_ref.dtype)

    items = [(c, h) for c in range(q_ref.shape[0] // ch) for h in range(REC_HEADS)]
    staged = []
    d1, d2 = 1, 2
    for i in range(len(items) + d2):
        if i < len(items):
            staged.append(operands(*items[i]))
        if d1 <= i < len(items) + d1:
            level_matmuls(staged[i - d1])
        if i >= d2:
            finish(staged[i - d2])


def _hgrn2(p, lb_logits, gn, nb, s, layer, rb=512, ch=128):
    hw = REC_HEADS * HEAD_W
    nr = s // rb
    c0 = 3 * ATT_HEADS * HEAD_W // hw
    kern = functools.partial(_hgrn2_kernel, ch=ch, layer=layer)
    spec = lambda off: pl.BlockSpec((rb, hw), lambda b, i: (b * nr + i, c0 + off))
    return pl.pallas_call(
        kern,
        grid=(nb, nr),
        in_specs=[pl.BlockSpec(lb_logits.shape, lambda b, i: (0, 0)),
                  pl.BlockSpec((1, hw), lambda b, i: (0, 0)),
                  spec(0), spec(1), spec(2), spec(3)],
        out_specs=pl.BlockSpec((rb, hw), lambda b, i: (b * nr + i, 0)),
        out_shape=jax.ShapeDtypeStruct((nb * s, hw), BF16),
        scratch_shapes=[pltpu.VMEM((REC_HEADS, HEAD_W, HEAD_W), F32)],
        compiler_params=_params(("parallel", "arbitrary")),
        name="hgrn2",
    )(lb_logits, gn, p, p, p, p)


def _merge_oproj_kernel(ya_ref, yr_ref, *refs, ng):
    ga_refs, gr_refs = refs[:ng], refs[ng:2 * ng]
    x_ref, gt_ref, wa_ref, wr_ref, wo_ref, o_ref = refs[2 * ng:]
    gw = wa_ref.shape[1] // ng
    parts = []
    for g in range(ng):
        cols = slice(g * gw, (g + 1) * gw)
        ta = jnp.dot(ya_ref[...], wa_ref[:, cols], preferred_element_type=F32)
        tr = jnp.dot(yr_ref[...], wr_ref[:, cols], preferred_element_type=F32)
        parts.append((jax.nn.sigmoid(ga_refs[g][...].astype(F32)) * ta
                      + jax.nn.sigmoid(gr_refs[g][...].astype(F32)) * tr).astype(BF16))
    m = jnp.concatenate(parts, axis=1)
    o_ref[...] = x_ref[...] + gt_ref[...] * jnp.dot(m, wo_ref[...], preferred_element_type=F32)


def _merge_oproj(ya, yr, p, x2, mod4, wa, wr, wo, ga_col0, rows_per_batch, tm=512, gw=1024):
    t, ka = ya.shape
    d = wa.shape[1]
    ng = d // gw
    ga0 = ga_col0 // gw
    gr0 = (ga_col0 + d) // gw
    tpb = rows_per_batch // tm
    gate = lambda b0, g: pl.BlockSpec((tm, gw), lambda i: (i, b0 + g))
    resident = lambda w: pl.BlockSpec(w.shape, lambda i: (0, 0), pipeline_mode=pl.Buffered(1))
    return pl.pallas_call(
        functools.partial(_merge_oproj_kernel, ng=ng),
        grid=(t // tm,),
        in_specs=[pl.BlockSpec((tm, ka), lambda i: (i, 0)),
                  pl.BlockSpec((tm, ka), lambda i: (i, 0))]
                 + [gate(ga0, g) for g in range(ng)] + [gate(gr0, g) for g in range(ng)]
                 + [pl.BlockSpec((tm, d), lambda i: (i, 0)),
                    pl.BlockSpec((None, None, 1, d), lambda i: (i // tpb, 2, 0, 0)),
                    resident(wa), resident(wr), resident(wo)],
        out_specs=pl.BlockSpec((tm, d), lambda i: (i, 0)),
        out_shape=jax.ShapeDtypeStruct((t, d), F32),
        compiler_params=_params(("parallel",)),
        name="merge_oproj",
    )(ya, yr, *([p] * (2 * ng)), x2, mod4, wa, wr, wo)


def _ffn_up(h1, g, mod4, wg, wu, rows_per_batch, cast_ws):
    swiglu = lambda gate, up: gate * jax.nn.sigmoid(gate) * up
    return _adaln_matmul(h1, g, mod4, 3, [wg, wu], rows_per_batch, swiglu, "ffn_up",
                         tm=2048, tn=512, rc=256, rows=256, cast_ws=cast_ws)


def _ffn_down_kernel(a_ref, h_ref, gt_ref, fg_ref, w_ref, o_ref, *, tn, final):
    j = pl.program_id(1)
    y = jnp.dot(a_ref[...], w_ref[...], preferred_element_type=F32)
    o_ref[:, pl.ds(pl.multiple_of(j * tn, tn), tn)] = h_ref[...] + gt_ref[...] * y

    if final:
        @pl.when(j == pl.num_programs(1) - 1)
        def _():
            o_ref[...] = _rms(o_ref[...]) * fg_ref[...]


def _ffn_down(a, h1, mod4, fg, w, rows_per_batch, final, tm=1024, tn=512):
    t, d = h1.shape
    f = a.shape[1]
    tpb = rows_per_batch // tm
    return pl.pallas_call(
        functools.partial(_ffn_down_kernel, tn=tn, final=final),
        grid=(t // tm, d // tn),
        in_specs=[pl.BlockSpec((tm, f), lambda i, j: (i, 0)),
                  pl.BlockSpec((tm, tn), lambda i, j: (i, j)),
                  pl.BlockSpec((None, None, 1, tn), lambda i, j: (i // tpb, 5, 0, j)),
                  pl.BlockSpec((1, d), lambda i, j: (0, 0)),
                  pl.BlockSpec((f, tn), lambda i, j: (0, j))],
        out_specs=pl.BlockSpec((tm, d), lambda i, j: (i, 0)),
        out_shape=jax.ShapeDtypeStruct((t, d), F32),
        compiler_params=_params(("parallel", "arbitrary")),
        name="ffn_down",
    )(a, h1, mod4, fg, w)


def kernel(x, c, w_mod, b_mod, norm1_g, w_in, lambda_q1, lambda_k1, lambda_q2, lambda_k2,
           subln_g, lb_logits, gnorm_g, w_att_out, w_rec_out, w_o, norm2_g,
           w_ffn_gate, w_ffn_up, w_ffn_down, final_g):
    nb, s, d = x.shape
    depth = w_mod.shape[0]
    h = x.reshape(nb * s, d)
    for l in range(depth):
        lam_init = 0.8 - 0.6 * math.exp(-0.3 * l)
        ct, b_row = c.T, b_mod[l].reshape(1, 6 * d)
        mod_head = _mod(ct, w_mod[l], b_row, 2 * d)
        p, (wa, wr, wo) = _in_proj(h, norm1_g[l].reshape(1, d), mod_head.reshape(nb, 2, 1, d),
                                   w_in[l], s, cast_ws=(w_att_out[l], w_rec_out[l], w_o[l]))
        lam4 = jnp.stack([lambda_q1[l], lambda_k1[l], lambda_q2[l], lambda_k2[l]])
        ya, mod_tail = _attention(p, lam4, subln_g[l].reshape(HEAD_W, 1), nb, s, lam_init,
                                  ct, w_mod[l], b_row, 2 * d)
        mod4 = jnp.concatenate([mod_head, mod_tail], axis=1).reshape(nb, 6, 1, d)
        yr = _hgrn2(p, lb_logits, gnorm_g[l].reshape(1, REC_HEADS * HEAD_W), nb, s, l)
        ga_col0 = 3 * ATT_HEADS * HEAD_W + 4 * REC_HEADS * HEAD_W
        h1 = _merge_oproj(ya, yr, p, h, mod4, wa, wr, wo, ga_col0, s)
        a, (wd,) = _ffn_up(h1, norm2_g[l].reshape(1, d), mod4, w_ffn_gate[l], w_ffn_up[l], s,
                           cast_ws=(w_ffn_down[l],))
        h = _ffn_down(a, h1, mod4, final_g.reshape(1, d), wd, s, l == depth - 1)
    return h.reshape(nb, s, d)
```

```python
import functools
import math

import jax
import jax.numpy as jnp
from jax import lax
from jax.experimental import pallas as pl
from jax.experimental.pallas import tpu as pltpu

F32 = jnp.float32
BF16 = jnp.bfloat16

NORM_EPS = 1e-6
ATT_HEADS = 8
ATT_HALF_DIM = 64
HEAD_W = 128
REC_HEADS = 8
MASK_CHUNK = 64
NEG_BIG = -1e30
BF16_SUBLANES = 16
ONES_ROWS = BF16_SUBLANES
DIAG_AHEAD = 4

VMEM_LIMIT = 60 * 1024 * 1024


def _params(sem, vmem=VMEM_LIMIT):
    return pltpu.CompilerParams(dimension_semantics=sem, vmem_limit_bytes=vmem)


def _rms(x):
    return x * lax.rsqrt(jnp.mean(x * x, axis=-1, keepdims=True) + NORM_EPS)


def _adaln(x, g, sc, sh):
    return _rms(x) * (g * (1.0 + sc)) + sh


def _mod_kernel(ct_ref, w_ref, b_ref, o_ref):
    ct = ct_ref[...]
    cond = ct * jax.nn.sigmoid(ct)
    w = w_ref[...]
    for b in range(ct.shape[1]):
        o_ref[b:b + 1, :] = jnp.sum(w * cond[:, b:b + 1], axis=0, keepdims=True) + b_ref[...]


def _mod(ct, w_mod, b_row, n, tn=1024):
    d, nb = ct.shape
    return pl.pallas_call(
        _mod_kernel,
        grid=(n // tn,),
        in_specs=[pl.BlockSpec((d, nb), lambda j: (0, 0)),
                  pl.BlockSpec((d, tn), lambda j: (0, j)),
                  pl.BlockSpec((1, tn), lambda j: (0, j))],
        out_specs=pl.BlockSpec((nb, tn), lambda j: (0, j)),
        out_shape=jax.ShapeDtypeStruct((nb, n), F32),
        compiler_params=_params(("arbitrary",)),
        name="mod",
    )(ct, w_mod, b_row)


def _adaln_matmul_kernel(x_ref, g_ref, sh_ref, sc_ref, *refs, nw, ncast, rc, rows, finish):
    w_refs, cast_in = refs[:nw], refs[nw:nw + ncast]
    o_ref, cast_out = refs[nw + ncast], refs[nw + ncast + 1:nw + 2 * ncast + 1]
    u_bufs = refs[nw + 2 * ncast + 1:]
    i, j = pl.program_id(0), pl.program_id(1)
    nchunks = u_bufs[0].shape[0] // rc

    def normalise_into(u_ref):
        normed = _adaln(x_ref[...], g_ref[...], sc_ref[...], sh_ref[...]).astype(BF16)
        r0 = pl.multiple_of(jnp.minimum(j, nchunks - 1) * rc, rc)
        u_ref[pl.ds(r0, rc), :] = normed
        return normed

    def multiply_from(u_ref, anchor):
        wb = [w[...].astype(BF16) for w in w_refs]
        nsub = u_ref.shape[0] // rows
        for k in range(nsub):
            u = u_ref[k * rows:(k + 1) * rows, :]
            if k == nsub - 1:
                head = jnp.where(j >= 0, u[:rc, :], anchor)
                u = head if rows == rc else jnp.concatenate([head, u[rc:, :]], axis=0)
            o_ref[k * rows:(k + 1) * rows, :] = finish(
                *[jnp.dot(u, w, preferred_element_type=F32) for w in wb]).astype(o_ref.dtype)

    @pl.when(i == 0)
    def _():
        normalise_into(u_bufs[0])

    for parity in range(2):
        @pl.when((i > 0) & (i % 2 == parity))
        def _(parity=parity):
            multiply_from(u_bufs[1 - parity], normalise_into(u_bufs[parity]))
            for w_ref, wb_ref in zip(cast_in, cast_out):
                wb_ref[...] = w_ref[...].astype(BF16)


def _adaln_matmul(x2, g, mod4, mod_row, ws, rows_per_batch, finish, name, tm, tn, rc=256,
                  rows=None, cast_ws=()):
    t, d = x2.shape
    n = ws[0].shape[1]
    nt, nj, nchunks, tpb = t // tm, n // tn, tm // rc, rows_per_batch // tm
    assert nchunks <= nj
    tile = lambda i: jnp.minimum(i, nt - 1)
    col = lambda i, j: jnp.where(i > 0, j, 0)
    def slab(w):
        k = w.shape[0]
        ns = max(c for c in range(1, nt * nj + 1) if k % (c * BF16_SUBLANES) == 0)
        return pl.BlockSpec((k // ns, w.shape[1]),
                            lambda i, j: (jnp.clip((i - 1) * nj + j, 0, ns - 1), 0))

    kern = functools.partial(_adaln_matmul_kernel, nw=len(ws), ncast=len(cast_ws), rc=rc,
                             rows=rows or tm, finish=finish)
    outs = pl.pallas_call(
        kern,
        grid=(nt + 1, nj),
        in_specs=[pl.BlockSpec((rc, d), lambda i, j: (tile(i) * nchunks
                                                      + jnp.minimum(j, nchunks - 1), 0)),
                  pl.BlockSpec((1, d), lambda i, j: (0, 0)),
                  pl.BlockSpec((None, None, 1, d), lambda i, j: (tile(i) // tpb, mod_row, 0, 0)),
                  pl.BlockSpec((None, None, 1, d),
                               lambda i, j: (tile(i) // tpb, mod_row + 1, 0, 0))]
                 + [pl.BlockSpec((d, tn), lambda i, j: (0, col(i, j))) for _ in ws]
                 + [slab(w) for w in cast_ws],
        out_specs=[pl.BlockSpec((tm, tn), lambda i, j: (jnp.maximum(i - 1, 0), col(i, j)))]
                  + [slab(w) for w in cast_ws],
        out_shape=[jax.ShapeDtypeStruct((t, n), BF16)]
                  + [jax.ShapeDtypeStruct(w.shape, BF16) for w in cast_ws],
        scratch_shapes=[pltpu.VMEM((tm, d), BF16), pltpu.VMEM((tm, d), BF16)],
        compiler_params=_params(("arbitrary", "arbitrary")),
        name=name,
    )(x2, g, mod4, mod4, *ws, *cast_ws)
    return outs[0], outs[1:]


def _in_proj(x2, g, mod4, w, rows_per_batch, cast_ws):
    return _adaln_matmul(x2, g, mod4, 0, [w], rows_per_batch, lambda y: y, "in_proj",
                         tm=2048, tn=1024, rows=512, cast_ws=cast_ws)


def _col_reduce(x, op, groups=8):
    k, n = x.shape
    if k % (groups * 8) == 0:
        x = op(x.reshape(groups, k // groups, n), axis=0)
    return op(x, axis=0, keepdims=True)


def _attn_kernel(lam_ref, sg_ref, q_ref, k_ref, v_ref, ct_ref, mw_ref, mb_ref, o_ref, mvo_ref,
                 vt_ref, acc_ref, m_ref, sa_ref, sb_ref, xa_ref, xb_ref, *, tq, sq, tk,
                 lam_init):
    nc = tq // sq

    lax.fori_loop(0, q_ref.shape[0] // tq,
                  functools.partial(_attn_q_tile, lam_ref, sg_ref, q_ref, k_ref, v_ref,
                                    (ct_ref, mw_ref, mb_ref, mvo_ref), o_ref,
                                    vt_ref, acc_ref, m_ref, sa_ref, sb_ref, xa_ref, xb_ref,
                                    tq, sq, tk, nc, lam_init), 0)


def _attn_q_tile(lam_ref, sg_ref, q_ref, k_ref, v_ref, side, o_ref, vt_ref, acc_ref, m_ref,
                 sa_ref, sb_ref, xa_ref, xb_ref, tq, sq, tk, nc, lam_init, qi, carry):
    q_rows = pl.ds(pl.multiple_of(qi * tq, tq), tq)
    q = (q_ref[q_rows, :].astype(F32) * (ATT_HALF_DIM ** -0.5 * math.log2(math.e))).astype(BF16)
    lane = lax.broadcasted_iota(jnp.int32, q.shape, 1)
    zero = jnp.zeros_like(q)
    qm = (jnp.where(lane < ATT_HALF_DIM, q, zero), jnp.where(lane >= ATT_HALF_DIM, q, zero))

    m_ref[...] = jnp.full(m_ref.shape, NEG_BIG, F32)
    acc_ref[...] = jnp.zeros(acc_ref.shape, F32)

    def scores(c, mi, k):
        return lax.dot_general(k, qm[mi][c * sq:(c + 1) * sq, :], (((1,), (1,)), ((), ())),
                               preferred_element_type=F32)

    def update(c, mi, vt, s, smax=None):
        cs = slice(c * sq, (c + 1) * sq)
        m_old = m_ref[mi, :, cs]
        m_new = jnp.maximum(m_old, _col_reduce(s, jnp.max) if smax is None else smax)
        p = jnp.exp2(s - m_new)
        alpha = jnp.exp2(m_old - m_new)
        acc_ref[mi, :, cs] = alpha * acc_ref[mi, :, cs] + jnp.dot(
            vt, p.astype(BF16), preferred_element_type=F32)
        m_ref[mi, :, cs] = m_new

    assert tq == 2 * tk
    chains = [(c, mi) for c in range(nc) for mi in range(2)]

    bufs = ((sa_ref, xa_ref), (sb_ref, xb_ref))

    def produce(idx, k, dst):
        c, mi = chains[idx]
        s = scores(c, mi, k)
        dst[0][idx] = s
        dst[1][idx] = _col_reduce(s, jnp.max)

    def stage(j, src, dst):
        vt = vt_ref[j]
        if dst is not None:
            kn = k_ref[pl.ds(pl.multiple_of((j + 1) * tk, tk), tk), :]
        for idx, (c, mi) in enumerate(chains):
            if dst is not None:
                produce(idx, kn, dst)
            update(c, mi, vt, src[0][idx], src[1][idx])

    @pl.when(qi > 0)
    def _():
        k0 = k_ref[0:tk, :]
        for idx in range(len(chains)):
            produce(idx, k0, bufs[0])

        def pair(pp, carry):
            stage(2 * pp, bufs[0], bufs[1])
            stage(2 * pp + 1, bufs[1], bufs[0])
            return carry

        lax.fori_loop(0, qi - 1, pair, 0)
        stage(2 * qi - 2, bufs[0], bufs[1])
        stage(2 * qi - 1, bufs[1], None)

    krow = lax.broadcasted_iota(jnp.int32, (sq, sq), 0)
    qcol = lax.broadcasted_iota(jnp.int32, (sq, sq), 1)
    dmask = krow // MASK_CHUNK <= qcol // MASK_CHUNK
    for jb in (qi * 2, qi * 2 + 1):
        v_blk = v_ref[pl.ds(pl.multiple_of(jb * tk, tk), tk), :]
        vt_ref[jb, 0:HEAD_W, :] = v_blk.astype(F32).T.astype(BF16)
        vt_ref[jb, HEAD_W:, :] = jnp.ones((vt_ref.shape[1] - HEAD_W, tk), BF16)
    vt_tile = jnp.concatenate([vt_ref[qi * 2], vt_ref[qi * 2 + 1]], axis=1)

    def diag_slot(n, piece):
        slot = 2 * n + piece
        return bufs[slot // len(chains)], slot % len(chains)

    def diag_produce(n, c, mi):
        nk = (c + 1) * sq
        s = scores(c, mi, k_ref[pl.ds(pl.multiple_of(qi * tq, tq), nk), :])
        tail = jnp.where(dmask, s[c * sq:, :], NEG_BIG)
        s = tail if c == 0 else jnp.concatenate([s[:c * sq, :], tail], axis=0)
        for piece, r0 in enumerate(range(0, nk, tk)):
            buf, slot = diag_slot(n, piece)
            rows = min(tk, nk - r0)
            buf[0][slot, 0:rows, :] = s[r0:r0 + rows, :]
        buf, slot = diag_slot(n, 0)
        buf[1][slot] = _col_reduce(s, jnp.max)

    ct_ref, mw_ref, mb_ref, mvo_ref = side
    mcols = mw_ref.shape[1] // (q_ref.shape[0] // tq)
    cols = pl.ds(pl.multiple_of(qi * mcols, mcols), mcols)
    ct = ct_ref[...]
    cond = ct * jax.nn.sigmoid(ct)
    mv_rows = [jnp.sum(mw_ref[:, cols] * cond[:, b:b + 1], axis=0, keepdims=True)
               + mb_ref[:, cols] for b in range(ct.shape[1])]
    for b, row_b in enumerate(mv_rows):
        mvo_ref[b:b + 1, cols] = row_b
    assert sq % (len(mv_rows) * mcols) == 0
    mv_anchor = jnp.broadcast_to(
        jnp.concatenate(mv_rows * (sq // (len(mv_rows) * mcols)), axis=1), (sq, sq))

    def diag_consume(n, c, mi):
        nk = (c + 1) * sq
        parts = []
        for piece, r0 in enumerate(range(0, nk, tk)):
            buf, slot = diag_slot(n, piece)
            parts.append(buf[0][slot, 0:min(tk, nk - r0), :])
        buf, slot = diag_slot(n, 0)
        s = jnp.concatenate(parts, axis=0)
        if n == len(order) - 1:
            assert s.shape == mv_anchor.shape
            s = jnp.where(qi >= 0, s, mv_anchor)
        update(c, mi, vt_tile[:, :nk], s, buf[1][slot])

    order = [(c, mi) for c in reversed(range(nc)) for mi in range(2)]
    for n in range(len(order) + DIAG_AHEAD):
        if n < len(order):
            diag_produce(n, *order[n])
        if n >= DIAG_AHEAD:
            diag_consume(n - DIAG_AHEAD, *order[n - DIAG_AHEAD])

    lp = lam_ref[...]
    lam = (jnp.exp(jnp.sum(lp[0:1] * lp[1:2], axis=-1, keepdims=True))
           - jnp.exp(jnp.sum(lp[2:3] * lp[3:4], axis=-1, keepdims=True)) + lam_init)
    o = (acc_ref[0, 0:HEAD_W, :] / acc_ref[0, HEAD_W:HEAD_W + 1, :]
         - lam * (acc_ref[1, 0:HEAD_W, :] / acc_ref[1, HEAD_W:HEAD_W + 1, :]))
    ms = jnp.mean(o * o, axis=0, keepdims=True)
    y = o * lax.rsqrt(ms + NORM_EPS) * sg_ref[...] * (1.0 - lam_init)
    o_ref[q_rows, :] = y.T.astype(o_ref.dtype)
    return carry


def _attention(p, lam4, sg_col, nb, s, lam_init, ct, w_mod, b_row, mod_col0, tq=1024, sq=256,
               tk=512):
    h = ATT_HEADS
    kern = functools.partial(_attn_kernel, tq=tq, sq=sq, tk=tk, lam_init=lam_init)
    nchains = 2 * tq // sq
    d = ct.shape[0]
    sw = (w_mod.shape[1] - mod_col0) // (nb * h)
    mslab = lambda rows, off: pl.BlockSpec((rows, sw), lambda b, hh: (0, off + b * h + hh))
    return pl.pallas_call(
        kern,
        grid=(nb, h),
        in_specs=[pl.BlockSpec((4, ATT_HALF_DIM), lambda b, hh: (0, 0)),
                  pl.BlockSpec((HEAD_W, 1), lambda b, hh: (0, 0)),
                  pl.BlockSpec((s, HEAD_W), lambda b, hh: (b, hh)),
                  pl.BlockSpec((s, HEAD_W), lambda b, hh: (b, h + hh)),
                  pl.BlockSpec((s, HEAD_W), lambda b, hh: (b, 2 * h + hh)),
                  pl.BlockSpec(ct.shape, lambda b, hh: (0, 0)),
                  mslab(d, mod_col0 // sw), mslab(1, mod_col0 // sw)],
        out_specs=[pl.BlockSpec((s, HEAD_W), lambda b, hh: (b, hh)), mslab(nb, 0)],
        out_shape=[jax.ShapeDtypeStruct((nb * s, h * HEAD_W), BF16),
                   jax.ShapeDtypeStruct((nb, w_mod.shape[1] - mod_col0), F32)],
        scratch_shapes=[pltpu.VMEM((s // tk, HEAD_W + ONES_ROWS, tk), BF16),
                        pltpu.VMEM((2, HEAD_W + ONES_ROWS, tq), F32),
                        pltpu.VMEM((2, 1, tq), F32),
                        pltpu.VMEM((nchains, tk, sq), F32),
                        pltpu.VMEM((nchains, tk, sq), F32),
                        pltpu.VMEM((nchains, 1, sq), F32),
                        pltpu.VMEM((nchains, 1, sq), F32)],
        compiler_params=_params(("parallel", "parallel")),
        name="diff_attn",
    )(lam4, sg_col, p, p, p, ct, w_mod, b_row)


def _group_row(a, s, grp=8):
    n, w = a.shape
    a3 = a.reshape(n // grp, grp, w)
    return jnp.broadcast_to(a3[:, s:s + 1, :], (n // grp, grp, w)).reshape(n, w)


def _hgrn2_kernel(lbl_ref, gn_ref, q_ref, f_ref, i_ref, g_ref, o_ref, st_ref, *, ch, layer):
    @pl.when(pl.program_id(1) == 0)
    def _():
        st_ref[...] = jnp.zeros(st_ref.shape, F32)

    w = HEAD_W
    lbl = lbl_ref[...]
    mx = jnp.max(lbl, axis=0, keepdims=True)
    e = jnp.exp(lbl - mx)
    lb = (jnp.sum(e[0:layer + 1], axis=0, keepdims=True)
          / jnp.sum(e, axis=0, keepdims=True))

    nt = (((1,), (1,)), ((), ()))
    row = lax.broadcasted_iota(jnp.int32, (ch, w), 0)
    arow = lax.broadcasted_iota(jnp.int32, (ch, ch), 0)
    acol = lax.broadcasted_iota(jnp.int32, (ch, ch), 1)
    tril = (arow >= acol).astype(BF16)
    on_diag = arow == acol
    levels = []
    hs = ch // 2
    while hs >= 1:
        upper = (row % (2 * hs)) >= hs
        levels.append((hs, upper, jnp.where(upper, 1.0, -1.0),
                       (arow // (2 * hs) == acol // (2 * hs))
                       & ((arow % (2 * hs)) >= hs) & ((acol % (2 * hs)) < hs)))
        hs //= 2

    def midpoint_row(b, hs):
        if hs >= 8:
            return jnp.concatenate(
                [jnp.broadcast_to(b[blk * 2 * hs + hs - 1:blk * 2 * hs + hs, :], (2 * hs, w))
                 for blk in range(ch // (2 * hs))], axis=0)
        if hs == 4:
            return _group_row(b, 3)
        return jnp.where((row % 8) < 4, _group_row(b, 1), _group_row(b, 5))

    chunk_vals = {}

    def chunk_prep(c):
        if c not in chunk_vals:
            rs = slice(c * ch, (c + 1) * ch)
            f_all = lb + (1.0 - lb) * jax.nn.sigmoid(f_ref[rs, :].astype(F32))
            g2 = jnp.log2(f_all)
            g2_hi = g2.astype(BF16)
            g2_lo = (g2 - g2_hi.astype(F32)).astype(BF16)
            b_all = (jnp.dot(tril, g2_hi, preferred_element_type=F32)
                     + jnp.dot(tril, g2_lo, preferred_element_type=F32))
            chunk_vals[c] = (f_all, b_all)
        return chunk_vals[c]

    def operands(c, h):
        rs, cs = slice(c * ch, (c + 1) * ch), slice(h * w, (h + 1) * w)
        f_all, b_all = chunk_prep(c)
        b, f = b_all[:, cs], f_all[:, cs]
        kk = 1.0 - f
        q = q_ref[rs, cs].astype(F32)
        blast = b[ch - 1:ch, :]
        ys = []
        for hs, upper, sign, _ in levels:
            if hs == 1:
                y = jnp.where(upper, q * f, kk)
            else:
                y = jnp.where(upper, q, kk) * jnp.exp2((b - midpoint_row(b, hs)) * sign)
            ys.append(y.astype(BF16))
        return dict(rs=rs, cs=cs, h=h, ys=ys,
                    qin=(q * jnp.exp2(b)).astype(BF16),
                    kout=(kk * jnp.exp2(blast - b)).astype(BF16),
                    keep=jnp.exp2(blast),
                    dsum=jnp.sum(q * kk, axis=-1, keepdims=True))

    def level_matmuls(x):
        st = st_ref[x["h"]]
        x["st"] = st
        x["o"] = lax.dot_general(x["qin"], st.astype(BF16), nt, preferred_element_type=F32)
        x["aa"] = [lax.dot_general(y, y, nt, preferred_element_type=F32) for y in x["ys"]]

    def finish(x):
        rs, cs = x["rs"], x["cs"]
        v = i_ref[rs, cs]
        att = jnp.where(on_diag, x["dsum"], 0.0).astype(BF16)
        for (_, _, _, amask), a in zip(levels, x["aa"]):
            att = jnp.where(amask, a.astype(BF16), att)
        o = x["o"] + jnp.dot(att, v, preferred_element_type=F32)
        st_ref[x["h"]] = x["keep"] * x["st"] + lax.dot_general(
            v, x["kout"], (((0,), (0,)), ((), ())), preferred_element_type=F32)
        y = _rms(o) * gn_ref[:, cs] * jax.nn.sigmoid(g_ref[rs, cs].astype(F32))
        o_ref[rs, cs] = y.astype(o_ref.dtype)

    items = [(c, h) for c in range(q_ref.shape[0] // ch) for h in range(REC_HEADS)]
    staged = []
    d1, d2 = 1, 2
    for i in range(len(items) + d2):
        if i < len(items):
            staged.append(operands(*items[i]))
        if d1 <= i < len(items) + d1:
            level_matmuls(staged[i - d1])
        if i >= d2:
            finish(staged[i - d2])


def _hgrn2(p, lb_logits, gn, nb, s, layer, rb=512, ch=128):
    hw = REC_HEADS * HEAD_W
    nr = s // rb
    c0 = 3 * ATT_HEADS * HEAD_W // hw
    kern = functools.partial(_hgrn2_kernel, ch=ch, layer=layer)
    spec = lambda off: pl.BlockSpec((rb, hw), lambda b, i: (b * nr + i, c0 + off))
    return pl.pallas_call(
        kern,
        grid=(nb, nr),
        in_specs=[pl.BlockSpec(lb_logits.shape, lambda b, i: (0, 0)),
                  pl.BlockSpec((1, hw), lambda b, i: (0, 0)),
                  spec(0), spec(1), spec(2), spec(3)],
        out_specs=pl.BlockSpec((rb, hw), lambda b, i: (b * nr + i, 0)),
        out_shape=jax.ShapeDtypeStruct((nb * s, hw), BF16),
        scratch_shapes=[pltpu.VMEM((REC_HEADS, HEAD_W, HEAD_W), F32)],
        compiler_params=_params(("parallel", "arbitrary")),
        name="hgrn2",
    )(lb_logits, gn, p, p, p, p)


def _merge_oproj_kernel(ya_ref, yr_ref, *refs, ng):
    ga_refs, gr_refs = refs[:ng], refs[ng:2 * ng]
    x_ref, gt_ref, wa_ref, wr_ref, wo_ref, o_ref = refs[2 * ng:]
    gw = wa_ref.shape[1] // ng
    parts = []
    for g in range(ng):
        cols = slice(g * gw, (g + 1) * gw)
        ta = jnp.dot(ya_ref[...], wa_ref[:, cols], preferred_element_type=F32)
        tr = jnp.dot(yr_ref[...], wr_ref[:, cols], preferred_element_type=F32)
        parts.append((jax.nn.sigmoid(ga_refs[g][...].astype(F32)) * ta
                      + jax.nn.sigmoid(gr_refs[g][...].astype(F32)) * tr).astype(BF16))
    m = jnp.concatenate(parts, axis=1)
    o_ref[...] = x_ref[...] + gt_ref[...] * jnp.dot(m, wo_ref[...], preferred_element_type=F32)


def _merge_oproj(ya, yr, p, x2, mod4, wa, wr, wo, ga_col0, rows_per_batch, tm=512, gw=1024):
    t, ka = ya.shape
    d = wa.shape[1]
    ng = d // gw
    ga0 = ga_col0 // gw
    gr0 = (ga_col0 + d) // gw
    tpb = rows_per_batch // tm
    gate = lambda b0, g: pl.BlockSpec((tm, gw), lambda i: (i, b0 + g))
    resident = lambda w: pl.BlockSpec(w.shape, lambda i: (0, 0), pipeline_mode=pl.Buffered(1))
    return pl.pallas_call(
        functools.partial(_merge_oproj_kernel, ng=ng),
        grid=(t // tm,),
        in_specs=[pl.BlockSpec((tm, ka), lambda i: (i, 0)),
                  pl.BlockSpec((tm, ka), lambda i: (i, 0))]
                 + [gate(ga0, g) for g in range(ng)] + [gate(gr0, g) for g in range(ng)]
                 + [pl.BlockSpec((tm, d), lambda i: (i, 0)),
                    pl.BlockSpec((None, None, 1, d), lambda i: (i // tpb, 2, 0, 0)),
                    resident(wa), resident(wr), resident(wo)],
        out_specs=pl.BlockSpec((tm, d), lambda i: (i, 0)),
        out_shape=jax.ShapeDtypeStruct((t, d), F32),
        compiler_params=_params(("parallel",)),
        name="merge_oproj",
    )(ya, yr, *([p] * (2 * ng)), x2, mod4, wa, wr, wo)


def _ffn_up(h1, g, mod4, wg, wu, rows_per_batch, cast_ws):
    swiglu = lambda gate, up: gate * jax.nn.sigmoid(gate) * up
    return _adaln_matmul(h1, g, mod4, 3, [wg, wu], rows_per_batch, swiglu, "ffn_up",
                         tm=2048, tn=512, rc=256, rows=256, cast_ws=cast_ws)


def _ffn_down_kernel(a_ref, h_ref, gt_ref, fg_ref, w_ref, o_ref, *, tn, final):
    j = pl.program_id(1)
    y = jnp.dot(a_ref[...], w_ref[...], preferred_element_type=F32)
    o_ref[:, pl.ds(pl.multiple_of(j * tn, tn), tn)] = h_ref[...] + gt_ref[...] * y

    if final:
        @pl.when(j == pl.num_programs(1) - 1)
        def _():
            o_ref[...] = _rms(o_ref[...]) * fg_ref[...]


def _ffn_down(a, h1, mod4, fg, w, rows_per_batch, final, tm=1024, tn=512):
    t, d = h1.shape
    f = a.shape[1]
    tpb = rows_per_batch // tm
    return pl.pallas_call(
        functools.partial(_ffn_down_kernel, tn=tn, final=final),
        grid=(t // tm, d // tn),
        in_specs=[pl.BlockSpec((tm, f), lambda i, j: (i, 0)),
                  pl.BlockSpec((tm, tn), lambda i, j: (i, j)),
                  pl.BlockSpec((None, None, 1, tn), lambda i, j: (i // tpb, 5, 0, j)),
                  pl.BlockSpec((1, d), lambda i, j: (0, 0)),
                  pl.BlockSpec((f, tn), lambda i, j: (0, j))],
        out_specs=pl.BlockSpec((tm, d), lambda i, j: (i, 0)),
        out_shape=jax.ShapeDtypeStruct((t, d), F32),
        compiler_params=_params(("parallel", "arbitrary")),
        name="ffn_down",
    )(a, h1, mod4, fg, w)


def kernel(x, c, w_mod, b_mod, norm1_g, w_in, lambda_q1, lambda_k1, lambda_q2, lambda_k2,
           subln_g, lb_logits, gnorm_g, w_att_out, w_rec_out, w_o, norm2_g,
           w_ffn_gate, w_ffn_up, w_ffn_down, final_g):
    nb, s, d = x.shape
    depth = w_mod.shape[0]
    h = x.reshape(nb * s, d)
    for l in range(depth):
        lam_init = 0.8 - 0.6 * math.exp(-0.3 * l)
        ct, b_row = c.T, b_mod[l].reshape(1, 6 * d)
        mod_head = _mod(ct, w_mod[l], b_row, 2 * d)
        p, (wa, wr, wo) = _in_proj(h, norm1_g[l].reshape(1, d), mod_head.reshape(nb, 2, 1, d),
                                   w_in[l], s, cast_ws=(w_att_out[l], w_rec_out[l], w_o[l]))
        lam4 = jnp.stack([lambda_q1[l], lambda_k1[l], lambda_q2[l], lambda_k2[l]])
        ya, mod_tail = _attention(p, lam4, subln_g[l].reshape(HEAD_W, 1), nb, s, lam_init,
                                  ct, w_mod[l], b_row, 2 * d)
        mod4 = jnp.concatenate([mod_head, mod_tail], axis=1).reshape(nb, 6, 1, d)
        yr = _hgrn2(p, lb_logits, gnorm_g[l].reshape(1, REC_HEADS * HEAD_W), nb, s, l)
        ga_col0 = 3 * ATT_HEADS * HEAD_W + 4 * REC_HEADS * HEAD_W
        h1 = _merge_oproj(ya, yr, p, h, mod4, wa, wr, wo, ga_col0, s)
        a, (wd,) = _ffn_up(h1, norm2_g[l].reshape(1, d), mod4, w_ffn_gate[l], w_ffn_up[l], s,
                           cast_ws=(w_ffn_down[l],))
        h = _ffn_down(a, h1, mod4, final_g.reshape(1, d), wd, s, l == depth - 1)
    return h.reshape(nb, s, d)
```

```python
import functools
import math

import jax
import jax.numpy as jnp
from jax import lax
from jax.experimental import pallas as pl
from jax.experimental.pallas import tpu as pltpu

F32 = jnp.float32
BF16 = jnp.bfloat16

NORM_EPS = 1e-6
ATT_HEADS = 8
ATT_HALF_DIM = 64
HEAD_W = 128
REC_HEADS = 8
MASK_CHUNK = 64
NEG_BIG = -1e30
BF16_SUBLANES = 16
ONES_ROWS = BF16_SUBLANES
DIAG_AHEAD = 4

VMEM_LIMIT = 60 * 1024 * 1024


def _params(sem, vmem=VMEM_LIMIT):
    return pltpu.CompilerParams(dimension_semantics=sem, vmem_limit_bytes=vmem)


def _rms(x):
    return x * lax.rsqrt(jnp.mean(x * x, axis=-1, keepdims=True) + NORM_EPS)


def _adaln(x, g, sc, sh):
    return _rms(x) * (g * (1.0 + sc)) + sh


def _mod_kernel(ct_ref, w_ref, b_ref, o_ref):
    ct = ct_ref[...]
    cond = ct * jax.nn.sigmoid(ct)
    w = w_ref[...]
    for b in range(ct.shape[1]):
        o_ref[b:b + 1, :] = jnp.sum(w * cond[:, b:b + 1], axis=0, keepdims=True) + b_ref[...]


def _mod(ct, w_mod, b_row, n, tn=1024):
    d, nb = ct.shape
    return pl.pallas_call(
        _mod_kernel,
        grid=(n // tn,),
        in_specs=[pl.BlockSpec((d, nb), lambda j: (0, 0)),
                  pl.BlockSpec((d, tn), lambda j: (0, j)),
                  pl.BlockSpec((1, tn), lambda j: (0, j))],
        out_specs=pl.BlockSpec((nb, tn), lambda j: (0, j)),
        out_shape=jax.ShapeDtypeStruct((nb, n), F32),
        compiler_params=_params(("arbitrary",)),
        name="mod",
    )(ct, w_mod, b_row)


def _adaln_matmul_kernel(x_ref, g_ref, sh_ref, sc_ref, *refs, nw, ncast, rc, rows, finish):
    w_refs, cast_in = refs[:nw], refs[nw:nw + ncast]
    o_ref, cast_out = refs[nw + ncast], refs[nw + ncast + 1:nw + 2 * ncast + 1]
    u_bufs = refs[nw + 2 * ncast + 1:]
    i, j = pl.program_id(0), pl.program_id(1)
    nchunks = u_bufs[0].shape[0] // rc

    def normalise_into(u_ref):
        normed = _adaln(x_ref[...], g_ref[...], sc_ref[...], sh_ref[...]).astype(BF16)
        r0 = pl.multiple_of(jnp.minimum(j, nchunks - 1) * rc, rc)
        u_ref[pl.ds(r0, rc), :] = normed
        return normed

    def multiply_from(u_ref, anchor):
        wb = [w[...].astype(BF16) for w in w_refs]
        nsub = u_ref.shape[0] // rows
        for k in range(nsub):
            u = u_ref[k * rows:(k + 1) * rows, :]
            if k == nsub - 1:
                head = jnp.where(j >= 0, u[:rc, :], anchor)
                u = head if rows == rc else jnp.concatenate([head, u[rc:, :]], axis=0)
            o_ref[k * rows:(k + 1) * rows, :] = finish(
                *[jnp.dot(u, w, preferred_element_type=F32) for w in wb]).astype(o_ref.dtype)

    @pl.when(i == 0)
    def _():
        normalise_into(u_bufs[0])

    for parity in range(2):
        @pl.when((i > 0) & (i % 2 == parity))
        def _(parity=parity):
            multiply_from(u_bufs[1 - parity], normalise_into(u_bufs[parity]))
            for w_ref, wb_ref in zip(cast_in, cast_out):
                wb_ref[...] = w_ref[...].astype(BF16)


def _adaln_matmul(x2, g, mod4, mod_row, ws, rows_per_batch, finish, name, tm, tn, rc=256,
                  rows=None, cast_ws=()):
    t, d = x2.shape
    n = ws[0].shape[1]
    nt, nj, nchunks, tpb = t // tm, n // tn, tm // rc, rows_per_batch // tm
    assert nchunks <= nj
    tile = lambda i: jnp.minimum(i, nt - 1)
    col = lambda i, j: jnp.where(i > 0, j, 0)

    def slab(w):
        k = w.shape[0]
        ns = max(c for c in range(1, nt * nj + 1) if k % (c * BF16_SUBLANES) == 0)
        return pl.BlockSpec((k // ns, w.shape[1]),
                            lambda i, j: (jnp.clip((i - 1) * nj + j, 0, ns - 1), 0))

    kern = functools.partial(_adaln_matmul_kernel, nw=len(ws), ncast=len(cast_ws), rc=rc,
                             rows=rows or tm, finish=finish)
    outs = pl.pallas_call(
        kern,
        grid=(nt + 1, nj),
        in_specs=[pl.BlockSpec((rc, d), lambda i, j: (tile(i) * nchunks
                                                      + jnp.minimum(j, nchunks - 1), 0)),
                  pl.BlockSpec((1, d), lambda i, j: (0, 0)),
                  pl.BlockSpec((None, None, 1, d), lambda i, j: (tile(i) // tpb, mod_row, 0, 0)),
                  pl.BlockSpec((None, None, 1, d),
                               lambda i, j: (tile(i) // tpb, mod_row + 1, 0, 0))]
                 + [pl.BlockSpec((d, tn), lambda i, j: (0, col(i, j))) for _ in ws]
                 + [slab(w) for w in cast_ws],
        out_specs=[pl.BlockSpec((tm, tn), lambda i, j: (jnp.maximum(i - 1, 0), col(i, j)))]
                  + [slab(w) for w in cast_ws],
        out_shape=[jax.ShapeDtypeStruct((t, n), BF16)]
                  + [jax.ShapeDtypeStruct(w.shape, BF16) for w in cast_ws],
        scratch_shapes=[pltpu.VMEM((tm, d), BF16), pltpu.VMEM((tm, d), BF16)],
        compiler_params=_params(("arbitrary", "arbitrary")),
        name=name,
    )(x2, g, mod4, mod4, *ws, *cast_ws)
    return outs[0], outs[1:]


def _in_proj(x2, g, mod4, w, rows_per_batch, cast_ws):
    return _adaln_matmul(x2, g, mod4, 0, [w], rows_per_batch, lambda y: y, "in_proj",
                         tm=2048, tn=1024, rows=512, cast_ws=cast_ws)


def _col_reduce(x, op, groups=8):
    k, n = x.shape
    if k % (groups * 8) == 0:
        x = op(x.reshape(groups, k // groups, n), axis=0)
    return op(x, axis=0, keepdims=True)


def _attn_kernel(lam_ref, sg_ref, q_ref, k_ref, v_ref, ct_ref, mw_ref, mb_ref, o_ref, mvo_ref,
                 vt_ref, acc_ref, m_ref, sa_ref, sb_ref, xa_ref, xb_ref, *, tq, sq, tk,
                 lam_init):
    nkb = vt_ref.shape[0]
    nc = tq // sq

    _mod_kernel(ct_ref, mw_ref, mb_ref, mvo_ref)

    for j in range(nkb):
        vt_ref[j, 0:HEAD_W, :] = v_ref[j * tk:(j + 1) * tk, :].astype(F32).T.astype(BF16)
        vt_ref[j, HEAD_W:, :] = jnp.ones((vt_ref.shape[1] - HEAD_W, tk), BF16)

    lax.fori_loop(0, q_ref.shape[0] // tq,
                  functools.partial(_attn_q_tile, lam_ref, sg_ref, q_ref, k_ref, o_ref, vt_ref,
                                    acc_ref, m_ref, sa_ref, sb_ref, xa_ref, xb_ref,
                                    tq, sq, tk, nc, lam_init), 0)


def _attn_q_tile(lam_ref, sg_ref, q_ref, k_ref, o_ref, vt_ref, acc_ref, m_ref, sa_ref, sb_ref,
                 xa_ref, xb_ref, tq, sq, tk, nc, lam_init, qi, carry):
    q_rows = pl.ds(pl.multiple_of(qi * tq, tq), tq)
    q = (q_ref[q_rows, :].astype(F32) * (ATT_HALF_DIM ** -0.5 * math.log2(math.e))).astype(BF16)
    lane = lax.broadcasted_iota(jnp.int32, q.shape, 1)
    zero = jnp.zeros_like(q)
    qm = (jnp.where(lane < ATT_HALF_DIM, q, zero), jnp.where(lane >= ATT_HALF_DIM, q, zero))

    m_ref[...] = jnp.full(m_ref.shape, NEG_BIG, F32)
    acc_ref[...] = jnp.zeros(acc_ref.shape, F32)

    def scores(c, mi, k):
        return lax.dot_general(k, qm[mi][c * sq:(c + 1) * sq, :], (((1,), (1,)), ((), ())),
                               preferred_element_type=F32)

    def update(c, mi, vt, s, smax):
        cs = slice(c * sq, (c + 1) * sq)
        m_old = m_ref[mi, :, cs]
        m_new = jnp.maximum(m_old, smax)
        p = jnp.exp2(s - m_new)
        alpha = jnp.exp2(m_old - m_new)
        acc_ref[mi, :, cs] = alpha * acc_ref[mi, :, cs] + jnp.dot(
            vt, p.astype(BF16), preferred_element_type=F32)
        m_ref[mi, :, cs] = m_new

    assert tq == 2 * tk
    chains = [(c, mi) for c in range(nc) for mi in range(2)]

    bufs = ((sa_ref, xa_ref), (sb_ref, xb_ref))

    def produce(idx, k, dst):
        c, mi = chains[idx]
        s = scores(c, mi, k)
        dst[0][idx] = s
        dst[1][idx] = _col_reduce(s, jnp.max)

    def stage(j, src, dst):
        vt = vt_ref[j]
        if dst is not None:
            kn = k_ref[pl.ds(pl.multiple_of((j + 1) * tk, tk), tk), :]
        for idx, (c, mi) in enumerate(chains):
            if dst is not None:
                produce(idx, kn, dst)
            update(c, mi, vt, src[0][idx], src[1][idx])

    @pl.when(qi > 0)
    def _():
        k0 = k_ref[0:tk, :]
        for idx in range(len(chains)):
            produce(idx, k0, bufs[0])

        def pair(pp, carry):
            stage(2 * pp, bufs[0], bufs[1])
            stage(2 * pp + 1, bufs[1], bufs[0])
            return carry

        lax.fori_loop(0, qi - 1, pair, 0)
        stage(2 * qi - 2, bufs[0], bufs[1])
        stage(2 * qi - 1, bufs[1], None)

    krow = lax.broadcasted_iota(jnp.int32, (sq, sq), 0)
    qcol = lax.broadcasted_iota(jnp.int32, (sq, sq), 1)
    dmask = krow // MASK_CHUNK <= qcol // MASK_CHUNK
    vt_tile = jnp.concatenate([vt_ref[qi * 2], vt_ref[qi * 2 + 1]], axis=1)

    def diag_slot(n, piece):
        slot = 2 * n + piece
        return bufs[slot // len(chains)], slot % len(chains)

    def diag_produce(n, c, mi):
        nk = (c + 1) * sq
        s = scores(c, mi, k_ref[pl.ds(pl.multiple_of(qi * tq, tq), nk), :])
        tail = jnp.where(dmask, s[c * sq:, :], NEG_BIG)
        s = tail if c == 0 else jnp.concatenate([s[:c * sq, :], tail], axis=0)
        for piece, r0 in enumerate(range(0, nk, tk)):
            buf, slot = diag_slot(n, piece)
            rows = min(tk, nk - r0)
            buf[0][slot, 0:rows, :] = s[r0:r0 + rows, :]
        buf, slot = diag_slot(n, 0)
        buf[1][slot] = _col_reduce(s, jnp.max)

    def diag_consume(n, c, mi):
        nk = (c + 1) * sq
        parts = []
        for piece, r0 in enumerate(range(0, nk, tk)):
            buf, slot = diag_slot(n, piece)
            parts.append(buf[0][slot, 0:min(tk, nk - r0), :])
        buf, slot = diag_slot(n, 0)
        update(c, mi, vt_tile[:, :nk], jnp.concatenate(parts, axis=0), buf[1][slot])

    order = [(c, mi) for c in reversed(range(nc)) for mi in range(2)]
    for n in range(len(order) + DIAG_AHEAD):
        if n < len(order):
            diag_produce(n, *order[n])
        if n >= DIAG_AHEAD:
            diag_consume(n - DIAG_AHEAD, *order[n - DIAG_AHEAD])

    lp = lam_ref[...]
    lam = (jnp.exp(jnp.sum(lp[0:1] * lp[1:2], axis=-1, keepdims=True))
           - jnp.exp(jnp.sum(lp[2:3] * lp[3:4], axis=-1, keepdims=True)) + lam_init)
    o = (acc_ref[0, 0:HEAD_W, :] / acc_ref[0, HEAD_W:HEAD_W + 1, :]
         - lam * (acc_ref[1, 0:HEAD_W, :] / acc_ref[1, HEAD_W:HEAD_W + 1, :]))
    ms = jnp.mean(o * o, axis=0, keepdims=True)
    y = o * lax.rsqrt(ms + NORM_EPS) * sg_ref[...] * (1.0 - lam_init)
    o_ref[q_rows, :] = y.T.astype(o_ref.dtype)
    return carry


def _attention(p, lam4, sg_col, nb, s, lam_init, ct, w_mod, b_row, mod_col0, tq=1024, sq=256,
               tk=512):
    h = ATT_HEADS
    kern = functools.partial(_attn_kernel, tq=tq, sq=sq, tk=tk, lam_init=lam_init)
    nchains = 2 * tq // sq
    d = ct.shape[0]
    sw = (w_mod.shape[1] - mod_col0) // (nb * h)
    mslab = lambda rows, off: pl.BlockSpec((rows, sw), lambda b, hh: (0, off + b * h + hh))
    return pl.pallas_call(
        kern,
        grid=(nb, h),
        in_specs=[pl.BlockSpec((4, ATT_HALF_DIM), lambda b, hh: (0, 0)),
                  pl.BlockSpec((HEAD_W, 1), lambda b, hh: (0, 0)),
                  pl.BlockSpec((s, HEAD_W), lambda b, hh: (b, hh)),
                  pl.BlockSpec((s, HEAD_W), lambda b, hh: (b, h + hh)),
                  pl.BlockSpec((s, HEAD_W), lambda b, hh: (b, 2 * h + hh)),
                  pl.BlockSpec(ct.shape, lambda b, hh: (0, 0)),
                  mslab(d, mod_col0 // sw), mslab(1, mod_col0 // sw)],
        out_specs=[pl.BlockSpec((s, HEAD_W), lambda b, hh: (b, hh)), mslab(nb, 0)],
        out_shape=[jax.ShapeDtypeStruct((nb * s, h * HEAD_W), BF16),
                   jax.ShapeDtypeStruct((nb, w_mod.shape[1] - mod_col0), F32)],
        scratch_shapes=[pltpu.VMEM((s // tk, HEAD_W + ONES_ROWS, tk), BF16),
                        pltpu.VMEM((2, HEAD_W + ONES_ROWS, tq), F32),
                        pltpu.VMEM((2, 1, tq), F32),
                        pltpu.VMEM((nchains, tk, sq), F32),
                        pltpu.VMEM((nchains, tk, sq), F32),
                        pltpu.VMEM((nchains, 1, sq), F32),
                        pltpu.VMEM((nchains, 1, sq), F32)],
        compiler_params=_params(("parallel", "parallel")),
        name="diff_attn",
    )(lam4, sg_col, p, p, p, ct, w_mod, b_row)


def _group_row(a, s, grp=8):
    n, w = a.shape
    a3 = a.reshape(n // grp, grp, w)
    return jnp.broadcast_to(a3[:, s:s + 1, :], (n // grp, grp, w)).reshape(n, w)


def _hgrn2_kernel(lbl_ref, gn_ref, q_ref, f_ref, i_ref, g_ref, o_ref, st_ref, *, ch, layer):
    @pl.when(pl.program_id(1) == 0)
    def _():
        st_ref[...] = jnp.zeros(st_ref.shape, F32)

    w = HEAD_W
    lbl = lbl_ref[...]
    mx = jnp.max(lbl, axis=0, keepdims=True)
    e = jnp.exp(lbl - mx)
    lb = (jnp.sum(e[0:layer + 1], axis=0, keepdims=True)
          / jnp.sum(e, axis=0, keepdims=True))

    nt = (((1,), (1,)), ((), ()))
    row = lax.broadcasted_iota(jnp.int32, (ch, w), 0)
    arow = lax.broadcasted_iota(jnp.int32, (ch, ch), 0)
    acol = lax.broadcasted_iota(jnp.int32, (ch, ch), 1)
    tril = (arow >= acol).astype(BF16)
    on_diag = arow == acol
    levels = []
    hs = ch // 2
    while hs >= 1:
        upper = (row % (2 * hs)) >= hs
        levels.append((hs, upper, jnp.where(upper, 1.0, -1.0),
                       (arow // (2 * hs) == acol // (2 * hs))
                       & ((arow % (2 * hs)) >= hs) & ((acol % (2 * hs)) < hs)))
        hs //= 2

    def midpoint_row(b, hs):
        if hs >= 8:
            return jnp.concatenate(
                [jnp.broadcast_to(b[blk * 2 * hs + hs - 1:blk * 2 * hs + hs, :], (2 * hs, w))
                 for blk in range(ch // (2 * hs))], axis=0)
        if hs == 4:
            return _group_row(b, 3)
        return jnp.where((row % 8) < 4, _group_row(b, 1), _group_row(b, 5))

    chunk_vals = {}

    def chunk_prep(c):
        if c not in chunk_vals:
            rs = slice(c * ch, (c + 1) * ch)
            f_all = lb + (1.0 - lb) * jax.nn.sigmoid(f_ref[rs, :].astype(F32))
            g2 = jnp.log2(f_all)
            g2_hi = g2.astype(BF16)
            g2_lo = (g2 - g2_hi.astype(F32)).astype(BF16)
            b_all = (jnp.dot(tril, g2_hi, preferred_element_type=F32)
                     + jnp.dot(tril, g2_lo, preferred_element_type=F32))
            chunk_vals[c] = (f_all, b_all)
        return chunk_vals[c]

    def operands(c, h):
        rs, cs = slice(c * ch, (c + 1) * ch), slice(h * w, (h + 1) * w)
        f_all, b_all = chunk_prep(c)
        b, f = b_all[:, cs], f_all[:, cs]
        kk = 1.0 - f
        q = q_ref[rs, cs].astype(F32)
        blast = b[ch - 1:ch, :]
        ys = []
        for hs, upper, sign, _ in levels:
            if hs == 1:
                y = jnp.where(upper, q * f, kk)
            else:
                y = jnp.where(upper, q, kk) * jnp.exp2((b - midpoint_row(b, hs)) * sign)
            ys.append(y.astype(BF16))
        return dict(rs=rs, cs=cs, h=h, ys=ys,
                    qin=(q * jnp.exp2(b)).astype(BF16),
                    kout=(kk * jnp.exp2(blast - b)).astype(BF16),
                    keep=jnp.exp2(blast),
                    dsum=jnp.sum(q * kk, axis=-1, keepdims=True))

    def level_matmuls(x):
        st = st_ref[x["h"]]
        x["st"] = st
        x["o"] = lax.dot_general(x["qin"], st.astype(BF16), nt, preferred_element_type=F32)
        x["aa"] = [lax.dot_general(y, y, nt, preferred_element_type=F32) for y in x["ys"]]

    def finish(x):
        rs, cs = x["rs"], x["cs"]
        v = i_ref[rs, cs]
        att = jnp.where(on_diag, x["dsum"], 0.0).astype(BF16)
        for (_, _, _, amask), a in zip(levels, x["aa"]):
            att = jnp.where(amask, a.astype(BF16), att)
        o = x["o"] + jnp.dot(att, v, preferred_element_type=F32)
        st_ref[x["h"]] = x["keep"] * x["st"] + lax.dot_general(
            v, x["kout"], (((0,), (0,)), ((), ())), preferred_element_type=F32)
        y = _rms(o) * gn_ref[:, cs] * jax.nn.sigmoid(g_ref[rs, cs].astype(F32))
        o_ref[rs, cs] = y.astype(o_ref.dtype)

    items = [(c, h) for c in range(q_ref.shape[0] // ch) for h in range(REC_HEADS)]
    staged = []
    d1, d2 = 1, 2
    for i in range(len(items) + d2):
        if i < len(items):
            staged.append(operands(*items[i]))
        if d1 <= i < len(items) + d1:
            level_matmuls(staged[i - d1])
        if i >= d2:
            finish(staged[i - d2])


def _hgrn2(p, lb_logits, gn, nb, s, layer, rb=512, ch=128):
    hw = REC_HEADS * HEAD_W
    nr = s // rb
    c0 = 3 * ATT_HEADS * HEAD_W // hw
    kern = functools.partial(_hgrn2_kernel, ch=ch, layer=layer)
    spec = lambda off: pl.BlockSpec((rb, hw), lambda b, i: (b * nr + i, c0 + off))
    return pl.pallas_call(
        kern,
        grid=(nb, nr),
        in_specs=[pl.BlockSpec(lb_logits.shape, lambda b, i: (0, 0)),
                  pl.BlockSpec((1, hw), lambda b, i: (0, 0)),
                  spec(0), spec(1), spec(2), spec(3)],
        out_specs=pl.BlockSpec((rb, hw), lambda b, i: (b * nr + i, 0)),
        out_shape=jax.ShapeDtypeStruct((nb * s, hw), BF16),
        scratch_shapes=[pltpu.VMEM((REC_HEADS, HEAD_W, HEAD_W), F32)],
        compiler_params=_params(("parallel", "arbitrary")),
        name="hgrn2",
    )(lb_logits, gn, p, p, p, p)


def _merge_oproj_kernel(ya_ref, yr_ref, *refs, ng):
    ga_refs, gr_refs = refs[:ng], refs[ng:2 * ng]
    x_ref, gt_ref, wa_ref, wr_ref, wo_ref, o_ref = refs[2 * ng:]
    gw = wa_ref.shape[1] // ng
    parts = []
    for g in range(ng):
        cols = slice(g * gw, (g + 1) * gw)
        ta = jnp.dot(ya_ref[...], wa_ref[:, cols], preferred_element_type=F32)
        tr = jnp.dot(yr_ref[...], wr_ref[:, cols], preferred_element_type=F32)
        parts.append((jax.nn.sigmoid(ga_refs[g][...].astype(F32)) * ta
                      + jax.nn.sigmoid(gr_refs[g][...].astype(F32)) * tr).astype(BF16))
    m = jnp.concatenate(parts, axis=1)
    o_ref[...] = x_ref[...] + gt_ref[...] * jnp.dot(m, wo_ref[...], preferred_element_type=F32)


def _merge_oproj(ya, yr, p, x2, mod4, wa, wr, wo, ga_col0, rows_per_batch, tm=512, gw=1024):
    t, ka = ya.shape
    d = wa.shape[1]
    ng = d // gw
    ga0 = ga_col0 // gw
    gr0 = (ga_col0 + d) // gw
    tpb = rows_per_batch // tm
    gate = lambda b0, g: pl.BlockSpec((tm, gw), lambda i: (i, b0 + g))
    resident = lambda w: pl.BlockSpec(w.shape, lambda i: (0, 0), pipeline_mode=pl.Buffered(1))
    return pl.pallas_call(
        functools.partial(_merge_oproj_kernel, ng=ng),
        grid=(t // tm,),
        in_specs=[pl.BlockSpec((tm, ka), lambda i: (i, 0)),
                  pl.BlockSpec((tm, ka), lambda i: (i, 0))]
                 + [gate(ga0, g) for g in range(ng)] + [gate(gr0, g) for g in range(ng)]
                 + [pl.BlockSpec((tm, d), lambda i: (i, 0)),
                    pl.BlockSpec((None, None, 1, d), lambda i: (i // tpb, 2, 0, 0)),
                    resident(wa), resident(wr), resident(wo)],
        out_specs=pl.BlockSpec((tm, d), lambda i: (i, 0)),
        out_shape=jax.ShapeDtypeStruct((t, d), F32),
        compiler_params=_params(("parallel",)),
        name="merge_oproj",
    )(ya, yr, *([p] * (2 * ng)), x2, mod4, wa, wr, wo)


def _ffn_up(h1, g, mod4, wg, wu, rows_per_batch, cast_ws):
    swiglu = lambda gate, up: gate * jax.nn.sigmoid(gate) * up
    return _adaln_matmul(h1, g, mod4, 3, [wg, wu], rows_per_batch, swiglu, "ffn_up",
                         tm=2048, tn=512, rc=256, rows=256, cast_ws=cast_ws)


def _ffn_down_kernel(a_ref, h_ref, gt_ref, fg_ref, w_ref, o_ref, *, tn, final):
    j = pl.program_id(1)
    y = jnp.dot(a_ref[...], w_ref[...], preferred_element_type=F32)
    o_ref[:, pl.ds(pl.multiple_of(j * tn, tn), tn)] = h_ref[...] + gt_ref[...] * y

    if final:
        @pl.when(j == pl.num_programs(1) - 1)
        def _():
            o_ref[...] = _rms(o_ref[...]) * fg_ref[...]


def _ffn_down(a, h1, mod4, fg, w, rows_per_batch, final, tm=1024, tn=512):
    t, d = h1.shape
    f = a.shape[1]
    tpb = rows_per_batch // tm
    return pl.pallas_call(
        functools.partial(_ffn_down_kernel, tn=tn, final=final),
        grid=(t // tm, d // tn),
        in_specs=[pl.BlockSpec((tm, f), lambda i, j: (i, 0)),
                  pl.BlockSpec((tm, tn), lambda i, j: (i, j)),
                  pl.BlockSpec((None, None, 1, tn), lambda i, j: (i // tpb, 5, 0, j)),
                  pl.BlockSpec((1, d), lambda i, j: (0, 0)),
                  pl.BlockSpec((f, tn), lambda i, j: (0, j))],
        out_specs=pl.BlockSpec((tm, d), lambda i, j: (i, 0)),
        out_shape=jax.ShapeDtypeStruct((t, d), F32),
        compiler_params=_params(("parallel", "arbitrary")),
        name="ffn_down",
    )(a, h1, mod4, fg, w)


def kernel(x, c, w_mod, b_mod, norm1_g, w_in, lambda_q1, lambda_k1, lambda_q2, lambda_k2,
           subln_g, lb_logits, gnorm_g, w_att_out, w_rec_out, w_o, norm2_g,
           w_ffn_gate, w_ffn_up, w_ffn_down, final_g):
    nb, s, d = x.shape
    depth = w_mod.shape[0]
    h = x.reshape(nb * s, d)
    for l in range(depth):
        lam_init = 0.8 - 0.6 * math.exp(-0.3 * l)
        ct, b_row = c.T, b_mod[l].reshape(1, 6 * d)
        mod_head = _mod(ct, w_mod[l], b_row, 2 * d)
        p, (wa, wr, wo) = _in_proj(h, norm1_g[l].reshape(1, d), mod_head.reshape(nb, 2, 1, d),
                                   w_in[l], s, cast_ws=(w_att_out[l], w_rec_out[l], w_o[l]))
        lam4 = jnp.stack([lambda_q1[l], lambda_k1[l], lambda_q2[l], lambda_k2[l]])
        ya, mod_tail = _attention(p, lam4, subln_g[l].reshape(HEAD_W, 1), nb, s, lam_init,
                                  ct, w_mod[l], b_row, 2 * d)
        mod4 = jnp.concatenate([mod_head, mod_tail], axis=1).reshape(nb, 6, 1, d)
        yr = _hgrn2(p, lb_logits, gnorm_g[l].reshape(1, REC_HEADS * HEAD_W), nb, s, l)
        ga_col0 = 3 * ATT_HEADS * HEAD_W + 4 * REC_HEADS * HEAD_W
        h1 = _merge_oproj(ya, yr, p, h, mod4, wa, wr, wo, ga_col0, s)
        a, (wd,) = _ffn_up(h1, norm2_g[l].reshape(1, d), mod4, w_ffn_gate[l], w_ffn_up[l], s,
                           cast_ws=(w_ffn_down[l],))
        h = _ffn_down(a, h1, mod4, final_g.reshape(1, d), wd, s, l == depth - 1)
    return h.reshape(nb, s, d)
```

```python
import functools
import math

import jax
import jax.numpy as jnp
from jax import lax
from jax.experimental import pallas as pl
from jax.experimental.pallas import tpu as pltpu

F32 = jnp.float32
BF16 = jnp.bfloat16

NORM_EPS = 1e-6
ATT_HEADS = 8
ATT_HALF_DIM = 64
HEAD_W = 128
REC_HEADS = 8
MASK_CHUNK = 64
NEG_BIG = -1e30
BF16_SUBLANES = 16
ONES_ROWS = BF16_SUBLANES
DIAG_AHEAD = 4

VMEM_LIMIT = 60 * 1024 * 1024


def _params(sem, vmem=VMEM_LIMIT):
    return pltpu.CompilerParams(dimension_semantics=sem, vmem_limit_bytes=vmem)


def _rms(x):
    return x * lax.rsqrt(jnp.mean(x * x, axis=-1, keepdims=True) + NORM_EPS)


def _adaln(x, g, sc, sh):
    return _rms(x) * (g * (1.0 + sc)) + sh


def _mod_kernel(ct_ref, w_ref, b_ref, o_ref):
    ct = ct_ref[...]
    cond = ct * jax.nn.sigmoid(ct)
    w = w_ref[...]
    for b in range(ct.shape[1]):
        o_ref[b:b + 1, :] = jnp.sum(w * cond[:, b:b + 1], axis=0, keepdims=True) + b_ref[...]


def _mod(ct, w_mod, b_row, n, tn=1024):
    d, nb = ct.shape
    return pl.pallas_call(
        _mod_kernel,
        grid=(n // tn,),
        in_specs=[pl.BlockSpec((d, nb), lambda j: (0, 0)),
                  pl.BlockSpec((d, tn), lambda j: (0, j)),
                  pl.BlockSpec((1, tn), lambda j: (0, j))],
        out_specs=pl.BlockSpec((nb, tn), lambda j: (0, j)),
        out_shape=jax.ShapeDtypeStruct((nb, n), F32),
        compiler_params=_params(("arbitrary",)),
        name="mod",
    )(ct, w_mod, b_row)


def _adaln_matmul_kernel(x_ref, g_ref, sh_ref, sc_ref, *refs, nw, ncast, rc, rows, finish):
    w_refs, cast_in = refs[:nw], refs[nw:nw + ncast]
    o_ref, cast_out = refs[nw + ncast], refs[nw + ncast + 1:nw + 2 * ncast + 1]
    u_bufs = refs[nw + 2 * ncast + 1:]
    i, j = pl.program_id(0), pl.program_id(1)
    nchunks = u_bufs[0].shape[0] // rc

    def normalise_into(u_ref):
        normed = _adaln(x_ref[...], g_ref[...], sc_ref[...], sh_ref[...]).astype(BF16)
        r0 = pl.multiple_of(jnp.minimum(j, nchunks - 1) * rc, rc)
        u_ref[pl.ds(r0, rc), :] = normed
        return normed

    def multiply_from(u_ref, anchor):
        wb = [w[...].astype(BF16) for w in w_refs]
        nsub = u_ref.shape[0] // rows
        for k in range(nsub):
            u = u_ref[k * rows:(k + 1) * rows, :]
            if k == nsub - 1:
                head = jnp.where(j >= 0, u[:rc, :], anchor)
                u = head if rows == rc else jnp.concatenate([head, u[rc:, :]], axis=0)
            o_ref[k * rows:(k + 1) * rows, :] = finish(
                *[jnp.dot(u, w, preferred_element_type=F32) for w in wb]).astype(o_ref.dtype)

    @pl.when(i == 0)
    def _():
        normalise_into(u_bufs[0])

    for parity in range(2):
        @pl.when((i > 0) & (i % 2 == parity))
        def _(parity=parity):
            multiply_from(u_bufs[1 - parity], normalise_into(u_bufs[parity]))
            for w_ref, wb_ref in zip(cast_in, cast_out):
                wb_ref[...] = w_ref[...].astype(BF16)


def _adaln_matmul(x2, g, mod4, mod_row, ws, rows_per_batch, finish, name, tm, tn, rc=256,
                  rows=None, cast_ws=()):
    t, d = x2.shape
    n = ws[0].shape[1]
    nt, nj, nchunks, tpb = t // tm, n // tn, tm // rc, rows_per_batch // tm
    assert nchunks <= nj
    tile = lambda i: jnp.minimum(i, nt - 1)
    col = lambda i, j: jnp.where(i > 0, j, 0)

    def slab(w):
        k = w.shape[0]
        ns = max(c for c in range(1, nt * nj + 1) if k % (c * BF16_SUBLANES) == 0)
        return pl.BlockSpec((k // ns, w.shape[1]),
                            lambda i, j: (jnp.clip((i - 1) * nj + j, 0, ns - 1), 0))

    kern = functools.partial(_adaln_matmul_kernel, nw=len(ws), ncast=len(cast_ws), rc=rc,
                             rows=rows or tm, finish=finish)
    outs = pl.pallas_call(
        kern,
        grid=(nt + 1, nj),
        in_specs=[pl.BlockSpec((rc, d), lambda i, j: (tile(i) * nchunks
                                                      + jnp.minimum(j, nchunks - 1), 0)),
                  pl.BlockSpec((1, d), lambda i, j: (0, 0)),
                  pl.BlockSpec((None, None, 1, d), lambda i, j: (tile(i) // tpb, mod_row, 0, 0)),
                  pl.BlockSpec((None, None, 1, d),
                               lambda i, j: (tile(i) // tpb, mod_row + 1, 0, 0))]
                 + [pl.BlockSpec((d, tn), lambda i, j: (0, col(i, j))) for _ in ws]
                 + [slab(w) for w in cast_ws],
        out_specs=[pl.BlockSpec((tm, tn), lambda i, j: (jnp.maximum(i - 1, 0), col(i, j)))]
                  + [slab(w) for w in cast_ws],
        out_shape=[jax.ShapeDtypeStruct((t, n), BF16)]
                  + [jax.ShapeDtypeStruct(w.shape, BF16) for w in cast_ws],
        scratch_shapes=[pltpu.VMEM((tm, d), BF16), pltpu.VMEM((tm, d), BF16)],
        compiler_params=_params(("arbitrary", "arbitrary")),
        name=name,
    )(x2, g, mod4, mod4, *ws, *cast_ws)
    return outs[0], outs[1:]


def _in_proj(x2, g, mod4, w, rows_per_batch, cast_ws):
    return _adaln_matmul(x2, g, mod4, 0, [w], rows_per_batch, lambda y: y, "in_proj",
                         tm=2048, tn=1024, rows=512, cast_ws=cast_ws)


def _col_reduce(x, op, groups=8):
    k, n = x.shape
    if k % (groups * 8) == 0:
        x = op(x.reshape(groups, k // groups, n), axis=0)
    return op(x, axis=0, keepdims=True)


def _attn_kernel(lam_ref, sg_ref, q_ref, k_ref, v_ref, ct_ref, mw_ref, mb_ref, o_ref, mvo_ref,
                 vt_ref, acc_ref, m_ref, sa_ref, sb_ref, xa_ref, xb_ref, *, tq, sq, tk,
                 lam_init):
    nkb = vt_ref.shape[0]
    nc = tq // sq

    _mod_kernel(ct_ref, mw_ref, mb_ref, mvo_ref)

    for j in range(nkb):
        vt_ref[j, 0:HEAD_W, :] = v_ref[j * tk:(j + 1) * tk, :].astype(F32).T.astype(BF16)
        vt_ref[j, HEAD_W:, :] = jnp.ones((vt_ref.shape[1] - HEAD_W, tk), BF16)

    lax.fori_loop(0, q_ref.shape[0] // tq,
                  functools.partial(_attn_q_tile, lam_ref, sg_ref, q_ref, k_ref, o_ref, vt_ref,
                                    acc_ref, m_ref, sa_ref, sb_ref, xa_ref, xb_ref,
                                    tq, sq, tk, nc, lam_init), 0)


def _attn_q_tile(lam_ref, sg_ref, q_ref, k_ref, o_ref, vt_ref, acc_ref, m_ref, sa_ref, sb_ref,
                 xa_ref, xb_ref, tq, sq, tk, nc, lam_init, qi, carry):
    q_rows = pl.ds(pl.multiple_of(qi * tq, tq), tq)
    q = (q_ref[q_rows, :].astype(F32) * (ATT_HALF_DIM ** -0.5 * math.log2(math.e))).astype(BF16)
    lane = lax.broadcasted_iota(jnp.int32, q.shape, 1)
    zero = jnp.zeros_like(q)
    qm = (jnp.where(lane < ATT_HALF_DIM, q, zero), jnp.where(lane >= ATT_HALF_DIM, q, zero))

    m_ref[...] = jnp.full(m_ref.shape, NEG_BIG, F32)
    acc_ref[...] = jnp.zeros(acc_ref.shape, F32)

    def scores(c, mi, k):
        return lax.dot_general(k, qm[mi][c * sq:(c + 1) * sq, :], (((1,), (1,)), ((), ())),
                               preferred_element_type=F32)

    def update(c, mi, vt, s, smax):
        cs = slice(c * sq, (c + 1) * sq)
        m_old = m_ref[mi, :, cs]
        m_new = jnp.maximum(m_old, smax)
        p = jnp.exp2(s - m_new)
        alpha = jnp.exp2(m_old - m_new)
        acc_ref[mi, :, cs] = alpha * acc_ref[mi, :, cs] + jnp.dot(
            vt, p.astype(BF16), preferred_element_type=F32)
        m_ref[mi, :, cs] = m_new

    assert tq == 2 * tk
    chains = [(c, mi) for c in range(nc) for mi in range(2)]

    bufs = ((sa_ref, xa_ref), (sb_ref, xb_ref))

    def produce(idx, k, dst):
        c, mi = chains[idx]
        s = scores(c, mi, k)
        dst[0][idx] = s
        dst[1][idx] = _col_reduce(s, jnp.max)

    def stage(j, src, dst):
        vt = vt_ref[j]
        if dst is not None:
            kn = k_ref[pl.ds(pl.multiple_of((j + 1) * tk, tk), tk), :]
        for idx, (c, mi) in enumerate(chains):
            if dst is not None:
                produce(idx, kn, dst)
            update(c, mi, vt, src[0][idx], src[1][idx])

    @pl.when(qi > 0)
    def _():
        k0 = k_ref[0:tk, :]
        for idx in range(len(chains)):
            produce(idx, k0, bufs[0])

        def pair(pp, carry):
            stage(2 * pp, bufs[0], bufs[1])
            stage(2 * pp + 1, bufs[1], bufs[0])
            return carry

        lax.fori_loop(0, qi - 1, pair, 0)
        stage(2 * qi - 2, bufs[0], bufs[1])
        stage(2 * qi - 1, bufs[1], None)

    krow = lax.broadcasted_iota(jnp.int32, (sq, sq), 0)
    qcol = lax.broadcasted_iota(jnp.int32, (sq, sq), 1)
    dmask = krow // MASK_CHUNK <= qcol // MASK_CHUNK
    vt_tile = jnp.concatenate([vt_ref[qi * 2], vt_ref[qi * 2 + 1]], axis=1)

    def diag_slot(n, piece):
        slot = 2 * n + piece
        return bufs[slot // len(chains)], slot % len(chains)

    def diag_produce(n, c, mi):
        nk = (c + 1) * sq
        s = scores(c, mi, k_ref[pl.ds(pl.multiple_of(qi * tq, tq), nk), :])
        tail = jnp.where(dmask, s[c * sq:, :], NEG_BIG)
        s = tail if c == 0 else jnp.concatenate([s[:c * sq, :], tail], axis=0)
        for piece, r0 in enumerate(range(0, nk, tk)):
            buf, slot = diag_slot(n, piece)
            rows = min(tk, nk - r0)
            buf[0][slot, 0:rows, :] = s[r0:r0 + rows, :]
        buf, slot = diag_slot(n, 0)
        buf[1][slot] = _col_reduce(s, jnp.max)

    def diag_consume(n, c, mi):
        nk = (c + 1) * sq
        parts = []
        for piece, r0 in enumerate(range(0, nk, tk)):
            buf, slot = diag_slot(n, piece)
            parts.append(buf[0][slot, 0:min(tk, nk - r0), :])
        buf, slot = diag_slot(n, 0)
        update(c, mi, vt_tile[:, :nk], jnp.concatenate(parts, axis=0), buf[1][slot])

    order = [(c, mi) for c in reversed(range(nc)) for mi in range(2)]
    for n in range(len(order) + DIAG_AHEAD):
        if n < len(order):
            diag_produce(n, *order[n])
        if n >= DIAG_AHEAD:
            diag_consume(n - DIAG_AHEAD, *order[n - DIAG_AHEAD])

    lp = lam_ref[...]
    lam = (jnp.exp(jnp.sum(lp[0:1] * lp[1:2], axis=-1, keepdims=True))
           - jnp.exp(jnp.sum(lp[2:3] * lp[3:4], axis=-1, keepdims=True)) + lam_init)
    o = (acc_ref[0, 0:HEAD_W, :] / acc_ref[0, HEAD_W:HEAD_W + 1, :]
         - lam * (acc_ref[1, 0:HEAD_W, :] / acc_ref[1, HEAD_W:HEAD_W + 1, :]))
    ms = jnp.mean(o * o, axis=0, keepdims=True)
    y = o * lax.rsqrt(ms + NORM_EPS) * sg_ref[...] * (1.0 - lam_init)
    o_ref[q_rows, :] = y.T.astype(o_ref.dtype)
    return carry


def _attention(p, lam4, sg_col, nb, s, lam_init, ct, w_mod, b_row, mod_col0, tq=1024, sq=256,
               tk=512):
    h = ATT_HEADS
    kern = functools.partial(_attn_kernel, tq=tq, sq=sq, tk=tk, lam_init=lam_init)
    nchains = 2 * tq // sq
    d = ct.shape[0]
    sw = (w_mod.shape[1] - mod_col0) // (nb * h)
    mslab = lambda rows, off: pl.BlockSpec((rows, sw), lambda b, hh: (0, off + b * h + hh))
    return pl.pallas_call(
        kern,
        grid=(nb, h),
        in_specs=[pl.BlockSpec((4, ATT_HALF_DIM), lambda b, hh: (0, 0)),
                  pl.BlockSpec((HEAD_W, 1), lambda b, hh: (0, 0)),
                  pl.BlockSpec((s, HEAD_W), lambda b, hh: (b, hh)),
                  pl.BlockSpec((s, HEAD_W), lambda b, hh: (b, h + hh)),
                  pl.BlockSpec((s, HEAD_W), lambda b, hh: (b, 2 * h + hh)),
                  pl.BlockSpec(ct.shape, lambda b, hh: (0, 0)),
                  mslab(d, mod_col0 // sw), mslab(1, mod_col0 // sw)],
        out_specs=[pl.BlockSpec((s, HEAD_W), lambda b, hh: (b, hh)), mslab(nb, 0)],
        out_shape=[jax.ShapeDtypeStruct((nb * s, h * HEAD_W), BF16),
                   jax.ShapeDtypeStruct((nb, w_mod.shape[1] - mod_col0), F32)],
        scratch_shapes=[pltpu.VMEM((s // tk, HEAD_W + ONES_ROWS, tk), BF16),
                        pltpu.VMEM((2, HEAD_W + ONES_ROWS, tq), F32),
                        pltpu.VMEM((2, 1, tq), F32),
                        pltpu.VMEM((nchains, tk, sq), F32),
                        pltpu.VMEM((nchains, tk, sq), F32),
                        pltpu.VMEM((nchains, 1, sq), F32),
                        pltpu.VMEM((nchains, 1, sq), F32)],
        compiler_params=_params(("parallel", "parallel")),
        name="diff_attn",
    )(lam4, sg_col, p, p, p, ct, w_mod, b_row)


def _group_row(a, s, grp=8):
    n, w = a.shape
    a3 = a.reshape(n // grp, grp, w)
    return jnp.broadcast_to(a3[:, s:s + 1, :], (n // grp, grp, w)).reshape(n, w)


def _hgrn2_kernel(lbl_ref, gn_ref, q_ref, f_ref, i_ref, g_ref, o_ref, st_ref, *, ch, layer):
    @pl.when(pl.program_id(1) == 0)
    def _():
        st_ref[...] = jnp.zeros(st_ref.shape, F32)

    w = HEAD_W
    lbl = lbl_ref[...]
    mx = jnp.max(lbl, axis=0, keepdims=True)
    e = jnp.exp(lbl - mx)
    lb = (jnp.sum(e[0:layer + 1], axis=0, keepdims=True)
          / jnp.sum(e, axis=0, keepdims=True))

    nt = (((1,), (1,)), ((), ()))
    row = lax.broadcasted_iota(jnp.int32, (ch, w), 0)
    arow = lax.broadcasted_iota(jnp.int32, (ch, ch), 0)
    acol = lax.broadcasted_iota(jnp.int32, (ch, ch), 1)
    tril = (arow >= acol).astype(BF16)
    on_diag = arow == acol
    levels = []
    hs = ch // 2
    while hs >= 1:
        upper = (row % (2 * hs)) >= hs
        levels.append((hs, upper, jnp.where(upper, 1.0, -1.0),
                       (arow // (2 * hs) == acol // (2 * hs))
                       & ((arow % (2 * hs)) >= hs) & ((acol % (2 * hs)) < hs)))
        hs //= 2

    def midpoint_row(b, hs):
        if hs >= 8:
            return jnp.concatenate(
                [jnp.broadcast_to(b[blk * 2 * hs + hs - 1:blk * 2 * hs + hs, :], (2 * hs, w))
                 for blk in range(ch // (2 * hs))], axis=0)
        if hs == 4:
            return _group_row(b, 3)
        return jnp.where((row % 8) < 4, _group_row(b, 1), _group_row(b, 5))

    chunk_vals = {}

    def chunk_prep(c):
        if c not in chunk_vals:
            rs = slice(c * ch, (c + 1) * ch)
            f_all = lb + (1.0 - lb) * jax.nn.sigmoid(f_ref[rs, :].astype(F32))
            g2 = jnp.log2(f_all)
            g2_hi = g2.astype(BF16)
            g2_lo = (g2 - g2_hi.astype(F32)).astype(BF16)
            b_all = (jnp.dot(tril, g2_hi, preferred_element_type=F32)
                     + jnp.dot(tril, g2_lo, preferred_element_type=F32))
            chunk_vals[c] = (f_all, b_all)
        return chunk_vals[c]

    def operands(c, h):
        rs, cs = slice(c * ch, (c + 1) * ch), slice(h * w, (h + 1) * w)
        f_all, b_all = chunk_prep(c)
        b, f = b_all[:, cs], f_all[:, cs]
        kk = 1.0 - f
        q = q_ref[rs, cs].astype(F32)
        blast = b[ch - 1:ch, :]
        ys = []
        for hs, upper, sign, _ in levels:
            if hs == 1:
                y = jnp.where(upper, q * f, kk)
            else:
                y = jnp.where(upper, q, kk) * jnp.exp2((b - midpoint_row(b, hs)) * sign)
            ys.append(y.astype(BF16))
        return dict(rs=rs, cs=cs, h=h, ys=ys,
                    qin=(q * jnp.exp2(b)).astype(BF16),
                    kout=(kk * jnp.exp2(blast - b)).astype(BF16),
                    keep=jnp.exp2(blast),
                    dsum=jnp.sum(q * kk, axis=-1, keepdims=True))

    def level_matmuls(x):
        st = st_ref[x["h"]]
        x["st"] = st
        x["o"] = lax.dot_general(x["qin"], st.astype(BF16), nt, preferred_element_type=F32)
        x["aa"] = [lax.dot_general(y, y, nt, preferred_element_type=F32) for y in x["ys"]]

    def finish(x):
        rs, cs = x["rs"], x["cs"]
        v = i_ref[rs, cs]
        att = jnp.where(on_diag, x["dsum"], 0.0).astype(BF16)
        for (_, _, _, amask), a in zip(levels, x["aa"]):
            att = jnp.where(amask, a.astype(BF16), att)
        o = x["o"] + jnp.dot(att, v, preferred_element_type=F32)
        st_ref[x["h"]] = x["keep"] * x["st"] + lax.dot_general(
            v, x["kout"], (((0,), (0,)), ((), ())), preferred_element_type=F32)
        y = _rms(o) * gn_ref[:, cs] * jax.nn.sigmoid(g_ref[rs, cs].astype(F32))
        o_ref[rs, cs] = y.astype(o_ref.dtype)

    items = [(c, h) for c in range(q_ref.shape[0] // ch) for h in range(REC_HEADS)]
    staged = []
    d1, d2 = 1, 2
    for i in range(len(items) + d2):
        if i < len(items):
            staged.append(operands(*items[i]))
        if d1 <= i < len(items) + d1:
            level_matmuls(staged[i - d1])
        if i >= d2:
            finish(staged[i - d2])


def _hgrn2(p, lb_logits, gn, nb, s, layer, rb=512, ch=128):
    hw = REC_HEADS * HEAD_W
    nr = s // rb
    c0 = 3 * ATT_HEADS * HEAD_W // hw
    kern = functools.partial(_hgrn2_kernel, ch=ch, layer=layer)
    spec = lambda off: pl.BlockSpec((rb, hw), lambda b, i: (b * nr + i, c0 + off))
    return pl.pallas_call(
        kern,
        grid=(nb, nr),
        in_specs=[pl.BlockSpec(lb_logits.shape, lambda b, i: (0, 0)),
                  pl.BlockSpec((1, hw), lambda b, i: (0, 0)),
                  spec(0), spec(1), spec(2), spec(3)],
        out_specs=pl.BlockSpec((rb, hw), lambda b, i: (b * nr + i, 0)),
        out_shape=jax.ShapeDtypeStruct((nb * s, hw), BF16),
        scratch_shapes=[pltpu.VMEM((REC_HEADS, HEAD_W, HEAD_W), F32)],
        compiler_params=_params(("parallel", "arbitrary")),
        name="hgrn2",
    )(lb_logits, gn, p, p, p, p)


def _merge_oproj_kernel(ya_ref, yr_ref, *refs, ng):
    ga_refs, gr_refs = refs[:ng], refs[ng:2 * ng]
    x_ref, gt_ref, wa_hbm, wr_hbm, wo_hbm, o_ref, wa_ref, wr_ref, wo_ref, sem = refs[2 * ng:]
    gw = wa_ref.shape[1] // ng
    first = pl.program_id(0) == 0

    def block_copies(g):
        cols = slice(g * gw, (g + 1) * gw)
        return [pltpu.make_async_copy(src.at[:, cols], dst.at[:, cols], sem.at[2 * g + n])
                for n, (src, dst) in enumerate(((wa_hbm, wa_ref), (wr_hbm, wr_ref)))]

    wo_copy = pltpu.make_async_copy(wo_hbm, wo_ref, sem.at[2 * ng])

    def compute(wait):
        parts = []
        for g in range(ng):
            cols = slice(g * gw, (g + 1) * gw)
            if wait:
                for copy in block_copies(g):
                    copy.wait()
            ta = jnp.dot(ya_ref[...], wa_ref[:, cols], preferred_element_type=F32)
            tr = jnp.dot(yr_ref[...], wr_ref[:, cols], preferred_element_type=F32)
            parts.append((jax.nn.sigmoid(ga_refs[g][...].astype(F32)) * ta
                          + jax.nn.sigmoid(gr_refs[g][...].astype(F32)) * tr).astype(BF16))
        m = jnp.concatenate(parts, axis=1)
        if wait:
            wo_copy.wait()
        o_ref[...] = x_ref[...] + gt_ref[...] * jnp.dot(m, wo_ref[...],
                                                         preferred_element_type=F32)

    @pl.when(first)
    def _():
        for g in range(ng):
            for copy in block_copies(g):
                copy.start()
        wo_copy.start()
        compute(wait=True)

    @pl.when(jnp.logical_not(first))
    def _():
        compute(wait=False)


def _merge_oproj(ya, yr, p, x2, mod4, wa, wr, wo, ga_col0, rows_per_batch, tm=512, gw=1024):
    t, ka = ya.shape
    d = wa.shape[1]
    ng = d // gw
    ga0 = ga_col0 // gw
    gr0 = (ga_col0 + d) // gw
    tpb = rows_per_batch // tm
    gate = lambda b0, g: pl.BlockSpec((tm, gw), lambda i: (i, b0 + g))
    in_hbm = pl.BlockSpec(memory_space=pltpu.MemorySpace.HBM)
    return pl.pallas_call(
        functools.partial(_merge_oproj_kernel, ng=ng),
        grid=(t // tm,),
        in_specs=[pl.BlockSpec((tm, ka), lambda i: (i, 0)),
                  pl.BlockSpec((tm, ka), lambda i: (i, 0))]
                 + [gate(ga0, g) for g in range(ng)] + [gate(gr0, g) for g in range(ng)]
                 + [pl.BlockSpec((tm, d), lambda i: (i, 0)),
                    pl.BlockSpec((None, None, 1, d), lambda i: (i // tpb, 2, 0, 0)),
                    in_hbm, in_hbm, in_hbm],
        out_specs=pl.BlockSpec((tm, d), lambda i: (i, 0)),
        out_shape=jax.ShapeDtypeStruct((t, d), F32),
        scratch_shapes=[pltpu.VMEM(w.shape, w.dtype) for w in (wa, wr, wo)]
                       + [pltpu.SemaphoreType.DMA((2 * ng + 1,))],
        compiler_params=_params(("arbitrary",)),
        name="merge_oproj",
    )(ya, yr, *([p] * (2 * ng)), x2, mod4, wa, wr, wo)


def _ffn_up(h1, g, mod4, wg, wu, rows_per_batch, cast_ws):
    swiglu = lambda gate, up: gate * jax.nn.sigmoid(gate) * up
    return _adaln_matmul(h1, g, mod4, 3, [wg, wu], rows_per_batch, swiglu, "ffn_up",
                         tm=2048, tn=512, rc=256, rows=256, cast_ws=cast_ws)


def _ffn_down_kernel(a_hbm, h_ref, gt_ref, fg_ref, w_ref, o_ref, a_buf, sem, *, tn, final):
    i, j = pl.program_id(0), pl.program_id(1)
    tm = a_buf.shape[1]
    slot = i % 2

    def a_copy(tile, s):
        rows = pl.ds(pl.multiple_of(tile * tm, tm), tm)
        return pltpu.make_async_copy(a_hbm.at[rows, :], a_buf.at[s], sem.at[s])

    @pl.when((i == 0) & (j == 0))
    def _():
        a_copy(0, 0).start()

    @pl.when(j == 0)
    def _():
        a_copy(i, slot).wait()

        @pl.when(i + 1 < pl.num_programs(0))
        def _():
            a_copy(i + 1, 1 - slot).start()

    y = jnp.dot(a_buf[slot], w_ref[...], preferred_element_type=F32)
    o_ref[:, pl.ds(pl.multiple_of(j * tn, tn), tn)] = h_ref[...] + gt_ref[...] * y

    if final:
        @pl.when(j == pl.num_programs(1) - 1)
        def _():
            o_ref[...] = _rms(o_ref[...]) * fg_ref[...]


def _ffn_down(a, h1, mod4, fg, w, rows_per_batch, final, tm=1024, tn=512):
    t, d = h1.shape
    f = a.shape[1]
    tpb = rows_per_batch // tm
    return pl.pallas_call(
        functools.partial(_ffn_down_kernel, tn=tn, final=final),
        grid=(t // tm, d // tn),
        in_specs=[pl.BlockSpec(memory_space=pltpu.MemorySpace.HBM),
                  pl.BlockSpec((tm, tn), lambda i, j: (i, j)),
                  pl.BlockSpec((None, None, 1, tn), lambda i, j: (i // tpb, 5, 0, j)),
                  pl.BlockSpec((1, d), lambda i, j: (0, 0)),
                  pl.BlockSpec((f, tn), lambda i, j: (0, j))],
        out_specs=pl.BlockSpec((tm, d), lambda i, j: (i, 0)),
        out_shape=jax.ShapeDtypeStruct((t, d), F32),
        scratch_shapes=[pltpu.VMEM((2, tm, f), a.dtype), pltpu.SemaphoreType.DMA((2,))],
        compiler_params=_params(("arbitrary", "arbitrary")),
        name="ffn_down",
    )(a, h1, mod4, fg, w)


def kernel(x, c, w_mod, b_mod, norm1_g, w_in, lambda_q1, lambda_k1, lambda_q2, lambda_k2,
           subln_g, lb_logits, gnorm_g, w_att_out, w_rec_out, w_o, norm2_g,
           w_ffn_gate, w_ffn_up, w_ffn_down, final_g):
    nb, s, d = x.shape
    depth = w_mod.shape[0]
    h = x.reshape(nb * s, d)
    for l in range(depth):
        lam_init = 0.8 - 0.6 * math.exp(-0.3 * l)
        ct, b_row = c.T, b_mod[l].reshape(1, 6 * d)
        mod_head = _mod(ct, w_mod[l], b_row, 2 * d)
        p, (wa, wr, wo) = _in_proj(h, norm1_g[l].reshape(1, d), mod_head.reshape(nb, 2, 1, d),
                                   w_in[l], s, cast_ws=(w_att_out[l], w_rec_out[l], w_o[l]))
        lam4 = jnp.stack([lambda_q1[l], lambda_k1[l], lambda_q2[l], lambda_k2[l]])
        ya, mod_tail = _attention(p, lam4, subln_g[l].reshape(HEAD_W, 1), nb, s, lam_init,
                                  ct, w_mod[l], b_row, 2 * d)
        mod4 = jnp.concatenate([mod_head, mod_tail], axis=1).reshape(nb, 6, 1, d)
        yr = _hgrn2(p, lb_logits, gnorm_g[l].reshape(1, REC_HEADS * HEAD_W), nb, s, l)
        ga_col0 = 3 * ATT_HEADS * HEAD_W + 4 * REC_HEADS * HEAD_W
        h1 = _merge_oproj(ya, yr, p, h, mod4, wa, wr, wo, ga_col0, s)
        a, (wd,) = _ffn_up(h1, norm2_g[l].reshape(1, d), mod4, w_ffn_gate[l], w_ffn_up[l], s,
                           cast_ws=(w_ffn_down[l],))
        h = _ffn_down(a, h1, mod4, final_g.reshape(1, d), wd, s, l == depth - 1)
    return h.reshape(nb, s, d)
```
